```python
import math
import jax, jax.numpy as jnp
from jax import lax
import numpy as np

D_MODEL = 1024
BATCH = 16
SEQ = 2048
DEPTH = 4

CTX_LEN = 256
GRID_W = 64
MIX = D_MODEL
BRANCH_W = MIX // 4
Q_BLOCK = 128
ROPE_THETA = 10000.0
EPS = 1e-6
DA_HEADS = 4
DA_QK = BRANCH_W // (2 * DA_HEADS)
DA_V = BRANCH_W // DA_HEADS
GQ_HEADS = 4
GQ_KV_HEADS = 2
GQ_HD = BRANCH_W // GQ_HEADS
SG_GROUPS = 4
SG_CHUNK = 128
GLA_HEADS = 4
GLA_DV = BRANCH_W // GLA_HEADS
GLA_DK = GLA_DV // 2
GLA_RANK = 16
GLA_NORMALIZER = 16.0
GLA_CHUNK = 32

IN_SIZES = (
    DA_HEADS * 2 * DA_QK, DA_HEADS * 2 * DA_QK, DA_HEADS * DA_V, BRANCH_W,
    GQ_HEADS * GQ_HD, GQ_KV_HEADS * GQ_HD, GQ_KV_HEADS * GQ_HD, BRANCH_W,
    BRANCH_W, BRANCH_W, BRANCH_W,
    GLA_HEADS * GLA_DK, GLA_HEADS * GLA_DK, GLA_HEADS * GLA_DV, BRANCH_W,
    GLA_RANK, GLA_RANK,
)
P_IN = sum(IN_SIZES)

kernel_name = "hybrid_parallel_heads_flow_backbone"


def rms_norm(x, g):
    xf = x.astype(jnp.float32)
    y = xf * lax.rsqrt(jnp.mean(xf * xf, axis=-1, keepdims=True) + EPS)
    return (y * g.astype(jnp.float32)).astype(x.dtype)


def layer_norm(x, g, b):
    xf = x.astype(jnp.float32)
    mu = jnp.mean(xf, axis=-1, keepdims=True)
    var = jnp.mean(jnp.square(xf - mu), axis=-1, keepdims=True)
    y = (xf - mu) * lax.rsqrt(var + EPS)
    return (y * g.astype(jnp.float32) + b.astype(jnp.float32)).astype(x.dtype)


def split_cols(p):
    out = []
    off = 0
    for n in IN_SIZES:
        out.append(p[..., off:off + n])
        off += n
    return out


def rope_angles(pos, dim):
    inv = ROPE_THETA ** (-jnp.arange(0, dim, 2, dtype=jnp.float32) / dim)
    return pos.astype(jnp.float32)[:, None] * inv[None, :]


def rope_1d(x, ang):
    half = x.shape[-1] // 2
    shape = (ang.shape[0],) + (1,) * (x.ndim - 3) + (half,)
    cos = jnp.cos(ang).reshape(shape)
    sin = jnp.sin(ang).reshape(shape)
    xf = x.astype(jnp.float32)
    x1, x2 = xf[..., :half], xf[..., half:]
    return jnp.concatenate([x1 * cos - x2 * sin, x1 * sin + x2 * cos], axis=-1).astype(x.dtype)


def rope_2d(x, rows, cols):
    h = x.shape[-1] // 2
    return jnp.concatenate([rope_1d(x[..., :h], rope_angles(rows, h)),
                            rope_1d(x[..., h:], rope_angles(cols, h))], axis=-1)


def sweep_query_blocks(fn, q):
    b, t = q.shape[:2]
    nb = t // Q_BLOCK
    qb = jnp.moveaxis(q.reshape((b, nb, Q_BLOCK) + q.shape[2:]), 1, 0)
    ob = lax.map(fn, qb)
    return jnp.moveaxis(ob, 0, 1).reshape((b, t) + ob.shape[3:])


def diff_attn_core(q, k, v, lam, lam_init, subln_g):
    s = jnp.einsum('bqhmd,bkhmd->bhmqk', q, k).astype(jnp.float32) * (DA_QK ** -0.5)
    p = jax.nn.softmax(s, axis=-1)
    w = p[:, :, 0] - lam * p[:, :, 1]
    o = jnp.einsum('bhqk,bkhd->bqhd', w.astype(v.dtype), v)
    return rms_norm(o, subln_g) * (1.0 - lam_init)


def diff_attention_branch(q_l, k_l, v_l, q_c, k_c, v_c, lq1, lk1, lq2, lk2, subln_g,
                          lam_init, rows, cols, with_ctx):
    b, t = q_l.shape[:2]
    tc = q_c.shape[1]
    q_l = rope_2d(q_l.reshape(b, t, DA_HEADS, 2, DA_QK), rows, cols)
    k_l = rope_2d(k_l.reshape(b, t, DA_HEADS, 2, DA_QK), rows, cols)
    k_c = k_c.reshape(b, tc, DA_HEADS, 2, DA_QK)
    v_l = v_l.reshape(b, t, DA_HEADS, DA_V)
    v_c = v_c.reshape(b, tc, DA_HEADS, DA_V)
    lam = (jnp.exp(jnp.sum((lq1 * lk1).astype(jnp.float32)))
           - jnp.exp(jnp.sum((lq2 * lk2).astype(jnp.float32))) + lam_init)
    k_all = jnp.concatenate([k_c, k_l], axis=1)
    v_all = jnp.concatenate([v_c, v_l], axis=1)
    y_l = sweep_query_blocks(lambda qb: diff_attn_core(qb, k_all, v_all, lam, lam_init, subln_g), q_l)
    y_l = y_l.reshape(b, t, DA_HEADS * DA_V)
    y_c = None
    if with_ctx:
        q_c = q_c.reshape(b, tc, DA_HEADS, 2, DA_QK)
        y_c = diff_attn_core(q_c, k_c, v_c, lam, lam_init, subln_g).reshape(b, tc, DA_HEADS * DA_V)
    return y_l, y_c


def gqa_core(q, k, v):
    s = jnp.einsum('bqhgd,bkhd->bhgqk', q, k).astype(jnp.float32) * (GQ_HD ** -0.5)
    p = jax.nn.softmax(s, axis=-1)
    return jnp.einsum('bhgqk,bkhd->bqhgd', p.astype(v.dtype), v)


def gqa_branch(q_l, k_l, v_l, q_c, k_c, v_c, qn_g, kn_g, rows, cols, with_ctx):
    b, t = q_l.shape[:2]
    tc = q_c.shape[1]
    grp = GQ_HEADS // GQ_KV_HEADS
    q_l = rope_2d(rms_norm(q_l.reshape(b, t, GQ_HEADS, GQ_HD), qn_g), rows, cols)
    k_l = rope_2d(rms_norm(k_l.reshape(b, t, GQ_KV_HEADS, GQ_HD), kn_g), rows, cols)
    k_c = rms_norm(k_c.reshape(b, tc, GQ_KV_HEADS, GQ_HD), kn_g)
    v_l = v_l.reshape(b, t, GQ_KV_HEADS, GQ_HD)
    v_c = v_c.reshape(b, tc, GQ_KV_HEADS, GQ_HD)
    k_all = jnp.concatenate([k_c, k_l], axis=1)
    v_all = jnp.concatenate([v_c, v_l], axis=1)
    q_l = q_l.reshape(b, t, GQ_KV_HEADS, grp, GQ_HD)
    y_l = sweep_query_blocks(lambda qb: gqa_core(qb, k_all, v_all), q_l).reshape(b, t, GQ_HEADS * GQ_HD)
    y_c = None
    if with_ctx:
        q_c = rms_norm(q_c.reshape(b, tc, GQ_HEADS, GQ_HD), qn_g).reshape(b, tc, GQ_KV_HEADS, grp, GQ_HD)
        y_c = gqa_core(q_c, k_c, v_c).reshape(b, tc, GQ_HEADS * GQ_HD)
    return y_l, y_c


def chunk_sgu(u, v, ln_g, ln_b, w_s, b_s):
    b, t, w = u.shape
    n = t // SG_CHUNK
    vn = layer_norm(v, ln_g, ln_b).reshape(b, n, SG_CHUNK, SG_GROUPS, w // SG_GROUPS)
    mixed = jnp.einsum('gij,bnjgc->bnigc', w_s, vn) + b_s.T[:, :, None]
    return u * mixed.reshape(b, t, w)


def gla_scan(q, k, v, g, s0):
    b, t, h, _ = q.shape
    dv = v.shape[-1]
    nc = t // GLA_CHUNK

    def to_chunks(a):
        return a.astype(jnp.float32).reshape(b, nc, GLA_CHUNK, h, a.shape[-1]).transpose(1, 0, 3, 2, 4)

    mask = jnp.tril(jnp.ones((GLA_CHUNK, GLA_CHUNK), dtype=bool))

    def step(s, inp):
        qc, kc, vc, gc = inp
        cum = jnp.cumsum(gc, axis=2)
        cum_last = cum[:, :, -1:, :]
        qe = qc * jnp.exp(cum)
        a = jnp.einsum('bhid,bhjd->bhij', qe, kc * jnp.exp(-cum))
        a = jnp.where(mask, a, 0.0)
        o = jnp.einsum('bhij,bhjv->bhiv', a, vc) + jnp.einsum('bhid,bhdv->bhiv', qe, s)
        s = (jnp.exp(cum_last[:, :, 0, :])[..., None] * s
             + jnp.einsum('bhjd,bhjv->bhdv', kc * jnp.exp(cum_last - cum), vc))
        return s, o

    s_fin, o = lax.scan(step, s0, (to_chunks(q), to_chunks(k), to_chunks(v), to_chunks(g)))
    o = o.transpose(1, 0, 3, 2, 4).reshape(b, t, h, dv)
    return o.astype(v.dtype), s_fin


def gla_heads(q, k, v, rf, rb, w2_f, b_f, w2_b, b_b):
    b, t = q.shape[:2]
    q = q.reshape(b, t, GLA_HEADS, GLA_DK) * (GLA_DK ** -0.5)
    k = k.reshape(b, t, GLA_HEADS, GLA_DK)
    v = v.reshape(b, t, GLA_HEADS, GLA_DV)
    gf = (jax.nn.log_sigmoid((rf @ w2_f + b_f).astype(jnp.float32)) / GLA_NORMALIZER).reshape(b, t, GLA_HEADS, GLA_DK)
    gb = (jax.nn.log_sigmoid((rb @ w2_b + b_b).astype(jnp.float32)) / GLA_NORMALIZER).reshape(b, t, GLA_HEADS, GLA_DK)
    return q, k, v, gf, gb


def gla_branch(q_l, k_l, v_l, rf_l, rb_l, q_c, k_c, v_c, rf_c, rb_c,
               w2_f, b_f, w2_b, b_b, norm_g, with_ctx):
    ql, kl, vl, gfl, gbl = gla_heads(q_l, k_l, v_l, rf_l, rb_l, w2_f, b_f, w2_b, b_b)
    qc, kc, vc, gfc, gbc = gla_heads(q_c, k_c, v_c, rf_c, rb_c, w2_f, b_f, w2_b, b_b)
    b, t = ql.shape[:2]
    tc = qc.shape[1]
    s0 = jnp.zeros((b, GLA_HEADS, GLA_DK, GLA_DV), jnp.float32)
    flip = lambda a: jnp.flip(a, axis=1)
    oc_f, sc_f = gla_scan(qc, kc, vc, gfc, s0)
    oc_b, sc_b = gla_scan(flip(qc), flip(kc), flip(vc), flip(gbc), s0)
    ol_f, _ = gla_scan(ql, kl, vl, gfl, sc_f)
    ol_b, _ = gla_scan(flip(ql), flip(kl), flip(vl), flip(gbl), sc_b)
    y_l = rms_norm(ol_f + flip(ol_b), norm_g).reshape(b, t, GLA_HEADS * GLA_DV)
    y_c = None
    if with_ctx:
        y_c = rms_norm(oc_f + flip(oc_b), norm_g).reshape(b, tc, GLA_HEADS * GLA_DV)
    return y_l, y_c


def setup_inputs(seed: int = 0) -> dict:
    key = jax.random.key(seed)
    ks = jax.random.split(key, 32)

    def nrm(k, shape, s):
        return jax.random.normal(k, shape, jnp.float32) * s

    L = DEPTH
    return {
        "x": nrm(ks[0], (BATCH, SEQ, D_MODEL), 1.0),
        "c": nrm(ks[1], (BATCH, D_MODEL), 1.0),
        "ctx": nrm(ks[2], (BATCH, CTX_LEN, D_MODEL), 1.0),
        "c_ctx": nrm(ks[3], (D_MODEL,), 1.0),
        "ada_w": nrm(ks[4], (L, D_MODEL, 3 * D_MODEL), 0.5 * D_MODEL ** -0.5),
        "ada_b": nrm(ks[5], (L, 3 * D_MODEL), 0.02),
        "norm_g": 1.0 + nrm(ks[6], (L, D_MODEL), 0.02),
        "w_in": nrm(ks[7], (L, D_MODEL, P_IN), D_MODEL ** -0.5),
        "da_lq1": nrm(ks[8], (L, DA_QK), 0.1),
        "da_lk1": nrm(ks[9], (L, DA_QK), 0.1),
        "da_lq2": nrm(ks[10], (L, DA_QK), 0.1),
        "da_lk2": nrm(ks[11], (L, DA_QK), 0.1),
        "da_subln_g": 1.0 + nrm(ks[12], (L, DA_V), 0.02),
        "gq_qnorm_g": 1.0 + nrm(ks[13], (L, GQ_HD), 0.02),
        "gq_knorm_g": 1.0 + nrm(ks[14], (L, GQ_HD), 0.02),
        "sg_ln_g": 1.0 + nrm(ks[15], (L, BRANCH_W), 0.02),
        "sg_ln_b": nrm(ks[16], (L, BRANCH_W), 0.02),
        "sg_w": nrm(ks[17], (L, SG_GROUPS, SG_CHUNK, SG_CHUNK), SG_CHUNK ** -0.5),
        "sg_b": 1.0 + nrm(ks[18], (L, SG_GROUPS, SG_CHUNK), 0.02),
        "gla_w2_f": nrm(ks[19], (L, GLA_RANK, GLA_HEADS * GLA_DK), GLA_RANK ** -0.5),
        "gla_b_f": nrm(ks[20], (L, GLA_HEADS * GLA_DK), 0.5),
        "gla_w2_b": nrm(ks[21], (L, GLA_RANK, GLA_HEADS * GLA_DK), GLA_RANK ** -0.5),
        "gla_b_b": nrm(ks[22], (L, GLA_HEADS * GLA_DK), 0.5),
        "gla_norm_g": 1.0 + nrm(ks[23], (L, GLA_DV), 0.02),
        "w_out": nrm(ks[24], (L, MIX, D_MODEL), MIX ** -0.5),
        "final_norm_g": 1.0 + nrm(ks[25], (D_MODEL,), 0.02),
    }


def reference(x, c, ctx, c_ctx, ada_w, ada_b, norm_g, w_in, da_lq1, da_lk1, da_lq2, da_lk2,
              da_subln_g, gq_qnorm_g, gq_knorm_g, sg_ln_g, sg_ln_b, sg_w, sg_b,
              gla_w2_f, gla_b_f, gla_w2_b, gla_b_b, gla_norm_g, w_out, final_norm_g):
    n_rows = x.shape[1] // GRID_W
    rows = jnp.repeat(jnp.arange(n_rows, dtype=jnp.int32), GRID_W)
    cols = jnp.tile(jnp.arange(GRID_W, dtype=jnp.int32), n_rows)
    silu_c = jax.nn.silu(c)
    silu_cc = jax.nn.silu(c_ctx)
    xl, xc = x, ctx
    for i in range(DEPTH):
        with_ctx = i < DEPTH - 1
        lam_init = 0.8 - 0.6 * math.exp(-0.3 * i)
        mod_l = silu_c @ ada_w[i] + ada_b[i]
        mod_c = silu_cc @ ada_w[i] + ada_b[i]
        sh_l, sc_l, gt_l = jnp.split(mod_l, 3, axis=-1)
        sh_c, sc_c, gt_c = jnp.split(mod_c, 3, axis=-1)
        hl = rms_norm(xl, norm_g[i]) * (1.0 + sc_l[:, None]) + sh_l[:, None]
        hc = rms_norm(xc, norm_g[i]) * (1.0 + sc_c) + sh_c
        (aq_l, ak_l, av_l, az_l, bq_l, bk_l, bv_l, bz_l, cu_l, cv_l, cz_l,
         dq_l, dk_l, dv_l, dz_l, drf_l, drb_l) = split_cols(hl @ w_in[i])
        (aq_c, ak_c, av_c, az_c, bq_c, bk_c, bv_c, bz_c, cu_c, cv_c, cz_c,
         dq_c, dk_c, dv_c, dz_c, drf_c, drb_c) = split_cols(hc @ w_in[i])

        ya_l, ya_c = diff_attention_branch(aq_l, ak_l, av_l, aq_c, ak_c, av_c,
                                           da_lq1[i], da_lk1[i], da_lq2[i], da_lk2[i], da_subln_g[i],
                                           lam_init, rows, cols, with_ctx)
        yb_l, yb_c = gqa_branch(bq_l, bk_l, bv_l, bq_c, bk_c, bv_c,
                                gq_qnorm_g[i], gq_knorm_g[i], rows, cols, with_ctx)
        yc_l = chunk_sgu(cu_l, cv_l, sg_ln_g[i], sg_ln_b[i], sg_w[i], sg_b[i])
        yd_l, yd_c = gla_branch(dq_l, dk_l, dv_l, drf_l, drb_l, dq_c, dk_c, dv_c, drf_c, drb_c,
                                gla_w2_f[i], gla_b_f[i], gla_w2_b[i], gla_b_b[i], gla_norm_g[i], with_ctx)

        y_l = jnp.concatenate([ya_l * jax.nn.silu(az_l), yb_l * jax.nn.silu(bz_l),
                               yc_l * jax.nn.silu(cz_l), yd_l * jax.nn.silu(dz_l)], axis=-1)
        xl = xl + gt_l[:, None] * (y_l @ w_out[i])
        if with_ctx:
            yc_c = chunk_sgu(cu_c, cv_c, sg_ln_g[i], sg_ln_b[i], sg_w[i], sg_b[i])
            y_c = jnp.concatenate([ya_c * jax.nn.silu(az_c), yb_c * jax.nn.silu(bz_c),
                                   yc_c * jax.nn.silu(cz_c), yd_c * jax.nn.silu(dz_c)], axis=-1)
            xc = xc + gt_c * (y_c @ w_out[i])
    return rms_norm(xl, final_norm_g)
```

```python
import functools
import math

import jax
import jax.numpy as jnp
from jax import lax
from jax.experimental import pallas as pl
from jax.experimental.pallas import tpu as pltpu

F32 = jnp.float32
BF16 = jnp.bfloat16

D_MODEL = 1024
DEPTH = 4
CTX_LEN = 256
GRID_W = 64
BRANCH_W = 256
ROPE_THETA = 10000.0
EPS = 1e-6
DA_HEADS = 4
DA_QK = 32
DA_V = 64
GQ_HEADS = 4
GQ_KV_HEADS = 2
GQ_HD = 64
SG_GROUPS = 4
SG_CHUNK = 128
GLA_HEADS = 4
GLA_DV = 64
GLA_DK = 32
GLA_RANK = 16
GLA_NORMALIZER = 16.0
GLA_CHUNK = 32

LANES = 128
ROW_TILE = 256
GLA_TILE = 128
VMEM_LIMIT = 56 * 1024 * 1024

C_AQ, C_AK, C_AV, C_AZ = 0, 256, 512, 768
C_BQ, C_BK, C_BV, C_BZ = 1024, 1280, 1408, 1664
C_CU, C_CV, C_CZ = 1920, 2176, 2432
C_DQ, C_DK, C_DV, C_DZ, C_DR = 2688, 2816, 2944, 3200, 3456
P_PACK = 3584


def _silu(x):
    return x * (1.0 / (1.0 + jnp.exp(-x)))


def _group_mean_matrix(width, group):
    r = lax.broadcasted_iota(jnp.int32, (width, width), 0) // group
    c = lax.broadcasted_iota(jnp.int32, (width, width), 1) // group
    return jnp.where(r == c, 1.0 / group, 0.0).astype(BF16)


def _group_mean(x, mat):
    hi = x.astype(BF16)
    lo = (x - hi.astype(F32)).astype(BF16)
    return (jnp.dot(hi, mat, preferred_element_type=F32)
            + jnp.dot(lo, mat, preferred_element_type=F32))


def _rope(x, cos, s_lo, s_hi, shift):
    outs = []
    for c in range(x.shape[1] // LANES):
        sl = slice(c * LANES, (c + 1) * LANES)
        xc = x[:, sl]
        up = pltpu.roll(xc, LANES - shift, axis=1)
        dn = pltpu.roll(xc, shift, axis=1)
        outs.append(xc * cos[:, sl] + up * s_lo[:, sl] + dn * s_hi[:, sl])
    return outs[0] if len(outs) == 1 else jnp.concatenate(outs, axis=1)


def _mod_kernel(c_ref, w_ref, b_ref, o_ref):
    s = _silu(c_ref[...]).astype(BF16)
    o_ref[0] = jnp.dot(s, w_ref[0].astype(BF16), preferred_element_type=F32) + b_ref[0]


def _modulation(cpad, ada_w, ada_b):
    n = cpad.shape[0]
    return pl.pallas_call(
        _mod_kernel,
        grid=(DEPTH, 3),
        in_specs=[
            pl.BlockSpec((n, D_MODEL), lambda i, j: (0, 0)),
            pl.BlockSpec((1, D_MODEL, D_MODEL), lambda i, j: (i, 0, j)),
            pl.BlockSpec((1, 1, D_MODEL), lambda i, j: (i, 0, j)),
        ],
        out_specs=pl.BlockSpec((1, n, D_MODEL), lambda i, j: (i, 0, j)),
        out_shape=jax.ShapeDtypeStruct((DEPTH, n, 3 * D_MODEL), F32),
        compiler_params=pltpu.CompilerParams(vmem_limit_bytes=VMEM_LIMIT),
        name="modulation",
    )(cpad, ada_w, ada_b.reshape(DEPTH, 1, 3 * D_MODEL))


def _in_kernel(x_ref, mod_ref, ng_ref, w_ref, taba_ref, tabb_ref, gqq_ref, gqk_ref,
               lng_ref, lnb_ref, sgw_ref, sgb_ref, w2_ref, b2_ref,
               aq_ref, akt_ref, av_ref, bq_ref, bkt_ref, bv_ref, gate_ref, yc_ref,
               dqk_ref, dv_ref, dg_ref):
    x = x_ref[0]
    shift = mod_ref[0, 0, 0:1, :]
    scale = mod_ref[0, 0, 1:2, :]
    y = x * lax.rsqrt(jnp.mean(x * x, axis=-1, keepdims=True) + EPS) * ng_ref[...]
    hb = (y * (1.0 + scale) + shift).astype(BF16)

    def proj(lo, width):
        return jnp.dot(hb, w_ref[:, lo:lo + width], preferred_element_type=F32)

    m64 = _group_mean_matrix(BRANCH_W, GQ_HD)

    ca, sa_lo, sa_hi = taba_ref[0], taba_ref[1], taba_ref[2]
    aq = _rope(proj(C_AQ, 256), ca, sa_lo, sa_hi, DA_QK // 4) * (DA_QK ** -0.5)
    aq_ref[0] = aq.astype(BF16)
    ak = _rope(proj(C_AK, 256), ca, sa_lo, sa_hi, DA_QK // 4)
    akt_ref[0] = ak.T.astype(BF16)
    av_ref[0] = proj(C_AV, 256).astype(BF16)

    cb, sb_lo, sb_hi = tabb_ref[0], tabb_ref[1], tabb_ref[2]
    bq = proj(C_BQ, 256)
    bq = bq * lax.rsqrt(_group_mean(bq * bq, m64) + EPS) * gqq_ref[...]
    bq_ref[0] = (_rope(bq, cb, sb_lo, sb_hi, GQ_HD // 4) * (GQ_HD ** -0.5)).astype(BF16)
    bk = proj(C_BK, 128)
    bk = bk * lax.rsqrt(_group_mean(bk * bk, m64[:128, :128]) + EPS) * gqk_ref[...]
    bk = _rope(bk, cb[:, :128], sb_lo[:, :128], sb_hi[:, :128], GQ_HD // 4)
    bkt_ref[0] = bk.T.astype(BF16)
    bv_ref[0] = proj(C_BV, 256).astype(BF16)

    gate_ref[0, :, 0:256] = _silu(proj(C_AZ, 256)).astype(BF16)
    gate_ref[0, :, 256:512] = _silu(proj(C_BZ, 256)).astype(BF16)
    gate_ref[0, :, 512:768] = _silu(proj(C_DZ, 256)).astype(BF16)

    cu = proj(C_CU, 256)
    cv = proj(C_CV, 256)
    mu = jnp.mean(cv, axis=-1, keepdims=True)
    cen = cv - mu
    var = jnp.mean(cen * cen, axis=-1, keepdims=True)
    vn = (cen * lax.rsqrt(var + EPS) * lng_ref[...] + lnb_ref[...]).astype(BF16)
    lane_group = lax.broadcasted_iota(jnp.int32, (SG_CHUNK, BRANCH_W), 1) // (BRANCH_W // SG_GROUPS)
    mixed = []
    for n in range(ROW_TILE // SG_CHUNK):
        vchunk = vn[n * SG_CHUNK:(n + 1) * SG_CHUNK, :]
        acc = jnp.zeros((SG_CHUNK, BRANCH_W), F32)
        for g in range(SG_GROUPS):
            acc = jnp.where(lane_group == g,
                            jnp.dot(sgw_ref[g], vchunk, preferred_element_type=F32), acc)
        mixed.append(acc + sgb_ref[...])
    yc = cu * jnp.concatenate(mixed, axis=0) * _silu(proj(C_CZ, 256))
    yc_ref[0] = yc.astype(BF16)

    dqk_ref[0, :, 0:128] = proj(C_DQ, 128) * (GLA_DK ** -0.5)
    dqk_ref[0, :, 128:256] = proj(C_DK, 128)
    dv_ref[0] = proj(C_DV, 256)
    r = proj(C_DR, 128).astype(BF16)
    gl = jnp.dot(r, w2_ref[...], preferred_element_type=F32) + b2_ref[...]
    log_sig = jnp.minimum(gl, 0.0) - jnp.log(1.0 + jnp.exp(-jnp.abs(gl)))
    dg_ref[0] = log_sig * (1.0 / GLA_NORMALIZER)


def _in_proj(xs, modsel, ng, wp, taba, tabb, gqq, gqk, lng, lnb, sgw, sgb, w2, b2):
    b, s, _ = xs.shape
    t = ROW_TILE
    row = lambda width: pl.BlockSpec((1, t, width), lambda i, j: (i, j, 0))
    colT = lambda height: pl.BlockSpec((1, height, t), lambda i, j: (i, 0, j))
    full = lambda a: pl.BlockSpec(a.shape, lambda i, j: (0,) * a.ndim)
    tab = pl.BlockSpec((3, t, 256), lambda i, j: (0, j, 0))
    shp = lambda width, dt: jax.ShapeDtypeStruct((b, s, width), dt)
    return pl.pallas_call(
        _in_kernel,
        grid=(b, s // t),
        in_specs=[
            row(D_MODEL),
            pl.BlockSpec((1, 1, 3, D_MODEL), lambda i, j: (i, jnp.minimum(j, 1), 0, 0)),
            full(ng), full(wp), tab, tab, full(gqq), full(gqk), full(lng), full(lnb),
            full(sgw), full(sgb), full(w2), full(b2),
        ],
        out_specs=[row(256), colT(256), row(256), row(256), colT(128), row(256),
                   row(768), row(256), row(256), row(256), row(256)],
        out_shape=[shp(256, BF16), jax.ShapeDtypeStruct((b, 256, s), BF16), shp(256, BF16),
                   shp(256, BF16), jax.ShapeDtypeStruct((b, 128, s), BF16), shp(256, BF16),
                   shp(768, BF16), shp(256, BF16), shp(256, F32), shp(256, F32), shp(256, F32)],
        compiler_params=pltpu.CompilerParams(
            dimension_semantics=("parallel", "parallel"), vmem_limit_bytes=VMEM_LIMIT),
        name="in_proj",
    )(xs, modsel, ng, wp, taba, tabb, gqq, gqk, lng, lnb, sgw, sgb, w2, b2)


def _softmax_parts(s):
    p = jnp.exp(s - jnp.max(s, axis=-1, keepdims=True))
    return p, jnp.sum(p, axis=-1, keepdims=True)


def _diff_attn_kernel(lam_init, q_ref, kt_ref, v_ref, gate_ref, lam_ref, sg_ref, o_ref):
    j = pl.program_id(1)
    t = q_ref.shape[1]
    lq1, lk1, lq2, lk2 = lam_ref[0:1, :], lam_ref[1:2, :], lam_ref[2:3, :], lam_ref[3:4, :]
    lam = (jnp.exp(jnp.sum(lq1 * lk1, axis=-1, keepdims=True))
           - jnp.exp(jnp.sum(lq2 * lk2, axis=-1, keepdims=True)) + lam_init)
    lane_head = lax.broadcasted_iota(jnp.int32, (t, BRANCH_W), 1) // DA_V

    def body(kv_len):
        q = q_ref[0]
        v = v_ref[0, :kv_len, :]
        acc = jnp.zeros((t, BRANCH_W), F32)
        for h in range(DA_HEADS):
            maps = []
            for m in range(2):
                c0 = (2 * h + m) * DA_QK
                s = jnp.dot(q[:, c0:c0 + DA_QK], kt_ref[0, c0:c0 + DA_QK, :kv_len],
                            preferred_element_type=F32)
                maps.append(_softmax_parts(s))
            (p0, l0), (p1, l1) = maps
            w = p0 * (1.0 / l0) - p1 * (lam / l1)
            acc = jnp.where(lane_head == h,
                            jnp.dot(w.astype(BF16), v, preferred_element_type=F32), acc)
        ms = _group_mean(acc * acc, _group_mean_matrix(BRANCH_W, DA_V))
        y = acc * lax.rsqrt(ms + EPS) * sg_ref[...] * (1.0 - lam_init)
        o_ref[0] = (y * gate_ref[0].astype(F32)).astype(BF16)

    @pl.when(j == 0)
    def _():
        body(CTX_LEN)

    @pl.when(j > 0)
    def _():
        body(kt_ref.shape[2])


def _gqa_attn_kernel(q_ref, kt_ref, v_ref, gate_ref, o_ref):
    j = pl.program_id(1)
    t = q_ref.shape[1]
    lane_head = lax.broadcasted_iota(jnp.int32, (t, BRANCH_W), 1) // GQ_HD
    grp = GQ_HEADS // GQ_KV_HEADS

    def body(kv_len):
        q = q_ref[0]
        v = v_ref[0, :kv_len, :]
        acc = jnp.zeros((t, BRANCH_W), F32)
        for h in range(GQ_HEADS):
            k0 = (h // grp) * GQ_HD
            s = jnp.dot(q[:, h * GQ_HD:(h + 1) * GQ_HD], kt_ref[0, k0:k0 + GQ_HD, :kv_len],
                        preferred_element_type=F32)
            p, l = _softmax_parts(s)
            o = jnp.dot(p.astype(BF16), v, preferred_element_type=F32) * (1.0 / l)
            acc = jnp.where(lane_head == h, o, acc)
        o_ref[0] = (acc * gate_ref[0].astype(F32)).astype(BF16)

    @pl.when(j == 0)
    def _():
        body(CTX_LEN)

    @pl.when(j > 0)
    def _():
        body(kt_ref.shape[2])


def _attention(kernel, q, kt, v, gate, gate_block, extra, name):
    b, s, _ = q.shape
    t = ROW_TILE
    full = lambda a: pl.BlockSpec(a.shape, lambda i, j: (0,) * a.ndim)
    return pl.pallas_call(
        kernel,
        grid=(b, s // t),
        in_specs=[
            pl.BlockSpec((1, t, 256), lambda i, j: (i, j, 0)),
            pl.BlockSpec((1, kt.shape[1], s), lambda i, j: (i, 0, 0)),
            pl.BlockSpec((1, s, 256), lambda i, j: (i, 0, 0)),
            pl.BlockSpec((1, t, 256), lambda i, j: (i, j, gate_block)),
        ] + [full(a) for a in extra],
        out_specs=pl.BlockSpec((1, t, 256), lambda i, j: (i, j, 0)),
        out_shape=jax.ShapeDtypeStruct((b, s, 256), BF16),
        compiler_params=pltpu.CompilerParams(
            dimension_semantics=("parallel", "arbitrary"), vmem_limit_bytes=VMEM_LIMIT),
        name=name,
    )(q, kt, v, gate, *extra)


def _split3(x):
    hi = x.astype(BF16)
    r = x - hi.astype(F32)
    mid = r.astype(BF16)
    lo = (r - mid.astype(F32)).astype(BF16)
    return hi, mid, lo


def _gla_kernel(qk_ref, v_ref, g_ref, gate_ref, ng_ref, o_ref, acc_ref, st_ref):
    n_tiles = qk_ref.shape[1] // GLA_TILE
    ctx_tiles = CTX_LEN // GLA_TILE
    per = GLA_TILE // GLA_CHUNK
    ri = lax.broadcasted_iota(jnp.int32, (GLA_TILE, GLA_TILE), 0)
    ci = lax.broadcasted_iota(jnp.int32, (GLA_TILE, GLA_TILE), 1)
    same_chunk = (ri // GLA_CHUNK) == (ci // GLA_CHUNK)
    head_of_k = lax.broadcasted_iota(jnp.int32, (GLA_TILE, GLA_HEADS * GLA_DK), 1) // GLA_DK
    head_of_v = lax.broadcasted_iota(jnp.int32, (GLA_TILE, BRANCH_W), 1) // GLA_DV
    st_mask = ((lax.broadcasted_iota(jnp.int32, (BRANCH_W, GLA_HEADS * GLA_DK), 0) // GLA_DV)
               == (lax.broadcasted_iota(jnp.int32, (BRANCH_W, GLA_HEADS * GLA_DK), 1) // GLA_DK))
    m64 = _group_mean_matrix(BRANCH_W, GLA_DV)

    def run(reverse):
        tri = jnp.where(same_chunk & ((ci >= ri) if reverse else (ci <= ri)), 1.0, 0.0)
        tri_b = tri.astype(BF16)
        col = 0 if not reverse else 128

        def tile_step(i, carry):
            if reverse:
                tix = jnp.where(i < ctx_tiles, ctx_tiles - 1 - i, n_tiles + ctx_tiles - 1 - i)
            else:
                tix = i
            r0 = pl.multiple_of(tix * GLA_TILE, GLA_TILE)
            rows = pl.ds(r0, GLA_TILE)
            q = qk_ref[0, rows, 0:128]
            k = qk_ref[0, rows, 128:256]
            v = v_ref[0, rows, :]
            g = g_ref[0, rows, col:col + 128]
            vb = v.astype(BF16)
            ghi, gmid, glo = _split3(g)
            cum = (jnp.dot(tri_b, ghi, preferred_element_type=F32)
                   + jnp.dot(tri_b, gmid, preferred_element_type=F32)
                   + jnp.dot(tri_b, glo, preferred_element_type=F32))
            last_rows = []
            for c in range(per):
                e = c * GLA_CHUNK if reverse else (c + 1) * GLA_CHUNK - 1
                last_rows.append(jnp.broadcast_to(cum[e:e + 1, :], (GLA_CHUNK, 128)))
            cum_last = jnp.concatenate(last_rows, axis=0)
            qe = q * jnp.exp(cum)
            kd = (k * jnp.exp(-cum)).astype(BF16)
            kl = (k * jnp.exp(cum_last - cum)).astype(BF16)
            qeb = qe.astype(BF16)
            o = jnp.zeros((GLA_TILE, BRANCH_W), F32)
            for h in range(GLA_HEADS):
                qh = jnp.where(head_of_k == h, qe, 0.0).astype(BF16)
                a = lax.dot_general(qh, kd, (((1,), (1,)), ((), ())), preferred_element_type=F32)
                a = (a * tri).astype(BF16)
                o = jnp.where(head_of_v == h, jnp.dot(a, vb, preferred_element_type=F32), o)
            outs = []
            order = range(per - 1, -1, -1) if reverse else range(per)
            o_inter = [None] * per
            for c in order:
                cs = slice(c * GLA_CHUNK, (c + 1) * GLA_CHUNK)
                st = st_ref[...]
                o_inter[c] = lax.dot_general(qeb[cs, :], st.astype(BF16), (((1,), (1,)), ((), ())),
                                             preferred_element_type=F32)
                e = c * GLA_CHUNK if reverse else (c + 1) * GLA_CHUNK - 1
                decay = jnp.exp(cum[e:e + 1, :])
                upd = lax.dot_general(vb[cs, :], kl[cs, :], (((0,), (0,)), ((), ())),
                                      preferred_element_type=F32)
                st_ref[...] = st * decay + jnp.where(st_mask, upd, 0.0)
            o = o + jnp.concatenate(o_inter, axis=0)
            if not reverse:
                acc_ref[rows, :] = o
            else:
                tot = acc_ref[rows, :] + o
                ms = _group_mean(tot * tot, m64)
                y = tot * lax.rsqrt(ms + EPS) * ng_ref[...]
                o_ref[0, rows, :] = (y * gate_ref[0, rows, :].astype(F32)).astype(BF16)
            return carry

        st_ref[...] = jnp.zeros_like(st_ref)
        lax.fori_loop(0, n_tiles, tile_step, 0)

    run(False)
    run(True)


def _gla(dqk, dv, dg, gate, ng):
    b, s, _ = dqk.shape
    blk = lambda width, cb: pl.BlockSpec((1, s, width), lambda i: (i, 0, cb))
    return pl.pallas_call(
        _gla_kernel,
        grid=(b,),
        in_specs=[blk(256, 0), blk(256, 0), blk(256, 0), blk(256, 2),
                  pl.BlockSpec(ng.shape, lambda i: (0, 0))],
        out_specs=blk(256, 0),
        out_shape=jax.ShapeDtypeStruct((b, s, 256), BF16),
        scratch_shapes=[pltpu.VMEM((s, BRANCH_W), F32),
                        pltpu.VMEM((BRANCH_W, GLA_HEADS * GLA_DK), F32)],
        compiler_params=pltpu.CompilerParams(
            dimension_semantics=("parallel",), vmem_limit_bytes=VMEM_LIMIT),
        name="gla",
    )(dqk, dv, dg, gate, ng)


def _out_kernel(final, x_ref, mod_ref, ya_ref, yb_ref, yc_ref, yd_ref, w_ref, fg_ref, o_ref):
    y = jnp.concatenate([ya_ref[0], yb_ref[0], yc_ref[0], yd_ref[0]], axis=1)
    upd = jnp.dot(y, w_ref[...], preferred_element_type=F32)
    xn = x_ref[0] + mod_ref[0, 0, 2:3, :] * upd
    if final:
        xn = xn * lax.rsqrt(jnp.mean(xn * xn, axis=-1, keepdims=True) + EPS) * fg_ref[...]
    o_ref[0] = xn


def _out_proj(xs, modsel, ya, yb, yc, yd, wo, fg, final):
    b, s, _ = xs.shape
    t = ROW_TILE
    skip = CTX_LEN // t if final else 0
    row = lambda width: pl.BlockSpec((1, t, width), lambda i, j: (i, j + skip, 0))
    return pl.pallas_call(
        functools.partial(_out_kernel, final),
        grid=(b, s // t - skip),
        in_specs=[
            row(D_MODEL),
            pl.BlockSpec((1, 1, 3, D_MODEL), lambda i, j: (i, jnp.minimum(j + skip, 1), 0, 0)),
            row(256), row(256), row(256), row(256),
            pl.BlockSpec(wo.shape, lambda i, j: (0, 0)),
            pl.BlockSpec(fg.shape, lambda i, j: (0, 0)),
        ],
        out_specs=pl.BlockSpec((1, t, D_MODEL), lambda i, j: (i, j, 0)),
        out_shape=jax.ShapeDtypeStruct((b, s - skip * t, D_MODEL), F32),
        compiler_params=pltpu.CompilerParams(
            dimension_semantics=("parallel", "parallel"), vmem_limit_bytes=VMEM_LIMIT),
        name="out_proj_final" if final else "out_proj",
    )(xs, modsel, ya, yb, yc, yd, wo, fg)


def _rope_tables(seq, dim, width):
    half = dim // 2
    quarter = half // 2
    lane = jnp.arange(width) % dim
    freq = ROPE_THETA ** (-(2.0 * (lane % quarter).astype(F32)) / half)
    pos_t = jnp.arange(seq)
    pos = jnp.where(lane[None, :] < half, (pos_t // GRID_W)[:, None], (pos_t % GRID_W)[:, None])
    ang = pos.astype(F32) * freq[None, :]
    cos, sin = jnp.cos(ang), jnp.sin(ang)
    first = (lane % half) < quarter
    s_lo = jnp.where(first[None, :], -sin, 0.0)
    s_hi = jnp.where(first[None, :], 0.0, sin)
    lat = jnp.stack([cos, s_lo, s_hi])
    ctx = jnp.stack([jnp.ones((CTX_LEN, width), F32), jnp.zeros((CTX_LEN, width), F32),
                     jnp.zeros((CTX_LEN, width), F32)])
    return jnp.concatenate([ctx, lat], axis=1)


def _pack_w_in(w):
    offs = [0]
    for n in (256, 256, 256, 256, 256, 128, 128, 256, 256, 256, 256, 128, 128, 256, 256, 16, 16):
        offs.append(offs[-1] + n)
    seg = [w[:, offs[i]:offs[i + 1]] for i in range(17)]
    aq, ak, av, az, bq, bk, bv, bz, cu, cv, cz, dq, dk, dv, dz, drf, drb = seg
    bv_exp = jnp.concatenate([bv[:, :64], bv[:, :64], bv[:, 64:], bv[:, 64:]], axis=1)
    pad = jnp.zeros((w.shape[0], P_PACK - C_DR - 2 * GLA_RANK), w.dtype)
    return jnp.concatenate([aq, ak, av, az, bq, bk, bv_exp, bz, cu, cv, cz, dq, dk, dv, dz,
                            drf, drb, pad], axis=1).astype(BF16)


def kernel(x, c, ctx, c_ctx, ada_w, ada_b, norm_g, w_in, da_lq1, da_lk1, da_lq2, da_lk2,
           da_subln_g, gq_qnorm_g, gq_knorm_g, sg_ln_g, sg_ln_b, sg_w, sg_b,
           gla_w2_f, gla_b_f, gla_w2_b, gla_b_b, gla_norm_g, w_out, final_norm_g):
    b, seq, d = x.shape
    assert (seq, d, ctx.shape[1]) == (seq // ROW_TILE * ROW_TILE, D_MODEL, CTX_LEN)

    n_mod = 32
    cpad = jnp.zeros((n_mod, d), F32).at[:b].set(c).at[b].set(c_ctx)
    mod = _modulation(cpad, ada_w, ada_b)

    taba = _rope_tables(seq, DA_QK, 256)
    tabb = _rope_tables(seq, GQ_HD, 256)
    fg = final_norm_g.reshape(1, d)

    xs = jnp.concatenate([ctx, x], axis=1)
    for i in range(DEPTH):
        final = i == DEPTH - 1
        lam_init = 0.8 - 0.6 * math.exp(-0.3 * i)
        mod_l = mod[i, :b].reshape(b, 1, 3, d)
        mod_c = jnp.broadcast_to(mod[i, b].reshape(1, 1, 3, d), (b, 1, 3, d))
        modsel = jnp.concatenate([mod_c, mod_l], axis=1)
        w2 = jnp.zeros((128, 256), F32)
        w2 = w2.at[0:GLA_RANK, 0:128].set(gla_w2_f[i]).at[GLA_RANK:2 * GLA_RANK, 128:256].set(gla_w2_b[i])
        b2 = jnp.concatenate([gla_b_f[i], gla_b_b[i]]).reshape(1, 256)
        sgb = jnp.repeat(sg_b[i].T, BRANCH_W // SG_GROUPS, axis=1)
        (aq, akt, av, bq, bkt, bv, gate, yc, dqk, dv, dg) = _in_proj(
            xs, modsel, norm_g[i].reshape(1, d), _pack_w_in(w_in[i]), taba, tabb,
            jnp.tile(gq_qnorm_g[i], GQ_HEADS).reshape(1, 256),
            jnp.tile(gq_knorm_g[i], GQ_KV_HEADS).reshape(1, 128),
            sg_ln_g[i].reshape(1, 256), sg_ln_b[i].reshape(1, 256),
            sg_w[i].astype(BF16), sgb, w2.astype(BF16), b2)
        lam_vecs = jnp.stack([da_lq1[i], da_lk1[i], da_lq2[i], da_lk2[i]])
        ya = _attention(functools.partial(_diff_attn_kernel, lam_init), aq, akt, av, gate, 0,
                        [lam_vecs, jnp.tile(da_subln_g[i], DA_HEADS).reshape(1, 256)], "diff_attn")
        yb = _attention(_gqa_attn_kernel, bq, bkt, bv, gate, 1, [], "gqa_attn")
        yd = _gla(dqk, dv, dg, gate, jnp.tile(gla_norm_g[i], GLA_HEADS).reshape(1, 256))
        xs = _out_proj(xs, modsel, ya, yb, yc, yd, w_out[i].astype(BF16), fg, final)
    return xs
```

```python
import functools
import math

import jax
import jax.numpy as jnp
from jax import lax
from jax.experimental import pallas as pl
from jax.experimental.pallas import tpu as pltpu

F32 = jnp.float32
BF16 = jnp.bfloat16

D_MODEL = 1024
DEPTH = 4
CTX_LEN = 256
GRID_W = 64
BRANCH_W = 256
ROPE_THETA = 10000.0
EPS = 1e-6
DA_HEADS = 4
DA_QK = 32
DA_V = 64
GQ_HEADS = 4
GQ_KV_HEADS = 2
GQ_HD = 64
SG_GROUPS = 4
SG_CHUNK = 128
GLA_HEADS = 4
GLA_DV = 64
GLA_DK = 32
GLA_RANK = 16
GLA_NORMALIZER = 16.0
GLA_CHUNK = 32

LANES = 128
ROW_TILE = 256
GLA_TILE = 128
GLA_BATCH = 2
VMEM_LIMIT = 56 * 1024 * 1024

C_AQ, C_AK, C_AV, C_AZ = 0, 256, 512, 1024
C_BQ, C_BK, C_BV, C_BZ = 1280, 1536, 1664, 1920
C_CU, C_CV, C_CZ = 2176, 2432, 2688
C_DQ, C_DK, C_DV, C_DZ, C_DR = 2944, 3072, 3200, 3456, 3712
P_PACK = 3840
LOG2E = math.log2(math.e)


def _silu(x):
    return x * (1.0 / (1.0 + jnp.exp(-x)))


def _group_mean_matrix(width, group):
    r = lax.broadcasted_iota(jnp.int32, (width, width), 0) // group
    c = lax.broadcasted_iota(jnp.int32, (width, width), 1) // group
    return jnp.where(r == c, 1.0 / group, 0.0).astype(BF16)


def _group_mean(x, mat):
    hi = x.astype(BF16)
    lo = (x - hi.astype(F32)).astype(BF16)
    return (jnp.dot(hi, mat, preferred_element_type=F32)
            + jnp.dot(lo, mat, preferred_element_type=F32))


def _rope(x, cos, s_lo, s_hi, shift):
    outs = []
    for c in range(x.shape[1] // LANES):
        sl = slice(c * LANES, (c + 1) * LANES)
        xc = x[:, sl]
        up = pltpu.roll(xc, LANES - shift, axis=1)
        dn = pltpu.roll(xc, shift, axis=1)
        outs.append(xc * cos[:, sl] + up * s_lo[:, sl] + dn * s_hi[:, sl])
    return outs[0] if len(outs) == 1 else jnp.concatenate(outs, axis=1)


def _with_ones(v):
    lane = lax.broadcasted_iota(jnp.int32, v.shape, 1) % LANES
    return jnp.where(lane < LANES // 2, v, 1.0)


def _normalize_by_sum_lanes(o):
    lane = lax.broadcasted_iota(jnp.int32, o.shape, 1)
    return jnp.where(lane < LANES // 2, o * (1.0 / pltpu.roll(o, LANES // 2, axis=1)), 0.0)


def _mod_kernel(c_ref, w_ref, b_ref, o_ref):
    s = _silu(c_ref[...]).astype(BF16)
    o_ref[0] = jnp.dot(s, w_ref[0].astype(BF16), preferred_element_type=F32) + b_ref[0]


def _modulation(cpad, ada_w, ada_b):
    n = cpad.shape[0]
    return pl.pallas_call(
        _mod_kernel,
        grid=(DEPTH, 3),
        in_specs=[
            pl.BlockSpec((n, D_MODEL), lambda i, j: (0, 0)),
            pl.BlockSpec((1, D_MODEL, D_MODEL), lambda i, j: (i, 0, j)),
            pl.BlockSpec((1, 1, D_MODEL), lambda i, j: (i, 0, j)),
        ],
        out_specs=pl.BlockSpec((1, n, D_MODEL), lambda i, j: (i, 0, j)),
        out_shape=jax.ShapeDtypeStruct((DEPTH, n, 3 * D_MODEL), F32),
        compiler_params=pltpu.CompilerParams(vmem_limit_bytes=VMEM_LIMIT),
        name="modulation",
    )(cpad, ada_w, ada_b.reshape(DEPTH, 1, 3 * D_MODEL))


def _in_kernel(x_ref, mod_ref, ng_ref, w_ref, taba_ref, tabb_ref, gqq_ref, gqk_ref,
               lng_ref, lnb_ref, sgw_ref, sgb_ref, w2_ref, b2_ref,
               aq_ref, akt_ref, av_ref, bq_ref, bkt_ref, bv_ref, gate_ref, yc_ref,
               dqk_ref, dv_ref, dg_ref):
    x = x_ref[0]
    shift = mod_ref[0, 0, 0:1, :]
    scale = mod_ref[0, 0, 1:2, :]
    y = x * lax.rsqrt(jnp.mean(x * x, axis=-1, keepdims=True) + EPS) * ng_ref[...]
    hb = (y * (1.0 + scale) + shift).astype(BF16)

    def proj(lo, width):
        return jnp.dot(hb, w_ref[:, lo:lo + width], preferred_element_type=F32)

    m64 = _group_mean_matrix(BRANCH_W, GQ_HD)

    ca, sa_lo, sa_hi = taba_ref[0], taba_ref[1], taba_ref[2]
    aq = _rope(proj(C_AQ, 256), ca, sa_lo, sa_hi, DA_QK // 4) * (DA_QK ** -0.5 * LOG2E)
    aq_ref[0] = aq.astype(BF16)
    ak = _rope(proj(C_AK, 256), ca, sa_lo, sa_hi, DA_QK // 4)
    akt_ref[0] = ak.T.astype(BF16)
    av_ref[0] = _with_ones(proj(C_AV, 512)).astype(BF16)

    cb, sb_lo, sb_hi = tabb_ref[0], tabb_ref[1], tabb_ref[2]
    bq = proj(C_BQ, 256)
    bq = bq * lax.rsqrt(_group_mean(bq * bq, m64) + EPS) * gqq_ref[...]
    bq_ref[0] = (_rope(bq, cb, sb_lo, sb_hi, GQ_HD // 4) * (GQ_HD ** -0.5 * LOG2E)).astype(BF16)
    bk = proj(C_BK, 128)
    bk = bk * lax.rsqrt(_group_mean(bk * bk, m64[:128, :128]) + EPS) * gqk_ref[...]
    bk = _rope(bk, cb[:, :128], sb_lo[:, :128], sb_hi[:, :128], GQ_HD // 4)
    bkt_ref[0] = bk.T.astype(BF16)
    bv_ref[0] = _with_ones(proj(C_BV, 256)).astype(BF16)

    gate_ref[0, :, 0:256] = _silu(proj(C_AZ, 256)).astype(BF16)
    gate_ref[0, :, 256:512] = _silu(proj(C_BZ, 256)).astype(BF16)
    gate_ref[0, :, 512:768] = _silu(proj(C_DZ, 256)).astype(BF16)

    cu = proj(C_CU, 256)
    cv = proj(C_CV, 256)
    mu = jnp.mean(cv, axis=-1, keepdims=True)
    cen = cv - mu
    var = jnp.mean(cen * cen, axis=-1, keepdims=True)
    vn = (cen * lax.rsqrt(var + EPS) * lng_ref[...] + lnb_ref[...]).astype(BF16)
    lane_group = lax.broadcasted_iota(jnp.int32, (SG_CHUNK, BRANCH_W), 1) // (BRANCH_W // SG_GROUPS)
    mixed = []
    for n in range(ROW_TILE // SG_CHUNK):
        vchunk = vn[n * SG_CHUNK:(n + 1) * SG_CHUNK, :]
        acc = jnp.zeros((SG_CHUNK, BRANCH_W), F32)
        for g in range(SG_GROUPS):
            acc = jnp.where(lane_group == g,
                            jnp.dot(sgw_ref[g], vchunk, preferred_element_type=F32), acc)
        mixed.append(acc + sgb_ref[...])
    yc = cu * jnp.concatenate(mixed, axis=0) * _silu(proj(C_CZ, 256))
    yc_ref[0] = yc.astype(BF16)

    dqk_ref[0, :, 0:128] = proj(C_DQ, 128) * (GLA_DK ** -0.5)
    dqk_ref[0, :, 128:256] = proj(C_DK, 128)
    dv_ref[0] = proj(C_DV, 256)
    r = proj(C_DR, 128).astype(BF16)
    gl = jnp.dot(r, w2_ref[...], preferred_element_type=F32) + b2_ref[...]
    log_sig = jnp.minimum(gl, 0.0) - jnp.log(1.0 + jnp.exp(-jnp.abs(gl)))
    dg_ref[0] = log_sig * (1.0 / GLA_NORMALIZER)


def _in_proj(xs, modsel, ng, wp, taba, tabb, gqq, gqk, lng, lnb, sgw, sgb, w2, b2):
    b, s, _ = xs.shape
    t = ROW_TILE
    row = lambda width: pl.BlockSpec((1, t, width), lambda i, j: (i, j, 0))
    colT = lambda height: pl.BlockSpec((1, height, t), lambda i, j: (i, 0, j))
    full = lambda a: pl.BlockSpec(a.shape, lambda i, j: (0,) * a.ndim)
    tab = pl.BlockSpec((3, t, 256), lambda i, j: (0, j, 0))
    shp = lambda width, dt: jax.ShapeDtypeStruct((b, s, width), dt)
    return pl.pallas_call(
        _in_kernel,
        grid=(b, s // t),
        in_specs=[
            row(D_MODEL),
            pl.BlockSpec((1, 1, 3, D_MODEL), lambda i, j: (i, jnp.minimum(j, 1), 0, 0)),
            full(ng), full(wp), tab, tab, full(gqq), full(gqk), full(lng), full(lnb),
            full(sgw), full(sgb), full(w2), full(b2),
        ],
        out_specs=[row(256), colT(256), row(512), row(256), colT(128), row(256),
                   row(768), row(256), row(256), row(256), row(256)],
        out_shape=[shp(256, BF16), jax.ShapeDtypeStruct((b, 256, s), BF16), shp(512, BF16),
                   shp(256, BF16), jax.ShapeDtypeStruct((b, 128, s), BF16), shp(256, BF16),
                   shp(768, BF16), shp(256, BF16), shp(256, F32), shp(256, F32), shp(256, F32)],
        compiler_params=pltpu.CompilerParams(
            dimension_semantics=("parallel", "parallel"), vmem_limit_bytes=VMEM_LIMIT),
        name="in_proj",
    )(xs, modsel, ng, wp, taba, tabb, gqq, gqk, lng, lnb, sgw, sgb, w2, b2)


def _softmax_pv(n_jobs, scores_fn, values_fn):
    def scores(job):
        s = scores_fn(job)
        return s, jnp.max(s, axis=-1, keepdims=True)

    outs = []
    cur = scores(0)
    for job in range(n_jobs):
        nxt = scores(job + 1) if job + 1 < n_jobs else None
        s, mx = cur
        p = jnp.exp2(s - mx).astype(BF16)
        outs.append(jnp.dot(p, values_fn(job), preferred_element_type=F32))
        cur = nxt
    return outs


def _pair_heads(y):
    return jnp.concatenate([y[0] + pltpu.roll(y[1], LANES // 2, axis=1),
                            y[2] + pltpu.roll(y[3], LANES // 2, axis=1)], axis=1)


def _diff_attn_kernel(lam_init, q_ref, kt_ref, v_ref, gate_ref, lam_ref, sg_ref, o_ref):
    j = pl.program_id(1)
    lq1, lk1, lq2, lk2 = lam_ref[0:1, :], lam_ref[1:2, :], lam_ref[2:3, :], lam_ref[3:4, :]
    lam = (jnp.exp(jnp.sum(lq1 * lk1, axis=-1, keepdims=True))
           - jnp.exp(jnp.sum(lq2 * lk2, axis=-1, keepdims=True)) + lam_init)

    def body(kv_len):
        q = q_ref[0]

        def scores(job):
            c0 = job * DA_QK
            return jnp.dot(q[:, c0:c0 + DA_QK], kt_ref[0, c0:c0 + DA_QK, :kv_len],
                           preferred_element_type=F32)

        def values(job):
            h = job // 2
            return v_ref[0, :kv_len, h * LANES:(h + 1) * LANES]

        o = _softmax_pv(2 * DA_HEADS, scores, values)
        y = _pair_heads([_normalize_by_sum_lanes(o[2 * h]) - lam * _normalize_by_sum_lanes(o[2 * h + 1])
                         for h in range(DA_HEADS)])
        ms = _group_mean(y * y, _group_mean_matrix(BRANCH_W, DA_V))
        y = y * lax.rsqrt(ms + EPS) * sg_ref[...] * (1.0 - lam_init)
        o_ref[0] = (y * gate_ref[0].astype(F32)).astype(BF16)

    @pl.when(j == 0)
    def _():
        body(CTX_LEN)

    @pl.when(j > 0)
    def _():
        body(kt_ref.shape[2])


def _gqa_attn_kernel(q_ref, kt_ref, v_ref, gate_ref, o_ref):
    j = pl.program_id(1)
    grp = GQ_HEADS // GQ_KV_HEADS

    def body(kv_len):
        q = q_ref[0]

        def scores(h):
            k0 = (h // grp) * GQ_HD
            return jnp.dot(q[:, h * GQ_HD:(h + 1) * GQ_HD], kt_ref[0, k0:k0 + GQ_HD, :kv_len],
                           preferred_element_type=F32)

        def values(h):
            kv = h // grp
            return v_ref[0, :kv_len, kv * LANES:(kv + 1) * LANES]

        o = _softmax_pv(GQ_HEADS, scores, values)
        y = _pair_heads([_normalize_by_sum_lanes(oh) for oh in o])
        o_ref[0] = (y * gate_ref[0].astype(F32)).astype(BF16)

    @pl.when(j == 0)
    def _():
        body(CTX_LEN)

    @pl.when(j > 0)
    def _():
        body(kt_ref.shape[2])


def _attention(kernel, q, kt, v, gate, gate_block, extra, name):
    b, s, _ = q.shape
    t = ROW_TILE
    full = lambda a: pl.BlockSpec(a.shape, lambda i, j: (0,) * a.ndim)
    return pl.pallas_call(
        kernel,
        grid=(b, s // t),
        in_specs=[
            pl.BlockSpec((1, t, 256), lambda i, j: (i, j, 0)),
            pl.BlockSpec((1, kt.shape[1], s), lambda i, j: (i, 0, 0)),
            pl.BlockSpec((1, s, v.shape[2]), lambda i, j: (i, 0, 0)),
            pl.BlockSpec((1, t, 256), lambda i, j: (i, j, gate_block)),
        ] + [full(a) for a in extra],
        out_specs=pl.BlockSpec((1, t, 256), lambda i, j: (i, j, 0)),
        out_shape=jax.ShapeDtypeStruct((b, s, 256), BF16),
        compiler_params=pltpu.CompilerParams(
            dimension_semantics=("parallel", "arbitrary"), vmem_limit_bytes=VMEM_LIMIT),
        name=name,
    )(q, kt, v, gate, *extra)


def _split3(x):
    hi = x.astype(BF16)
    r = x - hi.astype(F32)
    mid = r.astype(BF16)
    lo = (r - mid.astype(F32)).astype(BF16)
    return hi, mid, lo


def _gla_kernel(qk_ref, v_ref, g_ref, gate_ref, ng_ref, o_ref, acc_ref, st_ref):
    n_tiles = qk_ref.shape[1] // GLA_TILE
    ctx_tiles = CTX_LEN // GLA_TILE
    per = GLA_TILE // GLA_CHUNK
    ri = lax.broadcasted_iota(jnp.int32, (GLA_TILE, GLA_TILE), 0)
    ci = lax.broadcasted_iota(jnp.int32, (GLA_TILE, GLA_TILE), 1)
    same_chunk = (ri // GLA_CHUNK) == (ci // GLA_CHUNK)
    head_of_k = lax.broadcasted_iota(jnp.int32, (GLA_TILE, GLA_HEADS * GLA_DK), 1) // GLA_DK
    head_of_v = lax.broadcasted_iota(jnp.int32, (GLA_TILE, BRANCH_W), 1) // GLA_DV
    st_mask = ((lax.broadcasted_iota(jnp.int32, (BRANCH_W, GLA_HEADS * GLA_DK), 0) // GLA_DV)
               == (lax.broadcasted_iota(jnp.int32, (BRANCH_W, GLA_HEADS * GLA_DK), 1) // GLA_DK))
    m64 = _group_mean_matrix(BRANCH_W, GLA_DV)
    tris = [jnp.where(same_chunk & (ci <= ri), 1.0, 0.0), jnp.where(same_chunk & (ci >= ri), 1.0, 0.0)]
    tris_b = [t.astype(BF16) for t in tris]

    nt_dims = (((1,), (1,)), ((), ()))
    tn_dims = (((0,), (0,)), ((), ()))

    def scan_step(i, carry):
        rev = jnp.where(i < ctx_tiles, ctx_tiles - 1 - i, n_tiles + ctx_tiles - 1 - i)
        chains = [(bi, d) for bi in range(GLA_BATCH) for d in (0, 1)]
        nc = len(chains)
        rows = [pl.ds(pl.multiple_of((rev if d else i) * GLA_TILE, GLA_TILE), GLA_TILE)
                for _, d in chains]
        ends = [[c * GLA_CHUNK if d else (c + 1) * GLA_CHUNK - 1 for c in range(per)] for _, d in chains]

        cum = []
        for n, (bi, d) in enumerate(chains):
            parts = _split3(g_ref[bi, rows[n], 128 * d:128 * d + 128])
            cum.append(sum(jnp.dot(tris_b[d], p, preferred_element_type=F32) for p in parts))

        qe, kd, kl, vb = [], [], [], []
        for n, (bi, d) in enumerate(chains):
            q = qk_ref[bi, rows[n], 0:128]
            k = qk_ref[bi, rows[n], 128:256]
            cum_last = jnp.concatenate(
                [jnp.broadcast_to(cum[n][e:e + 1, :], (GLA_CHUNK, 128)) for e in ends[n]], axis=0)
            qe.append(q * jnp.exp(cum[n]))
            kd.append((k * jnp.exp(-cum[n])).astype(BF16))
            kl.append((k * jnp.exp(cum_last - cum[n])).astype(BF16))
            vb.append(v_ref[bi, rows[n], :].astype(BF16))

        att = [[None] * GLA_HEADS for _ in range(nc)]
        for h in range(GLA_HEADS):
            for n, (bi, d) in enumerate(chains):
                qh = jnp.where(head_of_k == h, qe[n], 0.0).astype(BF16)
                a = lax.dot_general(qh, kd[n], nt_dims, preferred_element_type=F32)
                att[n][h] = (a * tris[d]).astype(BF16)
        o = [jnp.zeros((GLA_TILE, BRANCH_W), F32) for _ in range(nc)]
        for h in range(GLA_HEADS):
            for n in range(nc):
                o[n] = jnp.where(head_of_v == h,
                                 jnp.dot(att[n][h], vb[n], preferred_element_type=F32), o[n])

        qeb = [x.astype(BF16) for x in qe]
        o_inter = [[None] * per for _ in range(nc)]
        for step in range(per):
            for n, (bi, d) in enumerate(chains):
                c = per - 1 - step if d else step
                cs = slice(c * GLA_CHUNK, (c + 1) * GLA_CHUNK)
                st = st_ref[n]
                o_inter[n][c] = lax.dot_general(qeb[n][cs, :], st.astype(BF16), nt_dims,
                                                preferred_element_type=F32)
                decay = jnp.exp(cum[n][ends[n][c]:ends[n][c] + 1, :])
                upd = lax.dot_general(vb[n][cs, :], kl[n][cs, :], tn_dims,
                                      preferred_element_type=F32)
                st_ref[n] = st * decay + jnp.where(st_mask, upd, 0.0)
        for n in range(nc):
            acc_ref[n, rows[n], :] = o[n] + jnp.concatenate(o_inter[n], axis=0)
        return carry

    st_ref[...] = jnp.zeros_like(st_ref)
    lax.fori_loop(0, n_tiles, scan_step, 0)

    def finish_step(i, carry):
        rows = pl.ds(pl.multiple_of(i * GLA_TILE, GLA_TILE), GLA_TILE)
        for bi in range(GLA_BATCH):
            tot = acc_ref[2 * bi, rows, :] + acc_ref[2 * bi + 1, rows, :]
            ms = _group_mean(tot * tot, m64)
            y = tot * lax.rsqrt(ms + EPS) * ng_ref[...]
            o_ref[bi, rows, :] = (y * gate_ref[bi, rows, :].astype(F32)).astype(BF16)
        return carry

    lax.fori_loop(0, n_tiles, finish_step, 0)


def _gla(dqk, dv, dg, gate, ng):
    b, s, _ = dqk.shape
    nb = GLA_BATCH
    blk = lambda width, cb: pl.BlockSpec((nb, s, width), lambda i: (i, 0, cb))
    return pl.pallas_call(
        _gla_kernel,
        grid=(b // nb,),
        in_specs=[blk(256, 0), blk(256, 0), blk(256, 0), blk(256, 2),
                  pl.BlockSpec(ng.shape, lambda i: (0, 0))],
        out_specs=blk(256, 0),
        out_shape=jax.ShapeDtypeStruct((b, s, 256), BF16),
        scratch_shapes=[pltpu.VMEM((2 * nb, s, BRANCH_W), F32),
                        pltpu.VMEM((2 * nb, BRANCH_W, GLA_HEADS * GLA_DK), F32)],
        compiler_params=pltpu.CompilerParams(
            dimension_semantics=("parallel",), vmem_limit_bytes=VMEM_LIMIT),
        name="gla",
    )(dqk, dv, dg, gate, ng)


def _out_kernel(final, x_ref, mod_ref, ya_ref, yb_ref, yc_ref, yd_ref, w_ref, fg_ref, o_ref):
    y = jnp.concatenate([ya_ref[0], yb_ref[0], yc_ref[0], yd_ref[0]], axis=1)
    upd = jnp.dot(y, w_ref[...], preferred_element_type=F32)
    xn = x_ref[0] + mod_ref[0, 0, 2:3, :] * upd
    if final:
        xn = xn * lax.rsqrt(jnp.mean(xn * xn, axis=-1, keepdims=True) + EPS) * fg_ref[...]
    o_ref[0] = xn


def _out_proj(xs, modsel, ya, yb, yc, yd, wo, fg, final):
    b, s, _ = xs.shape
    t = ROW_TILE
    skip = CTX_LEN // t if final else 0
    row = lambda width: pl.BlockSpec((1, t, width), lambda i, j: (i, j + skip, 0))
    return pl.pallas_call(
        functools.partial(_out_kernel, final),
        grid=(b, s // t - skip),
        in_specs=[
            row(D_MODEL),
            pl.BlockSpec((1, 1, 3, D_MODEL), lambda i, j: (i, jnp.minimum(j + skip, 1), 0, 0)),
            row(256), row(256), row(256), row(256),
            pl.BlockSpec(wo.shape, lambda i, j: (0, 0)),
            pl.BlockSpec(fg.shape, lambda i, j: (0, 0)),
        ],
        out_specs=pl.BlockSpec((1, t, D_MODEL), lambda i, j: (i, j, 0)),
        out_shape=jax.ShapeDtypeStruct((b, s - skip * t, D_MODEL), F32),
        compiler_params=pltpu.CompilerParams(
            dimension_semantics=("parallel", "parallel"), vmem_limit_bytes=VMEM_LIMIT),
        name="out_proj_final" if final else "out_proj",
    )(xs, modsel, ya, yb, yc, yd, wo, fg)


def _rope_tables(seq, dim, width):
    half = dim // 2
    quarter = half // 2
    lane = jnp.arange(width) % dim
    freq = ROPE_THETA ** (-(2.0 * (lane % quarter).astype(F32)) / half)
    pos_t = jnp.arange(seq)
    pos = jnp.where(lane[None, :] < half, (pos_t // GRID_W)[:, None], (pos_t % GRID_W)[:, None])
    ang = pos.astype(F32) * freq[None, :]
    cos, sin = jnp.cos(ang), jnp.sin(ang)
    first = (lane % half) < quarter
    s_lo = jnp.where(first[None, :], -sin, 0.0)
    s_hi = jnp.where(first[None, :], 0.0, sin)
    lat = jnp.stack([cos, s_lo, s_hi])
    ctx = jnp.stack([jnp.ones((CTX_LEN, width), F32), jnp.zeros((CTX_LEN, width), F32),
                     jnp.zeros((CTX_LEN, width), F32)])
    return jnp.concatenate([ctx, lat], axis=1)


def _pack_w_in(w):
    offs = [0]
    for n in (256, 256, 256, 256, 256, 128, 128, 256, 256, 256, 256, 128, 128, 256, 256, 16, 16):
        offs.append(offs[-1] + n)
    seg = [w[:, offs[i]:offs[i + 1]] for i in range(17)]
    aq, ak, av, az, bq, bk, bv, bz, cu, cv, cz, dq, dk, dv, dz, drf, drb = seg
    gap = jnp.zeros((w.shape[0], LANES // 2), w.dtype)
    av_sp = jnp.concatenate([p for h in range(DA_HEADS) for p in (av[:, h * 64:(h + 1) * 64], gap)], axis=1)
    bv_sp = jnp.concatenate([bv[:, :64], gap, bv[:, 64:], gap], axis=1)
    pad = jnp.zeros((w.shape[0], P_PACK - C_DR - 2 * GLA_RANK), w.dtype)
    return jnp.concatenate([aq, ak, av_sp, az, bq, bk, bv_sp, bz, cu, cv, cz, dq, dk, dv, dz,
                            drf, drb, pad], axis=1).astype(BF16)


def kernel(x, c, ctx, c_ctx, ada_w, ada_b, norm_g, w_in, da_lq1, da_lk1, da_lq2, da_lk2,
           da_subln_g, gq_qnorm_g, gq_knorm_g, sg_ln_g, sg_ln_b, sg_w, sg_b,
           gla_w2_f, gla_b_f, gla_w2_b, gla_b_b, gla_norm_g, w_out, final_norm_g):
    b, seq, d = x.shape
    assert (seq, d, ctx.shape[1]) == (seq // ROW_TILE * ROW_TILE, D_MODEL, CTX_LEN)

    n_mod = 32
    cpad = jnp.zeros((n_mod, d), F32).at[:b].set(c).at[b].set(c_ctx)
    mod = _modulation(cpad, ada_w, ada_b)

    taba = _rope_tables(seq, DA_QK, 256)
    tabb = _rope_tables(seq, GQ_HD, 256)
    fg = final_norm_g.reshape(1, d)

    xs = jnp.concatenate([ctx, x], axis=1)
    for i in range(DEPTH):
        final = i == DEPTH - 1
        lam_init = 0.8 - 0.6 * math.exp(-0.3 * i)
        mod_l = mod[i, :b].reshape(b, 1, 3, d)
        mod_c = jnp.broadcast_to(mod[i, b].reshape(1, 1, 3, d), (b, 1, 3, d))
        modsel = jnp.concatenate([mod_c, mod_l], axis=1)
        w2 = jnp.zeros((128, 256), F32)
        w2 = w2.at[0:GLA_RANK, 0:128].set(gla_w2_f[i]).at[GLA_RANK:2 * GLA_RANK, 128:256].set(gla_w2_b[i])
        b2 = jnp.concatenate([gla_b_f[i], gla_b_b[i]]).reshape(1, 256)
        sgb = jnp.repeat(sg_b[i].T, BRANCH_W // SG_GROUPS, axis=1)
        (aq, akt, av, bq, bkt, bv, gate, yc, dqk, dv, dg) = _in_proj(
            xs, modsel, norm_g[i].reshape(1, d), _pack_w_in(w_in[i]), taba, tabb,
            jnp.tile(gq_qnorm_g[i], GQ_HEADS).reshape(1, 256),
            jnp.tile(gq_knorm_g[i], GQ_KV_HEADS).reshape(1, 128),
            sg_ln_g[i].reshape(1, 256), sg_ln_b[i].reshape(1, 256),
            sg_w[i].astype(BF16), sgb, w2.astype(BF16), b2)
        lam_vecs = jnp.stack([da_lq1[i], da_lk1[i], da_lq2[i], da_lk2[i]])
        ya = _attention(functools.partial(_diff_attn_kernel, lam_init), aq, akt, av, gate, 0,
                        [lam_vecs, jnp.tile(da_subln_g[i], DA_HEADS).reshape(1, 256)], "diff_attn")
        yb = _attention(_gqa_attn_kernel, bq, bkt, bv, gate, 1, [], "gqa_attn")
        yd = _gla(dqk, dv, dg, gate, jnp.tile(gla_norm_g[i], GLA_HEADS).reshape(1, 256))
        xs = _out_proj(xs, modsel, ya, yb, yc, yd, w_out[i].astype(BF16), fg, final)
    return xs
```

```python
import functools
import math

import jax
import jax.numpy as jnp
from jax import lax
from jax.experimental import pallas as pl
from jax.experimental.pallas import tpu as pltpu

F32 = jnp.float32
BF16 = jnp.bfloat16

D_MODEL = 1024
DEPTH = 4
CTX_LEN = 256
GRID_W = 64
BRANCH_W = 256
ROPE_THETA = 10000.0
EPS = 1e-6
DA_HEADS = 4
DA_QK = 32
DA_V = 64
GQ_HEADS = 4
GQ_KV_HEADS = 2
GQ_HD = 64
SG_GROUPS = 4
SG_CHUNK = 128
GLA_HEADS = 4
GLA_DV = 64
GLA_DK = 32
GLA_RANK = 16
GLA_NORMALIZER = 16.0
GLA_CHUNK = 32

LANES = 128
ROW_TILE = 256
GLA_TILE = 128
GLA_BATCH = 2
VMEM_LIMIT = 56 * 1024 * 1024

C_AQ, C_AK, C_AV, C_AZ = 0, 256, 512, 1024
C_BQ, C_BVK, C_BZ = 1280, 1536, 1792
C_CU, C_CV, C_CZ = 2048, 2304, 2560
C_DQK, C_DV, C_DZ = 2816, 3072, 3328
P_PACK = 3584
R_LANE = 64
LOG2E = math.log2(math.e)


def _silu(x):
    return x * (1.0 / (1.0 + jnp.exp(-x)))


def _group_mean_matrix(width, group):
    r = lax.broadcasted_iota(jnp.int32, (width, width), 0) // group
    c = lax.broadcasted_iota(jnp.int32, (width, width), 1) // group
    return jnp.where(r == c, 1.0 / group, 0.0).astype(BF16)


def _split2(x):
    hi = x.astype(BF16)
    return hi, (x - hi.astype(F32)).astype(BF16)


def _group_mean(x, mat):
    return sum(jnp.dot(p, mat, preferred_element_type=F32) for p in _split2(x))


def _rope(x, cos, s_lo, s_hi, shift):
    outs = []
    for c in range(x.shape[1] // LANES):
        sl = slice(c * LANES, (c + 1) * LANES)
        xc = x[:, sl]
        up = pltpu.roll(xc, LANES - shift, axis=1)
        dn = pltpu.roll(xc, shift, axis=1)
        outs.append(xc * cos[:, sl] + up * s_lo[:, sl] + dn * s_hi[:, sl])
    return outs[0] if len(outs) == 1 else jnp.concatenate(outs, axis=1)


def _with_ones(v):
    lane = lax.broadcasted_iota(jnp.int32, v.shape, 1) % LANES
    return jnp.where(lane < LANES // 2, v, 1.0)


def _normalize_by_sum_lanes(o):
    lane = lax.broadcasted_iota(jnp.int32, o.shape, 1)
    return jnp.where(lane < LANES // 2, o * (1.0 / pltpu.roll(o, LANES // 2, axis=1)), 0.0)


def _mod_kernel(c_ref, w_ref, b_ref, o_ref):
    s = _silu(c_ref[...]).astype(BF16)
    o_ref[0] = jnp.dot(s, w_ref[0].astype(BF16), preferred_element_type=F32) + b_ref[0]


def _modulation(cpad, ada_w, ada_b):
    n = cpad.shape[0]
    return pl.pallas_call(
        _mod_kernel,
        grid=(DEPTH, 3),
        in_specs=[
            pl.BlockSpec((n, D_MODEL), lambda i, j: (0, 0)),
            pl.BlockSpec((1, D_MODEL, D_MODEL), lambda i, j: (i, 0, j)),
            pl.BlockSpec((1, 1, D_MODEL), lambda i, j: (i, 0, j)),
        ],
        out_specs=pl.BlockSpec((1, n, D_MODEL), lambda i, j: (i, 0, j)),
        out_shape=jax.ShapeDtypeStruct((DEPTH, n, 3 * D_MODEL), F32),
        compiler_params=pltpu.CompilerParams(vmem_limit_bytes=VMEM_LIMIT),
        name="modulation",
    )(cpad, ada_w, ada_b.reshape(DEPTH, 1, 3 * D_MODEL))


def _layer_kernel(first, *refs):
    if first:
        ctx_ref, x_ref = refs[:2]
        refs = refs[2:]
    else:
        xs_ref, modp_ref, ya_ref, yb_ref, yc_in_ref, yd_ref, wo_ref = refs[:7]
        refs = refs[7:]
    (mod_ref, ng_ref, w_ref, taba_ref, tabb_ref, gqq_ref, gqk_ref, lng_ref, lnb_ref, sgw_ref,
     sgb_ref, w2_ref, b2_ref,
     xs_out_ref, aq_ref, akt_ref, av_ref, bq_ref, bkt_ref, bv_ref, gate_ref, yc_ref,
     dqk_ref, dv_ref, dg_ref) = refs

    if first:
        x = jnp.where(pl.program_id(1) == 0, ctx_ref[0], x_ref[0])
    else:
        y_prev = jnp.concatenate([ya_ref[0], yb_ref[0], yc_in_ref[0], yd_ref[0]], axis=1)
        x = xs_ref[0] + modp_ref[0, 0, 2:3, :] * jnp.dot(y_prev, wo_ref[...],
                                                         preferred_element_type=F32)
    xs_out_ref[0] = x
    shift = mod_ref[0, 0, 0:1, :]
    scale = mod_ref[0, 0, 1:2, :]
    y = x * lax.rsqrt(jnp.mean(x * x, axis=-1, keepdims=True) + EPS) * ng_ref[...]
    hb = (y * (1.0 + scale) + shift).astype(BF16)

    def proj(lo, width):
        return jnp.dot(hb, w_ref[:, lo:lo + width], preferred_element_type=F32)

    m64 = _group_mean_matrix(BRANCH_W, GQ_HD)
    half_lane = lax.broadcasted_iota(jnp.int32, (ROW_TILE, LANES), 1) < LANES // 2
    p_bq = proj(C_BQ, 256)
    p_bvk = proj(C_BVK, 256)
    p_cv = proj(C_CV, 256)
    p_av = proj(C_AV, 512)

    bq_sq = _split2(p_bq * p_bq)
    bk = jnp.where(half_lane, pltpu.roll(p_bvk[:, :LANES], LANES // 2, axis=1), p_bvk[:, LANES:])
    bk_sq = _split2(bk * bk)
    mu = jnp.mean(p_cv, axis=-1, keepdims=True)
    cen = p_cv - mu
    var = jnp.mean(cen * cen, axis=-1, keepdims=True)
    vn = (cen * lax.rsqrt(var + EPS) * lng_ref[...] + lnb_ref[...]).astype(BF16)
    r = p_av[:, :LANES].astype(BF16)
    av_ref[0] = _with_ones(p_av).astype(BF16)
    bv_ref[0] = _with_ones(p_bvk).astype(BF16)

    ca, sa_lo, sa_hi = taba_ref[0], taba_ref[1], taba_ref[2]
    aq = _rope(proj(C_AQ, 256), ca, sa_lo, sa_hi, DA_QK // 4) * (DA_QK ** -0.5 * LOG2E)
    aq_ref[0] = aq.astype(BF16)
    ak = _rope(proj(C_AK, 256), ca, sa_lo, sa_hi, DA_QK // 4)
    akt_ref[0] = ak.T.astype(BF16)

    bq_ms = sum(jnp.dot(p, m64, preferred_element_type=F32) for p in bq_sq)
    bk_ms = sum(jnp.dot(p, m64[:LANES, :LANES], preferred_element_type=F32) for p in bk_sq)
    gl = jnp.dot(r, w2_ref[...], preferred_element_type=F32) + b2_ref[...]

    gate_ref[0, :, 0:256] = _silu(proj(C_AZ, 256)).astype(BF16)
    gate_ref[0, :, 256:512] = _silu(proj(C_BZ, 256)).astype(BF16)
    gate_ref[0, :, 512:768] = _silu(proj(C_DZ, 256)).astype(BF16)

    lane_group = lax.broadcasted_iota(jnp.int32, (SG_CHUNK, BRANCH_W), 1) // (BRANCH_W // SG_GROUPS)
    mixed = []
    for n in range(ROW_TILE // SG_CHUNK):
        vchunk = vn[n * SG_CHUNK:(n + 1) * SG_CHUNK, :]
        acc = jnp.zeros((SG_CHUNK, BRANCH_W), F32)
        for g in range(SG_GROUPS):
            acc = jnp.where(lane_group == g,
                            jnp.dot(sgw_ref[g], vchunk, preferred_element_type=F32), acc)
        mixed.append(acc + sgb_ref[...])

    cb, sb_lo, sb_hi = tabb_ref[0], tabb_ref[1], tabb_ref[2]
    bq = p_bq * lax.rsqrt(bq_ms + EPS) * gqq_ref[...]
    bq_ref[0] = (_rope(bq, cb, sb_lo, sb_hi, GQ_HD // 4) * (GQ_HD ** -0.5 * LOG2E)).astype(BF16)
    bk = bk * lax.rsqrt(bk_ms + EPS) * gqk_ref[...]
    bk = _rope(bk, cb[:, :LANES], sb_lo[:, :LANES], sb_hi[:, :LANES], GQ_HD // 4)
    bkt_ref[0] = bk.T.astype(BF16)

    yc = proj(C_CU, 256) * jnp.concatenate(mixed, axis=0) * _silu(proj(C_CZ, 256))
    yc_ref[0] = yc.astype(BF16)

    q_scale = jnp.where(lax.broadcasted_iota(jnp.int32, (1, 256), 1) < 128, GLA_DK ** -0.5, 1.0)
    dqk_ref[0] = proj(C_DQK, 256) * q_scale
    dv_ref[0] = proj(C_DV, 256)
    log_sig = jnp.minimum(gl, 0.0) - jnp.log(1.0 + jnp.exp(-jnp.abs(gl)))
    dg_ref[0] = log_sig * (1.0 / GLA_NORMALIZER)


def _layer(first, stream_in, modsel, ng, wp, taba, tabb, gqq, gqk, lng, lnb, sgw, sgb, w2, b2):
    t = ROW_TILE
    b = stream_in[0].shape[0]
    s = CTX_LEN + stream_in[1].shape[1] if first else stream_in[0].shape[1]
    row = lambda width: pl.BlockSpec((1, t, width), lambda i, j: (i, j, 0))
    colT = lambda height: pl.BlockSpec((1, height, t), lambda i, j: (i, 0, j))
    full = lambda a: pl.BlockSpec(a.shape, lambda i, j: (0,) * a.ndim)
    tab = pl.BlockSpec((3, t, 256), lambda i, j: (0, j, 0))
    mods = pl.BlockSpec((1, 1, 3, D_MODEL), lambda i, j: (i, jnp.minimum(j, 1), 0, 0))
    shp = lambda width, dt: jax.ShapeDtypeStruct((b, s, width), dt)
    if first:
        stream_specs = [pl.BlockSpec((1, t, D_MODEL), lambda i, j: (i, 0, 0)),
                        pl.BlockSpec((1, t, D_MODEL), lambda i, j: (i, jnp.maximum(j - 1, 0), 0))]
    else:
        stream_specs = [row(D_MODEL), mods, row(256), row(256), row(256), row(256),
                        full(stream_in[6])]
    return pl.pallas_call(
        functools.partial(_layer_kernel, first),
        grid=(b, s // t),
        in_specs=stream_specs + [
            mods, full(ng), full(wp), tab, tab, full(gqq), full(gqk), full(lng), full(lnb),
            full(sgw), full(sgb), full(w2), full(b2),
        ],
        out_specs=[row(D_MODEL), row(256), colT(256), row(512), row(256), colT(128), row(256),
                   row(768), row(256), row(256), row(256), row(256)],
        out_shape=[shp(D_MODEL, F32),
                   shp(256, BF16), jax.ShapeDtypeStruct((b, 256, s), BF16), shp(512, BF16),
                   shp(256, BF16), jax.ShapeDtypeStruct((b, 128, s), BF16), shp(256, BF16),
                   shp(768, BF16), shp(256, BF16), shp(256, F32), shp(256, F32), shp(256, F32)],
        compiler_params=pltpu.CompilerParams(
            dimension_semantics=("parallel", "arbitrary"), vmem_limit_bytes=VMEM_LIMIT),
        name="layer_first" if first else "layer",
    )(*stream_in, modsel, ng, wp, taba, tabb, gqq, gqk, lng, lnb, sgw, sgb, w2, b2)


def _softmax_pv(n_jobs, scores_fn, values_fn):
    def scores(job):
        s = scores_fn(job)
        return s, jnp.max(s, axis=-1, keepdims=True)

    outs = []
    cur = scores(0)
    for job in range(n_jobs):
        nxt = scores(job + 1) if job + 1 < n_jobs else None
        s, mx = cur
        p = jnp.exp2(s - mx).astype(BF16)
        outs.append(jnp.dot(p, values_fn(job), preferred_element_type=F32))
        cur = nxt
    return outs


def _pair_heads(y):
    return jnp.concatenate([y[0] + pltpu.roll(y[1], LANES // 2, axis=1),
                            y[2] + pltpu.roll(y[3], LANES // 2, axis=1)], axis=1)


def _diff_attn_kernel(lam_init, q_ref, kt_ref, v_ref, gate_ref, lam_ref, sg_ref, o_ref):
    j = pl.program_id(1)
    lq1, lk1, lq2, lk2 = lam_ref[0:1, :], lam_ref[1:2, :], lam_ref[2:3, :], lam_ref[3:4, :]
    lam = (jnp.exp(jnp.sum(lq1 * lk1, axis=-1, keepdims=True))
           - jnp.exp(jnp.sum(lq2 * lk2, axis=-1, keepdims=True)) + lam_init)

    def body(kv_len):
        q = q_ref[0]

        def scores(job):
            c0 = job * DA_QK
            return jnp.dot(q[:, c0:c0 + DA_QK], kt_ref[0, c0:c0 + DA_QK, :kv_len],
                           preferred_element_type=F32)

        def values(job):
            h = job // 2
            return v_ref[0, :kv_len, h * LANES:(h + 1) * LANES]

        o = _softmax_pv(2 * DA_HEADS, scores, values)
        y = _pair_heads([_normalize_by_sum_lanes(o[2 * h]) - lam * _normalize_by_sum_lanes(o[2 * h + 1])
                         for h in range(DA_HEADS)])
        ms = _group_mean(y * y, _group_mean_matrix(BRANCH_W, DA_V))
        y = y * lax.rsqrt(ms + EPS) * sg_ref[...] * (1.0 - lam_init)
        o_ref[0] = (y * gate_ref[0].astype(F32)).astype(BF16)

    @pl.when(j == 0)
    def _():
        body(CTX_LEN)

    @pl.when(j > 0)
    def _():
        body(kt_ref.shape[2])


def _gqa_attn_kernel(q_ref, kt_ref, v_ref, gate_ref, o_ref):
    j = pl.program_id(1)
    grp = GQ_HEADS // GQ_KV_HEADS

    def body(kv_len):
        q = q_ref[0]

        def scores(h):
            k0 = (h // grp) * GQ_HD
            return jnp.dot(q[:, h * GQ_HD:(h + 1) * GQ_HD], kt_ref[0, k0:k0 + GQ_HD, :kv_len],
                           preferred_element_type=F32)

        def values(h):
            kv = h // grp
            return v_ref[0, :kv_len, kv * LANES:(kv + 1) * LANES]

        o = _softmax_pv(GQ_HEADS, scores, values)
        y = _pair_heads([_normalize_by_sum_lanes(oh) for oh in o])
        o_ref[0] = (y * gate_ref[0].astype(F32)).astype(BF16)

    @pl.when(j == 0)
    def _():
        body(CTX_LEN)

    @pl.when(j > 0)
    def _():
        body(kt_ref.shape[2])


def _attention(kernel, q, kt, v, gate, gate_block, extra, name):
    b, s, _ = q.shape
    t = ROW_TILE
    full = lambda a: pl.BlockSpec(a.shape, lambda i, j: (0,) * a.ndim)
    return pl.pallas_call(
        kernel,
        grid=(b, s // t),
        in_specs=[
            pl.BlockSpec((1, t, 256), lambda i, j: (i, j, 0)),
            pl.BlockSpec((1, kt.shape[1], s), lambda i, j: (i, 0, 0)),
            pl.BlockSpec((1, s, v.shape[2]), lambda i, j: (i, 0, 0)),
            pl.BlockSpec((1, t, 256), lambda i, j: (i, j, gate_block)),
        ] + [full(a) for a in extra],
        out_specs=pl.BlockSpec((1, t, 256), lambda i, j: (i, j, 0)),
        out_shape=jax.ShapeDtypeStruct((b, s, 256), BF16),
        compiler_params=pltpu.CompilerParams(
            dimension_semantics=("parallel", "arbitrary"), vmem_limit_bytes=VMEM_LIMIT),
        name=name,
    )(q, kt, v, gate, *extra)


def _split3(x):
    hi = x.astype(BF16)
    r = x - hi.astype(F32)
    mid = r.astype(BF16)
    lo = (r - mid.astype(F32)).astype(BF16)
    return hi, mid, lo


def _gla_kernel(qk_ref, v_ref, g_ref, gate_ref, ng_ref, o_ref, acc_ref, st_ref):
    n_tiles = qk_ref.shape[1] // GLA_TILE
    ctx_tiles = CTX_LEN // GLA_TILE
    per = GLA_TILE // GLA_CHUNK
    ri = lax.broadcasted_iota(jnp.int32, (GLA_TILE, GLA_TILE), 0)
    ci = lax.broadcasted_iota(jnp.int32, (GLA_TILE, GLA_TILE), 1)
    same_chunk = (ri // GLA_CHUNK) == (ci // GLA_CHUNK)
    head_of_k = lax.broadcasted_iota(jnp.int32, (GLA_TILE, GLA_HEADS * GLA_DK), 1) // GLA_DK
    head_of_v = lax.broadcasted_iota(jnp.int32, (GLA_TILE, BRANCH_W), 1) // GLA_DV
    st_mask = ((lax.broadcasted_iota(jnp.int32, (BRANCH_W, GLA_HEADS * GLA_DK), 0) // GLA_DV)
               == (lax.broadcasted_iota(jnp.int32, (BRANCH_W, GLA_HEADS * GLA_DK), 1) // GLA_DK))
    m64 = _group_mean_matrix(BRANCH_W, GLA_DV)
    tris = [jnp.where(same_chunk & (ci <= ri), 1.0, 0.0), jnp.where(same_chunk & (ci >= ri), 1.0, 0.0)]
    tris_b = [t.astype(BF16) for t in tris]

    nt_dims = (((1,), (1,)), ((), ()))
    tn_dims = (((0,), (0,)), ((), ()))

    def scan_step(i, carry):
        rev = jnp.where(i < ctx_tiles, ctx_tiles - 1 - i, n_tiles + ctx_tiles - 1 - i)
        chains = [(bi, d) for bi in range(GLA_BATCH) for d in (0, 1)]
        nc = len(chains)
        rows = [pl.ds(pl.multiple_of((rev if d else i) * GLA_TILE, GLA_TILE), GLA_TILE)
                for _, d in chains]
        ends = [[c * GLA_CHUNK if d else (c + 1) * GLA_CHUNK - 1 for c in range(per)] for _, d in chains]

        cum = []
        for n, (bi, d) in enumerate(chains):
            parts = _split3(g_ref[bi, rows[n], 128 * d:128 * d + 128])
            cum.append(sum(jnp.dot(tris_b[d], p, preferred_element_type=F32) for p in parts))

        qe, kd, kl, vb = [], [], [], []
        for n, (bi, d) in enumerate(chains):
            q = qk_ref[bi, rows[n], 0:128]
            k = qk_ref[bi, rows[n], 128:256]
            cum_last = jnp.concatenate(
                [jnp.broadcast_to(cum[n][e:e + 1, :], (GLA_CHUNK, 128)) for e in ends[n]], axis=0)
            qe.append(q * jnp.exp(cum[n]))
            kd.append((k * jnp.exp(-cum[n])).astype(BF16))
            kl.append((k * jnp.exp(cum_last - cum[n])).astype(BF16))
            vb.append(v_ref[bi, rows[n], :].astype(BF16))

        att = [[None] * GLA_HEADS for _ in range(nc)]
        for h in range(GLA_HEADS):
            for n, (bi, d) in enumerate(chains):
                qh = jnp.where(head_of_k == h, qe[n], 0.0).astype(BF16)
                a = lax.dot_general(qh, kd[n], nt_dims, preferred_element_type=F32)
                att[n][h] = (a * tris[d]).astype(BF16)
        o = [jnp.zeros((GLA_TILE, BRANCH_W), F32) for _ in range(nc)]
        for h in range(GLA_HEADS):
            for n in range(nc):
                o[n] = jnp.where(head_of_v == h,
                                 jnp.dot(att[n][h], vb[n], preferred_element_type=F32), o[n])

        qeb = [x.astype(BF16) for x in qe]
        o_inter = [[None] * per for _ in range(nc)]
        for step in range(per):
            for n, (bi, d) in enumerate(chains):
                c = per - 1 - step if d else step
                cs = slice(c * GLA_CHUNK, (c + 1) * GLA_CHUNK)
                st = st_ref[n]
                o_inter[n][c] = lax.dot_general(qeb[n][cs, :], st.astype(BF16), nt_dims,
                                                preferred_element_type=F32)
                decay = jnp.exp(cum[n][ends[n][c]:ends[n][c] + 1, :])
                upd = lax.dot_general(vb[n][cs, :], kl[n][cs, :], tn_dims,
                                      preferred_element_type=F32)
                st_ref[n] = st * decay + jnp.where(st_mask, upd, 0.0)
        for n in range(nc):
            acc_ref[n, rows[n], :] = o[n] + jnp.concatenate(o_inter[n], axis=0)
        return carry

    st_ref[...] = jnp.zeros_like(st_ref)
    lax.fori_loop(0, n_tiles, scan_step, 0)

    def finish_step(i, carry):
        rows = pl.ds(pl.multiple_of(i * GLA_TILE, GLA_TILE), GLA_TILE)
        for bi in range(GLA_BATCH):
            tot = acc_ref[2 * bi, rows, :] + acc_ref[2 * bi + 1, rows, :]
            ms = _group_mean(tot * tot, m64)
            y = tot * lax.rsqrt(ms + EPS) * ng_ref[...]
            o_ref[bi, rows, :] = (y * gate_ref[bi, rows, :].astype(F32)).astype(BF16)
        return carry

    lax.fori_loop(0, n_tiles, finish_step, 0)


def _gla(dqk, dv, dg, gate, ng):
    b, s, _ = dqk.shape
    nb = GLA_BATCH
    blk = lambda width, cb: pl.BlockSpec((nb, s, width), lambda i: (i, 0, cb))
    return pl.pallas_call(
        _gla_kernel,
        grid=(b // nb,),
        in_specs=[blk(256, 0), blk(256, 0), blk(256, 0), blk(256, 2),
                  pl.BlockSpec(ng.shape, lambda i: (0, 0))],
        out_specs=blk(256, 0),
        out_shape=jax.ShapeDtypeStruct((b, s, 256), BF16),
        scratch_shapes=[pltpu.VMEM((2 * nb, s, BRANCH_W), F32),
                        pltpu.VMEM((2 * nb, BRANCH_W, GLA_HEADS * GLA_DK), F32)],
        compiler_params=pltpu.CompilerParams(
            dimension_semantics=("parallel",), vmem_limit_bytes=VMEM_LIMIT),
        name="gla",
    )(dqk, dv, dg, gate, ng)


def _out_kernel(x_ref, mod_ref, ya_ref, yb_ref, yc_ref, yd_ref, w_ref, fg_ref, o_ref):
    y = jnp.concatenate([ya_ref[0], yb_ref[0], yc_ref[0], yd_ref[0]], axis=1)
    upd = jnp.dot(y, w_ref[...], preferred_element_type=F32)
    xn = x_ref[0] + mod_ref[0, 0, 2:3, :] * upd
    o_ref[0] = xn * lax.rsqrt(jnp.mean(xn * xn, axis=-1, keepdims=True) + EPS) * fg_ref[...]


def _out_proj(xs, modsel, ya, yb, yc, yd, wo, fg):
    b, s, _ = xs.shape
    t = ROW_TILE
    skip = CTX_LEN // t
    row = lambda width: pl.BlockSpec((1, t, width), lambda i, j: (i, j + skip, 0))
    return pl.pallas_call(
        _out_kernel,
        grid=(b, s // t - skip),
        in_specs=[
            row(D_MODEL),
            pl.BlockSpec((1, 1, 3, D_MODEL), lambda i, j: (i, 1, 0, 0)),
            row(256), row(256), row(256), row(256),
            pl.BlockSpec(wo.shape, lambda i, j: (0, 0)),
            pl.BlockSpec(fg.shape, lambda i, j: (0, 0)),
        ],
        out_specs=pl.BlockSpec((1, t, D_MODEL), lambda i, j: (i, j, 0)),
        out_shape=jax.ShapeDtypeStruct((b, s - skip * t, D_MODEL), F32),
        compiler_params=pltpu.CompilerParams(
            dimension_semantics=("parallel", "parallel"), vmem_limit_bytes=VMEM_LIMIT),
        name="out_proj_final",
    )(xs, modsel, ya, yb, yc, yd, wo, fg)


def _rope_tables(seq, dim, width):
    half = dim // 2
    quarter = half // 2
    lane = jnp.arange(width) % dim
    freq = ROPE_THETA ** (-(2.0 * (lane % quarter).astype(F32)) / half)
    pos_t = jnp.arange(seq)
    pos = jnp.where(lane[None, :] < half, (pos_t // GRID_W)[:, None], (pos_t % GRID_W)[:, None])
    ang = pos.astype(F32) * freq[None, :]
    cos, sin = jnp.cos(ang), jnp.sin(ang)
    first = (lane % half) < quarter
    s_lo = jnp.where(first[None, :], -sin, 0.0)
    s_hi = jnp.where(first[None, :], 0.0, sin)
    lat = jnp.stack([cos, s_lo, s_hi])
    ctx = jnp.stack([jnp.ones((CTX_LEN, width), F32), jnp.zeros((CTX_LEN, width), F32),
                     jnp.zeros((CTX_LEN, width), F32)])
    return jnp.concatenate([ctx, lat], axis=1)


def _pack_w_in(w):
    offs = [0]
    for n in (256, 256, 256, 256, 256, 128, 128, 256, 256, 256, 256, 128, 128, 256, 256, 16, 16):
        offs.append(offs[-1] + n)
    seg = [w[:, offs[i]:offs[i + 1]] for i in range(17)]
    aq, ak, av, az, bq, bk, bv, bz, cu, cv, cz, dq, dk, dv, dz, drf, drb = seg
    gap = jnp.zeros((w.shape[0], LANES // 2), w.dtype)
    r_gap = jnp.concatenate([drf, drb, gap[:, :LANES // 2 - 2 * GLA_RANK]], axis=1)
    av_sp = jnp.concatenate([av[:, 0:64], r_gap, av[:, 64:128], gap, av[:, 128:192], gap,
                             av[:, 192:256], gap], axis=1)
    bvk = jnp.concatenate([bv[:, :64], bk[:, :64], bv[:, 64:], bk[:, 64:]], axis=1)
    packed = jnp.concatenate([aq, ak, av_sp, az, bq, bvk, bz, cu, cv, cz, dq, dk, dv, dz], axis=1)
    assert packed.shape[1] == P_PACK
    return packed.astype(BF16)


def kernel(x, c, ctx, c_ctx, ada_w, ada_b, norm_g, w_in, da_lq1, da_lk1, da_lq2, da_lk2,
           da_subln_g, gq_qnorm_g, gq_knorm_g, sg_ln_g, sg_ln_b, sg_w, sg_b,
           gla_w2_f, gla_b_f, gla_w2_b, gla_b_b, gla_norm_g, w_out, final_norm_g):
    b, seq, d = x.shape
    assert (seq, d, ctx.shape[1]) == (seq // ROW_TILE * ROW_TILE, D_MODEL, CTX_LEN)

    n_mod = 32
    cpad = jnp.zeros((n_mod, d), F32).at[:b].set(c).at[b].set(c_ctx)
    mod = _modulation(cpad, ada_w, ada_b)

    taba = _rope_tables(seq, DA_QK, 256)
    tabb = _rope_tables(seq, GQ_HD, 256)
    fg = final_norm_g.reshape(1, d)

    stream_in = (ctx, x)
    for i in range(DEPTH):
        lam_init = 0.8 - 0.6 * math.exp(-0.3 * i)
        mod_l = mod[i, :b].reshape(b, 1, 3, d)
        mod_c = jnp.broadcast_to(mod[i, b].reshape(1, 1, 3, d), (b, 1, 3, d))
        modsel = jnp.concatenate([mod_c, mod_l], axis=1)
        w2 = jnp.zeros((LANES, 256), F32)
        w2 = (w2.at[R_LANE:R_LANE + GLA_RANK, 0:128].set(gla_w2_f[i])
              .at[R_LANE + GLA_RANK:R_LANE + 2 * GLA_RANK, 128:256].set(gla_w2_b[i]))
        b2 = jnp.concatenate([gla_b_f[i], gla_b_b[i]]).reshape(1, 256)
        sgb = jnp.repeat(sg_b[i].T, BRANCH_W // SG_GROUPS, axis=1)
        (xs, aq, akt, av, bq, bkt, bv, gate, yc, dqk, dv, dg) = _layer(
            i == 0, stream_in, modsel, norm_g[i].reshape(1, d), _pack_w_in(w_in[i]), taba, tabb,
            jnp.tile(gq_qnorm_g[i], GQ_HEADS).reshape(1, 256),
            jnp.tile(gq_knorm_g[i], GQ_KV_HEADS).reshape(1, 128),
            sg_ln_g[i].reshape(1, 256), sg_ln_b[i].reshape(1, 256),
            sg_w[i].astype(BF16), sgb, w2.astype(BF16), b2)
        lam_vecs = jnp.stack([da_lq1[i], da_lk1[i], da_lq2[i], da_lk2[i]])
        ya = _attention(functools.partial(_diff_attn_kernel, lam_init), aq, akt, av, gate, 0,
                        [lam_vecs, jnp.tile(da_subln_g[i], DA_HEADS).reshape(1, 256)], "diff_attn")
        yb = _attention(_gqa_attn_kernel, bq, bkt, bv, gate, 1, [], "gqa_attn")
        yd = _gla(dqk, dv, dg, gate, jnp.tile(gla_norm_g[i], GLA_HEADS).reshape(1, 256))
        stream_in = (xs, modsel, ya, yb, yc, yd, w_out[i].astype(BF16))
    return _out_proj(*stream_in, fg)
```

```python
import functools
import math

import jax
import jax.numpy as jnp
from jax import lax
from jax.experimental import pallas as pl
from jax.experimental.pallas import tpu as pltpu

F32 = jnp.float32
BF16 = jnp.bfloat16

D_MODEL = 1024
DEPTH = 4
CTX_LEN = 256
GRID_W = 64
BRANCH_W = 256
ROPE_THETA = 10000.0
EPS = 1e-6
DA_HEADS = 4
DA_QK = 32
DA_V = 64
GQ_HEADS = 4
GQ_KV_HEADS = 2
GQ_HD = 64
SG_GROUPS = 4
SG_CHUNK = 128
GLA_HEADS = 4
GLA_DV = 64
GLA_DK = 32
GLA_RANK = 16
GLA_NORMALIZER = 16.0
GLA_CHUNK = 32

LANES = 128
ROW_TILE = 256
GLA_TILE = 128
GLA_BATCH = 2
VMEM_LIMIT = 56 * 1024 * 1024

C_AQ, C_AK, C_AV, C_AZ = 0, 256, 512, 1024
C_BQ, C_BVK, C_BZ = 1280, 1536, 1792
C_CU, C_CV, C_CZ = 2048, 2304, 2560
C_DQK, C_DV, C_DZ = 2816, 3072, 3328
P_PACK = 3584
R_LANE = 64
LOG2E = math.log2(math.e)


def _silu(x):
    return x * (1.0 / (1.0 + jnp.exp(-x)))


def _group_mean_matrix(width, group):
    r = lax.broadcasted_iota(jnp.int32, (width, width), 0) // group
    c = lax.broadcasted_iota(jnp.int32, (width, width), 1) // group
    return jnp.where(r == c, 1.0 / group, 0.0).astype(BF16)


def _split2(x):
    hi = x.astype(BF16)
    return hi, (x - hi.astype(F32)).astype(BF16)


def _group_mean(x, mat):
    return sum(jnp.dot(p, mat, preferred_element_type=F32) for p in _split2(x))


def _rope(x, cos, s_lo, s_hi, shift):
    outs = []
    for c in range(x.shape[1] // LANES):
        sl = slice(c * LANES, (c + 1) * LANES)
        xc = x[:, sl]
        up = pltpu.roll(xc, LANES - shift, axis=1)
        dn = pltpu.roll(xc, shift, axis=1)
        outs.append(xc * cos[:, sl] + up * s_lo[:, sl] + dn * s_hi[:, sl])
    return outs[0] if len(outs) == 1 else jnp.concatenate(outs, axis=1)


def _with_ones(v):
    lane = lax.broadcasted_iota(jnp.int32, v.shape, 1) % LANES
    return jnp.where(lane < LANES // 2, v, 1.0)


def _normalize_by_sum_lanes(o):
    lane = lax.broadcasted_iota(jnp.int32, o.shape, 1)
    return jnp.where(lane < LANES // 2, o * (1.0 / pltpu.roll(o, LANES // 2, axis=1)), 0.0)


def _mod_kernel(c_ref, w_ref, b_ref, o_ref):
    s = _silu(c_ref[...]).astype(BF16)
    o_ref[0] = jnp.dot(s, w_ref[0].astype(BF16), preferred_element_type=F32) + b_ref[0]


def _modulation(cpad, ada_w, ada_b):
    n = cpad.shape[0]
    return pl.pallas_call(
        _mod_kernel,
        grid=(DEPTH, 3),
        in_specs=[
            pl.BlockSpec((n, D_MODEL), lambda i, j: (0, 0)),
            pl.BlockSpec((1, D_MODEL, D_MODEL), lambda i, j: (i, 0, j)),
            pl.BlockSpec((1, 1, D_MODEL), lambda i, j: (i, 0, j)),
        ],
        out_specs=pl.BlockSpec((1, n, D_MODEL), lambda i, j: (i, 0, j)),
        out_shape=jax.ShapeDtypeStruct((DEPTH, n, 3 * D_MODEL), F32),
        compiler_params=pltpu.CompilerParams(vmem_limit_bytes=VMEM_LIMIT),
        name="modulation",
    )(cpad, ada_w, ada_b.reshape(DEPTH, 1, 3 * D_MODEL))


def _layer_kernel(first, *refs):
    if first:
        ctx_ref, x_ref = refs[:2]
        refs = refs[2:]
    else:
        xs_ref, modp_ref, ya_ref, yb_ref, yc_in_ref, yd_ref, wo_ref = refs[:7]
        refs = refs[7:]
    (mod_ref, ng_ref, w_ref, taba_ref, tabb_ref, gqq_ref, gqk_ref, lng_ref, lnb_ref, sgw_ref,
     sgb_ref, w2_ref, b2_ref,
     xs_out_ref, aq_ref, akt_ref, av_ref, bq_ref, bkt_ref, bv_ref, gate_ref, yc_ref,
     dqk_ref, dv_ref, dg_ref) = refs

    if first:
        x = jnp.where(pl.program_id(1) == 0, ctx_ref[0], x_ref[0])
    else:
        y_prev = jnp.concatenate([ya_ref[0], yb_ref[0], yc_in_ref[0], yd_ref[0]], axis=1)
        x = xs_ref[0] + modp_ref[0, 0, 2:3, :] * jnp.dot(y_prev, wo_ref[...],
                                                         preferred_element_type=F32)
    xs_out_ref[0] = x
    shift = mod_ref[0, 0, 0:1, :]
    scale = mod_ref[0, 0, 1:2, :]
    y = x * lax.rsqrt(jnp.mean(x * x, axis=-1, keepdims=True) + EPS) * ng_ref[...]
    hb = (y * (1.0 + scale) + shift).astype(BF16)

    def proj(lo, width):
        return jnp.dot(hb, w_ref[:, lo:lo + width], preferred_element_type=F32)

    m64 = _group_mean_matrix(BRANCH_W, GQ_HD)
    half_lane = lax.broadcasted_iota(jnp.int32, (ROW_TILE, LANES), 1) < LANES // 2
    p_bq = proj(C_BQ, 256)
    p_bvk = proj(C_BVK, 256)
    p_cv = proj(C_CV, 256)
    p_av = proj(C_AV, 512)

    bq_sq = _split2(p_bq * p_bq)
    bk = jnp.where(half_lane, pltpu.roll(p_bvk[:, :LANES], LANES // 2, axis=1), p_bvk[:, LANES:])
    bk_sq = _split2(bk * bk)
    mu = jnp.mean(p_cv, axis=-1, keepdims=True)
    cen = p_cv - mu
    var = jnp.mean(cen * cen, axis=-1, keepdims=True)
    vn = (cen * lax.rsqrt(var + EPS) * lng_ref[...] + lnb_ref[...]).astype(BF16)
    r = p_av[:, :LANES].astype(BF16)
    av_ref[0] = _with_ones(p_av).astype(BF16)
    bv_ref[0] = _with_ones(p_bvk).astype(BF16)

    ca, sa_lo, sa_hi = taba_ref[0], taba_ref[1], taba_ref[2]
    aq = _rope(proj(C_AQ, 256), ca, sa_lo, sa_hi, DA_QK // 4) * (DA_QK ** -0.5 * LOG2E)
    aq_ref[0] = aq.astype(BF16)
    ak = _rope(proj(C_AK, 256), ca, sa_lo, sa_hi, DA_QK // 4)
    akt_ref[0] = ak.T.astype(BF16)

    bq_ms = sum(jnp.dot(p, m64, preferred_element_type=F32) for p in bq_sq)
    bk_ms = sum(jnp.dot(p, m64[:LANES, :LANES], preferred_element_type=F32) for p in bk_sq)
    gl = jnp.dot(r, w2_ref[...], preferred_element_type=F32) + b2_ref[...]

    gate_ref[0, :, 0:256] = _silu(proj(C_AZ, 256)).astype(BF16)
    gate_ref[0, :, 256:512] = _silu(proj(C_BZ, 256)).astype(BF16)
    gate_ref[0, :, 512:768] = _silu(proj(C_DZ, 256)).astype(BF16)

    lane_group = lax.broadcasted_iota(jnp.int32, (SG_CHUNK, BRANCH_W), 1) // (BRANCH_W // SG_GROUPS)
    mixed = []
    for n in range(ROW_TILE // SG_CHUNK):
        vchunk = vn[n * SG_CHUNK:(n + 1) * SG_CHUNK, :]
        acc = jnp.zeros((SG_CHUNK, BRANCH_W), F32)
        for g in range(SG_GROUPS):
            acc = jnp.where(lane_group == g,
                            jnp.dot(sgw_ref[g], vchunk, preferred_element_type=F32), acc)
        mixed.append(acc + sgb_ref[...])

    cb, sb_lo, sb_hi = tabb_ref[0], tabb_ref[1], tabb_ref[2]
    bq = p_bq * lax.rsqrt(bq_ms + EPS) * gqq_ref[...]
    bq_ref[0] = (_rope(bq, cb, sb_lo, sb_hi, GQ_HD // 4) * (GQ_HD ** -0.5 * LOG2E)).astype(BF16)
    bk = bk * lax.rsqrt(bk_ms + EPS) * gqk_ref[...]
    bk = _rope(bk, cb[:, :LANES], sb_lo[:, :LANES], sb_hi[:, :LANES], GQ_HD // 4)
    bkt_ref[0] = bk.T.astype(BF16)

    yc = proj(C_CU, 256) * jnp.concatenate(mixed, axis=0) * _silu(proj(C_CZ, 256))
    yc_ref[0] = yc.astype(BF16)

    q_scale = jnp.where(lax.broadcasted_iota(jnp.int32, (1, 256), 1) < 128, GLA_DK ** -0.5, 1.0)
    dqk_ref[0] = proj(C_DQK, 256) * q_scale
    dv_ref[0] = proj(C_DV, 256).astype(BF16)
    log_sig = jnp.minimum(gl, 0.0) - jnp.log(1.0 + jnp.exp(-jnp.abs(gl)))
    dg_ref[0] = log_sig * (1.0 / GLA_NORMALIZER)


def _layer(first, stream_in, modsel, ng, wp, taba, tabb, gqq, gqk, lng, lnb, sgw, sgb, w2, b2):
    t = ROW_TILE
    b = stream_in[0].shape[0]
    s = CTX_LEN + stream_in[1].shape[1] if first else stream_in[0].shape[1]
    row = lambda width: pl.BlockSpec((1, t, width), lambda i, j: (i, j, 0))
    colT = lambda height: pl.BlockSpec((1, height, t), lambda i, j: (i, 0, j))
    full = lambda a: pl.BlockSpec(a.shape, lambda i, j: (0,) * a.ndim)
    tab = pl.BlockSpec((3, t, 256), lambda i, j: (0, j, 0))
    mods = pl.BlockSpec((1, 1, 3, D_MODEL), lambda i, j: (i, jnp.minimum(j, 1), 0, 0))
    shp = lambda width, dt: jax.ShapeDtypeStruct((b, s, width), dt)
    if first:
        stream_specs = [pl.BlockSpec((1, t, D_MODEL), lambda i, j: (i, 0, 0)),
                        pl.BlockSpec((1, t, D_MODEL), lambda i, j: (i, jnp.maximum(j - 1, 0), 0))]
    else:
        stream_specs = [row(D_MODEL), mods, row(256), row(256), row(256), row(256),
                        full(stream_in[6])]
    return pl.pallas_call(
        functools.partial(_layer_kernel, first),
        grid=(b, s // t),
        in_specs=stream_specs + [
            mods, full(ng), full(wp), tab, tab, full(gqq), full(gqk), full(lng), full(lnb),
            full(sgw), full(sgb), full(w2), full(b2),
        ],
        out_specs=[row(D_MODEL), row(256), colT(256), row(512), row(256), colT(128), row(256),
                   row(768), row(256), row(256), row(256), row(256)],
        out_shape=[shp(D_MODEL, F32),
                   shp(256, BF16), jax.ShapeDtypeStruct((b, 256, s), BF16), shp(512, BF16),
                   shp(256, BF16), jax.ShapeDtypeStruct((b, 128, s), BF16), shp(256, BF16),
                   shp(768, BF16), shp(256, BF16), shp(256, F32), shp(256, BF16), shp(256, F32)],
        compiler_params=pltpu.CompilerParams(
            dimension_semantics=("parallel", "arbitrary"), vmem_limit_bytes=VMEM_LIMIT),
        name="layer_first" if first else "layer",
    )(*stream_in, modsel, ng, wp, taba, tabb, gqq, gqk, lng, lnb, sgw, sgb, w2, b2)


def _softmax_pv(n_jobs, scores_fn, values_fn):
    def scores(job):
        s = scores_fn(job)
        return s, jnp.max(s, axis=-1, keepdims=True)

    outs = []
    cur = scores(0)
    for job in range(n_jobs):
        nxt = scores(job + 1) if job + 1 < n_jobs else None
        s, mx = cur
        p = jnp.exp2(s - mx).astype(BF16)
        outs.append(jnp.dot(p, values_fn(job), preferred_element_type=F32))
        cur = nxt
    return outs


def _pair_heads(y):
    return jnp.concatenate([y[0] + pltpu.roll(y[1], LANES // 2, axis=1),
                            y[2] + pltpu.roll(y[3], LANES // 2, axis=1)], axis=1)


def _diff_attn_kernel(lam_init, q_ref, kt_ref, v_ref, gate_ref, lam_ref, sg_ref, o_ref):
    j = pl.program_id(1)
    lq1, lk1, lq2, lk2 = lam_ref[0:1, :], lam_ref[1:2, :], lam_ref[2:3, :], lam_ref[3:4, :]
    lam = (jnp.exp(jnp.sum(lq1 * lk1, axis=-1, keepdims=True))
           - jnp.exp(jnp.sum(lq2 * lk2, axis=-1, keepdims=True)) + lam_init)

    def body(kv_len):
        q = q_ref[0]

        def scores(job):
            c0 = job * DA_QK
            return jnp.dot(q[:, c0:c0 + DA_QK], kt_ref[0, c0:c0 + DA_QK, :kv_len],
                           preferred_element_type=F32)

        def values(job):
            h = job // 2
            return v_ref[0, :kv_len, h * LANES:(h + 1) * LANES]

        o = _softmax_pv(2 * DA_HEADS, scores, values)
        y = _pair_heads([_normalize_by_sum_lanes(o[2 * h]) - lam * _normalize_by_sum_lanes(o[2 * h + 1])
                         for h in range(DA_HEADS)])
        ms = _group_mean(y * y, _group_mean_matrix(BRANCH_W, DA_V))
        y = y * lax.rsqrt(ms + EPS) * sg_ref[...] * (1.0 - lam_init)
        o_ref[0] = (y * gate_ref[0].astype(F32)).astype(BF16)

    @pl.when(j == 0)
    def _():
        body(CTX_LEN)

    @pl.when(j > 0)
    def _():
        body(kt_ref.shape[2])


def _gqa_attn_kernel(q_ref, kt_ref, v_ref, gate_ref, o_ref):
    j = pl.program_id(1)
    grp = GQ_HEADS // GQ_KV_HEADS

    def body(kv_len):
        q = q_ref[0]

        def scores(h):
            k0 = (h // grp) * GQ_HD
            return jnp.dot(q[:, h * GQ_HD:(h + 1) * GQ_HD], kt_ref[0, k0:k0 + GQ_HD, :kv_len],
                           preferred_element_type=F32)

        def values(h):
            kv = h // grp
            return v_ref[0, :kv_len, kv * LANES:(kv + 1) * LANES]

        o = _softmax_pv(GQ_HEADS, scores, values)
        y = _pair_heads([_normalize_by_sum_lanes(oh) for oh in o])
        o_ref[0] = (y * gate_ref[0].astype(F32)).astype(BF16)

    @pl.when(j == 0)
    def _():
        body(CTX_LEN)

    @pl.when(j > 0)
    def _():
        body(kt_ref.shape[2])


def _attention(kernel, q, kt, v, gate, gate_block, extra, name):
    b, s, _ = q.shape
    t = ROW_TILE
    full = lambda a: pl.BlockSpec(a.shape, lambda i, j: (0,) * a.ndim)
    return pl.pallas_call(
        kernel,
        grid=(b, s // t),
        in_specs=[
            pl.BlockSpec((1, t, 256), lambda i, j: (i, j, 0)),
            pl.BlockSpec((1, kt.shape[1], s), lambda i, j: (i, 0, 0)),
            pl.BlockSpec((1, s, v.shape[2]), lambda i, j: (i, 0, 0)),
            pl.BlockSpec((1, t, 256), lambda i, j: (i, j, gate_block)),
        ] + [full(a) for a in extra],
        out_specs=pl.BlockSpec((1, t, 256), lambda i, j: (i, j, 0)),
        out_shape=jax.ShapeDtypeStruct((b, s, 256), BF16),
        compiler_params=pltpu.CompilerParams(
            dimension_semantics=("parallel", "arbitrary"), vmem_limit_bytes=VMEM_LIMIT),
        name=name,
    )(q, kt, v, gate, *extra)


def _split3(x):
    hi = x.astype(BF16)
    r = x - hi.astype(F32)
    mid = r.astype(BF16)
    lo = (r - mid.astype(F32)).astype(BF16)
    return hi, mid, lo


def _gla_kernel(qk_ref, v_ref, g_ref, gate_ref, ng_ref, o_ref, acc_ref, st_ref):
    n_tiles = qk_ref.shape[1] // GLA_TILE
    ctx_tiles = CTX_LEN // GLA_TILE
    per = GLA_TILE // GLA_CHUNK
    ri = lax.broadcasted_iota(jnp.int32, (GLA_TILE, GLA_TILE), 0)
    ci = lax.broadcasted_iota(jnp.int32, (GLA_TILE, GLA_TILE), 1)
    same_chunk = (ri // GLA_CHUNK) == (ci // GLA_CHUNK)
    chunk_of_row = ri // GLA_CHUNK
    head_of_k = lax.broadcasted_iota(jnp.int32, (GLA_TILE, GLA_HEADS * GLA_DK), 1) // GLA_DK
    head_of_v = lax.broadcasted_iota(jnp.int32, (GLA_TILE, BRANCH_W), 1) // GLA_DV
    st_mask = ((lax.broadcasted_iota(jnp.int32, (BRANCH_W, GLA_HEADS * GLA_DK), 0) // GLA_DV)
               == (lax.broadcasted_iota(jnp.int32, (BRANCH_W, GLA_HEADS * GLA_DK), 1) // GLA_DK))
    m64 = _group_mean_matrix(BRANCH_W, GLA_DV)
    tris = [jnp.where(same_chunk & (ci <= ri), 1.0, 0.0), jnp.where(same_chunk & (ci >= ri), 1.0, 0.0)]
    tris_b = [t.astype(BF16) for t in tris]
    tris4 = [jnp.concatenate([t] * GLA_HEADS, axis=1) for t in tris]

    nt_dims = (((1,), (1,)), ((), ()))
    tn_dims = (((0,), (0,)), ((), ()))

    def scan_step(i, carry):
        rev = jnp.where(i < ctx_tiles, ctx_tiles - 1 - i, n_tiles + ctx_tiles - 1 - i)
        chains = [(bi, d) for bi in range(GLA_BATCH) for d in (0, 1)]
        nc = len(chains)
        rows = [pl.ds(pl.multiple_of((rev if d else i) * GLA_TILE, GLA_TILE), GLA_TILE)
                for _, d in chains]
        ends = [[c * GLA_CHUNK if d else (c + 1) * GLA_CHUNK - 1 for c in range(per)] for _, d in chains]

        cum = []
        for n, (bi, d) in enumerate(chains):
            parts = _split3(g_ref[bi, rows[n], 128 * d:128 * d + 128])
            cum.append(sum(jnp.dot(tris_b[d], p, preferred_element_type=F32) for p in parts))

        qe, kd, kl, vb = [], [], [], []
        for n, (bi, d) in enumerate(chains):
            q = qk_ref[bi, rows[n], 0:128]
            k = qk_ref[bi, rows[n], 128:256]
            cum_last = jnp.concatenate(
                [jnp.broadcast_to(cum[n][e:e + 1, :], (GLA_CHUNK, 128)) for e in ends[n]], axis=0)
            qe.append(q * jnp.exp(cum[n]))
            kd.append(k * jnp.exp(-cum[n]))
            kl.append(k * jnp.exp(cum_last - cum[n]))
            vb.append(v_ref[bi, rows[n], :])

        qeb = [x.astype(BF16) for x in qe]
        att = []
        for n, (bi, d) in enumerate(chains):
            kd_heads = jnp.concatenate(
                [jnp.where(head_of_k == h, kd[n], 0.0).astype(BF16) for h in range(GLA_HEADS)], axis=0)
            a = lax.dot_general(qeb[n], kd_heads, nt_dims, preferred_element_type=F32)
            att.append((a * tris4[d]).astype(BF16))
        o = []
        for n in range(nc):
            v_heads = jnp.concatenate(
                [jnp.where(head_of_v == h, vb[n], jnp.zeros_like(vb[n])) for h in range(GLA_HEADS)], axis=0)
            o.append(jnp.dot(att[n], v_heads, preferred_element_type=F32))

        upd = []
        for n in range(nc):
            kl_chunks = jnp.concatenate(
                [jnp.where(chunk_of_row == c, kl[n], 0.0).astype(BF16) for c in range(per)], axis=1)
            upd.append(lax.dot_general(vb[n], kl_chunks, tn_dims, preferred_element_type=F32))
        seen = [[None] * per for _ in range(nc)]
        for n, (bi, d) in enumerate(chains):
            st = st_ref[n]
            for c in (range(per - 1, -1, -1) if d else range(per)):
                seen[n][c] = st.astype(BF16)
                decay = jnp.exp(cum[n][ends[n][c]:ends[n][c] + 1, :])
                st = st * decay + jnp.where(st_mask, upd[n][:, c * 128:(c + 1) * 128], 0.0)
            st_ref[n] = st
        for n in range(nc):
            qe_chunks = jnp.concatenate(
                [jnp.where(chunk_of_row == c, qe[n], 0.0).astype(BF16) for c in range(per)], axis=1)
            o_inter = lax.dot_general(qe_chunks, jnp.concatenate(seen[n], axis=1), nt_dims,
                                      preferred_element_type=F32)
            acc_ref[n, rows[n], :] = o[n] + o_inter
        return carry

    st_ref[...] = jnp.zeros_like(st_ref)
    lax.fori_loop(0, n_tiles, scan_step, 0)

    def finish_step(i, carry):
        rows = pl.ds(pl.multiple_of(i * ROW_TILE, ROW_TILE), ROW_TILE)
        tot = [acc_ref[2 * bi, rows, :] + acc_ref[2 * bi + 1, rows, :] for bi in range(GLA_BATCH)]
        sq = [_split2(t * t) for t in tot]
        ms = [sum(jnp.dot(p, m64, preferred_element_type=F32) for p in s) for s in sq]
        for bi in range(GLA_BATCH):
            y = tot[bi] * lax.rsqrt(ms[bi] + EPS) * ng_ref[...]
            o_ref[bi, rows, :] = (y * gate_ref[bi, rows, :].astype(F32)).astype(BF16)
        return carry

    lax.fori_loop(0, qk_ref.shape[1] // ROW_TILE, finish_step, 0)


def _gla(dqk, dv, dg, gate, ng):
    b, s, _ = dqk.shape
    nb = GLA_BATCH
    blk = lambda width, cb: pl.BlockSpec((nb, s, width), lambda i: (i, 0, cb))
    return pl.pallas_call(
        _gla_kernel,
        grid=(b // nb,),
        in_specs=[blk(256, 0), blk(256, 0), blk(256, 0), blk(256, 2),
                  pl.BlockSpec(ng.shape, lambda i: (0, 0))],
        out_specs=blk(256, 0),
        out_shape=jax.ShapeDtypeStruct((b, s, 256), BF16),
        scratch_shapes=[pltpu.VMEM((2 * nb, s, BRANCH_W), F32),
                        pltpu.VMEM((2 * nb, BRANCH_W, GLA_HEADS * GLA_DK), F32)],
        compiler_params=pltpu.CompilerParams(
            dimension_semantics=("parallel",), vmem_limit_bytes=VMEM_LIMIT),
        name="gla",
    )(dqk, dv, dg, gate, ng)


def _out_kernel(x_ref, mod_ref, ya_ref, yb_ref, yc_ref, yd_ref, w_ref, fg_ref, o_ref):
    y = jnp.concatenate([ya_ref[0], yb_ref[0], yc_ref[0], yd_ref[0]], axis=1)
    upd = jnp.dot(y, w_ref[...], preferred_element_type=F32)
    xn = x_ref[0] + mod_ref[0, 0, 2:3, :] * upd
    o_ref[0] = xn * lax.rsqrt(jnp.mean(xn * xn, axis=-1, keepdims=True) + EPS) * fg_ref[...]


def _out_proj(xs, modsel, ya, yb, yc, yd, wo, fg):
    b, s, _ = xs.shape
    t = ROW_TILE
    skip = CTX_LEN // t
    row = lambda width: pl.BlockSpec((1, t, width), lambda i, j: (i, j + skip, 0))
    return pl.pallas_call(
        _out_kernel,
        grid=(b, s // t - skip),
        in_specs=[
            row(D_MODEL),
            pl.BlockSpec((1, 1, 3, D_MODEL), lambda i, j: (i, 1, 0, 0)),
            row(256), row(256), row(256), row(256),
            pl.BlockSpec(wo.shape, lambda i, j: (0, 0)),
            pl.BlockSpec(fg.shape, lambda i, j: (0, 0)),
        ],
        out_specs=pl.BlockSpec((1, t, D_MODEL), lambda i, j: (i, j, 0)),
        out_shape=jax.ShapeDtypeStruct((b, s - skip * t, D_MODEL), F32),
        compiler_params=pltpu.CompilerParams(
            dimension_semantics=("parallel", "parallel"), vmem_limit_bytes=VMEM_LIMIT),
        name="out_proj_final",
    )(xs, modsel, ya, yb, yc, yd, wo, fg)


def _rope_tables(seq, dim, width):
    half = dim // 2
    quarter = half // 2
    lane = jnp.arange(width) % dim
    freq = ROPE_THETA ** (-(2.0 * (lane % quarter).astype(F32)) / half)
    pos_t = jnp.arange(seq)
    pos = jnp.where(lane[None, :] < half, (pos_t // GRID_W)[:, None], (pos_t % GRID_W)[:, None])
    ang = pos.astype(F32) * freq[None, :]
    cos, sin = jnp.cos(ang), jnp.sin(ang)
    first = (lane % half) < quarter
    s_lo = jnp.where(first[None, :], -sin, 0.0)
    s_hi = jnp.where(first[None, :], 0.0, sin)
    lat = jnp.stack([cos, s_lo, s_hi])
    ctx = jnp.stack([jnp.ones((CTX_LEN, width), F32), jnp.zeros((CTX_LEN, width), F32),
                     jnp.zeros((CTX_LEN, width), F32)])
    return jnp.concatenate([ctx, lat], axis=1)


def _pack_w_in(w):
    offs = [0]
    for n in (256, 256, 256, 256, 256, 128, 128, 256, 256, 256, 256, 128, 128, 256, 256, 16, 16):
        offs.append(offs[-1] + n)
    seg = [w[:, offs[i]:offs[i + 1]] for i in range(17)]
    aq, ak, av, az, bq, bk, bv, bz, cu, cv, cz, dq, dk, dv, dz, drf, drb = seg
    gap = jnp.zeros((w.shape[0], LANES // 2), w.dtype)
    r_gap = jnp.concatenate([drf, drb, gap[:, :LANES // 2 - 2 * GLA_RANK]], axis=1)
    av_sp = jnp.concatenate([av[:, 0:64], r_gap, av[:, 64:128], gap, av[:, 128:192], gap,
                             av[:, 192:256], gap], axis=1)
    bvk = jnp.concatenate([bv[:, :64], bk[:, :64], bv[:, 64:], bk[:, 64:]], axis=1)
    packed = jnp.concatenate([aq, ak, av_sp, az, bq, bvk, bz, cu, cv, cz, dq, dk, dv, dz], axis=1)
    assert packed.shape[1] == P_PACK
    return packed.astype(BF16)


def kernel(x, c, ctx, c_ctx, ada_w, ada_b, norm_g, w_in, da_lq1, da_lk1, da_lq2, da_lk2,
           da_subln_g, gq_qnorm_g, gq_knorm_g, sg_ln_g, sg_ln_b, sg_w, sg_b,
           gla_w2_f, gla_b_f, gla_w2_b, gla_b_b, gla_norm_g, w_out, final_norm_g):
    b, seq, d = x.shape
    assert (seq, d, ctx.shape[1]) == (seq // ROW_TILE * ROW_TILE, D_MODEL, CTX_LEN)

    n_mod = 32
    cpad = jnp.zeros((n_mod, d), F32).at[:b].set(c).at[b].set(c_ctx)
    mod = _modulation(cpad, ada_w, ada_b)

    taba = _rope_tables(seq, DA_QK, 256)
    tabb = _rope_tables(seq, GQ_HD, 256)
    fg = final_norm_g.reshape(1, d)

    stream_in = (ctx, x)
    for i in range(DEPTH):
        lam_init = 0.8 - 0.6 * math.exp(-0.3 * i)
        mod_l = mod[i, :b].reshape(b, 1, 3, d)
        mod_c = jnp.broadcast_to(mod[i, b].reshape(1, 1, 3, d), (b, 1, 3, d))
        modsel = jnp.concatenate([mod_c, mod_l], axis=1)
        w2 = jnp.zeros((LANES, 256), F32)
        w2 = (w2.at[R_LANE:R_LANE + GLA_RANK, 0:128].set(gla_w2_f[i])
              .at[R_LANE + GLA_RANK:R_LANE + 2 * GLA_RANK, 128:256].set(gla_w2_b[i]))
        b2 = jnp.concatenate([gla_b_f[i], gla_b_b[i]]).reshape(1, 256)
        sgb = jnp.repeat(sg_b[i].T, BRANCH_W // SG_GROUPS, axis=1)
        (xs, aq, akt, av, bq, bkt, bv, gate, yc, dqk, dv, dg) = _layer(
            i == 0, stream_in, modsel, norm_g[i].reshape(1, d), _pack_w_in(w_in[i]), taba, tabb,
            jnp.tile(gq_qnorm_g[i], GQ_HEADS).reshape(1, 256),
            jnp.tile(gq_knorm_g[i], GQ_KV_HEADS).reshape(1, 128),
            sg_ln_g[i].reshape(1, 256), sg_ln_b[i].reshape(1, 256),
            sg_w[i].astype(BF16), sgb, w2.astype(BF16), b2)
        lam_vecs = jnp.stack([da_lq1[i], da_lk1[i], da_lq2[i], da_lk2[i]])
        ya = _attention(functools.partial(_diff_attn_kernel, lam_init), aq, akt, av, gate, 0,
                        [lam_vecs, jnp.tile(da_subln_g[i], DA_HEADS).reshape(1, 256)], "diff_attn")
        yb = _attention(_gqa_attn_kernel, bq, bkt, bv, gate, 1, [], "gqa_attn")
        yd = _gla(dqk, dv, dg, gate, jnp.tile(gla_norm_g[i], GLA_HEADS).reshape(1, 256))
        stream_in = (xs, modsel, ya, yb, yc, yd, w_out[i].astype(BF16))
    return _out_proj(*stream_in, fg)
```

```python
import functools
import math

import jax
import jax.numpy as jnp
from jax import lax
from jax.experimental import pallas as pl
from jax.experimental.pallas import tpu as pltpu

F32 = jnp.float32
BF16 = jnp.bfloat16

D_MODEL = 1024
DEPTH = 4
CTX_LEN = 256
GRID_W = 64
BRANCH_W = 256
ROPE_THETA = 10000.0
EPS = 1e-6
DA_HEADS = 4
DA_QK = 32
DA_V = 64
GQ_HEADS = 4
GQ_KV_HEADS = 2
GQ_HD = 64
SG_GROUPS = 4
SG_CHUNK = 128
GLA_HEADS = 4
GLA_DV = 64
GLA_DK = 32
GLA_RANK = 16
GLA_NORMALIZER = 16.0
GLA_CHUNK = 32

LANES = 128
ROW_TILE = 256
KV_CHUNK = 512
GLA_TILE = 128
GLA_BATCH = 2
VMEM_LIMIT = 56 * 1024 * 1024

C_AQ, C_AK, C_AV, C_AZ = 0, 256, 512, 1024
C_BQ, C_BVK, C_BZ = 1280, 1536, 1792
C_CU, C_CV, C_CZ = 2048, 2304, 2560
C_DQK, C_DV, C_DZ = 2816, 3072, 3328
P_PACK = 3584
R_LANE = 64
LOG2E = math.log2(math.e)


def _silu(x):
    return x * (1.0 / (1.0 + jnp.exp(-x)))


def _group_mean_matrix(width, group):
    r = lax.broadcasted_iota(jnp.int32, (width, width), 0) // group
    c = lax.broadcasted_iota(jnp.int32, (width, width), 1) // group
    return jnp.where(r == c, 1.0 / group, 0.0).astype(BF16)


def _split2(x):
    hi = x.astype(BF16)
    return hi, (x - hi.astype(F32)).astype(BF16)


def _group_mean(x, mat):
    return sum(jnp.dot(p, mat, preferred_element_type=F32) for p in _split2(x))


def _rope(x, cos, s_lo, s_hi, shift):
    outs = []
    for c in range(x.shape[1] // LANES):
        sl = slice(c * LANES, (c + 1) * LANES)
        xc = x[:, sl]
        up = pltpu.roll(xc, LANES - shift, axis=1)
        dn = pltpu.roll(xc, shift, axis=1)
        outs.append(xc * cos[:, sl] + up * s_lo[:, sl] + dn * s_hi[:, sl])
    return outs[0] if len(outs) == 1 else jnp.concatenate(outs, axis=1)


def _with_ones(v):
    lane = lax.broadcasted_iota(jnp.int32, v.shape, 1) % LANES
    return jnp.where(lane < LANES // 2, v, 1.0)


def _normalize_by_sum_lanes(o):
    lane = lax.broadcasted_iota(jnp.int32, o.shape, 1)
    return jnp.where(lane < LANES // 2, o * (1.0 / pltpu.roll(o, LANES // 2, axis=1)), 0.0)


def _mod_kernel(c_ref, w_ref, b_ref, o_ref):
    s = _silu(c_ref[...]).astype(BF16)
    o_ref[0] = jnp.dot(s, w_ref[0].astype(BF16), preferred_element_type=F32) + b_ref[0]


def _modulation(cpad, ada_w, ada_b):
    n = cpad.shape[0]
    return pl.pallas_call(
        _mod_kernel,
        grid=(DEPTH, 3),
        in_specs=[
            pl.BlockSpec((n, D_MODEL), lambda i, j: (0, 0)),
            pl.BlockSpec((1, D_MODEL, D_MODEL), lambda i, j: (i, 0, j)),
            pl.BlockSpec((1, 1, D_MODEL), lambda i, j: (i, 0, j)),
        ],
        out_specs=pl.BlockSpec((1, n, D_MODEL), lambda i, j: (i, 0, j)),
        out_shape=jax.ShapeDtypeStruct((DEPTH, n, 3 * D_MODEL), F32),
        compiler_params=pltpu.CompilerParams(vmem_limit_bytes=VMEM_LIMIT),
        name="modulation",
    )(cpad, ada_w, ada_b.reshape(DEPTH, 1, 3 * D_MODEL))


def _layer_kernel(first, *refs):
    if first:
        ctx_ref, x_ref = refs[:2]
        refs = refs[2:]
    else:
        xs_ref, modp_ref, ya_ref, yb_ref, yc_in_ref, yd_ref, wo_ref = refs[:7]
        refs = refs[7:]
    (mod_ref, ng_ref, w_ref, taba_ref, tabb_ref, gqq_ref, gqk_ref, lng_ref, lnb_ref, sgw_ref,
     sgb_ref, w2_ref, b2_ref,
     xs_out_ref, aq_ref, ak_ref, av_ref, bq_ref, bk_ref, bv_ref, gate_ref, yc_ref,
     dqk_ref, dv_ref, dg_ref) = refs

    if first:
        x = jnp.where(pl.program_id(1) == 0, ctx_ref[0], x_ref[0])
    else:
        y_prev = jnp.concatenate([ya_ref[0], yb_ref[0], yc_in_ref[0], yd_ref[0]], axis=1)
        x = xs_ref[0] + modp_ref[0, 0, 2:3, :] * jnp.dot(y_prev, wo_ref[...],
                                                         preferred_element_type=F32)
    xs_out_ref[0] = x
    shift = mod_ref[0, 0, 0:1, :]
    scale = mod_ref[0, 0, 1:2, :]
    y = x * lax.rsqrt(jnp.mean(x * x, axis=-1, keepdims=True) + EPS) * ng_ref[...]
    hb = (y * (1.0 + scale) + shift).astype(BF16)

    def proj(lo, width):
        return jnp.dot(hb, w_ref[:, lo:lo + width], preferred_element_type=F32)

    m64 = _group_mean_matrix(BRANCH_W, GQ_HD)
    half_lane = lax.broadcasted_iota(jnp.int32, (ROW_TILE, LANES), 1) < LANES // 2
    p_bq = proj(C_BQ, 256)
    p_bvk = proj(C_BVK, 256)
    p_cv = proj(C_CV, 256)
    p_av = proj(C_AV, 512)

    bq_sq = _split2(p_bq * p_bq)
    bk = jnp.where(half_lane, pltpu.roll(p_bvk[:, :LANES], LANES // 2, axis=1), p_bvk[:, LANES:])
    bk_sq = _split2(bk * bk)
    mu = jnp.mean(p_cv, axis=-1, keepdims=True)
    cen = p_cv - mu
    var = jnp.mean(cen * cen, axis=-1, keepdims=True)
    vn = (cen * lax.rsqrt(var + EPS) * lng_ref[...] + lnb_ref[...]).astype(BF16)
    r = p_av[:, :LANES].astype(BF16)
    av_ref[0, 0] = _with_ones(p_av).T.astype(BF16)
    bv_ref[0, 0] = _with_ones(p_bvk).T.astype(BF16)

    ca, sa_lo, sa_hi = taba_ref[0], taba_ref[1], taba_ref[2]
    aq = _rope(proj(C_AQ, 256), ca, sa_lo, sa_hi, DA_QK // 4) * (DA_QK ** -0.5 * LOG2E)
    aq_ref[0] = aq.T.astype(BF16)
    ak = _rope(proj(C_AK, 256), ca, sa_lo, sa_hi, DA_QK // 4)
    ak_ref[0] = ak.astype(BF16)

    bq_ms = sum(jnp.dot(p, m64, preferred_element_type=F32) for p in bq_sq)
    bk_ms = sum(jnp.dot(p, m64[:LANES, :LANES], preferred_element_type=F32) for p in bk_sq)
    gl = jnp.dot(r, w2_ref[...], preferred_element_type=F32) + b2_ref[...]

    gate_ref[0, :, 0:256] = _silu(proj(C_AZ, 256)).astype(BF16)
    gate_ref[0, :, 256:512] = _silu(proj(C_BZ, 256)).astype(BF16)
    gate_ref[0, :, 512:768] = _silu(proj(C_DZ, 256)).astype(BF16)

    lane_group = lax.broadcasted_iota(jnp.int32, (SG_CHUNK, BRANCH_W), 1) // (BRANCH_W // SG_GROUPS)
    mixed = []
    for n in range(ROW_TILE // SG_CHUNK):
        vchunk = vn[n * SG_CHUNK:(n + 1) * SG_CHUNK, :]
        acc = jnp.zeros((SG_CHUNK, BRANCH_W), F32)
        for g in range(SG_GROUPS):
            acc = jnp.where(lane_group == g,
                            jnp.dot(sgw_ref[g], vchunk, preferred_element_type=F32), acc)
        mixed.append(acc + sgb_ref[...])

    cb, sb_lo, sb_hi = tabb_ref[0], tabb_ref[1], tabb_ref[2]
    bq = p_bq * lax.rsqrt(bq_ms + EPS) * gqq_ref[...]
    bq_ref[0] = (_rope(bq, cb, sb_lo, sb_hi, GQ_HD // 4) * (GQ_HD ** -0.5 * LOG2E)).T.astype(BF16)
    bk = bk * lax.rsqrt(bk_ms + EPS) * gqk_ref[...]
    bk = _rope(bk, cb[:, :LANES], sb_lo[:, :LANES], sb_hi[:, :LANES], GQ_HD // 4)
    bk_ref[0] = bk.astype(BF16)

    yc = proj(C_CU, 256) * jnp.concatenate(mixed, axis=0) * _silu(proj(C_CZ, 256))
    yc_ref[0] = yc.astype(BF16)

    q_scale = jnp.where(lax.broadcasted_iota(jnp.int32, (1, 256), 1) < 128, GLA_DK ** -0.5, 1.0)
    dqk_ref[0] = proj(C_DQK, 256) * q_scale
    dv_ref[0] = proj(C_DV, 256).astype(BF16)
    log_sig = jnp.minimum(gl, 0.0) - jnp.log(1.0 + jnp.exp(-jnp.abs(gl)))
    dg_ref[0] = log_sig * (1.0 / GLA_NORMALIZER)


def _layer(first, stream_in, modsel, ng, wp, taba, tabb, gqq, gqk, lng, lnb, sgw, sgb, w2, b2):
    t = ROW_TILE
    b = stream_in[0].shape[0]
    s = CTX_LEN + stream_in[1].shape[1] if first else stream_in[0].shape[1]
    row = lambda width: pl.BlockSpec((1, t, width), lambda i, j: (i, j, 0))
    colT = lambda height: pl.BlockSpec((1, height, t), lambda i, j: (i, 0, j))
    full = lambda a: pl.BlockSpec(a.shape, lambda i, j: (0,) * a.ndim)
    tab = pl.BlockSpec((3, t, 256), lambda i, j: (0, j, 0))
    mods = pl.BlockSpec((1, 1, 3, D_MODEL), lambda i, j: (i, jnp.minimum(j, 1), 0, 0))
    shp = lambda width, dt: jax.ShapeDtypeStruct((b, s, width), dt)
    shpT = lambda height: jax.ShapeDtypeStruct((b, height, s), BF16)
    tileT = lambda height: pl.BlockSpec((1, 1, height, t), lambda i, j: (i, j, 0, 0))
    if first:
        stream_specs = [pl.BlockSpec((1, t, D_MODEL), lambda i, j: (i, 0, 0)),
                        pl.BlockSpec((1, t, D_MODEL), lambda i, j: (i, jnp.maximum(j - 1, 0), 0))]
    else:
        stream_specs = [row(D_MODEL), mods, row(256), row(256), row(256), row(256),
                        full(stream_in[6])]
    return pl.pallas_call(
        functools.partial(_layer_kernel, first),
        grid=(b, s // t),
        in_specs=stream_specs + [
            mods, full(ng), full(wp), tab, tab, full(gqq), full(gqk), full(lng), full(lnb),
            full(sgw), full(sgb), full(w2), full(b2),
        ],
        out_specs=[row(D_MODEL), colT(256), row(256), tileT(512), colT(256), row(128), tileT(256),
                   row(768), row(256), row(256), row(256), row(256)],
        out_shape=[shp(D_MODEL, F32),
                   shpT(256), shp(256, BF16), jax.ShapeDtypeStruct((b, s // t, 512, t), BF16),
                   shpT(256), shp(128, BF16), jax.ShapeDtypeStruct((b, s // t, 256, t), BF16),
                   shp(768, BF16), shp(256, BF16), shp(256, F32), shp(256, BF16), shp(256, F32)],
        compiler_params=pltpu.CompilerParams(
            dimension_semantics=("parallel", "arbitrary"), vmem_limit_bytes=VMEM_LIMIT),
        name="layer_first" if first else "layer",
    )(*stream_in, modsel, ng, wp, taba, tabb, gqq, gqk, lng, lnb, sgw, sgb, w2, b2)


def _max_over_rows(s):
    m = s
    for part in (256, 32):
        if m.shape[0] > part and m.shape[0] % part == 0:
            m = jnp.max(m.reshape(m.shape[0] // part, part, m.shape[1]), axis=0)
    return jnp.max(m, axis=0, keepdims=True)


def _online_softmax_pv(latent, qts, k_ref, vt_ref, v_row_of_job, acc_ref, m_ref, s_ref):
    n_jobs = len(qts)
    tiles_per_chunk = KV_CHUNK // ROW_TILE
    n_chunks = (k_ref.shape[1] - CTX_LEN) // KV_CHUNK

    def qk(k_rows, j):
        return jnp.dot(k_rows, qts[j], preferred_element_type=F32)

    def softmax_pv(s, vt, j, first):
        m_new = _max_over_rows(s)
        if not first:
            m_old = m_ref[j]
            m_new = jnp.maximum(m_old, m_new)
            alpha = jnp.exp2(m_old - m_new)
        m_ref[j] = m_new
        upd = jnp.dot(vt, jnp.exp2(s - m_new).astype(BF16), preferred_element_type=F32)
        acc_ref[j] = upd if first else acc_ref[j] * alpha + upd

    def vt_tile(tile, j):
        return vt_ref[0, tile, v_row_of_job[j]:v_row_of_job[j] + LANES, :]

    def vt_chunk(c, j):
        t0 = CTX_LEN // ROW_TILE + c * tiles_per_chunk
        return jnp.concatenate([vt_tile(t0 + i, j) for i in range(tiles_per_chunk)], axis=1)

    def k_chunk(c):
        return k_ref[0, CTX_LEN + c * KV_CHUNK:CTX_LEN + (c + 1) * KV_CHUNK, :]

    k_ctx = k_ref[0, 0:CTX_LEN, :]

    @pl.when(jnp.logical_not(latent))
    def _():
        s = [qk(k_ctx, j) for j in range(n_jobs)]
        for j in range(n_jobs):
            softmax_pv(s[j], vt_tile(0, j), j, True)

    @pl.when(latent)
    def _():
        s = [qk(k_ctx, j) for j in range(n_jobs)]
        k_next = k_chunk(0)
        s_ref[0, 0] = qk(k_next, 0)
        for j in range(n_jobs):
            softmax_pv(s[j], vt_tile(0, j), j, True)
            if j + 1 < n_jobs:
                s_ref[0, j + 1] = qk(k_next, j + 1)

        for c in range(n_chunks):
            slot = c % 2
            more = c + 1 < n_chunks
            if more:
                k_next = k_chunk(c + 1)
                s_ref[1 - slot, 0] = qk(k_next, 0)
            for j in range(n_jobs):
                softmax_pv(s_ref[slot, j], vt_chunk(c, j), j, False)
                if more and j + 1 < n_jobs:
                    s_ref[1 - slot, j + 1] = qk(k_next, j + 1)

    outs = []
    for j in range(n_jobs):
        acc = acc_ref[j]
        outs.append(acc[:LANES // 2, :] * (1.0 / acc[LANES // 2:LANES // 2 + 1, :]))
    return outs


def _rows_at(x, row0, total):
    parts = []
    if row0:
        parts.append(jnp.zeros((row0, x.shape[1]), x.dtype))
    parts.append(x)
    if total - row0 - x.shape[0]:
        parts.append(jnp.zeros((total - row0 - x.shape[0], x.shape[1]), x.dtype))
    return jnp.concatenate(parts, axis=0)


def _diff_attn_kernel(lam_init, qt_ref, k_ref, vt_ref, gate_ref, lam_ref, sg_ref, o_ref,
                      acc_ref, m_ref, s_ref):
    lq1, lk1, lq2, lk2 = lam_ref[0:1, :], lam_ref[1:2, :], lam_ref[2:3, :], lam_ref[3:4, :]
    lam = (jnp.exp(jnp.sum(lq1 * lk1, axis=-1, keepdims=True))
           - jnp.exp(jnp.sum(lq2 * lk2, axis=-1, keepdims=True)) + lam_init)
    qts = [_rows_at(qt_ref[0, job * DA_QK:(job + 1) * DA_QK, :], job * DA_QK, BRANCH_W)
           for job in range(2 * DA_HEADS)]
    o = _online_softmax_pv(pl.program_id(1) > 0, qts, k_ref, vt_ref,
                           [(job // 2) * LANES for job in range(2 * DA_HEADS)], acc_ref, m_ref, s_ref)
    yt = jnp.concatenate([o[2 * h] - lam * o[2 * h + 1] for h in range(DA_HEADS)], axis=0)
    y = yt.T
    ms = _group_mean(y * y, _group_mean_matrix(BRANCH_W, DA_V))
    y = y * lax.rsqrt(ms + EPS) * sg_ref[...] * (1.0 - lam_init)
    o_ref[0] = (y * gate_ref[0].astype(F32)).astype(BF16)


def _gqa_attn_kernel(qt_ref, k_ref, vt_ref, gate_ref, o_ref, acc_ref, m_ref, s_ref):
    grp = GQ_HEADS // GQ_KV_HEADS
    qts = [_rows_at(qt_ref[0, h * GQ_HD:(h + 1) * GQ_HD, :], (h // grp) * GQ_HD, GQ_KV_HEADS * GQ_HD)
           for h in range(GQ_HEADS)]
    o = _online_softmax_pv(pl.program_id(1) > 0, qts, k_ref, vt_ref,
                           [(h // grp) * LANES for h in range(GQ_HEADS)], acc_ref, m_ref, s_ref)
    y = jnp.concatenate(o, axis=0).T
    o_ref[0] = (y * gate_ref[0].astype(F32)).astype(BF16)


def _attention(kernel, n_jobs, qt, k, vt, gate, gate_block, extra, name):
    b, s, _ = k.shape
    t = ROW_TILE
    full = lambda a: pl.BlockSpec(a.shape, lambda i, j: (0,) * a.ndim)
    return pl.pallas_call(
        kernel,
        grid=(b, s // t),
        in_specs=[
            pl.BlockSpec((1, qt.shape[1], t), lambda i, j: (i, 0, j)),
            pl.BlockSpec((1, s, k.shape[2]), lambda i, j: (i, 0, 0)),
            pl.BlockSpec((1,) + vt.shape[1:], lambda i, j: (i, 0, 0, 0)),
            pl.BlockSpec((1, t, 256), lambda i, j: (i, j, gate_block)),
        ] + [full(a) for a in extra],
        out_specs=pl.BlockSpec((1, t, 256), lambda i, j: (i, j, 0)),
        out_shape=jax.ShapeDtypeStruct((b, s, 256), BF16),
        scratch_shapes=[pltpu.VMEM((n_jobs, LANES, t), F32), pltpu.VMEM((n_jobs, 1, t), F32),
                        pltpu.VMEM((2, n_jobs, KV_CHUNK, t), F32)],
        compiler_params=pltpu.CompilerParams(
            dimension_semantics=("parallel", "arbitrary"), vmem_limit_bytes=VMEM_LIMIT),
        name=name,
    )(qt, k, vt, gate, *extra)


def _split3(x):
    hi = x.astype(BF16)
    r = x - hi.astype(F32)
    mid = r.astype(BF16)
    lo = (r - mid.astype(F32)).astype(BF16)
    return hi, mid, lo


def _gla_kernel(qk_ref, v_ref, g_ref, gate_ref, ng_ref, o_ref, acc_ref, st_ref):
    n_tiles = qk_ref.shape[1] // GLA_TILE
    ctx_tiles = CTX_LEN // GLA_TILE
    per = GLA_TILE // GLA_CHUNK
    ri = lax.broadcasted_iota(jnp.int32, (GLA_TILE, GLA_TILE), 0)
    ci = lax.broadcasted_iota(jnp.int32, (GLA_TILE, GLA_TILE), 1)
    same_chunk = (ri // GLA_CHUNK) == (ci // GLA_CHUNK)
    chunk_of_row = ri // GLA_CHUNK
    head_of_k = lax.broadcasted_iota(jnp.int32, (GLA_TILE, GLA_HEADS * GLA_DK), 1) // GLA_DK
    head_of_v = lax.broadcasted_iota(jnp.int32, (GLA_TILE, BRANCH_W), 1) // GLA_DV
    st_mask = ((lax.broadcasted_iota(jnp.int32, (BRANCH_W, GLA_HEADS * GLA_DK), 0) // GLA_DV)
               == (lax.broadcasted_iota(jnp.int32, (BRANCH_W, GLA_HEADS * GLA_DK), 1) // GLA_DK))
    m64 = _group_mean_matrix(BRANCH_W, GLA_DV)
    tris = [jnp.where(same_chunk & (ci <= ri), 1.0, 0.0), jnp.where(same_chunk & (ci >= ri), 1.0, 0.0)]
    tris_b = [t.astype(BF16) for t in tris]
    tris4 = [jnp.concatenate([t] * GLA_HEADS, axis=1) for t in tris]

    nt_dims = (((1,), (1,)), ((), ()))
    tn_dims = (((0,), (0,)), ((), ()))

    def scan_step(i, carry):
        rev = jnp.where(i < ctx_tiles, ctx_tiles - 1 - i, n_tiles + ctx_tiles - 1 - i)
        chains = [(bi, d) for bi in range(GLA_BATCH) for d in (0, 1)]
        nc = len(chains)
        rows = [pl.ds(pl.multiple_of((rev if d else i) * GLA_TILE, GLA_TILE), GLA_TILE)
                for _, d in chains]
        ends = [[c * GLA_CHUNK if d else (c + 1) * GLA_CHUNK - 1 for c in range(per)] for _, d in chains]

        cum = []
        for n, (bi, d) in enumerate(chains):
            parts = _split3(g_ref[bi, rows[n], 128 * d:128 * d + 128])
            cum.append(sum(jnp.dot(tris_b[d], p, preferred_element_type=F32) for p in parts))

        qe, kd, kl, vb = [], [], [], []
        for n, (bi, d) in enumerate(chains):
            q = qk_ref[bi, rows[n], 0:128]
            k = qk_ref[bi, rows[n], 128:256]
            cum_last = jnp.concatenate(
                [jnp.broadcast_to(cum[n][e:e + 1, :], (GLA_CHUNK, 128)) for e in ends[n]], axis=0)
            qe.append(q * jnp.exp(cum[n]))
            kd.append(k * jnp.exp(-cum[n]))
            kl.append(k * jnp.exp(cum_last - cum[n]))
            vb.append(v_ref[bi, rows[n], :])

        qeb = [x.astype(BF16) for x in qe]
        att = []
        for n, (bi, d) in enumerate(chains):
            kd_heads = jnp.concatenate(
                [jnp.where(head_of_k == h, kd[n], 0.0).astype(BF16) for h in range(GLA_HEADS)], axis=0)
            a = lax.dot_general(qeb[n], kd_heads, nt_dims, preferred_element_type=F32)
            att.append((a * tris4[d]).astype(BF16))
        o = []
        for n in range(nc):
            v_heads = jnp.concatenate(
                [jnp.where(head_of_v == h, vb[n], jnp.zeros_like(vb[n])) for h in range(GLA_HEADS)], axis=0)
            o.append(jnp.dot(att[n], v_heads, preferred_element_type=F32))

        upd = []
        for n in range(nc):
            kl_chunks = jnp.concatenate(
                [jnp.where(chunk_of_row == c, kl[n], 0.0).astype(BF16) for c in range(per)], axis=1)
            upd.append(lax.dot_general(vb[n], kl_chunks, tn_dims, preferred_element_type=F32))
        seen = [[None] * per for _ in range(nc)]
        for n, (bi, d) in enumerate(chains):
            st = st_ref[n]
            for c in (range(per - 1, -1, -1) if d else range(per)):
                seen[n][c] = st.astype(BF16)
                decay = jnp.exp(cum[n][ends[n][c]:ends[n][c] + 1, :])
                st = st * decay + jnp.where(st_mask, upd[n][:, c * 128:(c + 1) * 128], 0.0)
            st_ref[n] = st
        for n in range(nc):
            qe_chunks = jnp.concatenate(
                [jnp.where(chunk_of_row == c, qe[n], 0.0).astype(BF16) for c in range(per)], axis=1)
            o_inter = lax.dot_general(qe_chunks, jnp.concatenate(seen[n], axis=1), nt_dims,
                                      preferred_element_type=F32)
            acc_ref[n, rows[n], :] = o[n] + o_inter
        return carry

    st_ref[...] = jnp.zeros_like(st_ref)
    lax.fori_loop(0, n_tiles, scan_step, 0)

    def finish_step(i, carry):
        rows = pl.ds(pl.multiple_of(i * ROW_TILE, ROW_TILE), ROW_TILE)
        tot = [acc_ref[2 * bi, rows, :] + acc_ref[2 * bi + 1, rows, :] for bi in range(GLA_BATCH)]
        sq = [_split2(t * t) for t in tot]
        ms = [sum(jnp.dot(p, m64, preferred_element_type=F32) for p in s) for s in sq]
        for bi in range(GLA_BATCH):
            y = tot[bi] * lax.rsqrt(ms[bi] + EPS) * ng_ref[...]
            o_ref[bi, rows, :] = (y * gate_ref[bi, rows, :].astype(F32)).astype(BF16)
        return carry

    lax.fori_loop(0, qk_ref.shape[1] // ROW_TILE, finish_step, 0)


def _gla(dqk, dv, dg, gate, ng):
    b, s, _ = dqk.shape
    nb = GLA_BATCH
    blk = lambda width, cb: pl.BlockSpec((nb, s, width), lambda i: (i, 0, cb))
    return pl.pallas_call(
        _gla_kernel,
        grid=(b // nb,),
        in_specs=[blk(256, 0), blk(256, 0), blk(256, 0), blk(256, 2),
                  pl.BlockSpec(ng.shape, lambda i: (0, 0))],
        out_specs=blk(256, 0),
        out_shape=jax.ShapeDtypeStruct((b, s, 256), BF16),
        scratch_shapes=[pltpu.VMEM((2 * nb, s, BRANCH_W), F32),
                        pltpu.VMEM((2 * nb, BRANCH_W, GLA_HEADS * GLA_DK), F32)],
        compiler_params=pltpu.CompilerParams(
            dimension_semantics=("parallel",), vmem_limit_bytes=VMEM_LIMIT),
        name="gla",
    )(dqk, dv, dg, gate, ng)


def _out_kernel(x_ref, mod_ref, ya_ref, yb_ref, yc_ref, yd_ref, w_ref, fg_ref, o_ref):
    y = jnp.concatenate([ya_ref[0], yb_ref[0], yc_ref[0], yd_ref[0]], axis=1)
    upd = jnp.dot(y, w_ref[...], preferred_element_type=F32)
    xn = x_ref[0] + mod_ref[0, 0, 2:3, :] * upd
    o_ref[0] = xn * lax.rsqrt(jnp.mean(xn * xn, axis=-1, keepdims=True) + EPS) * fg_ref[...]


def _out_proj(xs, modsel, ya, yb, yc, yd, wo, fg):
    b, s, _ = xs.shape
    t = ROW_TILE
    skip = CTX_LEN // t
    row = lambda width: pl.BlockSpec((1, t, width), lambda i, j: (i, j + skip, 0))
    return pl.pallas_call(
        _out_kernel,
        grid=(b, s // t - skip),
        in_specs=[
            row(D_MODEL),
            pl.BlockSpec((1, 1, 3, D_MODEL), lambda i, j: (i, 1, 0, 0)),
            row(256), row(256), row(256), row(256),
            pl.BlockSpec(wo.shape, lambda i, j: (0, 0)),
            pl.BlockSpec(fg.shape, lambda i, j: (0, 0)),
        ],
        out_specs=pl.BlockSpec((1, t, D_MODEL), lambda i, j: (i, j, 0)),
        out_shape=jax.ShapeDtypeStruct((b, s - skip * t, D_MODEL), F32),
        compiler_params=pltpu.CompilerParams(
            dimension_semantics=("parallel", "parallel"), vmem_limit_bytes=VMEM_LIMIT),
        name="out_proj_final",
    )(xs, modsel, ya, yb, yc, yd, wo, fg)


def _rope_tables(seq, dim, width):
    half = dim // 2
    quarter = half // 2
    lane = jnp.arange(width) % dim
    freq = ROPE_THETA ** (-(2.0 * (lane % quarter).astype(F32)) / half)
    pos_t = jnp.arange(seq)
    pos = jnp.where(lane[None, :] < half, (pos_t // GRID_W)[:, None], (pos_t % GRID_W)[:, None])
    ang = pos.astype(F32) * freq[None, :]
    cos, sin = jnp.cos(ang), jnp.sin(ang)
    first = (lane % half) < quarter
    s_lo = jnp.where(first[None, :], -sin, 0.0)
    s_hi = jnp.where(first[None, :], 0.0, sin)
    lat = jnp.stack([cos, s_lo, s_hi])
    ctx = jnp.stack([jnp.ones((CTX_LEN, width), F32), jnp.zeros((CTX_LEN, width), F32),
                     jnp.zeros((CTX_LEN, width), F32)])
    return jnp.concatenate([ctx, lat], axis=1)


def _pack_w_in(w):
    offs = [0]
    for n in (256, 256, 256, 256, 256, 128, 128, 256, 256, 256, 256, 128, 128, 256, 256, 16, 16):
        offs.append(offs[-1] + n)
    seg = [w[:, offs[i]:offs[i + 1]] for i in range(17)]
    aq, ak, av, az, bq, bk, bv, bz, cu, cv, cz, dq, dk, dv, dz, drf, drb = seg
    gap = jnp.zeros((w.shape[0], LANES // 2), w.dtype)
    r_gap = jnp.concatenate([drf, drb, gap[:, :LANES // 2 - 2 * GLA_RANK]], axis=1)
    av_sp = jnp.concatenate([av[:, 0:64], r_gap, av[:, 64:128], gap, av[:, 128:192], gap,
                             av[:, 192:256], gap], axis=1)
    bvk = jnp.concatenate([bv[:, :64], bk[:, :64], bv[:, 64:], bk[:, 64:]], axis=1)
    packed = jnp.concatenate([aq, ak, av_sp, az, bq, bvk, bz, cu, cv, cz, dq, dk, dv, dz], axis=1)
    assert packed.shape[1] == P_PACK
    return packed.astype(BF16)


def kernel(x, c, ctx, c_ctx, ada_w, ada_b, norm_g, w_in, da_lq1, da_lk1, da_lq2, da_lk2,
           da_subln_g, gq_qnorm_g, gq_knorm_g, sg_ln_g, sg_ln_b, sg_w, sg_b,
           gla_w2_f, gla_b_f, gla_w2_b, gla_b_b, gla_norm_g, w_out, final_norm_g):
    b, seq, d = x.shape
    assert (seq, d, ctx.shape[1]) == (seq // ROW_TILE * ROW_TILE, D_MODEL, CTX_LEN)

    n_mod = 32
    cpad = jnp.zeros((n_mod, d), F32).at[:b].set(c).at[b].set(c_ctx)
    mod = _modulation(cpad, ada_w, ada_b)

    taba = _rope_tables(seq, DA_QK, 256)
    tabb = _rope_tables(seq, GQ_HD, 256)
    fg = final_norm_g.reshape(1, d)

    stream_in = (ctx, x)
    for i in range(DEPTH):
        lam_init = 0.8 - 0.6 * math.exp(-0.3 * i)
        mod_l = mod[i, :b].reshape(b, 1, 3, d)
        mod_c = jnp.broadcast_to(mod[i, b].reshape(1, 1, 3, d), (b, 1, 3, d))
        modsel = jnp.concatenate([mod_c, mod_l], axis=1)
        w2 = jnp.zeros((LANES, 256), F32)
        w2 = (w2.at[R_LANE:R_LANE + GLA_RANK, 0:128].set(gla_w2_f[i])
              .at[R_LANE + GLA_RANK:R_LANE + 2 * GLA_RANK, 128:256].set(gla_w2_b[i]))
        b2 = jnp.concatenate([gla_b_f[i], gla_b_b[i]]).reshape(1, 256)
        sgb = jnp.repeat(sg_b[i].T, BRANCH_W // SG_GROUPS, axis=1)
        (xs, aqt, ak, avt, bqt, bk, bvt, gate, yc, dqk, dv, dg) = _layer(
            i == 0, stream_in, modsel, norm_g[i].reshape(1, d), _pack_w_in(w_in[i]), taba, tabb,
            jnp.tile(gq_qnorm_g[i], GQ_HEADS).reshape(1, 256),
            jnp.tile(gq_knorm_g[i], GQ_KV_HEADS).reshape(1, 128),
            sg_ln_g[i].reshape(1, 256), sg_ln_b[i].reshape(1, 256),
            sg_w[i].astype(BF16), sgb, w2.astype(BF16), b2)
        lam_vecs = jnp.stack([da_lq1[i], da_lk1[i], da_lq2[i], da_lk2[i]])
        ya = _attention(functools.partial(_diff_attn_kernel, lam_init), 2 * DA_HEADS, aqt, ak, avt, gate, 0,
                        [lam_vecs, jnp.tile(da_subln_g[i], DA_HEADS).reshape(1, 256)], "diff_attn")
        yb = _attention(_gqa_attn_kernel, GQ_HEADS, bqt, bk, bvt, gate, 1, [], "gqa_attn")
        yd = _gla(dqk, dv, dg, gate, jnp.tile(gla_norm_g[i], GLA_HEADS).reshape(1, 256))
        stream_in = (xs, modsel, ya, yb, yc, yd, w_out[i].astype(BF16))
    return _out_proj(*stream_in, fg)
```

```python
import functools
import math

import jax
import jax.numpy as jnp
import numpy as np
from jax import lax
from jax.experimental import pallas as pl
from jax.experimental.pallas import tpu as pltpu

F32 = jnp.float32
BF16 = jnp.bfloat16

D_MODEL = 1024
DEPTH = 4
CTX_LEN = 256
GRID_W = 64
BRANCH_W = 256
ROPE_THETA = 10000.0
EPS = 1e-6
DA_HEADS = 4
DA_QK = 32
DA_V = 64
GQ_HEADS = 4
GQ_KV_HEADS = 2
GQ_HD = 64
SG_GROUPS = 4
SG_CHUNK = 128
GLA_HEADS = 4
GLA_DV = 64
GLA_DK = 32
GLA_RANK = 16
GLA_NORMALIZER = 16.0
GLA_CHUNK = 32

LANES = 128
ROW_TILE = 256
KV_CHUNK = 512
GLA_TILE = 128
GLA_BATCH = 2
GLA_STEP_TILES = 3
VMEM_LIMIT = 56 * 1024 * 1024

C_AQ, C_AK, C_AV, C_AZ = 0, 256, 512, 1024
C_BQ, C_BVK, C_BZ = 1280, 1536, 1792
C_CU, C_CV, C_CZ = 2048, 2304, 2560
C_DQK, C_DV, C_DZ = 2816, 3072, 3328
P_PACK = 3584
R_LANE = 64
LOG2E = math.log2(math.e)


def _silu(x):
    return x * (1.0 / (1.0 + jnp.exp(-x)))


def _group_mean_matrix(width, group):
    r = lax.broadcasted_iota(jnp.int32, (width, width), 0) // group
    c = lax.broadcasted_iota(jnp.int32, (width, width), 1) // group
    return jnp.where(r == c, 1.0 / group, 0.0).astype(BF16)


def _split2(x):
    hi = x.astype(BF16)
    return hi, (x - hi.astype(F32)).astype(BF16)


def _group_mean(x, mat):
    return sum(jnp.dot(p, mat, preferred_element_type=F32) for p in _split2(x))


def _rope(x, cos, s_lo, s_hi, shift):
    outs = []
    for c in range(x.shape[1] // LANES):
        sl = slice(c * LANES, (c + 1) * LANES)
        xc = x[:, sl]
        up = pltpu.roll(xc, LANES - shift, axis=1)
        dn = pltpu.roll(xc, shift, axis=1)
        outs.append(xc * cos[:, sl] + up * s_lo[:, sl] + dn * s_hi[:, sl])
    return outs[0] if len(outs) == 1 else jnp.concatenate(outs, axis=1)


def _with_ones(v):
    lane = lax.broadcasted_iota(jnp.int32, v.shape, 1) % LANES
    return jnp.where(lane < LANES // 2, v, 1.0)


def _normalize_by_sum_lanes(o):
    lane = lax.broadcasted_iota(jnp.int32, o.shape, 1)
    return jnp.where(lane < LANES // 2, o * (1.0 / pltpu.roll(o, LANES // 2, axis=1)), 0.0)


def _mod_kernel(c_ref, w_ref, b_ref, o_ref):
    s = _silu(c_ref[...]).astype(BF16)
    o_ref[0] = jnp.dot(s, w_ref[0].astype(BF16), preferred_element_type=F32) + b_ref[0]


def _modulation(cpad, ada_w, ada_b):
    n = cpad.shape[0]
    return pl.pallas_call(
        _mod_kernel,
        grid=(DEPTH, 3),
        in_specs=[
            pl.BlockSpec((n, D_MODEL), lambda i, j: (0, 0)),
            pl.BlockSpec((1, D_MODEL, D_MODEL), lambda i, j: (i, 0, j)),
            pl.BlockSpec((1, 1, D_MODEL), lambda i, j: (i, 0, j)),
        ],
        out_specs=pl.BlockSpec((1, n, D_MODEL), lambda i, j: (i, 0, j)),
        out_shape=jax.ShapeDtypeStruct((DEPTH, n, 3 * D_MODEL), F32),
        compiler_params=pltpu.CompilerParams(vmem_limit_bytes=VMEM_LIMIT),
        name="modulation",
    )(cpad, ada_w, ada_b.reshape(DEPTH, 1, 3 * D_MODEL))


def _layer_kernel(first, *refs):
    if first:
        ctx_ref, x_ref = refs[:2]
        refs = refs[2:]
    else:
        xs_ref, modp_ref, ya_ref, yb_ref, yc_in_ref, yd_ref, wo_ref = refs[:7]
        refs = refs[7:]
    (mod_ref, ng_ref, w_ref, taba_ref, tabb_ref, gqq_ref, gqk_ref, lng_ref, lnb_ref, sgw_ref,
     sgb_ref, w2_ref, b2_ref,
     xs_out_ref, aq_ref, ak_ref, av_ref, bq_ref, bk_ref, bv_ref, gate_ref, yc_ref,
     dqk_ref, dv_ref, dg_ref) = refs

    if first:
        x = jnp.where(pl.program_id(1) == 0, ctx_ref[0], x_ref[0])
    else:
        y_prev = jnp.concatenate([ya_ref[0], yb_ref[0], yc_in_ref[0], yd_ref[0]], axis=1)
        x = xs_ref[0] + modp_ref[0, 0, 2:3, :] * jnp.dot(y_prev, wo_ref[...],
                                                         preferred_element_type=F32)
    xs_out_ref[0] = x
    shift = mod_ref[0, 0, 0:1, :]
    scale = mod_ref[0, 0, 1:2, :]
    y = x * lax.rsqrt(jnp.mean(x * x, axis=-1, keepdims=True) + EPS) * ng_ref[...]
    hb = (y * (1.0 + scale) + shift).astype(BF16)

    def proj(lo, width):
        return jnp.dot(hb, w_ref[:, lo:lo + width], preferred_element_type=F32)

    m64 = _group_mean_matrix(BRANCH_W, GQ_HD)
    half_lane = lax.broadcasted_iota(jnp.int32, (ROW_TILE, LANES), 1) < LANES // 2
    p_bq = proj(C_BQ, 256)
    p_bvk = proj(C_BVK, 256)
    p_cv = proj(C_CV, 256)
    p_av = proj(C_AV, 512)

    bq_sq = _split2(p_bq * p_bq)
    bk = jnp.where(half_lane, pltpu.roll(p_bvk[:, :LANES], LANES // 2, axis=1), p_bvk[:, LANES:])
    bk_sq = _split2(bk * bk)
    mu = jnp.mean(p_cv, axis=-1, keepdims=True)
    cen = p_cv - mu
    var = jnp.mean(cen * cen, axis=-1, keepdims=True)
    vn = (cen * lax.rsqrt(var + EPS) * lng_ref[...] + lnb_ref[...]).astype(BF16)
    r = p_av[:, :LANES].astype(BF16)
    av_ref[0, 0] = _with_ones(p_av).T.astype(BF16)
    bv_ref[0, 0] = _with_ones(p_bvk).T.astype(BF16)

    ca, sa_lo, sa_hi = taba_ref[0], taba_ref[1], taba_ref[2]
    aq = _rope(proj(C_AQ, 256), ca, sa_lo, sa_hi, DA_QK // 4) * (DA_QK ** -0.5 * LOG2E)
    aq_ref[0] = aq.T.astype(BF16)
    ak = _rope(proj(C_AK, 256), ca, sa_lo, sa_hi, DA_QK // 4)
    ak_ref[0] = ak.astype(BF16)

    bq_ms = sum(jnp.dot(p, m64, preferred_element_type=F32) for p in bq_sq)
    bk_ms = sum(jnp.dot(p, m64[:LANES, :LANES], preferred_element_type=F32) for p in bk_sq)
    gl = jnp.dot(r, w2_ref[...], preferred_element_type=F32) + b2_ref[...]

    gate_ref[0, :, 0:256] = _silu(proj(C_AZ, 256)).astype(BF16)
    gate_ref[0, :, 256:512] = _silu(proj(C_BZ, 256)).astype(BF16)
    gate_ref[0, :, 512:768] = _silu(proj(C_DZ, 256)).astype(BF16)

    lane_group = lax.broadcasted_iota(jnp.int32, (SG_CHUNK, BRANCH_W), 1) // (BRANCH_W // SG_GROUPS)
    mixed = []
    for n in range(ROW_TILE // SG_CHUNK):
        vchunk = vn[n * SG_CHUNK:(n + 1) * SG_CHUNK, :]
        acc = jnp.zeros((SG_CHUNK, BRANCH_W), F32)
        for g in range(SG_GROUPS):
            acc = jnp.where(lane_group == g,
                            jnp.dot(sgw_ref[g], vchunk, preferred_element_type=F32), acc)
        mixed.append(acc + sgb_ref[...])

    cb, sb_lo, sb_hi = tabb_ref[0], tabb_ref[1], tabb_ref[2]
    bq = p_bq * lax.rsqrt(bq_ms + EPS) * gqq_ref[...]
    bq_ref[0] = (_rope(bq, cb, sb_lo, sb_hi, GQ_HD // 4) * (GQ_HD ** -0.5 * LOG2E)).T.astype(BF16)
    bk = bk * lax.rsqrt(bk_ms + EPS) * gqk_ref[...]
    bk = _rope(bk, cb[:, :LANES], sb_lo[:, :LANES], sb_hi[:, :LANES], GQ_HD // 4)
    bk_ref[0] = bk.astype(BF16)

    yc = proj(C_CU, 256) * jnp.concatenate(mixed, axis=0) * _silu(proj(C_CZ, 256))
    yc_ref[0] = yc.astype(BF16)

    q_scale = jnp.where(lax.broadcasted_iota(jnp.int32, (1, 256), 1) < 128, GLA_DK ** -0.5, 1.0)
    dqk_ref[0] = proj(C_DQK, 256) * q_scale
    dv_ref[0] = proj(C_DV, 256).astype(BF16)
    log_sig = jnp.minimum(gl, 0.0) - jnp.log(1.0 + jnp.exp(-jnp.abs(gl)))
    dg_ref[0] = log_sig * (1.0 / GLA_NORMALIZER)


def _layer(first, stream_in, modsel, ng, wp, taba, tabb, gqq, gqk, lng, lnb, sgw, sgb, w2, b2):
    t = ROW_TILE
    b = stream_in[0].shape[0]
    s = CTX_LEN + stream_in[1].shape[1] if first else stream_in[0].shape[1]
    row = lambda width: pl.BlockSpec((1, t, width), lambda i, j: (i, j, 0))
    colT = lambda height: pl.BlockSpec((1, height, t), lambda i, j: (i, 0, j))
    full = lambda a: pl.BlockSpec(a.shape, lambda i, j: (0,) * a.ndim)
    tab = pl.BlockSpec((3, t, 256), lambda i, j: (0, j, 0))
    mods = pl.BlockSpec((1, 1, 3, D_MODEL), lambda i, j: (i, jnp.minimum(j, 1), 0, 0))
    shp = lambda width, dt: jax.ShapeDtypeStruct((b, s, width), dt)
    shpT = lambda height: jax.ShapeDtypeStruct((b, height, s), BF16)
    tileT = lambda height: pl.BlockSpec((1, 1, height, t), lambda i, j: (i, j, 0, 0))
    if first:
        stream_specs = [pl.BlockSpec((1, t, D_MODEL), lambda i, j: (i, 0, 0)),
                        pl.BlockSpec((1, t, D_MODEL), lambda i, j: (i, jnp.maximum(j - 1, 0), 0))]
    else:
        stream_specs = [row(D_MODEL), mods, row(256), row(256), row(256), row(256),
                        full(stream_in[6])]
    return pl.pallas_call(
        functools.partial(_layer_kernel, first),
        grid=(b, s // t),
        in_specs=stream_specs + [
            mods, full(ng), full(wp), tab, tab, full(gqq), full(gqk), full(lng), full(lnb),
            full(sgw), full(sgb), full(w2), full(b2),
        ],
        out_specs=[row(D_MODEL), colT(256), row(256), tileT(512), colT(256), row(128), tileT(256),
                   row(768), row(256), row(256), row(256), row(256)],
        out_shape=[shp(D_MODEL, F32),
                   shpT(256), shp(256, BF16), jax.ShapeDtypeStruct((b, s // t, 512, t), BF16),
                   shpT(256), shp(128, BF16), jax.ShapeDtypeStruct((b, s // t, 256, t), BF16),
                   shp(768, BF16), shp(256, BF16), shp(256, F32), shp(256, BF16), shp(256, F32)],
        compiler_params=pltpu.CompilerParams(
            dimension_semantics=("parallel", "arbitrary"), vmem_limit_bytes=VMEM_LIMIT),
        name="layer_first" if first else "layer",
    )(*stream_in, modsel, ng, wp, taba, tabb, gqq, gqk, lng, lnb, sgw, sgb, w2, b2)


def _max_over_rows(s):
    m = s
    for part in (256, 32):
        if m.shape[0] > part and m.shape[0] % part == 0:
            m = jnp.max(m.reshape(m.shape[0] // part, part, m.shape[1]), axis=0)
    return jnp.max(m, axis=0, keepdims=True)


def _online_softmax_pv(latent, qts, k_ref, vt_ref, v_row_of_job, acc_ref, m_ref, s_ref):
    n_jobs = len(qts)
    tiles_per_chunk = KV_CHUNK // ROW_TILE
    n_chunks = (k_ref.shape[1] - CTX_LEN) // KV_CHUNK

    def qk(k_rows, j):
        return jnp.dot(k_rows, qts[j], preferred_element_type=F32)

    def softmax_pv(s, vt, j, first):
        m_new = _max_over_rows(s)
        if not first:
            m_old = m_ref[j]
            m_new = jnp.maximum(m_old, m_new)
            alpha = jnp.exp2(m_old - m_new)
        m_ref[j] = m_new
        upd = jnp.dot(vt, jnp.exp2(s - m_new).astype(BF16), preferred_element_type=F32)
        acc_ref[j] = upd if first else acc_ref[j] * alpha + upd

    def vt_tile(tile, j):
        return vt_ref[0, tile, v_row_of_job[j]:v_row_of_job[j] + LANES, :]

    def vt_chunk(c, j):
        t0 = CTX_LEN // ROW_TILE + c * tiles_per_chunk
        return jnp.concatenate([vt_tile(t0 + i, j) for i in range(tiles_per_chunk)], axis=1)

    def k_chunk(c):
        return k_ref[0, CTX_LEN + c * KV_CHUNK:CTX_LEN + (c + 1) * KV_CHUNK, :]

    k_ctx = k_ref[0, 0:CTX_LEN, :]

    @pl.when(jnp.logical_not(latent))
    def _():
        s = [qk(k_ctx, j) for j in range(n_jobs)]
        for j in range(n_jobs):
            softmax_pv(s[j], vt_tile(0, j), j, True)

    @pl.when(latent)
    def _():
        s = [qk(k_ctx, j) for j in range(n_jobs)]
        k_next = k_chunk(0)
        s_ref[0, 0] = qk(k_next, 0)
        for j in range(n_jobs):
            softmax_pv(s[j], vt_tile(0, j), j, True)
            if j + 1 < n_jobs:
                s_ref[0, j + 1] = qk(k_next, j + 1)

        for c in range(n_chunks):
            slot = c % 2
            more = c + 1 < n_chunks
            if more:
                k_next = k_chunk(c + 1)
                s_ref[1 - slot, 0] = qk(k_next, 0)
            for j in range(n_jobs):
                softmax_pv(s_ref[slot, j], vt_chunk(c, j), j, False)
                if more and j + 1 < n_jobs:
                    s_ref[1 - slot, j + 1] = qk(k_next, j + 1)

    outs = []
    for j in range(n_jobs):
        acc = acc_ref[j]
        outs.append(acc[:LANES // 2, :] * (1.0 / acc[LANES // 2:LANES // 2 + 1, :]))
    return outs


def _rows_at(x, row0, total):
    parts = []
    if row0:
        parts.append(jnp.zeros((row0, x.shape[1]), x.dtype))
    parts.append(x)
    if total - row0 - x.shape[0]:
        parts.append(jnp.zeros((total - row0 - x.shape[0], x.shape[1]), x.dtype))
    return jnp.concatenate(parts, axis=0)


def _diff_attn_kernel(lam_init, qt_ref, k_ref, vt_ref, gate_ref, lam_ref, sg_ref, o_ref,
                      acc_ref, m_ref, s_ref):
    lq1, lk1, lq2, lk2 = lam_ref[0:1, :], lam_ref[1:2, :], lam_ref[2:3, :], lam_ref[3:4, :]
    lam = (jnp.exp(jnp.sum(lq1 * lk1, axis=-1, keepdims=True))
           - jnp.exp(jnp.sum(lq2 * lk2, axis=-1, keepdims=True)) + lam_init)
    qts = [_rows_at(qt_ref[0, job * DA_QK:(job + 1) * DA_QK, :], job * DA_QK, BRANCH_W)
           for job in range(2 * DA_HEADS)]
    o = _online_softmax_pv(pl.program_id(1) > 0, qts, k_ref, vt_ref,
                           [(job // 2) * LANES for job in range(2 * DA_HEADS)], acc_ref, m_ref, s_ref)
    yt = jnp.concatenate([o[2 * h] - lam * o[2 * h + 1] for h in range(DA_HEADS)], axis=0)
    y = yt.T
    ms = _group_mean(y * y, _group_mean_matrix(BRANCH_W, DA_V))
    y = y * lax.rsqrt(ms + EPS) * sg_ref[...] * (1.0 - lam_init)
    o_ref[0] = (y * gate_ref[0].astype(F32)).astype(BF16)


def _gqa_attn_kernel(qt_ref, k_ref, vt_ref, gate_ref, o_ref, acc_ref, m_ref, s_ref):
    grp = GQ_HEADS // GQ_KV_HEADS
    qts = [_rows_at(qt_ref[0, h * GQ_HD:(h + 1) * GQ_HD, :], (h // grp) * GQ_HD, GQ_KV_HEADS * GQ_HD)
           for h in range(GQ_HEADS)]
    o = _online_softmax_pv(pl.program_id(1) > 0, qts, k_ref, vt_ref,
                           [(h // grp) * LANES for h in range(GQ_HEADS)], acc_ref, m_ref, s_ref)
    y = jnp.concatenate(o, axis=0).T
    o_ref[0] = (y * gate_ref[0].astype(F32)).astype(BF16)


def _attention(kernel, n_jobs, qt, k, vt, gate, gate_block, extra, name):
    b, s, _ = k.shape
    t = ROW_TILE
    full = lambda a: pl.BlockSpec(a.shape, lambda i, j: (0,) * a.ndim)
    return pl.pallas_call(
        kernel,
        grid=(b, s // t),
        in_specs=[
            pl.BlockSpec((1, qt.shape[1], t), lambda i, j: (i, 0, j)),
            pl.BlockSpec((1, s, k.shape[2]), lambda i, j: (i, 0, 0)),
            pl.BlockSpec((1,) + vt.shape[1:], lambda i, j: (i, 0, 0, 0)),
            pl.BlockSpec((1, t, 256), lambda i, j: (i, j, gate_block)),
        ] + [full(a) for a in extra],
        out_specs=pl.BlockSpec((1, t, 256), lambda i, j: (i, j, 0)),
        out_shape=jax.ShapeDtypeStruct((b, s, 256), BF16),
        scratch_shapes=[pltpu.VMEM((n_jobs, LANES, t), F32), pltpu.VMEM((n_jobs, 1, t), F32),
                        pltpu.VMEM((2, n_jobs, KV_CHUNK, t), F32)],
        compiler_params=pltpu.CompilerParams(
            dimension_semantics=("parallel", "arbitrary"), vmem_limit_bytes=VMEM_LIMIT),
        name=name,
    )(qt, k, vt, gate, *extra)


def _chunk_cumsum(g, row_in_chunk, reverse):
    x = g
    sh = 1
    while sh < GLA_CHUNK:
        if reverse:
            nb = pltpu.roll(x, x.shape[0] - sh, axis=0)
            x = x + jnp.where(row_in_chunk < GLA_CHUNK - sh, nb, 0.0)
        else:
            nb = pltpu.roll(x, sh, axis=0)
            x = x + jnp.where(row_in_chunk >= sh, nb, 0.0)
        sh *= 2
    return x


def _gla_kernel(qk_ref, v_ref, g_ref, gate_ref, ng_ref, o_ref, acc_ref, st_ref):
    n_tiles = qk_ref.shape[1] // GLA_TILE
    ctx_tiles = CTX_LEN // GLA_TILE
    per = GLA_TILE // GLA_CHUNK
    ri = lax.broadcasted_iota(jnp.int32, (GLA_TILE, GLA_TILE), 0)
    ci = lax.broadcasted_iota(jnp.int32, (GLA_TILE, GLA_TILE), 1)
    same_chunk = (ri // GLA_CHUNK) == (ci // GLA_CHUNK)
    chunk_of_row = ri // GLA_CHUNK
    row_in_chunk = ri % GLA_CHUNK
    head_of_k = lax.broadcasted_iota(jnp.int32, (GLA_TILE, GLA_HEADS * GLA_DK), 1) // GLA_DK
    head_of_v = lax.broadcasted_iota(jnp.int32, (GLA_TILE, BRANCH_W), 1) // GLA_DV
    st_mask = ((lax.broadcasted_iota(jnp.int32, (BRANCH_W, GLA_HEADS * GLA_DK), 0) // GLA_DV)
               == (lax.broadcasted_iota(jnp.int32, (BRANCH_W, GLA_HEADS * GLA_DK), 1) // GLA_DK))
    m64 = _group_mean_matrix(BRANCH_W, GLA_DV)
    tris = [jnp.where(same_chunk & (ci <= ri), 1.0, 0.0), jnp.where(same_chunk & (ci >= ri), 1.0, 0.0)]
    tris4 = [jnp.concatenate([t] * GLA_HEADS, axis=1) for t in tris]

    nt_dims = (((1,), (1,)), ((), ()))
    tn_dims = (((0,), (0,)), ((), ()))

    def scan_step(i, carry):
        def tile_index(d, sub):
            t = i * GLA_STEP_TILES + sub
            if not d:
                return t
            return jnp.where(t < ctx_tiles, ctx_tiles - 1 - t, n_tiles + ctx_tiles - 1 - t)

        chains = [(bi, d, sub) for bi in range(GLA_BATCH) for d in (0, 1) for sub in range(GLA_STEP_TILES)]
        nc = len(chains)
        rows = [pl.ds(pl.multiple_of(tile_index(d, sub) * GLA_TILE, GLA_TILE), GLA_TILE)
                for _, d, sub in chains]
        ends = [[c * GLA_CHUNK if d else (c + 1) * GLA_CHUNK - 1 for c in range(per)] for _, d, _ in chains]

        cum = []
        for n, (bi, d, _) in enumerate(chains):
            cum.append(_chunk_cumsum(g_ref[bi, rows[n], 128 * d:128 * d + 128], row_in_chunk, bool(d)))

        qe, kd, kl, vb = [], [], [], []
        for n, (bi, d, _) in enumerate(chains):
            q = qk_ref[bi, rows[n], 0:128]
            k = qk_ref[bi, rows[n], 128:256]
            cum_last = jnp.concatenate(
                [jnp.broadcast_to(cum[n][e:e + 1, :], (GLA_CHUNK, 128)) for e in ends[n]], axis=0)
            qe.append(q * jnp.exp(cum[n]))
            kd.append(k * jnp.exp(-cum[n]))
            kl.append(k * jnp.exp(cum_last - cum[n]))
            vb.append(v_ref[bi, rows[n], :])

        qeb = [x.astype(BF16) for x in qe]
        att = []
        for n, (bi, d, _) in enumerate(chains):
            kd_heads = jnp.concatenate(
                [jnp.where(head_of_k == h, kd[n], 0.0).astype(BF16) for h in range(GLA_HEADS)], axis=0)
            a = lax.dot_general(qeb[n], kd_heads, nt_dims, preferred_element_type=F32)
            att.append((a * tris4[d]).astype(BF16))
        o = []
        for n in range(nc):
            v_heads = jnp.concatenate(
                [jnp.where(head_of_v == h, vb[n], jnp.zeros_like(vb[n])) for h in range(GLA_HEADS)], axis=0)
            o.append(jnp.dot(att[n], v_heads, preferred_element_type=F32))

        upd = []
        for n in range(nc):
            kl_chunks = jnp.concatenate(
                [jnp.where(chunk_of_row == c, kl[n], 0.0).astype(BF16) for c in range(per)], axis=1)
            upd.append(lax.dot_general(vb[n], kl_chunks, tn_dims, preferred_element_type=F32))
        seen = [[None] * per for _ in range(nc)]
        for slot in range(2 * GLA_BATCH):
            st = st_ref[slot]
            for n in range(slot * GLA_STEP_TILES, (slot + 1) * GLA_STEP_TILES):
                for c in (range(per - 1, -1, -1) if slot % 2 else range(per)):
                    seen[n][c] = st.astype(BF16)
                    decay = jnp.exp(cum[n][ends[n][c]:ends[n][c] + 1, :])
                    st = st * decay + jnp.where(st_mask, upd[n][:, c * 128:(c + 1) * 128], 0.0)
            st_ref[slot] = st
        for n in range(nc):
            qe_chunks = jnp.concatenate(
                [jnp.where(chunk_of_row == c, qe[n], 0.0).astype(BF16) for c in range(per)], axis=1)
            o_inter = lax.dot_general(qe_chunks, jnp.concatenate(seen[n], axis=1), nt_dims,
                                      preferred_element_type=F32)
            acc_ref[n // GLA_STEP_TILES, rows[n], :] = o[n] + o_inter
        return carry

    st_ref[...] = jnp.zeros_like(st_ref)
    lax.fori_loop(0, n_tiles // GLA_STEP_TILES, scan_step, 0)

    def finish_step(i, carry):
        rows = pl.ds(pl.multiple_of(i * ROW_TILE, ROW_TILE), ROW_TILE)
        tot = [acc_ref[2 * bi, rows, :] + acc_ref[2 * bi + 1, rows, :] for bi in range(GLA_BATCH)]
        sq = [_split2(t * t) for t in tot]
        ms = [sum(jnp.dot(p, m64, preferred_element_type=F32) for p in s) for s in sq]
        for bi in range(GLA_BATCH):
            y = tot[bi] * lax.rsqrt(ms[bi] + EPS) * ng_ref[...]
            o_ref[bi, rows, :] = (y * gate_ref[bi, rows, :].astype(F32)).astype(BF16)
        return carry

    lax.fori_loop(0, qk_ref.shape[1] // ROW_TILE, finish_step, 0)


def _gla(dqk, dv, dg, gate, ng):
    b, s, _ = dqk.shape
    nb = GLA_BATCH
    blk = lambda width, cb: pl.BlockSpec((nb, s, width), lambda i: (i, 0, cb))
    return pl.pallas_call(
        _gla_kernel,
        grid=(b // nb,),
        in_specs=[blk(256, 0), blk(256, 0), blk(256, 0), blk(256, 2),
                  pl.BlockSpec(ng.shape, lambda i: (0, 0))],
        out_specs=blk(256, 0),
        out_shape=jax.ShapeDtypeStruct((b, s, 256), BF16),
        scratch_shapes=[pltpu.VMEM((2 * nb, s, BRANCH_W), F32),
                        pltpu.VMEM((2 * nb, BRANCH_W, GLA_HEADS * GLA_DK), F32)],
        compiler_params=pltpu.CompilerParams(
            dimension_semantics=("parallel",), vmem_limit_bytes=VMEM_LIMIT),
        name="gla",
    )(dqk, dv, dg, gate, ng)


def _out_kernel(x_ref, mod_ref, ya_ref, yb_ref, yc_ref, yd_ref, w_ref, fg_ref, o_ref):
    y = jnp.concatenate([ya_ref[0], yb_ref[0], yc_ref[0], yd_ref[0]], axis=1)
    upd = jnp.dot(y, w_ref[...], preferred_element_type=F32)
    xn = x_ref[0] + mod_ref[0, 0, 2:3, :] * upd
    o_ref[0] = xn * lax.rsqrt(jnp.mean(xn * xn, axis=-1, keepdims=True) + EPS) * fg_ref[...]


def _out_proj(xs, modsel, ya, yb, yc, yd, wo, fg):
    b, s, _ = xs.shape
    t = ROW_TILE
    skip = CTX_LEN // t
    row = lambda width: pl.BlockSpec((1, t, width), lambda i, j: (i, j + skip, 0))
    return pl.pallas_call(
        _out_kernel,
        grid=(b, s // t - skip),
        in_specs=[
            row(D_MODEL),
            pl.BlockSpec((1, 1, 3, D_MODEL), lambda i, j: (i, 1, 0, 0)),
            row(256), row(256), row(256), row(256),
            pl.BlockSpec(wo.shape, lambda i, j: (0, 0)),
            pl.BlockSpec(fg.shape, lambda i, j: (0, 0)),
        ],
        out_specs=pl.BlockSpec((1, t, D_MODEL), lambda i, j: (i, j, 0)),
        out_shape=jax.ShapeDtypeStruct((b, s - skip * t, D_MODEL), F32),
        compiler_params=pltpu.CompilerParams(
            dimension_semantics=("parallel", "parallel"), vmem_limit_bytes=VMEM_LIMIT),
        name="out_proj_final",
    )(xs, modsel, ya, yb, yc, yd, wo, fg)


def _rope_tables(seq, dim, width):
    half = dim // 2
    quarter = half // 2
    lane = np.arange(width) % dim
    freq = ROPE_THETA ** (-(2.0 * (lane % quarter)) / half)
    pos_t = np.arange(seq)
    pos = np.where(lane[None, :] < half, (pos_t // GRID_W)[:, None], (pos_t % GRID_W)[:, None])
    ang = pos * freq[None, :]
    cos, sin = np.cos(ang), np.sin(ang)
    first = (lane % half) < quarter
    s_lo = np.where(first[None, :], -sin, 0.0)
    s_hi = np.where(first[None, :], 0.0, sin)
    lat = np.stack([cos, s_lo, s_hi])
    ctx = np.stack([np.ones((CTX_LEN, width)), np.zeros((CTX_LEN, width)), np.zeros((CTX_LEN, width))])
    return jnp.asarray(np.concatenate([ctx, lat], axis=1), dtype=F32)


def _pack_w_in(w):
    offs = [0]
    for n in (256, 256, 256, 256, 256, 128, 128, 256, 256, 256, 256, 128, 128, 256, 256, 16, 16):
        offs.append(offs[-1] + n)
    seg = [w[..., offs[i]:offs[i + 1]] for i in range(17)]
    aq, ak, av, az, bq, bk, bv, bz, cu, cv, cz, dq, dk, dv, dz, drf, drb = seg
    gap = jnp.zeros(w.shape[:-1] + (LANES // 2,), w.dtype)
    r_gap = jnp.concatenate([drf, drb, gap[..., :LANES // 2 - 2 * GLA_RANK]], axis=-1)
    av_sp = jnp.concatenate([av[..., 0:64], r_gap, av[..., 64:128], gap, av[..., 128:192], gap,
                             av[..., 192:256], gap], axis=-1)
    bvk = jnp.concatenate([bv[..., :64], bk[..., :64], bv[..., 64:], bk[..., 64:]], axis=-1)
    packed = jnp.concatenate([aq, ak, av_sp, az, bq, bvk, bz, cu, cv, cz, dq, dk, dv, dz], axis=-1)
    assert packed.shape[-1] == P_PACK
    return packed.astype(BF16)


def kernel(x, c, ctx, c_ctx, ada_w, ada_b, norm_g, w_in, da_lq1, da_lk1, da_lq2, da_lk2,
           da_subln_g, gq_qnorm_g, gq_knorm_g, sg_ln_g, sg_ln_b, sg_w, sg_b,
           gla_w2_f, gla_b_f, gla_w2_b, gla_b_b, gla_norm_g, w_out, final_norm_g):
    b, seq, d = x.shape
    assert (seq, d, ctx.shape[1]) == (seq // ROW_TILE * ROW_TILE, D_MODEL, CTX_LEN)

    n_mod = 32
    cpad = jnp.zeros((n_mod, d), F32).at[:b].set(c).at[b].set(c_ctx)
    mod = _modulation(cpad, ada_w, ada_b)

    taba = _rope_tables(seq, DA_QK, 256)
    tabb = _rope_tables(seq, GQ_HD, 256)
    fg = final_norm_g.reshape(1, d)
    wp_all, wo_all, sgw_all = _pack_w_in(w_in), w_out.astype(BF16), sg_w.astype(BF16)

    nl = DEPTH
    modsel_all = jnp.concatenate([jnp.broadcast_to(mod[:, b].reshape(nl, 1, 1, 3, d), (nl, b, 1, 3, d)),
                                  mod[:, :b].reshape(nl, b, 1, 3, d)], axis=2)
    w2_all = (jnp.zeros((nl, LANES, 256), F32)
              .at[:, R_LANE:R_LANE + GLA_RANK, 0:128].set(gla_w2_f)
              .at[:, R_LANE + GLA_RANK:R_LANE + 2 * GLA_RANK, 128:256].set(gla_w2_b)).astype(BF16)
    b2_all = jnp.concatenate([gla_b_f, gla_b_b], axis=1).reshape(nl, 1, 256)
    sgb_all = jnp.repeat(jnp.swapaxes(sg_b, 1, 2), BRANCH_W // SG_GROUPS, axis=2)
    tiled = lambda g, n: jnp.tile(g, (1, n)).reshape(nl, 1, g.shape[1] * n)
    gqq_all, gqk_all = tiled(gq_qnorm_g, GQ_HEADS), tiled(gq_knorm_g, GQ_KV_HEADS)
    subln_all, glan_all = tiled(da_subln_g, DA_HEADS), tiled(gla_norm_g, GLA_HEADS)
    lam_all = jnp.stack([da_lq1, da_lk1, da_lq2, da_lk2], axis=1)

    stream_in = (ctx, x)
    for i in range(DEPTH):
        lam_init = 0.8 - 0.6 * math.exp(-0.3 * i)
        (xs, aqt, ak, avt, bqt, bk, bvt, gate, yc, dqk, dv, dg) = _layer(
            i == 0, stream_in, modsel_all[i], norm_g[i].reshape(1, d), wp_all[i], taba, tabb,
            gqq_all[i], gqk_all[i], sg_ln_g[i].reshape(1, 256), sg_ln_b[i].reshape(1, 256),
            sgw_all[i], sgb_all[i], w2_all[i], b2_all[i])
        ya = _attention(functools.partial(_diff_attn_kernel, lam_init), 2 * DA_HEADS, aqt, ak, avt, gate, 0,
                        [lam_all[i], subln_all[i]], "diff_attn")
        yb = _attention(_gqa_attn_kernel, GQ_HEADS, bqt, bk, bvt, gate, 1, [], "gqa_attn")
        yd = _gla(dqk, dv, dg, gate, glan_all[i])
        stream_in = (xs, modsel_all[i], ya, yb, yc, yd, wo_all[i])
    return _out_proj(*stream_in, fg)
```

```python
import functools
import math

import jax
import jax.numpy as jnp
import numpy as np
from jax import lax
from jax.experimental import pallas as pl
from jax.experimental.pallas import tpu as pltpu

F32 = jnp.float32
BF16 = jnp.bfloat16

D_MODEL = 1024
DEPTH = 4
CTX_LEN = 256
GRID_W = 64
BRANCH_W = 256
ROPE_THETA = 10000.0
EPS = 1e-6
DA_HEADS = 4
DA_QK = 32
DA_V = 64
GQ_HEADS = 4
GQ_KV_HEADS = 2
GQ_HD = 64
SG_GROUPS = 4
SG_CHUNK = 128
GLA_HEADS = 4
GLA_DV = 64
GLA_DK = 32
GLA_RANK = 16
GLA_NORMALIZER = 16.0
GLA_CHUNK = 32

LANES = 128
ROW_TILE = 256
KV_CHUNK = 512
GLA_TILE = 128
GLA_BATCH = 2
GLA_STEP_TILES = 3
VMEM_LIMIT = 56 * 1024 * 1024

C_AQ, C_AK, C_AV, C_AZ = 0, 256, 512, 1024
C_BQ, C_BVK, C_BZ = 1280, 1536, 1792
C_CU, C_CV, C_CZ = 2048, 2304, 2560
C_DQK, C_DV, C_DZ = 2816, 3072, 3328
P_PACK = 3584
R_LANE = 64
LOG2E = math.log2(math.e)


def _silu(x):
    return x * (1.0 / (1.0 + jnp.exp(-x)))


def _group_mean_matrix(width, group):
    r = lax.broadcasted_iota(jnp.int32, (width, width), 0) // group
    c = lax.broadcasted_iota(jnp.int32, (width, width), 1) // group
    return jnp.where(r == c, 1.0 / group, 0.0).astype(BF16)


def _split2(x):
    hi = x.astype(BF16)
    return hi, (x - hi.astype(F32)).astype(BF16)


def _group_mean(x, mat):
    return sum(jnp.dot(p, mat, preferred_element_type=F32) for p in _split2(x))


def _rope(x, cos, s_lo, s_hi, shift):
    outs = []
    for c in range(x.shape[1] // LANES):
        sl = slice(c * LANES, (c + 1) * LANES)
        xc = x[:, sl]
        up = pltpu.roll(xc, LANES - shift, axis=1)
        dn = pltpu.roll(xc, shift, axis=1)
        outs.append(xc * cos[:, sl] + up * s_lo[:, sl] + dn * s_hi[:, sl])
    return outs[0] if len(outs) == 1 else jnp.concatenate(outs, axis=1)


def _with_ones(v):
    lane = lax.broadcasted_iota(jnp.int32, v.shape, 1) % LANES
    return jnp.where(lane < LANES // 2, v, 1.0)


def _normalize_by_sum_lanes(o):
    lane = lax.broadcasted_iota(jnp.int32, o.shape, 1)
    return jnp.where(lane < LANES // 2, o * (1.0 / pltpu.roll(o, LANES // 2, axis=1)), 0.0)


def _mod_kernel(c_ref, w_ref, b_ref, o_ref):
    s = _silu(c_ref[...]).astype(BF16)
    o_ref[0] = jnp.dot(s, w_ref[0].astype(BF16), preferred_element_type=F32) + b_ref[0]


def _modulation(cpad, ada_w, ada_b):
    n = cpad.shape[0]
    return pl.pallas_call(
        _mod_kernel,
        grid=(DEPTH, 3),
        in_specs=[
            pl.BlockSpec((n, D_MODEL), lambda i, j: (0, 0)),
            pl.BlockSpec((1, D_MODEL, D_MODEL), lambda i, j: (i, 0, j)),
            pl.BlockSpec((1, 1, D_MODEL), lambda i, j: (i, 0, j)),
        ],
        out_specs=pl.BlockSpec((1, n, D_MODEL), lambda i, j: (i, 0, j)),
        out_shape=jax.ShapeDtypeStruct((DEPTH, n, 3 * D_MODEL), F32),
        compiler_params=pltpu.CompilerParams(vmem_limit_bytes=VMEM_LIMIT),
        name="modulation",
    )(cpad, ada_w, ada_b.reshape(DEPTH, 1, 3 * D_MODEL))


def _layer_kernel(first, *refs):
    if first:
        ctx_ref, x_ref = refs[:2]
        refs = refs[2:]
    else:
        xs_ref, modp_ref, ya_ref, yb_ref, yc_in_ref, yd_ref, wo_ref = refs[:7]
        refs = refs[7:]
    (mod_ref, ng_ref, w_ref, taba_ref, tabb_ref, gqq_ref, gqk_ref, lng_ref, lnb_ref, sgw_ref,
     sgb_ref, w2_ref, b2_ref,
     xs_out_ref, aq_ref, ak_ref, av_ref, bq_ref, bk_ref, bv_ref, gate_ref, yc_ref,
     dqk_ref, dv_ref, dg_ref) = refs

    if first:
        x = jnp.where(pl.program_id(1) == 0, ctx_ref[0], x_ref[0])
    else:
        y_prev = jnp.concatenate([ya_ref[0], yb_ref[0], yc_in_ref[0], yd_ref[0]], axis=1)
        x = xs_ref[0] + modp_ref[0, 0, 2:3, :] * jnp.dot(y_prev, wo_ref[...],
                                                         preferred_element_type=F32)
    xs_out_ref[0] = x
    shift = mod_ref[0, 0, 0:1, :]
    scale = mod_ref[0, 0, 1:2, :]
    y = x * lax.rsqrt(jnp.mean(x * x, axis=-1, keepdims=True) + EPS) * ng_ref[...]
    hb = (y * (1.0 + scale) + shift).astype(BF16)

    def proj(lo, width):
        return jnp.dot(hb, w_ref[:, lo:lo + width], preferred_element_type=F32)

    m64 = _group_mean_matrix(BRANCH_W, GQ_HD)
    half_lane = lax.broadcasted_iota(jnp.int32, (ROW_TILE, LANES), 1) < LANES // 2
    p_bq = proj(C_BQ, 256)
    p_bvk = proj(C_BVK, 256)
    p_cv = proj(C_CV, 256)
    p_av = proj(C_AV, 512)

    bq_sq = _split2(p_bq * p_bq)
    bk = jnp.where(half_lane, pltpu.roll(p_bvk[:, :LANES], LANES // 2, axis=1), p_bvk[:, LANES:])
    bk_sq = _split2(bk * bk)
    mu = jnp.mean(p_cv, axis=-1, keepdims=True)
    cen = p_cv - mu
    var = jnp.mean(cen * cen, axis=-1, keepdims=True)
    vn = (cen * lax.rsqrt(var + EPS) * lng_ref[...] + lnb_ref[...]).astype(BF16)
    r = p_av[:, :LANES].astype(BF16)
    av_ref[0, 0] = _with_ones(p_av).T.astype(BF16)
    bv_ref[0, 0] = _with_ones(p_bvk).T.astype(BF16)

    ca, sa_lo, sa_hi = taba_ref[0], taba_ref[1], taba_ref[2]
    aq = _rope(proj(C_AQ, 256), ca, sa_lo, sa_hi, DA_QK // 4) * (DA_QK ** -0.5 * LOG2E)
    aq_ref[0] = aq.T.astype(BF16)
    ak = _rope(proj(C_AK, 256), ca, sa_lo, sa_hi, DA_QK // 4)
    ak_ref[0] = ak.astype(BF16)

    bq_ms = sum(jnp.dot(p, m64, preferred_element_type=F32) for p in bq_sq)
    bk_ms = sum(jnp.dot(p, m64[:LANES, :LANES], preferred_element_type=F32) for p in bk_sq)
    gl = jnp.dot(r, w2_ref[...], preferred_element_type=F32) + b2_ref[...]

    gate_ref[0, :, 0:256] = _silu(proj(C_AZ, 256)).astype(BF16)
    gate_ref[0, :, 256:512] = _silu(proj(C_BZ, 256)).astype(BF16)
    gate_ref[0, :, 512:768] = _silu(proj(C_DZ, 256)).astype(BF16)

    lane_group = lax.broadcasted_iota(jnp.int32, (SG_CHUNK, BRANCH_W), 1) // (BRANCH_W // SG_GROUPS)
    mixed = []
    for n in range(ROW_TILE // SG_CHUNK):
        vchunk = vn[n * SG_CHUNK:(n + 1) * SG_CHUNK, :]
        acc = jnp.zeros((SG_CHUNK, BRANCH_W), F32)
        for g in range(SG_GROUPS):
            acc = jnp.where(lane_group == g,
                            jnp.dot(sgw_ref[g], vchunk, preferred_element_type=F32), acc)
        mixed.append(acc + sgb_ref[...])

    cb, sb_lo, sb_hi = tabb_ref[0], tabb_ref[1], tabb_ref[2]
    bq = p_bq * lax.rsqrt(bq_ms + EPS) * gqq_ref[...]
    bq_ref[0] = (_rope(bq, cb, sb_lo, sb_hi, GQ_HD // 4) * (GQ_HD ** -0.5 * LOG2E)).T.astype(BF16)
    bk = bk * lax.rsqrt(bk_ms + EPS) * gqk_ref[...]
    bk = _rope(bk, cb[:, :LANES], sb_lo[:, :LANES], sb_hi[:, :LANES], GQ_HD // 4)
    bk_ref[0] = bk.astype(BF16)

    yc = proj(C_CU, 256) * jnp.concatenate(mixed, axis=0) * _silu(proj(C_CZ, 256))
    yc_ref[0] = yc.astype(BF16)

    q_scale = jnp.where(lax.broadcasted_iota(jnp.int32, (1, 256), 1) < 128, GLA_DK ** -0.5, 1.0)
    dqk_ref[0] = proj(C_DQK, 256) * q_scale
    dv_ref[0] = proj(C_DV, 256).astype(BF16)
    log_sig = jnp.minimum(gl, 0.0) - jnp.log(1.0 + jnp.exp(-jnp.abs(gl)))
    dg_ref[0] = log_sig * (1.0 / GLA_NORMALIZER)


def _layer(layer, stream_in, modsel, ng, wp, taba, tabb, gqq, gqk, lng, lnb, sgw, sgb, w2, b2):
    first = layer == 0
    t = ROW_TILE
    stacked = lambda a, idx: pl.BlockSpec((None,) + a.shape[1:], lambda i, j: (idx,) + (0,) * (a.ndim - 1))
    b = stream_in[0].shape[0]
    s = CTX_LEN + stream_in[1].shape[1] if first else stream_in[0].shape[1]
    row = lambda width: pl.BlockSpec((1, t, width), lambda i, j: (i, j, 0))
    colT = lambda height: pl.BlockSpec((1, height, t), lambda i, j: (i, 0, j))
    full = lambda a: pl.BlockSpec(a.shape, lambda i, j: (0,) * a.ndim)
    tab = pl.BlockSpec((3, t, 256), lambda i, j: (0, j, 0))
    mods = pl.BlockSpec((1, 1, 3, D_MODEL), lambda i, j: (i, jnp.minimum(j, 1), 0, 0))
    shp = lambda width, dt: jax.ShapeDtypeStruct((b, s, width), dt)
    shpT = lambda height: jax.ShapeDtypeStruct((b, height, s), BF16)
    tileT = lambda height: pl.BlockSpec((1, 1, height, t), lambda i, j: (i, j, 0, 0))
    if first:
        stream_specs = [pl.BlockSpec((1, t, D_MODEL), lambda i, j: (i, 0, 0)),
                        pl.BlockSpec((1, t, D_MODEL), lambda i, j: (i, jnp.maximum(j - 1, 0), 0))]
    else:
        stream_specs = [row(D_MODEL), mods, row(256), row(256), row(256), row(256),
                        stacked(stream_in[6], layer - 1)]
    return pl.pallas_call(
        functools.partial(_layer_kernel, first),
        grid=(b, s // t),
        in_specs=stream_specs + [
            mods, full(ng), stacked(wp, layer), tab, tab, full(gqq), full(gqk), full(lng), full(lnb),
            stacked(sgw, layer), full(sgb), full(w2), full(b2),
        ],
        out_specs=[row(D_MODEL), colT(256), row(256), tileT(512), colT(256), row(128), tileT(256),
                   row(768), row(256), row(256), row(256), row(256)],
        out_shape=[shp(D_MODEL, F32),
                   shpT(256), shp(256, BF16), jax.ShapeDtypeStruct((b, s // t, 512, t), BF16),
                   shpT(256), shp(128, BF16), jax.ShapeDtypeStruct((b, s // t, 256, t), BF16),
                   shp(768, BF16), shp(256, BF16), shp(256, F32), shp(256, BF16), shp(256, F32)],
        compiler_params=pltpu.CompilerParams(
            dimension_semantics=("parallel", "arbitrary"), vmem_limit_bytes=VMEM_LIMIT),
        name="layer_first" if first else "layer",
    )(*stream_in, modsel, ng, wp, taba, tabb, gqq, gqk, lng, lnb, sgw, sgb, w2, b2)


def _max_over_rows(s):
    m = s
    for part in (256, 32):
        if m.shape[0] > part and m.shape[0] % part == 0:
            m = jnp.max(m.reshape(m.shape[0] // part, part, m.shape[1]), axis=0)
    return jnp.max(m, axis=0, keepdims=True)


def _online_softmax_pv(latent, qts, k_ref, vt_ref, v_row_of_job, acc_ref, m_ref, s_ref):
    n_jobs = len(qts)
    tiles_per_chunk = KV_CHUNK // ROW_TILE
    n_chunks = (k_ref.shape[1] - CTX_LEN) // KV_CHUNK

    def qk(k_rows, j):
        return jnp.dot(k_rows, qts[j], preferred_element_type=F32)

    def softmax_pv(s, vt, j, first):
        m_new = _max_over_rows(s)
        if not first:
            m_old = m_ref[j]
            m_new = jnp.maximum(m_old, m_new)
            alpha = jnp.exp2(m_old - m_new)
        m_ref[j] = m_new
        upd = jnp.dot(vt, jnp.exp2(s - m_new).astype(BF16), preferred_element_type=F32)
        acc_ref[j] = upd if first else acc_ref[j] * alpha + upd

    def vt_tile(tile, j):
        return vt_ref[0, tile, v_row_of_job[j]:v_row_of_job[j] + LANES, :]

    def vt_chunk(c, j):
        t0 = CTX_LEN // ROW_TILE + c * tiles_per_chunk
        return jnp.concatenate([vt_tile(t0 + i, j) for i in range(tiles_per_chunk)], axis=1)

    def k_chunk(c):
        return k_ref[0, CTX_LEN + c * KV_CHUNK:CTX_LEN + (c + 1) * KV_CHUNK, :]

    k_ctx = k_ref[0, 0:CTX_LEN, :]

    @pl.when(jnp.logical_not(latent))
    def _():
        s = [qk(k_ctx, j) for j in range(n_jobs)]
        for j in range(n_jobs):
            softmax_pv(s[j], vt_tile(0, j), j, True)

    @pl.when(latent)
    def _():
        s = [qk(k_ctx, j) for j in range(n_jobs)]
        k_next = k_chunk(0)
        s_ref[0, 0] = qk(k_next, 0)
        for j in range(n_jobs):
            softmax_pv(s[j], vt_tile(0, j), j, True)
            if j + 1 < n_jobs:
                s_ref[0, j + 1] = qk(k_next, j + 1)

        for c in range(n_chunks):
            slot = c % 2
            more = c + 1 < n_chunks
            if more:
                k_next = k_chunk(c + 1)
                s_ref[1 - slot, 0] = qk(k_next, 0)
            for j in range(n_jobs):
                softmax_pv(s_ref[slot, j], vt_chunk(c, j), j, False)
                if more and j + 1 < n_jobs:
                    s_ref[1 - slot, j + 1] = qk(k_next, j + 1)

    outs = []
    for j in range(n_jobs):
        acc = acc_ref[j]
        outs.append(acc[:LANES // 2, :] * (1.0 / acc[LANES // 2:LANES // 2 + 1, :]))
    return outs


def _rows_at(x, row0, total):
    parts = []
    if row0:
        parts.append(jnp.zeros((row0, x.shape[1]), x.dtype))
    parts.append(x)
    if total - row0 - x.shape[0]:
        parts.append(jnp.zeros((total - row0 - x.shape[0], x.shape[1]), x.dtype))
    return jnp.concatenate(parts, axis=0)


def _diff_attn_kernel(lam_init, qt_ref, k_ref, vt_ref, gate_ref, lam_ref, sg_ref, o_ref,
                      acc_ref, m_ref, s_ref):
    lq1, lk1, lq2, lk2 = lam_ref[0:1, :], lam_ref[1:2, :], lam_ref[2:3, :], lam_ref[3:4, :]
    lam = (jnp.exp(jnp.sum(lq1 * lk1, axis=-1, keepdims=True))
           - jnp.exp(jnp.sum(lq2 * lk2, axis=-1, keepdims=True)) + lam_init)
    qts = [_rows_at(qt_ref[0, job * DA_QK:(job + 1) * DA_QK, :], job * DA_QK, BRANCH_W)
           for job in range(2 * DA_HEADS)]
    o = _online_softmax_pv(pl.program_id(1) > 0, qts, k_ref, vt_ref,
                           [(job // 2) * LANES for job in range(2 * DA_HEADS)], acc_ref, m_ref, s_ref)
    yt = jnp.concatenate([o[2 * h] - lam * o[2 * h + 1] for h in range(DA_HEADS)], axis=0)
    y = yt.T
    ms = _group_mean(y * y, _group_mean_matrix(BRANCH_W, DA_V))
    y = y * lax.rsqrt(ms + EPS) * sg_ref[...] * (1.0 - lam_init)
    o_ref[0] = (y * gate_ref[0].astype(F32)).astype(BF16)


def _gqa_attn_kernel(qt_ref, k_ref, vt_ref, gate_ref, o_ref, acc_ref, m_ref, s_ref):
    grp = GQ_HEADS // GQ_KV_HEADS
    qts = [_rows_at(qt_ref[0, h * GQ_HD:(h + 1) * GQ_HD, :], (h // grp) * GQ_HD, GQ_KV_HEADS * GQ_HD)
           for h in range(GQ_HEADS)]
    o = _online_softmax_pv(pl.program_id(1) > 0, qts, k_ref, vt_ref,
                           [(h // grp) * LANES for h in range(GQ_HEADS)], acc_ref, m_ref, s_ref)
    y = jnp.concatenate(o, axis=0).T
    o_ref[0] = (y * gate_ref[0].astype(F32)).astype(BF16)


def _attention(kernel, n_jobs, qt, k, vt, gate, gate_block, extra, name):
    b, s, _ = k.shape
    t = ROW_TILE
    full = lambda a: pl.BlockSpec(a.shape, lambda i, j: (0,) * a.ndim)
    return pl.pallas_call(
        kernel,
        grid=(b, s // t),
        in_specs=[
            pl.BlockSpec((1, qt.shape[1], t), lambda i, j: (i, 0, j)),
            pl.BlockSpec((1, s, k.shape[2]), lambda i, j: (i, 0, 0)),
            pl.BlockSpec((1,) + vt.shape[1:], lambda i, j: (i, 0, 0, 0)),
            pl.BlockSpec((1, t, 256), lambda i, j: (i, j, gate_block)),
        ] + [full(a) for a in extra],
        out_specs=pl.BlockSpec((1, t, 256), lambda i, j: (i, j, 0)),
        out_shape=jax.ShapeDtypeStruct((b, s, 256), BF16),
        scratch_shapes=[pltpu.VMEM((n_jobs, LANES, t), F32), pltpu.VMEM((n_jobs, 1, t), F32),
                        pltpu.VMEM((2, n_jobs, KV_CHUNK, t), F32)],
        compiler_params=pltpu.CompilerParams(
            dimension_semantics=("parallel", "arbitrary"), vmem_limit_bytes=VMEM_LIMIT),
        name=name,
    )(qt, k, vt, gate, *extra)


def _chunk_cumsum(g, row_in_chunk, reverse):
    x = g
    sh = 1
    while sh < GLA_CHUNK:
        if reverse:
            nb = pltpu.roll(x, x.shape[0] - sh, axis=0)
            x = x + jnp.where(row_in_chunk < GLA_CHUNK - sh, nb, 0.0)
        else:
            nb = pltpu.roll(x, sh, axis=0)
            x = x + jnp.where(row_in_chunk >= sh, nb, 0.0)
        sh *= 2
    return x


def _gla_kernel(qk_ref, v_ref, g_ref, gate_ref, ng_ref, o_ref, acc_ref, st_ref):
    n_tiles = qk_ref.shape[1] // GLA_TILE
    ctx_tiles = CTX_LEN // GLA_TILE
    per = GLA_TILE // GLA_CHUNK
    ri = lax.broadcasted_iota(jnp.int32, (GLA_TILE, GLA_TILE), 0)
    ci = lax.broadcasted_iota(jnp.int32, (GLA_TILE, GLA_TILE), 1)
    same_chunk = (ri // GLA_CHUNK) == (ci // GLA_CHUNK)
    chunk_of_row = ri // GLA_CHUNK
    row_in_chunk = ri % GLA_CHUNK
    head_of_k = lax.broadcasted_iota(jnp.int32, (GLA_TILE, GLA_HEADS * GLA_DK), 1) // GLA_DK
    head_of_v = lax.broadcasted_iota(jnp.int32, (GLA_TILE, BRANCH_W), 1) // GLA_DV
    st_mask = ((lax.broadcasted_iota(jnp.int32, (BRANCH_W, GLA_HEADS * GLA_DK), 0) // GLA_DV)
               == (lax.broadcasted_iota(jnp.int32, (BRANCH_W, GLA_HEADS * GLA_DK), 1) // GLA_DK))
    m64 = _group_mean_matrix(BRANCH_W, GLA_DV)
    tris = [jnp.where(same_chunk & (ci <= ri), 1.0, 0.0), jnp.where(same_chunk & (ci >= ri), 1.0, 0.0)]
    tris4 = [jnp.concatenate([t] * GLA_HEADS, axis=1) for t in tris]

    nt_dims = (((1,), (1,)), ((), ()))
    tn_dims = (((0,), (0,)), ((), ()))

    def scan_step(i, carry):
        def tile_index(d, sub):
            t = i * GLA_STEP_TILES + sub
            if not d:
                return t
            return jnp.where(t < ctx_tiles, ctx_tiles - 1 - t, n_tiles + ctx_tiles - 1 - t)

        chains = [(bi, d, sub) for bi in range(GLA_BATCH) for d in (0, 1) for sub in range(GLA_STEP_TILES)]
        nc = len(chains)
        rows = [pl.ds(pl.multiple_of(tile_index(d, sub) * GLA_TILE, GLA_TILE), GLA_TILE)
                for _, d, sub in chains]
        ends = [[c * GLA_CHUNK if d else (c + 1) * GLA_CHUNK - 1 for c in range(per)] for _, d, _ in chains]

        cum = []
        for n, (bi, d, _) in enumerate(chains):
            cum.append(_chunk_cumsum(g_ref[bi, rows[n], 128 * d:128 * d + 128], row_in_chunk, bool(d)))

        qe, kd, kl, vb = [], [], [], []
        for n, (bi, d, _) in enumerate(chains):
            q = qk_ref[bi, rows[n], 0:128]
            k = qk_ref[bi, rows[n], 128:256]
            cum_last = jnp.concatenate(
                [jnp.broadcast_to(cum[n][e:e + 1, :], (GLA_CHUNK, 128)) for e in ends[n]], axis=0)
            qe.append(q * jnp.exp(cum[n]))
            kd.append(k * jnp.exp(-cum[n]))
            kl.append(k * jnp.exp(cum_last - cum[n]))
            vb.append(v_ref[bi, rows[n], :])

        qeb = [x.astype(BF16) for x in qe]
        att = []
        for n, (bi, d, _) in enumerate(chains):
            kd_heads = jnp.concatenate(
                [jnp.where(head_of_k == h, kd[n], 0.0).astype(BF16) for h in range(GLA_HEADS)], axis=0)
            a = lax.dot_general(qeb[n], kd_heads, nt_dims, preferred_element_type=F32)
            att.append((a * tris4[d]).astype(BF16))
        o = []
        for n in range(nc):
            v_heads = jnp.concatenate(
                [jnp.where(head_of_v == h, vb[n], jnp.zeros_like(vb[n])) for h in range(GLA_HEADS)], axis=0)
            o.append(jnp.dot(att[n], v_heads, preferred_element_type=F32))

        upd = []
        for n in range(nc):
            kl_chunks = jnp.concatenate(
                [jnp.where(chunk_of_row == c, kl[n], 0.0).astype(BF16) for c in range(per)], axis=1)
            upd.append(lax.dot_general(vb[n], kl_chunks, tn_dims, preferred_element_type=F32))
        seen = [[None] * per for _ in range(nc)]
        for slot in range(2 * GLA_BATCH):
            st = st_ref[slot]
            for n in range(slot * GLA_STEP_TILES, (slot + 1) * GLA_STEP_TILES):
                for c in (range(per - 1, -1, -1) if slot % 2 else range(per)):
                    seen[n][c] = st.astype(BF16)
                    decay = jnp.exp(cum[n][ends[n][c]:ends[n][c] + 1, :])
                    st = st * decay + jnp.where(st_mask, upd[n][:, c * 128:(c + 1) * 128], 0.0)
            st_ref[slot] = st
        for n in range(nc):
            qe_chunks = jnp.concatenate(
                [jnp.where(chunk_of_row == c, qe[n], 0.0).astype(BF16) for c in range(per)], axis=1)
            o_inter = lax.dot_general(qe_chunks, jnp.concatenate(seen[n], axis=1), nt_dims,
                                      preferred_element_type=F32)
            acc_ref[n // GLA_STEP_TILES, rows[n], :] = o[n] + o_inter
        return carry

    st_ref[...] = jnp.zeros_like(st_ref)
    lax.fori_loop(0, n_tiles // GLA_STEP_TILES, scan_step, 0)

    def finish_step(i, carry):
        rows = pl.ds(pl.multiple_of(i * ROW_TILE, ROW_TILE), ROW_TILE)
        tot = [acc_ref[2 * bi, rows, :] + acc_ref[2 * bi + 1, rows, :] for bi in range(GLA_BATCH)]
        sq = [_split2(t * t) for t in tot]
        ms = [sum(jnp.dot(p, m64, preferred_element_type=F32) for p in s) for s in sq]
        for bi in range(GLA_BATCH):
            y = tot[bi] * lax.rsqrt(ms[bi] + EPS) * ng_ref[...]
            o_ref[bi, rows, :] = (y * gate_ref[bi, rows, :].astype(F32)).astype(BF16)
        return carry

    lax.fori_loop(0, qk_ref.shape[1] // ROW_TILE, finish_step, 0)


def _gla(dqk, dv, dg, gate, ng):
    b, s, _ = dqk.shape
    nb = GLA_BATCH
    blk = lambda width, cb: pl.BlockSpec((nb, s, width), lambda i: (i, 0, cb))
    return pl.pallas_call(
        _gla_kernel,
        grid=(b // nb,),
        in_specs=[blk(256, 0), blk(256, 0), blk(256, 0), blk(256, 2),
                  pl.BlockSpec(ng.shape, lambda i: (0, 0))],
        out_specs=blk(256, 0),
        out_shape=jax.ShapeDtypeStruct((b, s, 256), BF16),
        scratch_shapes=[pltpu.VMEM((2 * nb, s, BRANCH_W), F32),
                        pltpu.VMEM((2 * nb, BRANCH_W, GLA_HEADS * GLA_DK), F32)],
        compiler_params=pltpu.CompilerParams(
            dimension_semantics=("parallel",), vmem_limit_bytes=VMEM_LIMIT),
        name="gla",
    )(dqk, dv, dg, gate, ng)


def _out_kernel(x_ref, mod_ref, ya_ref, yb_ref, yc_ref, yd_ref, w_ref, fg_ref, o_ref):
    y = jnp.concatenate([ya_ref[0], yb_ref[0], yc_ref[0], yd_ref[0]], axis=1)
    upd = jnp.dot(y, w_ref[...], preferred_element_type=F32)
    xn = x_ref[0] + mod_ref[0, 0, 2:3, :] * upd
    o_ref[0] = xn * lax.rsqrt(jnp.mean(xn * xn, axis=-1, keepdims=True) + EPS) * fg_ref[...]


def _out_proj(xs, modsel, ya, yb, yc, yd, wo, fg):
    b, s, _ = xs.shape
    t = ROW_TILE
    skip = CTX_LEN // t
    row = lambda width: pl.BlockSpec((1, t, width), lambda i, j: (i, j + skip, 0))
    return pl.pallas_call(
        _out_kernel,
        grid=(b, s // t - skip),
        in_specs=[
            row(D_MODEL),
            pl.BlockSpec((1, 1, 3, D_MODEL), lambda i, j: (i, 1, 0, 0)),
            row(256), row(256), row(256), row(256),
            pl.BlockSpec((None,) + wo.shape[1:], lambda i, j: (wo.shape[0] - 1, 0, 0)),
            pl.BlockSpec(fg.shape, lambda i, j: (0, 0)),
        ],
        out_specs=pl.BlockSpec((1, t, D_MODEL), lambda i, j: (i, j, 0)),
        out_shape=jax.ShapeDtypeStruct((b, s - skip * t, D_MODEL), F32),
        compiler_params=pltpu.CompilerParams(
            dimension_semantics=("parallel", "parallel"), vmem_limit_bytes=VMEM_LIMIT),
        name="out_proj_final",
    )(xs, modsel, ya, yb, yc, yd, wo, fg)


def _rope_tables(seq, dim, width):
    half = dim // 2
    quarter = half // 2
    lane = np.arange(width) % dim
    freq = ROPE_THETA ** (-(2.0 * (lane % quarter)) / half)
    pos_t = np.arange(seq)
    pos = np.where(lane[None, :] < half, (pos_t // GRID_W)[:, None], (pos_t % GRID_W)[:, None])
    ang = pos * freq[None, :]
    cos, sin = np.cos(ang), np.sin(ang)
    first = (lane % half) < quarter
    s_lo = np.where(first[None, :], -sin, 0.0)
    s_hi = np.where(first[None, :], 0.0, sin)
    lat = np.stack([cos, s_lo, s_hi])
    ctx = np.stack([np.ones((CTX_LEN, width)), np.zeros((CTX_LEN, width)), np.zeros((CTX_LEN, width))])
    return jnp.asarray(np.concatenate([ctx, lat], axis=1), dtype=F32)


def _pack_w_in(w):
    offs = [0]
    for n in (256, 256, 256, 256, 256, 128, 128, 256, 256, 256, 256, 128, 128, 256, 256, 16, 16):
        offs.append(offs[-1] + n)
    seg = [w[..., offs[i]:offs[i + 1]] for i in range(17)]
    aq, ak, av, az, bq, bk, bv, bz, cu, cv, cz, dq, dk, dv, dz, drf, drb = seg
    gap = jnp.zeros(w.shape[:-1] + (LANES // 2,), w.dtype)
    r_gap = jnp.concatenate([drf, drb, gap[..., :LANES // 2 - 2 * GLA_RANK]], axis=-1)
    av_sp = jnp.concatenate([av[..., 0:64], r_gap, av[..., 64:128], gap, av[..., 128:192], gap,
                             av[..., 192:256], gap], axis=-1)
    bvk = jnp.concatenate([bv[..., :64], bk[..., :64], bv[..., 64:], bk[..., 64:]], axis=-1)
    packed = jnp.concatenate([aq, ak, av_sp, az, bq, bvk, bz, cu, cv, cz, dq, dk, dv, dz], axis=-1)
    assert packed.shape[-1] == P_PACK
    return packed.astype(BF16)


def kernel(x, c, ctx, c_ctx, ada_w, ada_b, norm_g, w_in, da_lq1, da_lk1, da_lq2, da_lk2,
           da_subln_g, gq_qnorm_g, gq_knorm_g, sg_ln_g, sg_ln_b, sg_w, sg_b,
           gla_w2_f, gla_b_f, gla_w2_b, gla_b_b, gla_norm_g, w_out, final_norm_g):
    b, seq, d = x.shape
    assert (seq, d, ctx.shape[1]) == (seq // ROW_TILE * ROW_TILE, D_MODEL, CTX_LEN)

    n_mod = 32
    cpad = jnp.zeros((n_mod, d), F32).at[:b].set(c).at[b].set(c_ctx)
    mod = _modulation(cpad, ada_w, ada_b)

    taba = _rope_tables(seq, DA_QK, 256)
    tabb = _rope_tables(seq, GQ_HD, 256)
    fg = final_norm_g.reshape(1, d)
    wp_all, wo_all, sgw_all = _pack_w_in(w_in), w_out.astype(BF16), sg_w.astype(BF16)

    nl = DEPTH
    modsel_all = jnp.concatenate([jnp.broadcast_to(mod[:, b].reshape(nl, 1, 1, 3, d), (nl, b, 1, 3, d)),
                                  mod[:, :b].reshape(nl, b, 1, 3, d)], axis=2)
    w2_all = (jnp.zeros((nl, LANES, 256), F32)
              .at[:, R_LANE:R_LANE + GLA_RANK, 0:128].set(gla_w2_f)
              .at[:, R_LANE + GLA_RANK:R_LANE + 2 * GLA_RANK, 128:256].set(gla_w2_b)).astype(BF16)
    b2_all = jnp.concatenate([gla_b_f, gla_b_b], axis=1).reshape(nl, 1, 256)
    sgb_all = jnp.repeat(jnp.swapaxes(sg_b, 1, 2), BRANCH_W // SG_GROUPS, axis=2)
    tiled = lambda g, n: jnp.tile(g, (1, n)).reshape(nl, 1, g.shape[1] * n)
    gqq_all, gqk_all = tiled(gq_qnorm_g, GQ_HEADS), tiled(gq_knorm_g, GQ_KV_HEADS)
    subln_all, glan_all = tiled(da_subln_g, DA_HEADS), tiled(gla_norm_g, GLA_HEADS)
    lam_all = jnp.stack([da_lq1, da_lk1, da_lq2, da_lk2], axis=1)

    stream_in = (ctx, x)
    for i in range(DEPTH):
        lam_init = 0.8 - 0.6 * math.exp(-0.3 * i)
        (xs, aqt, ak, avt, bqt, bk, bvt, gate, yc, dqk, dv, dg) = _layer(
            i, stream_in, modsel_all[i], norm_g[i].reshape(1, d), wp_all, taba, tabb,
            gqq_all[i], gqk_all[i], sg_ln_g[i].reshape(1, 256), sg_ln_b[i].reshape(1, 256),
            sgw_all, sgb_all[i], w2_all[i], b2_all[i])
        ya = _attention(functools.partial(_diff_attn_kernel, lam_init), 2 * DA_HEADS, aqt, ak, avt, gate, 0,
                        [lam_all[i], subln_all[i]], "diff_attn")
        yb = _attention(_gqa_attn_kernel, GQ_HEADS, bqt, bk, bvt, gate, 1, [], "gqa_attn")
        yd = _gla(dqk, dv, dg, gate, glan_all[i])
        stream_in = (xs, modsel_all[i], ya, yb, yc, yd, wo_all)
    return _out_proj(*stream_in, fg)
```

```python
import functools
import math

import jax
import jax.numpy as jnp
import numpy as np
from jax import lax
from jax.experimental import pallas as pl
from jax.experimental.pallas import tpu as pltpu

F32 = jnp.float32
BF16 = jnp.bfloat16

D_MODEL = 1024
DEPTH = 4
CTX_LEN = 256
GRID_W = 64
BRANCH_W = 256
ROPE_THETA = 10000.0
EPS = 1e-6
DA_HEADS = 4
DA_QK = 32
DA_V = 64
GQ_HEADS = 4
GQ_KV_HEADS = 2
GQ_HD = 64
SG_GROUPS = 4
SG_CHUNK = 128
GLA_HEADS = 4
GLA_DV = 64
GLA_DK = 32
GLA_RANK = 16
GLA_NORMALIZER = 16.0
GLA_CHUNK = 32

LANES = 128
ROW_TILE = 256
KV_CHUNK = 512
GLA_TILE = 128
GLA_BATCH = 2
GLA_STEP_TILES = 3
VMEM_LIMIT = 56 * 1024 * 1024

C_AQ, C_AK, C_AV, C_AZ = 0, 256, 512, 1024
C_BQ, C_BVK, C_BZ = 1280, 1536, 1792
C_CU, C_CV, C_CZ = 2048, 2304, 2560
C_DQK, C_DV, C_DZ = 2816, 3072, 3328
P_PACK = 3584
R_LANE = 64
LOG2E = math.log2(math.e)


def _silu(x):
    return x * (1.0 / (1.0 + jnp.exp(-x)))


def _group_mean_matrix(width, group):
    r = lax.broadcasted_iota(jnp.int32, (width, width), 0) // group
    c = lax.broadcasted_iota(jnp.int32, (width, width), 1) // group
    return jnp.where(r == c, 1.0 / group, 0.0).astype(BF16)


def _split2(x):
    hi = x.astype(BF16)
    return hi, (x - hi.astype(F32)).astype(BF16)


def _group_mean(x, mat):
    return sum(jnp.dot(p, mat, preferred_element_type=F32) for p in _split2(x))


def _rope(x, cos, s_lo, s_hi, shift):
    outs = []
    for c in range(x.shape[1] // LANES):
        sl = slice(c * LANES, (c + 1) * LANES)
        xc = x[:, sl]
        up = pltpu.roll(xc, LANES - shift, axis=1)
        dn = pltpu.roll(xc, shift, axis=1)
        outs.append(xc * cos[:, sl] + up * s_lo[:, sl] + dn * s_hi[:, sl])
    return outs[0] if len(outs) == 1 else jnp.concatenate(outs, axis=1)


def _with_ones(v):
    lane = lax.broadcasted_iota(jnp.int32, v.shape, 1) % LANES
    return jnp.where(lane < LANES // 2, v, 1.0)


def _mod_kernel(c_ref, w_ref, b_ref, o_ref):
    s = _silu(c_ref[...]).astype(BF16)
    o_ref[0] = jnp.dot(s, w_ref[0].astype(BF16), preferred_element_type=F32) + b_ref[0]


def _modulation(cpad, ada_w, ada_b):
    n = cpad.shape[0]
    return pl.pallas_call(
        _mod_kernel,
        grid=(DEPTH, 3),
        in_specs=[
            pl.BlockSpec((n, D_MODEL), lambda i, j: (0, 0)),
            pl.BlockSpec((1, D_MODEL, D_MODEL), lambda i, j: (i, 0, j)),
            pl.BlockSpec((1, 1, D_MODEL), lambda i, j: (i, 0, j)),
        ],
        out_specs=pl.BlockSpec((1, n, D_MODEL), lambda i, j: (i, 0, j)),
        out_shape=jax.ShapeDtypeStruct((DEPTH, n, 3 * D_MODEL), F32),
        compiler_params=pltpu.CompilerParams(vmem_limit_bytes=VMEM_LIMIT),
        name="modulation",
    )(cpad, ada_w, ada_b.reshape(DEPTH, 1, 3 * D_MODEL))


def _layer_kernel(first, *refs):
    if first:
        ctx_ref, x_ref = refs[:2]
        refs = refs[2:]
    else:
        xs_ref, modp_ref, ya_ref, yb_ref, yc_in_ref, yd_ref, wo_ref = refs[:7]
        refs = refs[7:]
    (mod_ref, ng_ref, w_ref, taba_ref, tabb_ref, gqq_ref, gqk_ref, lng_ref, lnb_ref, sgw_ref,
     sgb_ref, w2_ref, b2_ref,
     xs_out_ref, aq_ref, ak_ref, av_ref, bq_ref, bk_ref, bv_ref, gate_ref, yc_ref,
     dqk_ref, dv_ref, dg_ref) = refs

    if first:
        x = jnp.where(pl.program_id(1) == 0, ctx_ref[0], x_ref[0])
    else:
        y_prev = jnp.concatenate([ya_ref[0], yb_ref[0], yc_in_ref[0], yd_ref[0]], axis=1)
        x = xs_ref[0] + modp_ref[0, 0, 2:3, :] * jnp.dot(y_prev, wo_ref[...],
                                                         preferred_element_type=F32)
    xs_out_ref[0] = x
    shift = mod_ref[0, 0, 0:1, :]
    scale = mod_ref[0, 0, 1:2, :]
    y = x * lax.rsqrt(jnp.mean(x * x, axis=-1, keepdims=True) + EPS) * ng_ref[...]
    hb = (y * (1.0 + scale) + shift).astype(BF16)

    def proj(lo, width):
        return jnp.dot(hb, w_ref[:, lo:lo + width], preferred_element_type=F32)

    m64 = _group_mean_matrix(BRANCH_W, GQ_HD)
    half_lane = lax.broadcasted_iota(jnp.int32, (ROW_TILE, LANES), 1) < LANES // 2
    p_bq = proj(C_BQ, 256)
    p_bvk = proj(C_BVK, 256)
    p_cv = proj(C_CV, 256)
    p_av = proj(C_AV, 512)

    bq_sq = _split2(p_bq * p_bq)
    bk = jnp.where(half_lane, pltpu.roll(p_bvk[:, :LANES], LANES // 2, axis=1), p_bvk[:, LANES:])
    bk_sq = _split2(bk * bk)
    mu = jnp.mean(p_cv, axis=-1, keepdims=True)
    cen = p_cv - mu
    var = jnp.mean(cen * cen, axis=-1, keepdims=True)
    vn = (cen * lax.rsqrt(var + EPS) * lng_ref[...] + lnb_ref[...]).astype(BF16)
    r = p_av[:, :LANES].astype(BF16)
    av_ref[0, 0] = _with_ones(p_av).T.astype(BF16)
    bv_ref[0, 0] = _with_ones(p_bvk).T.astype(BF16)

    ca, sa_lo, sa_hi = taba_ref[0], taba_ref[1], taba_ref[2]
    aq = _rope(proj(C_AQ, 256), ca, sa_lo, sa_hi, DA_QK // 4) * (DA_QK ** -0.5 * LOG2E)
    aq_ref[0] = aq.T.astype(BF16)
    ak = _rope(proj(C_AK, 256), ca, sa_lo, sa_hi, DA_QK // 4)
    ak_ref[0] = ak.astype(BF16)

    bq_ms = sum(jnp.dot(p, m64, preferred_element_type=F32) for p in bq_sq)
    bk_ms = sum(jnp.dot(p, m64[:LANES, :LANES], preferred_element_type=F32) for p in bk_sq)
    gl = jnp.dot(r, w2_ref[...], preferred_element_type=F32) + b2_ref[...]

    gate_ref[0, :, 0:256] = _silu(proj(C_AZ, 256)).astype(BF16)
    gate_ref[0, :, 256:512] = _silu(proj(C_BZ, 256)).astype(BF16)
    gate_ref[0, :, 512:768] = _silu(proj(C_DZ, 256)).astype(BF16)

    lane_group = lax.broadcasted_iota(jnp.int32, (SG_CHUNK, BRANCH_W), 1) // (BRANCH_W // SG_GROUPS)
    mixed = []
    for n in range(ROW_TILE // SG_CHUNK):
        vchunk = vn[n * SG_CHUNK:(n + 1) * SG_CHUNK, :]
        acc = jnp.zeros((SG_CHUNK, BRANCH_W), F32)
        for g in range(SG_GROUPS):
            acc = jnp.where(lane_group == g,
                            jnp.dot(sgw_ref[g], vchunk, preferred_element_type=F32), acc)
        mixed.append(acc + sgb_ref[...])

    cb, sb_lo, sb_hi = tabb_ref[0], tabb_ref[1], tabb_ref[2]
    bq = p_bq * lax.rsqrt(bq_ms + EPS) * gqq_ref[...]
    bq_ref[0] = (_rope(bq, cb, sb_lo, sb_hi, GQ_HD // 4) * (GQ_HD ** -0.5 * LOG2E)).T.astype(BF16)
    bk = bk * lax.rsqrt(bk_ms + EPS) * gqk_ref[...]
    bk = _rope(bk, cb[:, :LANES], sb_lo[:, :LANES], sb_hi[:, :LANES], GQ_HD // 4)
    bk_ref[0] = bk.astype(BF16)

    yc = proj(C_CU, 256) * jnp.concatenate(mixed, axis=0) * _silu(proj(C_CZ, 256))
    yc_ref[0] = yc.astype(BF16)

    q_scale = jnp.where(lax.broadcasted_iota(jnp.int32, (1, 256), 1) < 128, GLA_DK ** -0.5, 1.0)
    dqk_ref[0] = proj(C_DQK, 256) * q_scale
    dv_ref[0] = proj(C_DV, 256).astype(BF16)
    log_sig = jnp.minimum(gl, 0.0) - jnp.log(1.0 + jnp.exp(-jnp.abs(gl)))
    dg_ref[0] = log_sig * (1.0 / GLA_NORMALIZER)


def _layer(layer, stream_in, modsel, ng, wp, taba, tabb, gqq, gqk, lng, lnb, sgw, sgb, w2, b2):
    first = layer == 0
    t = ROW_TILE
    stacked = lambda a, idx: pl.BlockSpec((None,) + a.shape[1:], lambda i, j: (idx,) + (0,) * (a.ndim - 1))
    b = stream_in[0].shape[0]
    s = CTX_LEN + stream_in[1].shape[1] if first else stream_in[0].shape[1]
    row = lambda width: pl.BlockSpec((1, t, width), lambda i, j: (i, j, 0))
    colT = lambda height: pl.BlockSpec((1, height, t), lambda i, j: (i, 0, j))
    full = lambda a: pl.BlockSpec(a.shape, lambda i, j: (0,) * a.ndim)
    tab = pl.BlockSpec((3, t, 256), lambda i, j: (0, j, 0))
    mods = pl.BlockSpec((1, 1, 3, D_MODEL), lambda i, j: (i, jnp.minimum(j, 1), 0, 0))
    shp = lambda width, dt: jax.ShapeDtypeStruct((b, s, width), dt)
    shpT = lambda height: jax.ShapeDtypeStruct((b, height, s), BF16)
    tileT = lambda height: pl.BlockSpec((1, 1, height, t), lambda i, j: (i, j, 0, 0))
    if first:
        stream_specs = [pl.BlockSpec((1, t, D_MODEL), lambda i, j: (i, 0, 0)),
                        pl.BlockSpec((1, t, D_MODEL), lambda i, j: (i, jnp.maximum(j - 1, 0), 0))]
    else:
        stream_specs = [row(D_MODEL), mods, row(256), row(256), row(256), row(256),
                        stacked(stream_in[6], layer - 1)]
    return pl.pallas_call(
        functools.partial(_layer_kernel, first),
        grid=(b, s // t),
        in_specs=stream_specs + [
            mods, full(ng), stacked(wp, layer), tab, tab, full(gqq), full(gqk), full(lng), full(lnb),
            stacked(sgw, layer), full(sgb), full(w2), full(b2),
        ],
        out_specs=[row(D_MODEL), colT(256), row(256), tileT(512), colT(256), row(128), tileT(256),
                   row(768), row(256), row(256), row(256), row(256)],
        out_shape=[shp(D_MODEL, F32),
                   shpT(256), shp(256, BF16), jax.ShapeDtypeStruct((b, s // t, 512, t), BF16),
                   shpT(256), shp(128, BF16), jax.ShapeDtypeStruct((b, s // t, 256, t), BF16),
                   shp(768, BF16), shp(256, BF16), shp(256, F32), shp(256, BF16), shp(256, F32)],
        compiler_params=pltpu.CompilerParams(
            dimension_semantics=("parallel", "arbitrary"), vmem_limit_bytes=VMEM_LIMIT),
        name="layer_first" if first else "layer",
    )(*stream_in, modsel, ng, wp, taba, tabb, gqq, gqk, lng, lnb, sgw, sgb, w2, b2)


def _max_over_rows(s):
    m = s
    for part in (256, 32):
        if m.shape[0] > part and m.shape[0] % part == 0:
            m = jnp.max(m.reshape(m.shape[0] // part, part, m.shape[1]), axis=0)
    return jnp.max(m, axis=0, keepdims=True)


def _online_softmax_pv(latent, qts, k_ref, vt_ref, v_row_of_job, acc_ref, m_ref, s_ref):
    n_jobs = len(qts)
    tiles_per_chunk = KV_CHUNK // ROW_TILE
    n_chunks = (k_ref.shape[1] - CTX_LEN) // KV_CHUNK

    def qk(k_rows, j):
        return jnp.dot(k_rows, qts[j], preferred_element_type=F32)

    def softmax_pv(s, vt, j, first):
        m_new = _max_over_rows(s)
        if not first:
            m_old = m_ref[j]
            m_new = jnp.maximum(m_old, m_new)
            alpha = jnp.exp2(m_old - m_new)
        m_ref[j] = m_new
        upd = jnp.dot(vt, jnp.exp2(s - m_new).astype(BF16), preferred_element_type=F32)
        acc_ref[j] = upd if first else acc_ref[j] * alpha + upd

    def vt_tile(tile, j):
        return vt_ref[0, tile, v_row_of_job[j]:v_row_of_job[j] + LANES, :]

    def vt_chunk(c, j):
        t0 = CTX_LEN // ROW_TILE + c * tiles_per_chunk
        return jnp.concatenate([vt_tile(t0 + i, j) for i in range(tiles_per_chunk)], axis=1)

    def k_chunk(c):
        return k_ref[0, CTX_LEN + c * KV_CHUNK:CTX_LEN + (c + 1) * KV_CHUNK, :]

    k_ctx = k_ref[0, 0:CTX_LEN, :]

    @pl.when(jnp.logical_not(latent))
    def _():
        s = [qk(k_ctx, j) for j in range(n_jobs)]
        for j in range(n_jobs):
            softmax_pv(s[j], vt_tile(0, j), j, True)

    @pl.when(latent)
    def _():
        s = [qk(k_ctx, j) for j in range(n_jobs)]
        k_next = k_chunk(0)
        s_ref[0, 0] = qk(k_next, 0)
        for j in range(n_jobs):
            softmax_pv(s[j], vt_tile(0, j), j, True)
            if j + 1 < n_jobs:
                s_ref[0, j + 1] = qk(k_next, j + 1)

        for c in range(n_chunks):
            slot = c % 2
            more = c + 1 < n_chunks
            if more:
                k_next = k_chunk(c + 1)
                s_ref[1 - slot, 0] = qk(k_next, 0)
            for j in range(n_jobs):
                softmax_pv(s_ref[slot, j], vt_chunk(c, j), j, False)
                if more and j + 1 < n_jobs:
                    s_ref[1 - slot, j + 1] = qk(k_next, j + 1)

    outs = []
    for j in range(n_jobs):
        acc = acc_ref[j]
        outs.append(acc[:LANES // 2, :] * (1.0 / acc[LANES // 2:LANES // 2 + 1, :]))
    return outs


def _rows_at(x, row0, total):
    parts = []
    if row0:
        parts.append(jnp.zeros((row0, x.shape[1]), x.dtype))
    parts.append(x)
    if total - row0 - x.shape[0]:
        parts.append(jnp.zeros((total - row0 - x.shape[0], x.shape[1]), x.dtype))
    return jnp.concatenate(parts, axis=0)


def _diff_attn_kernel(lam_init, qt_ref, k_ref, vt_ref, gate_ref, lam_ref, sg_ref, o_ref,
                      acc_ref, m_ref, s_ref):
    lq1, lk1, lq2, lk2 = lam_ref[0:1, :], lam_ref[1:2, :], lam_ref[2:3, :], lam_ref[3:4, :]
    lam = (jnp.exp(jnp.sum(lq1 * lk1, axis=-1, keepdims=True))
           - jnp.exp(jnp.sum(lq2 * lk2, axis=-1, keepdims=True)) + lam_init)
    qts = [_rows_at(qt_ref[0, job * DA_QK:(job + 1) * DA_QK, :], job * DA_QK, BRANCH_W)
           for job in range(2 * DA_HEADS)]
    o = _online_softmax_pv(pl.program_id(1) > 0, qts, k_ref, vt_ref,
                           [(job // 2) * LANES for job in range(2 * DA_HEADS)], acc_ref, m_ref, s_ref)
    yt = jnp.concatenate([o[2 * h] - lam * o[2 * h + 1] for h in range(DA_HEADS)], axis=0)
    y = yt.T
    ms = _group_mean(y * y, _group_mean_matrix(BRANCH_W, DA_V))
    y = y * lax.rsqrt(ms + EPS) * sg_ref[...] * (1.0 - lam_init)
    o_ref[0] = (y * gate_ref[0].astype(F32)).astype(BF16)


def _gqa_attn_kernel(qt_ref, k_ref, vt_ref, gate_ref, o_ref, acc_ref, m_ref, s_ref):
    grp = GQ_HEADS // GQ_KV_HEADS
    qts = [_rows_at(qt_ref[0, h * GQ_HD:(h + 1) * GQ_HD, :], (h // grp) * GQ_HD, GQ_KV_HEADS * GQ_HD)
           for h in range(GQ_HEADS)]
    o = _online_softmax_pv(pl.program_id(1) > 0, qts, k_ref, vt_ref,
                           [(h // grp) * LANES for h in range(GQ_HEADS)], acc_ref, m_ref, s_ref)
    y = jnp.concatenate(o, axis=0).T
    o_ref[0] = (y * gate_ref[0].astype(F32)).astype(BF16)


def _attention(kernel, n_jobs, qt, k, vt, gate, gate_block, extra, name):
    b, s, _ = k.shape
    t = ROW_TILE
    full = lambda a: pl.BlockSpec(a.shape, lambda i, j: (0,) * a.ndim)
    return pl.pallas_call(
        kernel,
        grid=(b, s // t),
        in_specs=[
            pl.BlockSpec((1, qt.shape[1], t), lambda i, j: (i, 0, j)),
            pl.BlockSpec((1, s, k.shape[2]), lambda i, j: (i, 0, 0)),
            pl.BlockSpec((1,) + vt.shape[1:], lambda i, j: (i, 0, 0, 0)),
            pl.BlockSpec((1, t, 256), lambda i, j: (i, j, gate_block)),
        ] + [full(a) for a in extra],
        out_specs=pl.BlockSpec((1, t, 256), lambda i, j: (i, j, 0)),
        out_shape=jax.ShapeDtypeStruct((b, s, 256), BF16),
        scratch_shapes=[pltpu.VMEM((n_jobs, LANES, t), F32), pltpu.VMEM((n_jobs, 1, t), F32),
                        pltpu.VMEM((2, n_jobs, KV_CHUNK, t), F32)],
        compiler_params=pltpu.CompilerParams(
            dimension_semantics=("parallel", "arbitrary"), vmem_limit_bytes=VMEM_LIMIT),
        name=name,
    )(qt, k, vt, gate, *extra)


def _chunk_cumsum(g, row_in_chunk, reverse):
    x = g
    sh = 1
    while sh < GLA_CHUNK:
        if reverse:
            nb = pltpu.roll(x, x.shape[0] - sh, axis=0)
            x = x + jnp.where(row_in_chunk < GLA_CHUNK - sh, nb, 0.0)
        else:
            nb = pltpu.roll(x, sh, axis=0)
            x = x + jnp.where(row_in_chunk >= sh, nb, 0.0)
        sh *= 2
    return x


def _gla_kernel(qk_ref, v_ref, g_ref, gate_ref, ng_ref, o_ref, acc_ref, st_ref):
    n_tiles = qk_ref.shape[1] // GLA_TILE
    ctx_tiles = CTX_LEN // GLA_TILE
    per = GLA_TILE // GLA_CHUNK
    ri = lax.broadcasted_iota(jnp.int32, (GLA_TILE, GLA_TILE), 0)
    ci = lax.broadcasted_iota(jnp.int32, (GLA_TILE, GLA_TILE), 1)
    same_chunk = (ri // GLA_CHUNK) == (ci // GLA_CHUNK)
    chunk_of_row = ri // GLA_CHUNK
    row_in_chunk = ri % GLA_CHUNK
    head_of_k = lax.broadcasted_iota(jnp.int32, (GLA_TILE, GLA_HEADS * GLA_DK), 1) // GLA_DK
    head_of_v = lax.broadcasted_iota(jnp.int32, (GLA_TILE, BRANCH_W), 1) // GLA_DV
    st_mask = ((lax.broadcasted_iota(jnp.int32, (BRANCH_W, GLA_HEADS * GLA_DK), 0) // GLA_DV)
               == (lax.broadcasted_iota(jnp.int32, (BRANCH_W, GLA_HEADS * GLA_DK), 1) // GLA_DK))
    m64 = _group_mean_matrix(BRANCH_W, GLA_DV)
    tris = [jnp.where(same_chunk & (ci <= ri), 1.0, 0.0), jnp.where(same_chunk & (ci >= ri), 1.0, 0.0)]
    tris4 = [jnp.concatenate([t] * GLA_HEADS, axis=1) for t in tris]

    nt_dims = (((1,), (1,)), ((), ()))
    tn_dims = (((0,), (0,)), ((), ()))

    def scan_step(i, carry):
        def tile_index(d, sub):
            t = i * GLA_STEP_TILES + sub
            if not d:
                return t
            return jnp.where(t < ctx_tiles, ctx_tiles - 1 - t, n_tiles + ctx_tiles - 1 - t)

        chains = [(bi, d, sub) for bi in range(GLA_BATCH) for d in (0, 1) for sub in range(GLA_STEP_TILES)]
        nc = len(chains)
        rows = [pl.ds(pl.multiple_of(tile_index(d, sub) * GLA_TILE, GLA_TILE), GLA_TILE)
                for _, d, sub in chains]
        ends = [[c * GLA_CHUNK if d else (c + 1) * GLA_CHUNK - 1 for c in range(per)] for _, d, _ in chains]

        cum = []
        for n, (bi, d, _) in enumerate(chains):
            cum.append(_chunk_cumsum(g_ref[bi, rows[n], 128 * d:128 * d + 128], row_in_chunk, bool(d)))

        qe, kd, kl, vb = [], [], [], []
        for n, (bi, d, _) in enumerate(chains):
            q = qk_ref[bi, rows[n], 0:128]
            k = qk_ref[bi, rows[n], 128:256]
            cum_last = jnp.concatenate(
                [jnp.broadcast_to(cum[n][e:e + 1, :], (GLA_CHUNK, 128)) for e in ends[n]], axis=0)
            qe.append(q * jnp.exp(cum[n]))
            kd.append(k * jnp.exp(-cum[n]))
            kl.append(k * jnp.exp(cum_last - cum[n]))
            vb.append(v_ref[bi, rows[n], :])

        qeb = [x.astype(BF16) for x in qe]
        att = []
        for n, (bi, d, _) in enumerate(chains):
            kd_heads = jnp.concatenate(
                [jnp.where(head_of_k == h, kd[n], 0.0).astype(BF16) for h in range(GLA_HEADS)], axis=0)
            a = lax.dot_general(qeb[n], kd_heads, nt_dims, preferred_element_type=F32)
            att.append((a * tris4[d]).astype(BF16))
        o = []
        for n in range(nc):
            v_heads = jnp.concatenate(
                [jnp.where(head_of_v == h, vb[n], jnp.zeros_like(vb[n])) for h in range(GLA_HEADS)], axis=0)
            o.append(jnp.dot(att[n], v_heads, preferred_element_type=F32))

        upd = []
        for n in range(nc):
            kl_chunks = jnp.concatenate(
                [jnp.where(chunk_of_row == c, kl[n], 0.0).astype(BF16) for c in range(per)], axis=1)
            upd.append(lax.dot_general(vb[n], kl_chunks, tn_dims, preferred_element_type=F32))
        seen = [[None] * per for _ in range(nc)]
        for slot in range(2 * GLA_BATCH):
            st = st_ref[slot]
            for n in range(slot * GLA_STEP_TILES, (slot + 1) * GLA_STEP_TILES):
                for c in (range(per - 1, -1, -1) if slot % 2 else range(per)):
                    seen[n][c] = st.astype(BF16)
                    decay = jnp.exp(cum[n][ends[n][c]:ends[n][c] + 1, :])
                    st = st * decay + jnp.where(st_mask, upd[n][:, c * 128:(c + 1) * 128], 0.0)
            st_ref[slot] = st
        for n in range(nc):
            qe_chunks = jnp.concatenate(
                [jnp.where(chunk_of_row == c, qe[n], 0.0).astype(BF16) for c in range(per)], axis=1)
            o_inter = lax.dot_general(qe_chunks, jnp.concatenate(seen[n], axis=1), nt_dims,
                                      preferred_element_type=F32)
            acc_ref[n // GLA_STEP_TILES, rows[n], :] = o[n] + o_inter
        return carry

    st_ref[...] = jnp.zeros_like(st_ref)
    lax.fori_loop(0, n_tiles // GLA_STEP_TILES, scan_step, 0)

    def finish_step(i, carry):
        rows = pl.ds(pl.multiple_of(i * ROW_TILE, ROW_TILE), ROW_TILE)
        tot = [acc_ref[2 * bi, rows, :] + acc_ref[2 * bi + 1, rows, :] for bi in range(GLA_BATCH)]
        sq = [_split2(t * t) for t in tot]
        ms = [sum(jnp.dot(p, m64, preferred_element_type=F32) for p in s) for s in sq]
        for bi in range(GLA_BATCH):
            y = tot[bi] * lax.rsqrt(ms[bi] + EPS) * ng_ref[...]
            o_ref[bi, rows, :] = (y * gate_ref[bi, rows, :].astype(F32)).astype(BF16)
        return carry

    lax.fori_loop(0, qk_ref.shape[1] // ROW_TILE, finish_step, 0)


def _gla(dqk, dv, dg, gate, ng):
    b, s, _ = dqk.shape
    nb = GLA_BATCH
    blk = lambda width, cb: pl.BlockSpec((nb, s, width), lambda i: (i, 0, cb))
    return pl.pallas_call(
        _gla_kernel,
        grid=(b // nb,),
        in_specs=[blk(256, 0), blk(256, 0), blk(256, 0), blk(256, 2),
                  pl.BlockSpec(ng.shape, lambda i: (0, 0))],
        out_specs=blk(256, 0),
        out_shape=jax.ShapeDtypeStruct((b, s, 256), BF16),
        scratch_shapes=[pltpu.VMEM((2 * nb, s, BRANCH_W), F32),
                        pltpu.VMEM((2 * nb, BRANCH_W, GLA_HEADS * GLA_DK), F32)],
        compiler_params=pltpu.CompilerParams(
            dimension_semantics=("parallel",), vmem_limit_bytes=VMEM_LIMIT),
        name="gla",
    )(dqk, dv, dg, gate, ng)


def _out_kernel(x_ref, mod_ref, ya_ref, yb_ref, yc_ref, yd_ref, w_ref, fg_ref, o_ref):
    y = jnp.concatenate([ya_ref[0], yb_ref[0], yc_ref[0], yd_ref[0]], axis=1)
    upd = jnp.dot(y, w_ref[...], preferred_element_type=F32)
    xn = x_ref[0] + mod_ref[0, 0, 2:3, :] * upd
    o_ref[0] = xn * lax.rsqrt(jnp.mean(xn * xn, axis=-1, keepdims=True) + EPS) * fg_ref[...]


def _out_proj(xs, modsel, ya, yb, yc, yd, wo, fg):
    b, s, _ = xs.shape
    t = ROW_TILE
    skip = CTX_LEN // t
    row = lambda width: pl.BlockSpec((1, t, width), lambda i, j: (i, j + skip, 0))
    return pl.pallas_call(
        _out_kernel,
        grid=(b, s // t - skip),
        in_specs=[
            row(D_MODEL),
            pl.BlockSpec((1, 1, 3, D_MODEL), lambda i, j: (i, 1, 0, 0)),
            row(256), row(256), row(256), row(256),
            pl.BlockSpec((None,) + wo.shape[1:], lambda i, j: (wo.shape[0] - 1, 0, 0)),
            pl.BlockSpec(fg.shape, lambda i, j: (0, 0)),
        ],
        out_specs=pl.BlockSpec((1, t, D_MODEL), lambda i, j: (i, j, 0)),
        out_shape=jax.ShapeDtypeStruct((b, s - skip * t, D_MODEL), F32),
        compiler_params=pltpu.CompilerParams(
            dimension_semantics=("parallel", "parallel"), vmem_limit_bytes=VMEM_LIMIT),
        name="out_proj_final",
    )(xs, modsel, ya, yb, yc, yd, wo, fg)


def _rope_tables(seq, dim, width):
    half = dim // 2
    quarter = half // 2
    lane = np.arange(width) % dim
    freq = ROPE_THETA ** (-(2.0 * (lane % quarter)) / half)
    pos_t = np.arange(seq)
    pos = np.where(lane[None, :] < half, (pos_t // GRID_W)[:, None], (pos_t % GRID_W)[:, None])
    ang = pos * freq[None, :]
    cos, sin = np.cos(ang), np.sin(ang)
    first = (lane % half) < quarter
    s_lo = np.where(first[None, :], -sin, 0.0)
    s_hi = np.where(first[None, :], 0.0, sin)
    lat = np.stack([cos, s_lo, s_hi])
    ctx = np.stack([np.ones((CTX_LEN, width)), np.zeros((CTX_LEN, width)), np.zeros((CTX_LEN, width))])
    return jnp.asarray(np.concatenate([ctx, lat], axis=1), dtype=F32)


def _pack_w_in(w):
    offs = [0]
    for n in (256, 256, 256, 256, 256, 128, 128, 256, 256, 256, 256, 128, 128, 256, 256, 16, 16):
        offs.append(offs[-1] + n)
    seg = [w[..., offs[i]:offs[i + 1]] for i in range(17)]
    aq, ak, av, az, bq, bk, bv, bz, cu, cv, cz, dq, dk, dv, dz, drf, drb = seg
    gap = jnp.zeros(w.shape[:-1] + (LANES // 2,), w.dtype)
    r_gap = jnp.concatenate([drf, drb, gap[..., :LANES // 2 - 2 * GLA_RANK]], axis=-1)
    av_sp = jnp.concatenate([av[..., 0:64], r_gap, av[..., 64:128], gap, av[..., 128:192], gap,
                             av[..., 192:256], gap], axis=-1)
    bvk = jnp.concatenate([bv[..., :64], bk[..., :64], bv[..., 64:], bk[..., 64:]], axis=-1)
    packed = jnp.concatenate([aq, ak, av_sp, az, bq, bvk, bz, cu, cv, cz, dq, dk, dv, dz], axis=-1)
    assert packed.shape[-1] == P_PACK
    return packed.astype(BF16)


def kernel(x, c, ctx, c_ctx, ada_w, ada_b, norm_g, w_in, da_lq1, da_lk1, da_lq2, da_lk2,
           da_subln_g, gq_qnorm_g, gq_knorm_g, sg_ln_g, sg_ln_b, sg_w, sg_b,
           gla_w2_f, gla_b_f, gla_w2_b, gla_b_b, gla_norm_g, w_out, final_norm_g):
    b, seq, d = x.shape
    assert (seq, d, ctx.shape[1]) == (seq // ROW_TILE * ROW_TILE, D_MODEL, CTX_LEN)

    n_mod = 32
    cpad = jnp.zeros((n_mod, d), F32).at[:b].set(c).at[b].set(c_ctx)
    mod = _modulation(cpad, ada_w, ada_b)

    taba = _rope_tables(seq, DA_QK, 256)
    tabb = _rope_tables(seq, GQ_HD, 256)
    fg = final_norm_g.reshape(1, d)
    wp_all, wo_all, sgw_all = _pack_w_in(w_in), w_out.astype(BF16), sg_w.astype(BF16)

    nl = DEPTH
    modsel_all = jnp.concatenate([jnp.broadcast_to(mod[:, b].reshape(nl, 1, 1, 3, d), (nl, b, 1, 3, d)),
                                  mod[:, :b].reshape(nl, b, 1, 3, d)], axis=2)
    w2_all = (jnp.zeros((nl, LANES, 256), F32)
              .at[:, R_LANE:R_LANE + GLA_RANK, 0:128].set(gla_w2_f)
              .at[:, R_LANE + GLA_RANK:R_LANE + 2 * GLA_RANK, 128:256].set(gla_w2_b)).astype(BF16)
    b2_all = jnp.concatenate([gla_b_f, gla_b_b], axis=1).reshape(nl, 1, 256)
    sgb_all = jnp.repeat(jnp.swapaxes(sg_b, 1, 2), BRANCH_W // SG_GROUPS, axis=2)
    tiled = lambda g, n: jnp.tile(g, (1, n)).reshape(nl, 1, g.shape[1] * n)
    gqq_all, gqk_all = tiled(gq_qnorm_g, GQ_HEADS), tiled(gq_knorm_g, GQ_KV_HEADS)
    subln_all, glan_all = tiled(da_subln_g, DA_HEADS), tiled(gla_norm_g, GLA_HEADS)
    lam_all = jnp.stack([da_lq1, da_lk1, da_lq2, da_lk2], axis=1)

    stream_in = (ctx, x)
    for i in range(DEPTH):
        lam_init = 0.8 - 0.6 * math.exp(-0.3 * i)
        (xs, aqt, ak, avt, bqt, bk, bvt, gate, yc, dqk, dv, dg) = _layer(
            i, stream_in, modsel_all[i], norm_g[i].reshape(1, d), wp_all, taba, tabb,
            gqq_all[i], gqk_all[i], sg_ln_g[i].reshape(1, 256), sg_ln_b[i].reshape(1, 256),
            sgw_all, sgb_all[i], w2_all[i], b2_all[i])
        ya = _attention(functools.partial(_diff_attn_kernel, lam_init), 2 * DA_HEADS, aqt, ak, avt, gate, 0,
                        [lam_all[i], subln_all[i]], "diff_attn")
        yb = _attention(_gqa_attn_kernel, GQ_HEADS, bqt, bk, bvt, gate, 1, [], "gqa_attn")
        yd = _gla(dqk, dv, dg, gate, glan_all[i])
        stream_in = (xs, modsel_all[i], ya, yb, yc, yd, wo_all)
    return _out_proj(*stream_in, fg)
```

```python
import functools
import math

import jax
import jax.numpy as jnp
import numpy as np
from jax import lax
from jax.experimental import pallas as pl
from jax.experimental.pallas import tpu as pltpu

F32 = jnp.float32
BF16 = jnp.bfloat16

D_MODEL = 1024
DEPTH = 4
CTX_LEN = 256
GRID_W = 64
BRANCH_W = 256
ROPE_THETA = 10000.0
EPS = 1e-6
DA_HEADS = 4
DA_QK = 32
DA_V = 64
GQ_HEADS = 4
GQ_KV_HEADS = 2
GQ_HD = 64
SG_GROUPS = 4
SG_CHUNK = 128
GLA_HEADS = 4
GLA_DV = 64
GLA_DK = 32
GLA_RANK = 16
GLA_NORMALIZER = 16.0
GLA_CHUNK = 32

LANES = 128
ROW_TILE = 256
KV_CHUNK = 512
GLA_TILE = 128
GLA_BATCH = 2
GLA_STEP_TILES = 3
VMEM_LIMIT = 56 * 1024 * 1024

C_AQ, C_AK, C_AV, C_AZ = 0, 256, 512, 1024
C_BQ, C_BVK, C_BZ = 1280, 1536, 1792
C_CU, C_CV, C_CZ = 2048, 2304, 2560
C_DQK, C_DV, C_DZ = 2816, 3072, 3328
P_PACK = 3584
R_LANE = 64
LOG2E = math.log2(math.e)


def _silu(x):
    return x * (1.0 / (1.0 + jnp.exp(-x)))


def _group_mean_matrix(width, group):
    r = lax.broadcasted_iota(jnp.int32, (width, width), 0) // group
    c = lax.broadcasted_iota(jnp.int32, (width, width), 1) // group
    return jnp.where(r == c, 1.0 / group, 0.0).astype(BF16)


def _split2(x):
    hi = x.astype(BF16)
    return hi, (x - hi.astype(F32)).astype(BF16)


def _group_mean(x, mat):
    return sum(jnp.dot(p, mat, preferred_element_type=F32) for p in _split2(x))


def _rope(x, cos, s_lo, s_hi, shift):
    outs = []
    for c in range(x.shape[1] // LANES):
        sl = slice(c * LANES, (c + 1) * LANES)
        xc = x[:, sl]
        up = pltpu.roll(xc, LANES - shift, axis=1)
        dn = pltpu.roll(xc, shift, axis=1)
        outs.append(xc * cos[:, sl] + up * s_lo[:, sl] + dn * s_hi[:, sl])
    return outs[0] if len(outs) == 1 else jnp.concatenate(outs, axis=1)


def _with_ones(v):
    lane = lax.broadcasted_iota(jnp.int32, v.shape, 1) % LANES
    return jnp.where(lane < LANES // 2, v, 1.0)


def _mod_kernel(c_ref, w_ref, b_ref, o_ref):
    s = _silu(c_ref[...]).astype(BF16)
    o_ref[0] = jnp.dot(s, w_ref[0].astype(BF16), preferred_element_type=F32) + b_ref[0]


def _modulation(cpad, ada_w, ada_b):
    n = cpad.shape[0]
    return pl.pallas_call(
        _mod_kernel,
        grid=(DEPTH, 3),
        in_specs=[
            pl.BlockSpec((n, D_MODEL), lambda i, j: (0, 0)),
            pl.BlockSpec((1, D_MODEL, D_MODEL), lambda i, j: (i, 0, j)),
            pl.BlockSpec((1, 1, D_MODEL), lambda i, j: (i, 0, j)),
        ],
        out_specs=pl.BlockSpec((1, n, D_MODEL), lambda i, j: (i, 0, j)),
        out_shape=jax.ShapeDtypeStruct((DEPTH, n, 3 * D_MODEL), F32),
        compiler_params=pltpu.CompilerParams(vmem_limit_bytes=VMEM_LIMIT),
        name="modulation",
    )(cpad, ada_w, ada_b.reshape(DEPTH, 1, 3 * D_MODEL))


def _layer_kernel(first, *refs):
    if first:
        ctx_ref, x_ref = refs[:2]
        refs = refs[2:]
    else:
        xs_ref, modp_ref, ya_ref, yb_ref, yc_in_ref, yd_ref, wo_ref = refs[:7]
        refs = refs[7:]
    (mod_ref, ng_ref, w_ref, taba_ref, tabb_ref, gqq_ref, gqk_ref, lng_ref, lnb_ref, sgw_ref,
     sgb_ref, w2_ref, b2_ref,
     xs_out_ref, aq_ref, ak_ref, av_ref, bq_ref, bk_ref, bv_ref, gate_ref, yc_ref,
     dqk_ref, dv_ref, dg_ref) = refs

    if first:
        x = jnp.where(pl.program_id(1) == 0, ctx_ref[0], x_ref[0])
    else:
        y_prev = jnp.concatenate([ya_ref[0], yb_ref[0], yc_in_ref[0], yd_ref[0]], axis=1)
        x = xs_ref[0] + modp_ref[0, 0, 2:3, :] * jnp.dot(y_prev, wo_ref[...],
                                                         preferred_element_type=F32)
    xs_out_ref[0] = x
    shift = mod_ref[0, 0, 0:1, :]
    scale = mod_ref[0, 0, 1:2, :]
    y = x * lax.rsqrt(jnp.mean(x * x, axis=-1, keepdims=True) + EPS) * ng_ref[...]
    hb = (y * (1.0 + scale) + shift).astype(BF16)

    def proj(lo, width):
        return jnp.dot(hb, w_ref[:, lo:lo + width], preferred_element_type=F32)

    m64 = _group_mean_matrix(BRANCH_W, GQ_HD)
    half_lane = lax.broadcasted_iota(jnp.int32, (ROW_TILE, LANES), 1) < LANES // 2
    p_bq = proj(C_BQ, 256)
    p_bvk = proj(C_BVK, 256)
    p_cv = proj(C_CV, 256)
    p_av = proj(C_AV, 512)

    bq_sq = _split2(p_bq * p_bq)
    bk = jnp.where(half_lane, pltpu.roll(p_bvk[:, :LANES], LANES // 2, axis=1), p_bvk[:, LANES:])
    bk_sq = _split2(bk * bk)
    mu = jnp.mean(p_cv, axis=-1, keepdims=True)
    cen = p_cv - mu
    var = jnp.mean(cen * cen, axis=-1, keepdims=True)
    vn = (cen * lax.rsqrt(var + EPS) * lng_ref[...] + lnb_ref[...]).astype(BF16)
    r = p_av[:, :LANES].astype(BF16)
    av_ref[0, 0] = _with_ones(p_av).T.astype(BF16)
    bv_ref[0, 0] = _with_ones(p_bvk).T.astype(BF16)

    ca, sa_lo, sa_hi = taba_ref[0], taba_ref[1], taba_ref[2]
    aq = _rope(proj(C_AQ, 256), ca, sa_lo, sa_hi, DA_QK // 4) * (DA_QK ** -0.5 * LOG2E)
    aq_ref[0] = aq.T.astype(BF16)
    ak = _rope(proj(C_AK, 256), ca, sa_lo, sa_hi, DA_QK // 4)
    ak_ref[0] = ak.astype(BF16)

    bq_ms = sum(jnp.dot(p, m64, preferred_element_type=F32) for p in bq_sq)
    bk_ms = sum(jnp.dot(p, m64[:LANES, :LANES], preferred_element_type=F32) for p in bk_sq)
    gl = jnp.dot(r, w2_ref[...], preferred_element_type=F32) + b2_ref[...]

    gate_ref[0, :, 0:256] = _silu(proj(C_AZ, 256)).astype(BF16)
    gate_ref[0, :, 256:512] = _silu(proj(C_BZ, 256)).astype(BF16)
    gate_ref[0, :, 512:768] = _silu(proj(C_DZ, 256)).astype(BF16)

    lane_group = lax.broadcasted_iota(jnp.int32, (SG_CHUNK, BRANCH_W), 1) // (BRANCH_W // SG_GROUPS)
    mixed = []
    for n in range(ROW_TILE // SG_CHUNK):
        vchunk = vn[n * SG_CHUNK:(n + 1) * SG_CHUNK, :]
        acc = jnp.zeros((SG_CHUNK, BRANCH_W), F32)
        for g in range(SG_GROUPS):
            acc = jnp.where(lane_group == g,
                            jnp.dot(sgw_ref[g], vchunk, preferred_element_type=F32), acc)
        mixed.append(acc + sgb_ref[...])

    cb, sb_lo, sb_hi = tabb_ref[0], tabb_ref[1], tabb_ref[2]
    bq = p_bq * lax.rsqrt(bq_ms + EPS) * gqq_ref[...]
    bq_ref[0] = (_rope(bq, cb, sb_lo, sb_hi, GQ_HD // 4) * (GQ_HD ** -0.5 * LOG2E)).T.astype(BF16)
    bk = bk * lax.rsqrt(bk_ms + EPS) * gqk_ref[...]
    bk = _rope(bk, cb[:, :LANES], sb_lo[:, :LANES], sb_hi[:, :LANES], GQ_HD // 4)
    bk_ref[0] = bk.astype(BF16)

    yc = proj(C_CU, 256) * jnp.concatenate(mixed, axis=0) * _silu(proj(C_CZ, 256))
    yc_ref[0] = yc.astype(BF16)

    q_scale = jnp.where(lax.broadcasted_iota(jnp.int32, (1, 256), 1) < 128, GLA_DK ** -0.5, 1.0)
    dqk_ref[0] = proj(C_DQK, 256) * q_scale
    dv_ref[0] = proj(C_DV, 256).astype(BF16)
    log_sig = jnp.minimum(gl, 0.0) - jnp.log(1.0 + jnp.exp(-jnp.abs(gl)))
    dg_ref[0] = log_sig * (1.0 / GLA_NORMALIZER)


def _layer(layer, stream_in, modsel, ng, wp, taba, tabb, gqq, gqk, lng, lnb, sgw, sgb, w2, b2):
    first = layer == 0
    t = ROW_TILE
    stacked = lambda a, idx: pl.BlockSpec((None,) + a.shape[1:], lambda i, j: (idx,) + (0,) * (a.ndim - 1))
    b = stream_in[0].shape[0]
    s = CTX_LEN + stream_in[1].shape[1] if first else stream_in[0].shape[1]
    row = lambda width: pl.BlockSpec((1, t, width), lambda i, j: (i, j, 0))
    colT = lambda height: pl.BlockSpec((1, height, t), lambda i, j: (i, 0, j))
    full = lambda a: pl.BlockSpec(a.shape, lambda i, j: (0,) * a.ndim)
    tab = pl.BlockSpec((3, t, 256), lambda i, j: (0, j, 0))
    mods = pl.BlockSpec((1, 1, 3, D_MODEL), lambda i, j: (i, jnp.minimum(j, 1), 0, 0))
    shp = lambda width, dt: jax.ShapeDtypeStruct((b, s, width), dt)
    shpT = lambda height: jax.ShapeDtypeStruct((b, height, s), BF16)
    tileT = lambda height: pl.BlockSpec((1, 1, height, t), lambda i, j: (i, j, 0, 0))
    if first:
        stream_specs = [pl.BlockSpec((1, t, D_MODEL), lambda i, j: (i, 0, 0)),
                        pl.BlockSpec((1, t, D_MODEL), lambda i, j: (i, jnp.maximum(j - 1, 0), 0))]
    else:
        stream_specs = [row(D_MODEL), mods, row(256), row(256), row(256), row(256),
                        stacked(stream_in[6], layer - 1)]
    return pl.pallas_call(
        functools.partial(_layer_kernel, first),
        grid=(b, s // t),
        in_specs=stream_specs + [
            mods, full(ng), stacked(wp, layer), tab, tab, full(gqq), full(gqk), full(lng), full(lnb),
            stacked(sgw, layer), full(sgb), full(w2), full(b2),
        ],
        out_specs=[row(D_MODEL), colT(256), row(256), tileT(512), colT(256), row(128), tileT(256),
                   row(768), row(256), row(256), row(256), row(256)],
        out_shape=[shp(D_MODEL, F32),
                   shpT(256), shp(256, BF16), jax.ShapeDtypeStruct((b, s // t, 512, t), BF16),
                   shpT(256), shp(128, BF16), jax.ShapeDtypeStruct((b, s // t, 256, t), BF16),
                   shp(768, BF16), shp(256, BF16), shp(256, F32), shp(256, BF16), shp(256, F32)],
        compiler_params=pltpu.CompilerParams(
            dimension_semantics=("parallel", "arbitrary"), vmem_limit_bytes=VMEM_LIMIT),
        name="layer_first" if first else "layer",
    )(*stream_in, modsel, ng, wp, taba, tabb, gqq, gqk, lng, lnb, sgw, sgb, w2, b2)


def _max_over_rows(s):
    m = s
    for part in (256, 32):
        if m.shape[0] > part and m.shape[0] % part == 0:
            m = jnp.max(m.reshape(m.shape[0] // part, part, m.shape[1]), axis=0)
    return jnp.max(m, axis=0, keepdims=True)


def _online_softmax_pv(latent, qts, k_ref, vt_ref, v_row_of_job, acc_ref, m_ref, s_ref):
    n_jobs = len(qts)
    tiles_per_chunk = KV_CHUNK // ROW_TILE
    n_chunks = (k_ref.shape[1] - CTX_LEN) // KV_CHUNK

    def qk(k_rows, j):
        return jnp.dot(k_rows, qts[j], preferred_element_type=F32)

    def softmax_pv(s, vt, j, first):
        m_new = _max_over_rows(s)
        if not first:
            m_old = m_ref[j]
            m_new = jnp.maximum(m_old, m_new)
            alpha = jnp.exp2(m_old - m_new)
        m_ref[j] = m_new
        upd = jnp.dot(vt, jnp.exp2(s - m_new).astype(BF16), preferred_element_type=F32)
        acc_ref[j] = upd if first else acc_ref[j] * alpha + upd

    def vt_tile(tile, j):
        return vt_ref[0, tile, v_row_of_job[j]:v_row_of_job[j] + LANES, :]

    def vt_chunk(c, j):
        t0 = CTX_LEN // ROW_TILE + c * tiles_per_chunk
        return jnp.concatenate([vt_tile(t0 + i, j) for i in range(tiles_per_chunk)], axis=1)

    def k_chunk(c):
        return k_ref[0, CTX_LEN + c * KV_CHUNK:CTX_LEN + (c + 1) * KV_CHUNK, :]

    k_ctx = k_ref[0, 0:CTX_LEN, :]

    @pl.when(jnp.logical_not(latent))
    def _():
        s = [qk(k_ctx, j) for j in range(n_jobs)]
        for j in range(n_jobs):
            softmax_pv(s[j], vt_tile(0, j), j, True)

    @pl.when(latent)
    def _():
        s = [qk(k_ctx, j) for j in range(n_jobs)]
        k_next = k_chunk(0)
        s_ref[0, 0] = qk(k_next, 0)
        for j in range(n_jobs):
            softmax_pv(s[j], vt_tile(0, j), j, True)
            if j + 1 < n_jobs:
                s_ref[0, j + 1] = qk(k_next, j + 1)

        for c in range(n_chunks):
            slot = c % 2
            more = c + 1 < n_chunks
            if more:
                k_next = k_chunk(c + 1)
                s_ref[1 - slot, 0] = qk(k_next, 0)
            for j in range(n_jobs):
                softmax_pv(s_ref[slot, j], vt_chunk(c, j), j, False)
                if more and j + 1 < n_jobs:
                    s_ref[1 - slot, j + 1] = qk(k_next, j + 1)

    outs = []
    for j in range(n_jobs):
        acc = acc_ref[j]
        outs.append(acc[:LANES // 2, :] * (1.0 / acc[LANES // 2:LANES // 2 + 1, :]))
    return outs


def _rows_at(x, row0, total):
    parts = []
    if row0:
        parts.append(jnp.zeros((row0, x.shape[1]), x.dtype))
    parts.append(x)
    if total - row0 - x.shape[0]:
        parts.append(jnp.zeros((total - row0 - x.shape[0], x.shape[1]), x.dtype))
    return jnp.concatenate(parts, axis=0)


def _diff_attn_kernel(lam_init, qt_ref, k_ref, vt_ref, gate_ref, lam_ref, sg_ref, o_ref,
                      acc_ref, m_ref, s_ref):
    lq1, lk1, lq2, lk2 = lam_ref[0:1, :], lam_ref[1:2, :], lam_ref[2:3, :], lam_ref[3:4, :]
    lam = (jnp.exp(jnp.sum(lq1 * lk1, axis=-1, keepdims=True))
           - jnp.exp(jnp.sum(lq2 * lk2, axis=-1, keepdims=True)) + lam_init)
    qts = [_rows_at(qt_ref[0, job * DA_QK:(job + 1) * DA_QK, :], job * DA_QK, BRANCH_W)
           for job in range(2 * DA_HEADS)]
    o = _online_softmax_pv(pl.program_id(1) > 0, qts, k_ref, vt_ref,
                           [(job // 2) * LANES for job in range(2 * DA_HEADS)], acc_ref, m_ref, s_ref)
    heads = []
    for h in range(DA_HEADS):
        d = o[2 * h] - lam * o[2 * h + 1]
        heads.append(d * lax.rsqrt(jnp.mean(d * d, axis=0, keepdims=True) + EPS))
    yt = jnp.concatenate(heads, axis=0) * (sg_ref[...] * (1.0 - lam_init))
    o_ref[0] = (yt.T * gate_ref[0].astype(F32)).astype(BF16)


def _gqa_attn_kernel(qt_ref, k_ref, vt_ref, gate_ref, o_ref, acc_ref, m_ref, s_ref):
    grp = GQ_HEADS // GQ_KV_HEADS
    qts = [_rows_at(qt_ref[0, h * GQ_HD:(h + 1) * GQ_HD, :], (h // grp) * GQ_HD, GQ_KV_HEADS * GQ_HD)
           for h in range(GQ_HEADS)]
    o = _online_softmax_pv(pl.program_id(1) > 0, qts, k_ref, vt_ref,
                           [(h // grp) * LANES for h in range(GQ_HEADS)], acc_ref, m_ref, s_ref)
    y = jnp.concatenate(o, axis=0).T
    o_ref[0] = (y * gate_ref[0].astype(F32)).astype(BF16)


def _attention(kernel, n_jobs, qt, k, vt, gate, gate_block, extra, name):
    b, s, _ = k.shape
    t = ROW_TILE
    full = lambda a: pl.BlockSpec(a.shape, lambda i, j: (0,) * a.ndim)
    return pl.pallas_call(
        kernel,
        grid=(b, s // t),
        in_specs=[
            pl.BlockSpec((1, qt.shape[1], t), lambda i, j: (i, 0, j)),
            pl.BlockSpec((1, s, k.shape[2]), lambda i, j: (i, 0, 0)),
            pl.BlockSpec((1,) + vt.shape[1:], lambda i, j: (i, 0, 0, 0)),
            pl.BlockSpec((1, t, 256), lambda i, j: (i, j, gate_block)),
        ] + [full(a) for a in extra],
        out_specs=pl.BlockSpec((1, t, 256), lambda i, j: (i, j, 0)),
        out_shape=jax.ShapeDtypeStruct((b, s, 256), BF16),
        scratch_shapes=[pltpu.VMEM((n_jobs, LANES, t), F32), pltpu.VMEM((n_jobs, 1, t), F32),
                        pltpu.VMEM((2, n_jobs, KV_CHUNK, t), F32)],
        compiler_params=pltpu.CompilerParams(
            dimension_semantics=("parallel", "arbitrary"), vmem_limit_bytes=VMEM_LIMIT),
        name=name,
    )(qt, k, vt, gate, *extra)


def _chunk_cumsum(g, row_in_chunk, reverse):
    x = g
    sh = 1
    while sh < GLA_CHUNK:
        if reverse:
            nb = pltpu.roll(x, x.shape[0] - sh, axis=0)
            x = x + jnp.where(row_in_chunk < GLA_CHUNK - sh, nb, 0.0)
        else:
            nb = pltpu.roll(x, sh, axis=0)
            x = x + jnp.where(row_in_chunk >= sh, nb, 0.0)
        sh *= 2
    return x


def _gla_kernel(qk_ref, v_ref, g_ref, gate_ref, ng_ref, o_ref, acc_ref, st_ref):
    n_tiles = qk_ref.shape[1] // GLA_TILE
    ctx_tiles = CTX_LEN // GLA_TILE
    per = GLA_TILE // GLA_CHUNK
    ri = lax.broadcasted_iota(jnp.int32, (GLA_TILE, GLA_TILE), 0)
    ci = lax.broadcasted_iota(jnp.int32, (GLA_TILE, GLA_TILE), 1)
    same_chunk = (ri // GLA_CHUNK) == (ci // GLA_CHUNK)
    chunk_of_row = ri // GLA_CHUNK
    row_in_chunk = ri % GLA_CHUNK
    head_of_k = lax.broadcasted_iota(jnp.int32, (GLA_TILE, GLA_HEADS * GLA_DK), 1) // GLA_DK
    head_of_v = lax.broadcasted_iota(jnp.int32, (GLA_TILE, BRANCH_W), 1) // GLA_DV
    st_mask = ((lax.broadcasted_iota(jnp.int32, (BRANCH_W, GLA_HEADS * GLA_DK), 0) // GLA_DV)
               == (lax.broadcasted_iota(jnp.int32, (BRANCH_W, GLA_HEADS * GLA_DK), 1) // GLA_DK))
    m64 = _group_mean_matrix(BRANCH_W, GLA_DV)
    tris = [jnp.where(same_chunk & (ci <= ri), 1.0, 0.0), jnp.where(same_chunk & (ci >= ri), 1.0, 0.0)]
    tris4 = [jnp.concatenate([t] * GLA_HEADS, axis=1) for t in tris]

    nt_dims = (((1,), (1,)), ((), ()))
    tn_dims = (((0,), (0,)), ((), ()))

    def scan_step(i, carry):
        def tile_index(d, sub):
            t = i * GLA_STEP_TILES + sub
            if not d:
                return t
            return jnp.where(t < ctx_tiles, ctx_tiles - 1 - t, n_tiles + ctx_tiles - 1 - t)

        chains = [(bi, d, sub) for bi in range(GLA_BATCH) for d in (0, 1) for sub in range(GLA_STEP_TILES)]
        nc = len(chains)
        rows = [pl.ds(pl.multiple_of(tile_index(d, sub) * GLA_TILE, GLA_TILE), GLA_TILE)
                for _, d, sub in chains]
        ends = [[c * GLA_CHUNK if d else (c + 1) * GLA_CHUNK - 1 for c in range(per)] for _, d, _ in chains]

        cum = []
        for n, (bi, d, _) in enumerate(chains):
            cum.append(_chunk_cumsum(g_ref[bi, rows[n], 128 * d:128 * d + 128], row_in_chunk, bool(d)))

        qe, kd, kl, vb = [], [], [], []
        for n, (bi, d, _) in enumerate(chains):
            q = qk_ref[bi, rows[n], 0:128]
            k = qk_ref[bi, rows[n], 128:256]
            cum_last = jnp.concatenate(
                [jnp.broadcast_to(cum[n][e:e + 1, :], (GLA_CHUNK, 128)) for e in ends[n]], axis=0)
            qe.append(q * jnp.exp(cum[n]))
            kd.append(k * jnp.exp(-cum[n]))
            kl.append(k * jnp.exp(cum_last - cum[n]))
            vb.append(v_ref[bi, rows[n], :])

        qeb = [x.astype(BF16) for x in qe]
        att = []
        for n, (bi, d, _) in enumerate(chains):
            kd_heads = jnp.concatenate(
                [jnp.where(head_of_k == h, kd[n], 0.0).astype(BF16) for h in range(GLA_HEADS)], axis=0)
            a = lax.dot_general(qeb[n], kd_heads, nt_dims, preferred_element_type=F32)
            att.append((a * tris4[d]).astype(BF16))
        o = []
        for n in range(nc):
            v_heads = jnp.concatenate(
                [jnp.where(head_of_v == h, vb[n], jnp.zeros_like(vb[n])) for h in range(GLA_HEADS)], axis=0)
            o.append(jnp.dot(att[n], v_heads, preferred_element_type=F32))

        upd = []
        for n in range(nc):
            kl_chunks = jnp.concatenate(
                [jnp.where(chunk_of_row == c, kl[n], 0.0).astype(BF16) for c in range(per)], axis=1)
            upd.append(lax.dot_general(vb[n], kl_chunks, tn_dims, preferred_element_type=F32))
        seen = [[None] * per for _ in range(nc)]
        for slot in range(2 * GLA_BATCH):
            st = st_ref[slot]
            for n in range(slot * GLA_STEP_TILES, (slot + 1) * GLA_STEP_TILES):
                for c in (range(per - 1, -1, -1) if slot % 2 else range(per)):
                    seen[n][c] = st.astype(BF16)
                    decay = jnp.exp(cum[n][ends[n][c]:ends[n][c] + 1, :])
                    st = st * decay + jnp.where(st_mask, upd[n][:, c * 128:(c + 1) * 128], 0.0)
            st_ref[slot] = st
        for n in range(nc):
            qe_chunks = jnp.concatenate(
                [jnp.where(chunk_of_row == c, qe[n], 0.0).astype(BF16) for c in range(per)], axis=1)
            o_inter = lax.dot_general(qe_chunks, jnp.concatenate(seen[n], axis=1), nt_dims,
                                      preferred_element_type=F32)
            acc_ref[n // GLA_STEP_TILES, rows[n], :] = o[n] + o_inter
        return carry

    st_ref[...] = jnp.zeros_like(st_ref)
    lax.fori_loop(0, n_tiles // GLA_STEP_TILES, scan_step, 0)

    def finish_step(i, carry):
        rows = pl.ds(pl.multiple_of(i * ROW_TILE, ROW_TILE), ROW_TILE)
        tot = [acc_ref[2 * bi, rows, :] + acc_ref[2 * bi + 1, rows, :] for bi in range(GLA_BATCH)]
        sq = [_split2(t * t) for t in tot]
        ms = [sum(jnp.dot(p, m64, preferred_element_type=F32) for p in s) for s in sq]
        for bi in range(GLA_BATCH):
            y = tot[bi] * lax.rsqrt(ms[bi] + EPS) * ng_ref[...]
            o_ref[bi, rows, :] = (y * gate_ref[bi, rows, :].astype(F32)).astype(BF16)
        return carry

    lax.fori_loop(0, qk_ref.shape[1] // ROW_TILE, finish_step, 0)


def _gla(dqk, dv, dg, gate, ng):
    b, s, _ = dqk.shape
    nb = GLA_BATCH
    blk = lambda width, cb: pl.BlockSpec((nb, s, width), lambda i: (i, 0, cb))
    return pl.pallas_call(
        _gla_kernel,
        grid=(b // nb,),
        in_specs=[blk(256, 0), blk(256, 0), blk(256, 0), blk(256, 2),
                  pl.BlockSpec(ng.shape, lambda i: (0, 0))],
        out_specs=blk(256, 0),
        out_shape=jax.ShapeDtypeStruct((b, s, 256), BF16),
        scratch_shapes=[pltpu.VMEM((2 * nb, s, BRANCH_W), F32),
                        pltpu.VMEM((2 * nb, BRANCH_W, GLA_HEADS * GLA_DK), F32)],
        compiler_params=pltpu.CompilerParams(
            dimension_semantics=("parallel",), vmem_limit_bytes=VMEM_LIMIT),
        name="gla",
    )(dqk, dv, dg, gate, ng)


def _out_kernel(x_ref, mod_ref, ya_ref, yb_ref, yc_ref, yd_ref, w_ref, fg_ref, o_ref):
    y = jnp.concatenate([ya_ref[0], yb_ref[0], yc_ref[0], yd_ref[0]], axis=1)
    upd = jnp.dot(y, w_ref[...], preferred_element_type=F32)
    xn = x_ref[0] + mod_ref[0, 0, 2:3, :] * upd
    o_ref[0] = xn * lax.rsqrt(jnp.mean(xn * xn, axis=-1, keepdims=True) + EPS) * fg_ref[...]


def _out_proj(xs, modsel, ya, yb, yc, yd, wo, fg):
    b, s, _ = xs.shape
    t = ROW_TILE
    skip = CTX_LEN // t
    row = lambda width: pl.BlockSpec((1, t, width), lambda i, j: (i, j + skip, 0))
    return pl.pallas_call(
        _out_kernel,
        grid=(b, s // t - skip),
        in_specs=[
            row(D_MODEL),
            pl.BlockSpec((1, 1, 3, D_MODEL), lambda i, j: (i, 1, 0, 0)),
            row(256), row(256), row(256), row(256),
            pl.BlockSpec((None,) + wo.shape[1:], lambda i, j: (wo.shape[0] - 1, 0, 0)),
            pl.BlockSpec(fg.shape, lambda i, j: (0, 0)),
        ],
        out_specs=pl.BlockSpec((1, t, D_MODEL), lambda i, j: (i, j, 0)),
        out_shape=jax.ShapeDtypeStruct((b, s - skip * t, D_MODEL), F32),
        compiler_params=pltpu.CompilerParams(
            dimension_semantics=("parallel", "parallel"), vmem_limit_bytes=VMEM_LIMIT),
        name="out_proj_final",
    )(xs, modsel, ya, yb, yc, yd, wo, fg)


def _rope_tables(seq, dim, width):
    half = dim // 2
    quarter = half // 2
    lane = np.arange(width) % dim
    freq = ROPE_THETA ** (-(2.0 * (lane % quarter)) / half)
    pos_t = np.arange(seq)
    pos = np.where(lane[None, :] < half, (pos_t // GRID_W)[:, None], (pos_t % GRID_W)[:, None])
    ang = pos * freq[None, :]
    cos, sin = np.cos(ang), np.sin(ang)
    first = (lane % half) < quarter
    s_lo = np.where(first[None, :], -sin, 0.0)
    s_hi = np.where(first[None, :], 0.0, sin)
    lat = np.stack([cos, s_lo, s_hi])
    ctx = np.stack([np.ones((CTX_LEN, width)), np.zeros((CTX_LEN, width)), np.zeros((CTX_LEN, width))])
    return jnp.asarray(np.concatenate([ctx, lat], axis=1), dtype=F32)


def _pack_w_in(w):
    offs = [0]
    for n in (256, 256, 256, 256, 256, 128, 128, 256, 256, 256, 256, 128, 128, 256, 256, 16, 16):
        offs.append(offs[-1] + n)
    seg = [w[..., offs[i]:offs[i + 1]] for i in range(17)]
    aq, ak, av, az, bq, bk, bv, bz, cu, cv, cz, dq, dk, dv, dz, drf, drb = seg
    gap = jnp.zeros(w.shape[:-1] + (LANES // 2,), w.dtype)
    r_gap = jnp.concatenate([drf, drb, gap[..., :LANES // 2 - 2 * GLA_RANK]], axis=-1)
    av_sp = jnp.concatenate([av[..., 0:64], r_gap, av[..., 64:128], gap, av[..., 128:192], gap,
                             av[..., 192:256], gap], axis=-1)
    bvk = jnp.concatenate([bv[..., :64], bk[..., :64], bv[..., 64:], bk[..., 64:]], axis=-1)
    packed = jnp.concatenate([aq, ak, av_sp, az, bq, bvk, bz, cu, cv, cz, dq, dk, dv, dz], axis=-1)
    assert packed.shape[-1] == P_PACK
    return packed.astype(BF16)


def kernel(x, c, ctx, c_ctx, ada_w, ada_b, norm_g, w_in, da_lq1, da_lk1, da_lq2, da_lk2,
           da_subln_g, gq_qnorm_g, gq_knorm_g, sg_ln_g, sg_ln_b, sg_w, sg_b,
           gla_w2_f, gla_b_f, gla_w2_b, gla_b_b, gla_norm_g, w_out, final_norm_g):
    b, seq, d = x.shape
    assert (seq, d, ctx.shape[1]) == (seq // ROW_TILE * ROW_TILE, D_MODEL, CTX_LEN)

    n_mod = 32
    cpad = jnp.zeros((n_mod, d), F32).at[:b].set(c).at[b].set(c_ctx)
    mod = _modulation(cpad, ada_w, ada_b)

    taba = _rope_tables(seq, DA_QK, 256)
    tabb = _rope_tables(seq, GQ_HD, 256)
    fg = final_norm_g.reshape(1, d)
    wp_all, wo_all, sgw_all = _pack_w_in(w_in), w_out.astype(BF16), sg_w.astype(BF16)

    nl = DEPTH
    modsel_all = jnp.concatenate([jnp.broadcast_to(mod[:, b].reshape(nl, 1, 1, 3, d), (nl, b, 1, 3, d)),
                                  mod[:, :b].reshape(nl, b, 1, 3, d)], axis=2)
    w2_all = (jnp.zeros((nl, LANES, 256), F32)
              .at[:, R_LANE:R_LANE + GLA_RANK, 0:128].set(gla_w2_f)
              .at[:, R_LANE + GLA_RANK:R_LANE + 2 * GLA_RANK, 128:256].set(gla_w2_b)).astype(BF16)
    b2_all = jnp.concatenate([gla_b_f, gla_b_b], axis=1).reshape(nl, 1, 256)
    sgb_all = jnp.repeat(jnp.swapaxes(sg_b, 1, 2), BRANCH_W // SG_GROUPS, axis=2)
    tiled = lambda g, n: jnp.tile(g, (1, n)).reshape(nl, 1, g.shape[1] * n)
    gqq_all, gqk_all = tiled(gq_qnorm_g, GQ_HEADS), tiled(gq_knorm_g, GQ_KV_HEADS)
    subln_all, glan_all = tiled(da_subln_g, DA_HEADS), tiled(gla_norm_g, GLA_HEADS)
    lam_all = jnp.stack([da_lq1, da_lk1, da_lq2, da_lk2], axis=1)

    stream_in = (ctx, x)
    for i in range(DEPTH):
        lam_init = 0.8 - 0.6 * math.exp(-0.3 * i)
        (xs, aqt, ak, avt, bqt, bk, bvt, gate, yc, dqk, dv, dg) = _layer(
            i, stream_in, modsel_all[i], norm_g[i].reshape(1, d), wp_all, taba, tabb,
            gqq_all[i], gqk_all[i], sg_ln_g[i].reshape(1, 256), sg_ln_b[i].reshape(1, 256),
            sgw_all, sgb_all[i], w2_all[i], b2_all[i])
        ya = _attention(functools.partial(_diff_attn_kernel, lam_init), 2 * DA_HEADS, aqt, ak, avt, gate, 0,
                        [lam_all[i], subln_all[i].reshape(BRANCH_W, 1)], "diff_attn")
        yb = _attention(_gqa_attn_kernel, GQ_HEADS, bqt, bk, bvt, gate, 1, [], "gqa_attn")
        yd = _gla(dqk, dv, dg, gate, glan_all[i])
        stream_in = (xs, modsel_all[i], ya, yb, yc, yd, wo_all)
    return _out_proj(*stream_in, fg)
```

```python
import functools
import math

import jax
import jax.numpy as jnp
import numpy as np
from jax import lax
from jax.experimental import pallas as pl
from jax.experimental.pallas import tpu as pltpu

F32 = jnp.float32
BF16 = jnp.bfloat16

D_MODEL = 1024
DEPTH = 4
CTX_LEN = 256
GRID_W = 64
BRANCH_W = 256
ROPE_THETA = 10000.0
EPS = 1e-6
DA_HEADS = 4
DA_QK = 32
DA_V = 64
GQ_HEADS = 4
GQ_KV_HEADS = 2
GQ_HD = 64
SG_GROUPS = 4
SG_CHUNK = 128
GLA_HEADS = 4
GLA_DV = 64
GLA_DK = 32
GLA_RANK = 16
GLA_NORMALIZER = 16.0
GLA_CHUNK = 32

LANES = 128
ROW_TILE = 256
KV_CHUNK = 512
GLA_TILE = 128
GLA_BATCH = 2
GLA_STEP_TILES = 3
VMEM_LIMIT = 56 * 1024 * 1024

C_AQ, C_AK, C_AV, C_AZ = 0, 256, 512, 1024
C_BQ, C_BVK, C_BZ = 1280, 1536, 1792
C_CU, C_CV, C_CZ = 2048, 2304, 2560
C_DQK, C_DV, C_DZ = 2816, 3072, 3328
P_PACK = 3584
R_LANE = 64
LOG2E = math.log2(math.e)


def _silu(x):
    return x * (1.0 / (1.0 + jnp.exp(-x)))


def _group_mean_matrix(width, group):
    r = lax.broadcasted_iota(jnp.int32, (width, width), 0) // group
    c = lax.broadcasted_iota(jnp.int32, (width, width), 1) // group
    return jnp.where(r == c, 1.0 / group, 0.0).astype(BF16)


def _split2(x):
    hi = x.astype(BF16)
    return hi, (x - hi.astype(F32)).astype(BF16)


def _group_mean(x, mat):
    return sum(jnp.dot(p, mat, preferred_element_type=F32) for p in _split2(x))


def _rope(x, cos, s_lo, s_hi, shift):
    outs = []
    for c in range(x.shape[1] // LANES):
        sl = slice(c * LANES, (c + 1) * LANES)
        xc = x[:, sl]
        up = pltpu.roll(xc, LANES - shift, axis=1)
        dn = pltpu.roll(xc, shift, axis=1)
        outs.append(xc * cos[:, sl] + up * s_lo[:, sl] + dn * s_hi[:, sl])
    return outs[0] if len(outs) == 1 else jnp.concatenate(outs, axis=1)


def _with_ones(v):
    lane = lax.broadcasted_iota(jnp.int32, v.shape, 1) % LANES
    return jnp.where(lane < LANES // 2, v, 1.0)


def _mod_kernel(c_ref, w_ref, b_ref, o_ref):
    s = _silu(c_ref[...]).astype(BF16)
    o_ref[0] = jnp.dot(s, w_ref[0].astype(BF16), preferred_element_type=F32) + b_ref[0]


def _modulation(cpad, ada_w, ada_b):
    n = cpad.shape[0]
    return pl.pallas_call(
        _mod_kernel,
        grid=(DEPTH, 3),
        in_specs=[
            pl.BlockSpec((n, D_MODEL), lambda i, j: (0, 0)),
            pl.BlockSpec((1, D_MODEL, D_MODEL), lambda i, j: (i, 0, j)),
            pl.BlockSpec((1, 1, D_MODEL), lambda i, j: (i, 0, j)),
        ],
        out_specs=pl.BlockSpec((1, n, D_MODEL), lambda i, j: (i, 0, j)),
        out_shape=jax.ShapeDtypeStruct((DEPTH, n, 3 * D_MODEL), F32),
        compiler_params=pltpu.CompilerParams(vmem_limit_bytes=VMEM_LIMIT),
        name="modulation",
    )(cpad, ada_w, ada_b.reshape(DEPTH, 1, 3 * D_MODEL))


def _layer_kernel(first, *refs):
    if first:
        ctx_ref, x_ref = refs[:2]
        refs = refs[2:]
    else:
        xs_ref, modp_ref, ya_ref, yb_ref, yc_in_ref, yd_ref, wo_ref = refs[:7]
        refs = refs[7:]
    (mod_ref, ng_ref, w_ref, taba_ref, tabb_ref, gqq_ref, gqk_ref, lng_ref, lnb_ref, sgw_ref,
     sgb_ref, w2_ref, b2_ref,
     xs_out_ref, aq_ref, ak_ref, av_ref, bq_ref, bk_ref, bv_ref, gate_ref, yc_ref,
     dqk_ref, dv_ref, dg_ref) = refs

    if first:
        x = jnp.where(pl.program_id(1) == 0, ctx_ref[0], x_ref[0])
    else:
        y_prev = jnp.concatenate([ya_ref[0], yb_ref[0], yc_in_ref[0], yd_ref[0]], axis=1)
        x = xs_ref[0] + modp_ref[0, 0, 2:3, :] * jnp.dot(y_prev, wo_ref[...],
                                                         preferred_element_type=F32)
    xs_out_ref[0] = x
    shift = mod_ref[0, 0, 0:1, :]
    scale = mod_ref[0, 0, 1:2, :]
    y = x * lax.rsqrt(jnp.mean(x * x, axis=-1, keepdims=True) + EPS) * ng_ref[...]
    hb = (y * (1.0 + scale) + shift).astype(BF16)

    def proj(lo, width):
        return jnp.dot(hb, w_ref[:, lo:lo + width], preferred_element_type=F32)

    m64 = _group_mean_matrix(BRANCH_W, GQ_HD)
    half_lane = lax.broadcasted_iota(jnp.int32, (ROW_TILE, LANES), 1) < LANES // 2
    p_bq = proj(C_BQ, 256)
    p_bvk = proj(C_BVK, 256)
    p_cv = proj(C_CV, 256)
    p_av = proj(C_AV, 512)

    bq_sq = _split2(p_bq * p_bq)
    bk = jnp.where(half_lane, pltpu.roll(p_bvk[:, :LANES], LANES // 2, axis=1), p_bvk[:, LANES:])
    bk_sq = _split2(bk * bk)
    mu = jnp.mean(p_cv, axis=-1, keepdims=True)
    cen = p_cv - mu
    var = jnp.mean(cen * cen, axis=-1, keepdims=True)
    vn = (cen * lax.rsqrt(var + EPS) * lng_ref[...] + lnb_ref[...]).astype(BF16)
    r = p_av[:, :LANES].astype(BF16)
    av_ref[0, 0] = _with_ones(p_av).T.astype(BF16)
    bv_ref[0, 0] = _with_ones(p_bvk).T.astype(BF16)

    ca, sa_lo, sa_hi = taba_ref[0], taba_ref[1], taba_ref[2]
    aq = _rope(proj(C_AQ, 256), ca, sa_lo, sa_hi, DA_QK // 4) * (DA_QK ** -0.5 * LOG2E)
    aq_ref[0] = aq.T.astype(BF16)
    ak = _rope(proj(C_AK, 256), ca, sa_lo, sa_hi, DA_QK // 4)
    ak_ref[0] = ak.astype(BF16)

    bq_ms = sum(jnp.dot(p, m64, preferred_element_type=F32) for p in bq_sq)
    bk_ms = sum(jnp.dot(p, m64[:LANES, :LANES], preferred_element_type=F32) for p in bk_sq)
    gl = jnp.dot(r, w2_ref[...], preferred_element_type=F32) + b2_ref[...]

    gate_ref[0, :, 0:256] = _silu(proj(C_AZ, 256)).astype(BF16)
    gate_ref[0, :, 256:512] = _silu(proj(C_BZ, 256)).astype(BF16)
    gate_ref[0, :, 512:768] = _silu(proj(C_DZ, 256)).astype(BF16)

    lane_group = lax.broadcasted_iota(jnp.int32, (SG_CHUNK, BRANCH_W), 1) // (BRANCH_W // SG_GROUPS)
    mixed = []
    for n in range(ROW_TILE // SG_CHUNK):
        vchunk = vn[n * SG_CHUNK:(n + 1) * SG_CHUNK, :]
        acc = jnp.zeros((SG_CHUNK, BRANCH_W), F32)
        for g in range(SG_GROUPS):
            acc = jnp.where(lane_group == g,
                            jnp.dot(sgw_ref[g], vchunk, preferred_element_type=F32), acc)
        mixed.append(acc + sgb_ref[...])

    cb, sb_lo, sb_hi = tabb_ref[0], tabb_ref[1], tabb_ref[2]
    bq = p_bq * lax.rsqrt(bq_ms + EPS) * gqq_ref[...]
    bq_ref[0] = (_rope(bq, cb, sb_lo, sb_hi, GQ_HD // 4) * (GQ_HD ** -0.5 * LOG2E)).T.astype(BF16)
    bk = bk * lax.rsqrt(bk_ms + EPS) * gqk_ref[...]
    bk = _rope(bk, cb[:, :LANES], sb_lo[:, :LANES], sb_hi[:, :LANES], GQ_HD // 4)
    bk_ref[0] = bk.astype(BF16)

    yc = proj(C_CU, 256) * jnp.concatenate(mixed, axis=0) * _silu(proj(C_CZ, 256))
    yc_ref[0] = yc.astype(BF16)

    q_scale = jnp.where(lax.broadcasted_iota(jnp.int32, (1, 256), 1) < 128, GLA_DK ** -0.5, 1.0)
    dqk_ref[0] = proj(C_DQK, 256) * q_scale
    dv_ref[0] = proj(C_DV, 256).astype(BF16)
    log_sig = jnp.minimum(gl, 0.0) - jnp.log(1.0 + jnp.exp(-jnp.abs(gl)))
    dg_ref[0] = log_sig * (1.0 / GLA_NORMALIZER)


def _layer(layer, stream_in, modsel, ng, wp, taba, tabb, gqq, gqk, lng, lnb, sgw, sgb, w2, b2):
    first = layer == 0
    t = ROW_TILE
    stacked = lambda a, idx: pl.BlockSpec((None,) + a.shape[1:], lambda i, j: (idx,) + (0,) * (a.ndim - 1))
    b = stream_in[0].shape[0]
    s = CTX_LEN + stream_in[1].shape[1] if first else stream_in[0].shape[1]
    row = lambda width: pl.BlockSpec((1, t, width), lambda i, j: (i, j, 0))
    colT = lambda height: pl.BlockSpec((1, height, t), lambda i, j: (i, 0, j))
    full = lambda a: pl.BlockSpec(a.shape, lambda i, j: (0,) * a.ndim)
    tab = pl.BlockSpec((3, t, 256), lambda i, j: (0, j, 0))
    mods = pl.BlockSpec((1, 1, 3, D_MODEL), lambda i, j: (i, jnp.minimum(j, 1), 0, 0))
    shp = lambda width, dt: jax.ShapeDtypeStruct((b, s, width), dt)
    shpT = lambda height: jax.ShapeDtypeStruct((b, height, s), BF16)
    tileT = lambda height: pl.BlockSpec((1, 1, height, t), lambda i, j: (i, j, 0, 0))
    if first:
        stream_specs = [pl.BlockSpec((1, t, D_MODEL), lambda i, j: (i, 0, 0)),
                        pl.BlockSpec((1, t, D_MODEL), lambda i, j: (i, jnp.maximum(j - 1, 0), 0))]
    else:
        stream_specs = [row(D_MODEL), mods, row(256), row(256), row(256), row(256),
                        stacked(stream_in[6], layer - 1)]
    return pl.pallas_call(
        functools.partial(_layer_kernel, first),
        grid=(b, s // t),
        in_specs=stream_specs + [
            mods, full(ng), stacked(wp, layer), tab, tab, full(gqq), full(gqk), full(lng), full(lnb),
            stacked(sgw, layer), full(sgb), full(w2), full(b2),
        ],
        out_specs=[row(D_MODEL), colT(256), row(256), tileT(512), colT(256), row(128), tileT(256),
                   row(768), row(256), row(256), row(256), row(256)],
        out_shape=[shp(D_MODEL, F32),
                   shpT(256), shp(256, BF16), jax.ShapeDtypeStruct((b, s // t, 512, t), BF16),
                   shpT(256), shp(128, BF16), jax.ShapeDtypeStruct((b, s // t, 256, t), BF16),
                   shp(768, BF16), shp(256, BF16), shp(256, F32), shp(256, BF16), shp(256, F32)],
        compiler_params=pltpu.CompilerParams(
            dimension_semantics=("parallel", "arbitrary"), vmem_limit_bytes=VMEM_LIMIT),
        name="layer_first" if first else "layer",
    )(*stream_in, modsel, ng, wp, taba, tabb, gqq, gqk, lng, lnb, sgw, sgb, w2, b2)


def _max_over_rows(s):
    m = s
    for part in (256, 32):
        if m.shape[0] > part and m.shape[0] % part == 0:
            m = jnp.max(m.reshape(m.shape[0] // part, part, m.shape[1]), axis=0)
    return jnp.max(m, axis=0, keepdims=True)


def _online_softmax_pv(latent, jobs, acc_ref, m_ref, s_ref):
    n_jobs = len(jobs)
    tiles_per_chunk = KV_CHUNK // ROW_TILE
    n_chunks = (jobs[0][1].shape[1] - CTX_LEN) // KV_CHUNK

    def key_loader():
        loaded = {}

        def rows(j, lo, hi):
            k_ref = jobs[j][1]
            if (id(k_ref), lo) not in loaded:
                loaded[(id(k_ref), lo)] = k_ref[0, lo:hi, :]
            return loaded[(id(k_ref), lo)]

        return rows

    def qk(k_rows, j):
        return jnp.dot(k_rows, jobs[j][0], preferred_element_type=F32)

    def softmax_pv(s, vt, j, first):
        m_new = _max_over_rows(s)
        if not first:
            m_old = m_ref[j]
            m_new = jnp.maximum(m_old, m_new)
            alpha = jnp.exp2(m_old - m_new)
        m_ref[j] = m_new
        upd = jnp.dot(vt, jnp.exp2(s - m_new).astype(BF16), preferred_element_type=F32)
        acc_ref[j] = upd if first else acc_ref[j] * alpha + upd

    def vt_tile(tile, j):
        _, _, vt_ref, v_row = jobs[j]
        return vt_ref[0, tile, v_row:v_row + LANES, :]

    def vt_chunk(c, j):
        t0 = CTX_LEN // ROW_TILE + c * tiles_per_chunk
        return jnp.concatenate([vt_tile(t0 + i, j) for i in range(tiles_per_chunk)], axis=1)

    def chunk_rows(c):
        return CTX_LEN + c * KV_CHUNK, CTX_LEN + (c + 1) * KV_CHUNK

    @pl.when(jnp.logical_not(latent))
    def _():
        keys = key_loader()
        s = [qk(keys(j, 0, CTX_LEN), j) for j in range(n_jobs)]
        for j in range(n_jobs):
            softmax_pv(s[j], vt_tile(0, j), j, True)

    @pl.when(latent)
    def _():
        keys = key_loader()
        s = [qk(keys(j, 0, CTX_LEN), j) for j in range(n_jobs)]
        s_ref[0, 0] = qk(keys(0, *chunk_rows(0)), 0)
        for j in range(n_jobs):
            softmax_pv(s[j], vt_tile(0, j), j, True)
            if j + 1 < n_jobs:
                s_ref[0, j + 1] = qk(keys(j + 1, *chunk_rows(0)), j + 1)

        for c in range(n_chunks):
            slot = c % 2
            more = c + 1 < n_chunks
            if more:
                s_ref[1 - slot, 0] = qk(keys(0, *chunk_rows(c + 1)), 0)
            for j in range(n_jobs):
                softmax_pv(s_ref[slot, j], vt_chunk(c, j), j, False)
                if more and j + 1 < n_jobs:
                    s_ref[1 - slot, j + 1] = qk(keys(j + 1, *chunk_rows(c + 1)), j + 1)

    outs = []
    for j in range(n_jobs):
        acc = acc_ref[j]
        outs.append(acc[:LANES // 2, :] * (1.0 / acc[LANES // 2:LANES // 2 + 1, :]))
    return outs


def _rows_at(x, row0, total):
    parts = []
    if row0:
        parts.append(jnp.zeros((row0, x.shape[1]), x.dtype))
    parts.append(x)
    if total - row0 - x.shape[0]:
        parts.append(jnp.zeros((total - row0 - x.shape[0], x.shape[1]), x.dtype))
    return jnp.concatenate(parts, axis=0)


ATTN_JOBS = 2 * DA_HEADS + GQ_HEADS


def _attn_kernel(lam_init, aqt_ref, ak_ref, avt_ref, bqt_ref, bk_ref, bvt_ref, gate_ref, lam_ref,
                 sg_ref, ya_ref, yb_ref, acc_ref, m_ref, s_ref):
    lq1, lk1, lq2, lk2 = lam_ref[0:1, :], lam_ref[1:2, :], lam_ref[2:3, :], lam_ref[3:4, :]
    lam = (jnp.exp(jnp.sum(lq1 * lk1, axis=-1, keepdims=True))
           - jnp.exp(jnp.sum(lq2 * lk2, axis=-1, keepdims=True)) + lam_init)
    grp = GQ_HEADS // GQ_KV_HEADS
    jobs = [(_rows_at(aqt_ref[0, job * DA_QK:(job + 1) * DA_QK, :], job * DA_QK, BRANCH_W),
             ak_ref, avt_ref, (job // 2) * LANES) for job in range(2 * DA_HEADS)]
    jobs += [(_rows_at(bqt_ref[0, h * GQ_HD:(h + 1) * GQ_HD, :], (h // grp) * GQ_HD, GQ_KV_HEADS * GQ_HD),
              bk_ref, bvt_ref, (h // grp) * LANES) for h in range(GQ_HEADS)]
    o = _online_softmax_pv(pl.program_id(1) > 0, jobs, acc_ref, m_ref, s_ref)

    heads = []
    for h in range(DA_HEADS):
        d = o[2 * h] - lam * o[2 * h + 1]
        heads.append(d * lax.rsqrt(jnp.mean(d * d, axis=0, keepdims=True) + EPS))
    yt = jnp.concatenate(heads, axis=0) * (sg_ref[...] * (1.0 - lam_init))
    ya_ref[0] = (yt.T * gate_ref[0, :, 0:BRANCH_W].astype(F32)).astype(BF16)
    yb = jnp.concatenate(o[2 * DA_HEADS:], axis=0).T
    yb_ref[0] = (yb * gate_ref[0, :, BRANCH_W:2 * BRANCH_W].astype(F32)).astype(BF16)


def _attention(lam_init, aqt, ak, avt, bqt, bk, bvt, gate, lam_vecs, subln_col):
    b, s, _ = ak.shape
    t = ROW_TILE
    full = lambda a: pl.BlockSpec(a.shape, lambda i, j: (0,) * a.ndim)
    qt_spec = pl.BlockSpec((1, BRANCH_W, t), lambda i, j: (i, 0, j))
    keys = lambda a: pl.BlockSpec((1, s, a.shape[2]), lambda i, j: (i, 0, 0))
    vals = lambda a: pl.BlockSpec((1,) + a.shape[1:], lambda i, j: (i, 0, 0, 0))
    y_spec = pl.BlockSpec((1, t, BRANCH_W), lambda i, j: (i, j, 0))
    y_shape = jax.ShapeDtypeStruct((b, s, BRANCH_W), BF16)
    return pl.pallas_call(
        functools.partial(_attn_kernel, lam_init),
        grid=(b, s // t),
        in_specs=[qt_spec, keys(ak), vals(avt), qt_spec, keys(bk), vals(bvt),
                  pl.BlockSpec((1, t, gate.shape[2]), lambda i, j: (i, j, 0)),
                  full(lam_vecs), full(subln_col)],
        out_specs=[y_spec, y_spec],
        out_shape=[y_shape, y_shape],
        scratch_shapes=[pltpu.VMEM((ATTN_JOBS, LANES, t), F32), pltpu.VMEM((ATTN_JOBS, 1, t), F32),
                        pltpu.VMEM((2, ATTN_JOBS, KV_CHUNK, t), F32)],
        compiler_params=pltpu.CompilerParams(
            dimension_semantics=("parallel", "arbitrary"), vmem_limit_bytes=VMEM_LIMIT),
        name="attention",
    )(aqt, ak, avt, bqt, bk, bvt, gate, lam_vecs, subln_col)


def _chunk_cumsum(g, row_in_chunk, reverse):
    x = g
    sh = 1
    while sh < GLA_CHUNK:
        if reverse:
            nb = pltpu.roll(x, x.shape[0] - sh, axis=0)
            x = x + jnp.where(row_in_chunk < GLA_CHUNK - sh, nb, 0.0)
        else:
            nb = pltpu.roll(x, sh, axis=0)
            x = x + jnp.where(row_in_chunk >= sh, nb, 0.0)
        sh *= 2
    return x


def _gla_kernel(qk_ref, v_ref, g_ref, gate_ref, ng_ref, o_ref, acc_ref, st_ref):
    n_tiles = qk_ref.shape[1] // GLA_TILE
    ctx_tiles = CTX_LEN // GLA_TILE
    per = GLA_TILE // GLA_CHUNK
    ri = lax.broadcasted_iota(jnp.int32, (GLA_TILE, GLA_TILE), 0)
    ci = lax.broadcasted_iota(jnp.int32, (GLA_TILE, GLA_TILE), 1)
    same_chunk = (ri // GLA_CHUNK) == (ci // GLA_CHUNK)
    chunk_of_row = ri // GLA_CHUNK
    row_in_chunk = ri % GLA_CHUNK
    head_of_k = lax.broadcasted_iota(jnp.int32, (GLA_TILE, GLA_HEADS * GLA_DK), 1) // GLA_DK
    head_of_v = lax.broadcasted_iota(jnp.int32, (GLA_TILE, BRANCH_W), 1) // GLA_DV
    st_mask = ((lax.broadcasted_iota(jnp.int32, (BRANCH_W, GLA_HEADS * GLA_DK), 0) // GLA_DV)
               == (lax.broadcasted_iota(jnp.int32, (BRANCH_W, GLA_HEADS * GLA_DK), 1) // GLA_DK))
    m64 = _group_mean_matrix(BRANCH_W, GLA_DV)
    tris = [jnp.where(same_chunk & (ci <= ri), 1.0, 0.0), jnp.where(same_chunk & (ci >= ri), 1.0, 0.0)]
    tris4 = [jnp.concatenate([t] * GLA_HEADS, axis=1) for t in tris]

    nt_dims = (((1,), (1,)), ((), ()))
    tn_dims = (((0,), (0,)), ((), ()))

    def scan_step(i, carry):
        def tile_index(d, sub):
            t = i * GLA_STEP_TILES + sub
            if not d:
                return t
            return jnp.where(t < ctx_tiles, ctx_tiles - 1 - t, n_tiles + ctx_tiles - 1 - t)

        chains = [(bi, d, sub) for bi in range(GLA_BATCH) for d in (0, 1) for sub in range(GLA_STEP_TILES)]
        nc = len(chains)
        rows = [pl.ds(pl.multiple_of(tile_index(d, sub) * GLA_TILE, GLA_TILE), GLA_TILE)
                for _, d, sub in chains]
        ends = [[c * GLA_CHUNK if d else (c + 1) * GLA_CHUNK - 1 for c in range(per)] for _, d, _ in chains]

        cum = []
        for n, (bi, d, _) in enumerate(chains):
            cum.append(_chunk_cumsum(g_ref[bi, rows[n], 128 * d:128 * d + 128], row_in_chunk, bool(d)))

        qe, kd, kl, vb = [], [], [], []
        for n, (bi, d, _) in enumerate(chains):
            q = qk_ref[bi, rows[n], 0:128]
            k = qk_ref[bi, rows[n], 128:256]
            cum_last = jnp.concatenate(
                [jnp.broadcast_to(cum[n][e:e + 1, :], (GLA_CHUNK, 128)) for e in ends[n]], axis=0)
            qe.append(q * jnp.exp(cum[n]))
            kd.append(k * jnp.exp(-cum[n]))
            kl.append(k * jnp.exp(cum_last - cum[n]))
            vb.append(v_ref[bi, rows[n], :])

        qeb = [x.astype(BF16) for x in qe]
        att = []
        for n, (bi, d, _) in enumerate(chains):
            kd_heads = jnp.concatenate(
                [jnp.where(head_of_k == h, kd[n], 0.0).astype(BF16) for h in range(GLA_HEADS)], axis=0)
            a = lax.dot_general(qeb[n], kd_heads, nt_dims, preferred_element_type=F32)
            att.append((a * tris4[d]).astype(BF16))
        o = []
        for n in range(nc):
            v_heads = jnp.concatenate(
                [jnp.where(head_of_v == h, vb[n], jnp.zeros_like(vb[n])) for h in range(GLA_HEADS)], axis=0)
            o.append(jnp.dot(att[n], v_heads, preferred_element_type=F32))

        upd = []
        for n in range(nc):
            kl_chunks = jnp.concatenate(
                [jnp.where(chunk_of_row == c, kl[n], 0.0).astype(BF16) for c in range(per)], axis=1)
            upd.append(lax.dot_general(vb[n], kl_chunks, tn_dims, preferred_element_type=F32))
        seen = [[None] * per for _ in range(nc)]
        for slot in range(2 * GLA_BATCH):
            st = st_ref[slot]
            for n in range(slot * GLA_STEP_TILES, (slot + 1) * GLA_STEP_TILES):
                for c in (range(per - 1, -1, -1) if slot % 2 else range(per)):
                    seen[n][c] = st.astype(BF16)
                    decay = jnp.exp(cum[n][ends[n][c]:ends[n][c] + 1, :])
                    st = st * decay + jnp.where(st_mask, upd[n][:, c * 128:(c + 1) * 128], 0.0)
            st_ref[slot] = st
        for n in range(nc):
            qe_chunks = jnp.concatenate(
                [jnp.where(chunk_of_row == c, qe[n], 0.0).astype(BF16) for c in range(per)], axis=1)
            o_inter = lax.dot_general(qe_chunks, jnp.concatenate(seen[n], axis=1), nt_dims,
                                      preferred_element_type=F32)
            acc_ref[n // GLA_STEP_TILES, rows[n], :] = o[n] + o_inter
        return carry

    st_ref[...] = jnp.zeros_like(st_ref)
    lax.fori_loop(0, n_tiles // GLA_STEP_TILES, scan_step, 0)

    def finish_step(i, carry):
        rows = pl.ds(pl.multiple_of(i * ROW_TILE, ROW_TILE), ROW_TILE)
        tot = [acc_ref[2 * bi, rows, :] + acc_ref[2 * bi + 1, rows, :] for bi in range(GLA_BATCH)]
        sq = [_split2(t * t) for t in tot]
        ms = [sum(jnp.dot(p, m64, preferred_element_type=F32) for p in s) for s in sq]
        for bi in range(GLA_BATCH):
            y = tot[bi] * lax.rsqrt(ms[bi] + EPS) * ng_ref[...]
            o_ref[bi, rows, :] = (y * gate_ref[bi, rows, :].astype(F32)).astype(BF16)
        return carry

    lax.fori_loop(0, qk_ref.shape[1] // ROW_TILE, finish_step, 0)


def _gla(dqk, dv, dg, gate, ng):
    b, s, _ = dqk.shape
    nb = GLA_BATCH
    blk = lambda width, cb: pl.BlockSpec((nb, s, width), lambda i: (i, 0, cb))
    return pl.pallas_call(
        _gla_kernel,
        grid=(b // nb,),
        in_specs=[blk(256, 0), blk(256, 0), blk(256, 0), blk(256, 2),
                  pl.BlockSpec(ng.shape, lambda i: (0, 0))],
        out_specs=blk(256, 0),
        out_shape=jax.ShapeDtypeStruct((b, s, 256), BF16),
        scratch_shapes=[pltpu.VMEM((2 * nb, s, BRANCH_W), F32),
                        pltpu.VMEM((2 * nb, BRANCH_W, GLA_HEADS * GLA_DK), F32)],
        compiler_params=pltpu.CompilerParams(
            dimension_semantics=("parallel",), vmem_limit_bytes=VMEM_LIMIT),
        name="gla",
    )(dqk, dv, dg, gate, ng)


def _out_kernel(x_ref, mod_ref, ya_ref, yb_ref, yc_ref, yd_ref, w_ref, fg_ref, o_ref):
    y = jnp.concatenate([ya_ref[0], yb_ref[0], yc_ref[0], yd_ref[0]], axis=1)
    upd = jnp.dot(y, w_ref[...], preferred_element_type=F32)
    xn = x_ref[0] + mod_ref[0, 0, 2:3, :] * upd
    o_ref[0] = xn * lax.rsqrt(jnp.mean(xn * xn, axis=-1, keepdims=True) + EPS) * fg_ref[...]


def _out_proj(xs, modsel, ya, yb, yc, yd, wo, fg):
    b, s, _ = xs.shape
    t = ROW_TILE
    skip = CTX_LEN // t
    row = lambda width: pl.BlockSpec((1, t, width), lambda i, j: (i, j + skip, 0))
    return pl.pallas_call(
        _out_kernel,
        grid=(b, s // t - skip),
        in_specs=[
            row(D_MODEL),
            pl.BlockSpec((1, 1, 3, D_MODEL), lambda i, j: (i, 1, 0, 0)),
            row(256), row(256), row(256), row(256),
            pl.BlockSpec((None,) + wo.shape[1:], lambda i, j: (wo.shape[0] - 1, 0, 0)),
            pl.BlockSpec(fg.shape, lambda i, j: (0, 0)),
        ],
        out_specs=pl.BlockSpec((1, t, D_MODEL), lambda i, j: (i, j, 0)),
        out_shape=jax.ShapeDtypeStruct((b, s - skip * t, D_MODEL), F32),
        compiler_params=pltpu.CompilerParams(
            dimension_semantics=("parallel", "parallel"), vmem_limit_bytes=VMEM_LIMIT),
        name="out_proj_final",
    )(xs, modsel, ya, yb, yc, yd, wo, fg)


def _rope_tables(seq, dim, width):
    half = dim // 2
    quarter = half // 2
    lane = np.arange(width) % dim
    freq = ROPE_THETA ** (-(2.0 * (lane % quarter)) / half)
    pos_t = np.arange(seq)
    pos = np.where(lane[None, :] < half, (pos_t // GRID_W)[:, None], (pos_t % GRID_W)[:, None])
    ang = pos * freq[None, :]
    cos, sin = np.cos(ang), np.sin(ang)
    first = (lane % half) < quarter
    s_lo = np.where(first[None, :], -sin, 0.0)
    s_hi = np.where(first[None, :], 0.0, sin)
    lat = np.stack([cos, s_lo, s_hi])
    ctx = np.stack([np.ones((CTX_LEN, width)), np.zeros((CTX_LEN, width)), np.zeros((CTX_LEN, width))])
    return jnp.asarray(np.concatenate([ctx, lat], axis=1), dtype=F32)


def _pack_w_in(w):
    offs = [0]
    for n in (256, 256, 256, 256, 256, 128, 128, 256, 256, 256, 256, 128, 128, 256, 256, 16, 16):
        offs.append(offs[-1] + n)
    seg = [w[..., offs[i]:offs[i + 1]] for i in range(17)]
    aq, ak, av, az, bq, bk, bv, bz, cu, cv, cz, dq, dk, dv, dz, drf, drb = seg
    gap = jnp.zeros(w.shape[:-1] + (LANES // 2,), w.dtype)
    r_gap = jnp.concatenate([drf, drb, gap[..., :LANES // 2 - 2 * GLA_RANK]], axis=-1)
    av_sp = jnp.concatenate([av[..., 0:64], r_gap, av[..., 64:128], gap, av[..., 128:192], gap,
                             av[..., 192:256], gap], axis=-1)
    bvk = jnp.concatenate([bv[..., :64], bk[..., :64], bv[..., 64:], bk[..., 64:]], axis=-1)
    packed = jnp.concatenate([aq, ak, av_sp, az, bq, bvk, bz, cu, cv, cz, dq, dk, dv, dz], axis=-1)
    assert packed.shape[-1] == P_PACK
    return packed.astype(BF16)


def kernel(x, c, ctx, c_ctx, ada_w, ada_b, norm_g, w_in, da_lq1, da_lk1, da_lq2, da_lk2,
           da_subln_g, gq_qnorm_g, gq_knorm_g, sg_ln_g, sg_ln_b, sg_w, sg_b,
           gla_w2_f, gla_b_f, gla_w2_b, gla_b_b, gla_norm_g, w_out, final_norm_g):
    b, seq, d = x.shape
    assert (seq, d, ctx.shape[1]) == (seq // ROW_TILE * ROW_TILE, D_MODEL, CTX_LEN)

    n_mod = 32
    cpad = jnp.zeros((n_mod, d), F32).at[:b].set(c).at[b].set(c_ctx)
    mod = _modulation(cpad, ada_w, ada_b)

    taba = _rope_tables(seq, DA_QK, 256)
    tabb = _rope_tables(seq, GQ_HD, 256)
    fg = final_norm_g.reshape(1, d)
    wp_all, wo_all, sgw_all = _pack_w_in(w_in), w_out.astype(BF16), sg_w.astype(BF16)

    nl = DEPTH
    modsel_all = jnp.concatenate([jnp.broadcast_to(mod[:, b].reshape(nl, 1, 1, 3, d), (nl, b, 1, 3, d)),
                                  mod[:, :b].reshape(nl, b, 1, 3, d)], axis=2)
    w2_all = (jnp.zeros((nl, LANES, 256), F32)
              .at[:, R_LANE:R_LANE + GLA_RANK, 0:128].set(gla_w2_f)
              .at[:, R_LANE + GLA_RANK:R_LANE + 2 * GLA_RANK, 128:256].set(gla_w2_b)).astype(BF16)
    b2_all = jnp.concatenate([gla_b_f, gla_b_b], axis=1).reshape(nl, 1, 256)
    sgb_all = jnp.repeat(jnp.swapaxes(sg_b, 1, 2), BRANCH_W // SG_GROUPS, axis=2)
    tiled = lambda g, n: jnp.tile(g, (1, n)).reshape(nl, 1, g.shape[1] * n)
    gqq_all, gqk_all = tiled(gq_qnorm_g, GQ_HEADS), tiled(gq_knorm_g, GQ_KV_HEADS)
    subln_all, glan_all = tiled(da_subln_g, DA_HEADS), tiled(gla_norm_g, GLA_HEADS)
    lam_all = jnp.stack([da_lq1, da_lk1, da_lq2, da_lk2], axis=1)

    stream_in = (ctx, x)
    for i in range(DEPTH):
        lam_init = 0.8 - 0.6 * math.exp(-0.3 * i)
        (xs, aqt, ak, avt, bqt, bk, bvt, gate, yc, dqk, dv, dg) = _layer(
            i, stream_in, modsel_all[i], norm_g[i].reshape(1, d), wp_all, taba, tabb,
            gqq_all[i], gqk_all[i], sg_ln_g[i].reshape(1, 256), sg_ln_b[i].reshape(1, 256),
            sgw_all, sgb_all[i], w2_all[i], b2_all[i])
        ya, yb = _attention(lam_init, aqt, ak, avt, bqt, bk, bvt, gate, lam_all[i],
                            subln_all[i].reshape(BRANCH_W, 1))
        yd = _gla(dqk, dv, dg, gate, glan_all[i])
        stream_in = (xs, modsel_all[i], ya, yb, yc, yd, wo_all)
    return _out_proj(*stream_in, fg)
```

```python
import functools
import math

import jax
import jax.numpy as jnp
import numpy as np
from jax import lax
from jax.experimental import pallas as pl
from jax.experimental.pallas import tpu as pltpu

F32 = jnp.float32
BF16 = jnp.bfloat16

D_MODEL = 1024
DEPTH = 4
CTX_LEN = 256
GRID_W = 64
BRANCH_W = 256
ROPE_THETA = 10000.0
EPS = 1e-6
DA_HEADS = 4
DA_QK = 32
DA_V = 64
GQ_HEADS = 4
GQ_KV_HEADS = 2
GQ_HD = 64
SG_GROUPS = 4
SG_CHUNK = 128
GLA_HEADS = 4
GLA_DV = 64
GLA_DK = 32
GLA_RANK = 16
GLA_NORMALIZER = 16.0
GLA_CHUNK = 32

LANES = 128
ROW_TILE = 256
KV_CHUNK = 512
LAYER_BATCH = 2
GLA_TILE = 128
GLA_BATCH = 2
GLA_STEP_TILES = 3
VMEM_LIMIT = 56 * 1024 * 1024

C_AQ, C_AK, C_AV, C_AZ = 0, 256, 512, 1024
C_BQ, C_BVK, C_BZ = 1280, 1536, 1792
C_CU, C_CV, C_CZ = 2048, 2304, 2560
C_DQK, C_DV, C_DZ = 2816, 3072, 3328
P_PACK = 3584
R_LANE = 64
LOG2E = math.log2(math.e)


def _silu(x):
    return x * (1.0 / (1.0 + jnp.exp(-x)))


def _group_mean_matrix(width, group):
    r = lax.broadcasted_iota(jnp.int32, (width, width), 0) // group
    c = lax.broadcasted_iota(jnp.int32, (width, width), 1) // group
    return jnp.where(r == c, 1.0 / group, 0.0).astype(BF16)


def _split2(x):
    hi = x.astype(BF16)
    return hi, (x - hi.astype(F32)).astype(BF16)


def _group_mean(x, mat):
    return sum(jnp.dot(p, mat, preferred_element_type=F32) for p in _split2(x))


def _rope(x, cos, s_lo, s_hi, shift):
    outs = []
    for c in range(x.shape[1] // LANES):
        sl = slice(c * LANES, (c + 1) * LANES)
        xc = x[:, sl]
        up = pltpu.roll(xc, LANES - shift, axis=1)
        dn = pltpu.roll(xc, shift, axis=1)
        outs.append(xc * cos[:, sl] + up * s_lo[:, sl] + dn * s_hi[:, sl])
    return outs[0] if len(outs) == 1 else jnp.concatenate(outs, axis=1)


def _with_ones(v):
    lane = lax.broadcasted_iota(jnp.int32, v.shape, 1) % LANES
    return jnp.where(lane < LANES // 2, v, 1.0)


def _mod_kernel(c_ref, w_ref, b_ref, o_ref):
    s = _silu(c_ref[...]).astype(BF16)
    o_ref[0] = jnp.dot(s, w_ref[0].astype(BF16), preferred_element_type=F32) + b_ref[0]


def _modulation(cpad, ada_w, ada_b):
    n = cpad.shape[0]
    return pl.pallas_call(
        _mod_kernel,
        grid=(DEPTH, 3),
        in_specs=[
            pl.BlockSpec((n, D_MODEL), lambda i, j: (0, 0)),
            pl.BlockSpec((1, D_MODEL, D_MODEL), lambda i, j: (i, 0, j)),
            pl.BlockSpec((1, 1, D_MODEL), lambda i, j: (i, 0, j)),
        ],
        out_specs=pl.BlockSpec((1, n, D_MODEL), lambda i, j: (i, 0, j)),
        out_shape=jax.ShapeDtypeStruct((DEPTH, n, 3 * D_MODEL), F32),
        compiler_params=pltpu.CompilerParams(vmem_limit_bytes=VMEM_LIMIT),
        name="modulation",
    )(cpad, ada_w, ada_b.reshape(DEPTH, 1, 3 * D_MODEL))


def _layer_kernel(first, *refs):
    if first:
        ctx_ref, x_ref = refs[:2]
        refs = refs[2:]
    else:
        xs_ref, modp_ref, ya_ref, yb_ref, yc_in_ref, yd_ref, wo_ref = refs[:7]
        refs = refs[7:]
    (mod_ref, ng_ref, w_ref, taba_ref, tabb_ref, gqq_ref, gqk_ref, lng_ref, lnb_ref, sgw_ref,
     sgb_ref, w2_ref, b2_ref,
     xs_out_ref, aq_ref, ak_ref, av_ref, bq_ref, bk_ref, bv_ref, gate_ref, yc_ref,
     dqk_ref, dv_ref, dg_ref) = refs

    def residual_and_norm(bi):
        if first:
            x = jnp.where(pl.program_id(1) == 0, ctx_ref[bi], x_ref[bi])
        else:
            y_prev = jnp.concatenate([ya_ref[bi], yb_ref[bi], yc_in_ref[bi], yd_ref[bi]], axis=1)
            x = xs_ref[bi] + modp_ref[bi, 0, 2:3, :] * jnp.dot(y_prev, wo_ref[...],
                                                               preferred_element_type=F32)
        xs_out_ref[bi] = x
        shift = mod_ref[bi, 0, 0:1, :]
        scale = mod_ref[bi, 0, 1:2, :]
        y = x * lax.rsqrt(jnp.mean(x * x, axis=-1, keepdims=True) + EPS) * ng_ref[...]
        return (y * (1.0 + scale) + shift).astype(BF16)

    def project(bi, hb):
        def proj(lo, width):
            return jnp.dot(hb, w_ref[:, lo:lo + width], preferred_element_type=F32)

        m64 = _group_mean_matrix(BRANCH_W, GQ_HD)
        half_lane = lax.broadcasted_iota(jnp.int32, (ROW_TILE, LANES), 1) < LANES // 2
        p_bq = proj(C_BQ, 256)
        p_bvk = proj(C_BVK, 256)
        p_cv = proj(C_CV, 256)
        p_av = proj(C_AV, 512)

        bq_sq = _split2(p_bq * p_bq)
        bk = jnp.where(half_lane, pltpu.roll(p_bvk[:, :LANES], LANES // 2, axis=1), p_bvk[:, LANES:])
        bk_sq = _split2(bk * bk)
        mu = jnp.mean(p_cv, axis=-1, keepdims=True)
        cen = p_cv - mu
        var = jnp.mean(cen * cen, axis=-1, keepdims=True)
        vn = (cen * lax.rsqrt(var + EPS) * lng_ref[...] + lnb_ref[...]).astype(BF16)
        r = p_av[:, :LANES].astype(BF16)
        av_ref[bi, 0] = _with_ones(p_av).T.astype(BF16)
        bv_ref[bi, 0] = _with_ones(p_bvk).T.astype(BF16)

        ca, sa_lo, sa_hi = taba_ref[0], taba_ref[1], taba_ref[2]
        aq = _rope(proj(C_AQ, 256), ca, sa_lo, sa_hi, DA_QK // 4) * (DA_QK ** -0.5 * LOG2E)
        aq_ref[bi] = aq.T.astype(BF16)
        ak = _rope(proj(C_AK, 256), ca, sa_lo, sa_hi, DA_QK // 4)
        ak_ref[bi] = ak.astype(BF16)

        bq_ms = sum(jnp.dot(p, m64, preferred_element_type=F32) for p in bq_sq)
        bk_ms = sum(jnp.dot(p, m64[:LANES, :LANES], preferred_element_type=F32) for p in bk_sq)
        gl = jnp.dot(r, w2_ref[...], preferred_element_type=F32) + b2_ref[...]

        gate_ref[bi, :, 0:256] = _silu(proj(C_AZ, 256)).astype(BF16)
        gate_ref[bi, :, 256:512] = _silu(proj(C_BZ, 256)).astype(BF16)
        gate_ref[bi, :, 512:768] = _silu(proj(C_DZ, 256)).astype(BF16)

        lane_group = lax.broadcasted_iota(jnp.int32, (SG_CHUNK, BRANCH_W), 1) // (BRANCH_W // SG_GROUPS)
        mixed = []
        for n in range(ROW_TILE // SG_CHUNK):
            vchunk = vn[n * SG_CHUNK:(n + 1) * SG_CHUNK, :]
            acc = jnp.zeros((SG_CHUNK, BRANCH_W), F32)
            for g in range(SG_GROUPS):
                acc = jnp.where(lane_group == g,
                                jnp.dot(sgw_ref[g], vchunk, preferred_element_type=F32), acc)
            mixed.append(acc + sgb_ref[...])

        cb, sb_lo, sb_hi = tabb_ref[0], tabb_ref[1], tabb_ref[2]
        bq = p_bq * lax.rsqrt(bq_ms + EPS) * gqq_ref[...]
        bq_ref[bi] = (_rope(bq, cb, sb_lo, sb_hi, GQ_HD // 4) * (GQ_HD ** -0.5 * LOG2E)).T.astype(BF16)
        bk = bk * lax.rsqrt(bk_ms + EPS) * gqk_ref[...]
        bk = _rope(bk, cb[:, :LANES], sb_lo[:, :LANES], sb_hi[:, :LANES], GQ_HD // 4)
        bk_ref[bi] = bk.astype(BF16)

        yc = proj(C_CU, 256) * jnp.concatenate(mixed, axis=0) * _silu(proj(C_CZ, 256))
        yc_ref[bi] = yc.astype(BF16)

        q_scale = jnp.where(lax.broadcasted_iota(jnp.int32, (1, 256), 1) < 128, GLA_DK ** -0.5, 1.0)
        dqk_ref[bi] = proj(C_DQK, 256) * q_scale
        dv_ref[bi] = proj(C_DV, 256).astype(BF16)
        log_sig = jnp.minimum(gl, 0.0) - jnp.log(1.0 + jnp.exp(-jnp.abs(gl)))
        dg_ref[bi] = log_sig * (1.0 / GLA_NORMALIZER)

    nb = xs_out_ref.shape[0]
    hbs = [residual_and_norm(bi) for bi in range(nb)]
    for bi in range(nb):
        project(bi, hbs[bi])


def _layer(layer, stream_in, modsel, ng, wp, taba, tabb, gqq, gqk, lng, lnb, sgw, sgb, w2, b2):
    first = layer == 0
    t = ROW_TILE
    stacked = lambda a, idx: pl.BlockSpec((None,) + a.shape[1:], lambda i, j: (idx,) + (0,) * (a.ndim - 1))
    b = stream_in[0].shape[0]
    s = CTX_LEN + stream_in[1].shape[1] if first else stream_in[0].shape[1]
    nb = LAYER_BATCH
    row = lambda width: pl.BlockSpec((nb, t, width), lambda i, j: (i, j, 0))
    colT = lambda height: pl.BlockSpec((nb, height, t), lambda i, j: (i, 0, j))
    full = lambda a: pl.BlockSpec(a.shape, lambda i, j: (0,) * a.ndim)
    tab = pl.BlockSpec((3, t, 256), lambda i, j: (0, j, 0))
    mods = pl.BlockSpec((nb, 1, 3, D_MODEL), lambda i, j: (i, jnp.minimum(j, 1), 0, 0))
    shp = lambda width, dt: jax.ShapeDtypeStruct((b, s, width), dt)
    shpT = lambda height: jax.ShapeDtypeStruct((b, height, s), BF16)
    tileT = lambda height: pl.BlockSpec((nb, 1, height, t), lambda i, j: (i, j, 0, 0))
    if first:
        stream_specs = [pl.BlockSpec((nb, t, D_MODEL), lambda i, j: (i, 0, 0)),
                        pl.BlockSpec((nb, t, D_MODEL), lambda i, j: (i, jnp.maximum(j - 1, 0), 0))]
    else:
        stream_specs = [row(D_MODEL), mods, row(256), row(256), row(256), row(256),
                        stacked(stream_in[6], layer - 1)]
    return pl.pallas_call(
        functools.partial(_layer_kernel, first),
        grid=(b // nb, s // t),
        in_specs=stream_specs + [
            mods, full(ng), stacked(wp, layer), tab, tab, full(gqq), full(gqk), full(lng), full(lnb),
            stacked(sgw, layer), full(sgb), full(w2), full(b2),
        ],
        out_specs=[row(D_MODEL), colT(256), row(256), tileT(512), colT(256), row(128), tileT(256),
                   row(768), row(256), row(256), row(256), row(256)],
        out_shape=[shp(D_MODEL, F32),
                   shpT(256), shp(256, BF16), jax.ShapeDtypeStruct((b, s // t, 512, t), BF16),
                   shpT(256), shp(128, BF16), jax.ShapeDtypeStruct((b, s // t, 256, t), BF16),
                   shp(768, BF16), shp(256, BF16), shp(256, F32), shp(256, BF16), shp(256, F32)],
        compiler_params=pltpu.CompilerParams(
            dimension_semantics=("parallel", "arbitrary"), vmem_limit_bytes=VMEM_LIMIT),
        name="layer_first" if first else "layer",
    )(*stream_in, modsel, ng, wp, taba, tabb, gqq, gqk, lng, lnb, sgw, sgb, w2, b2)


def _max_over_rows(s):
    m = s
    for part in (256, 32):
        if m.shape[0] > part and m.shape[0] % part == 0:
            m = jnp.max(m.reshape(m.shape[0] // part, part, m.shape[1]), axis=0)
    return jnp.max(m, axis=0, keepdims=True)


def _online_softmax_pv(latent, jobs, acc_ref, m_ref, s_ref):
    n_jobs = len(jobs)
    tiles_per_chunk = KV_CHUNK // ROW_TILE
    n_chunks = (jobs[0][1].shape[1] - CTX_LEN) // KV_CHUNK

    def key_loader():
        loaded = {}

        def rows(j, lo, hi):
            k_ref = jobs[j][1]
            if (id(k_ref), lo) not in loaded:
                loaded[(id(k_ref), lo)] = k_ref[0, lo:hi, :]
            return loaded[(id(k_ref), lo)]

        return rows

    def qk(k_rows, j):
        return jnp.dot(k_rows, jobs[j][0], preferred_element_type=F32)

    def softmax_pv(s, vt, j, first):
        m_new = _max_over_rows(s)
        if not first:
            m_old = m_ref[j]
            m_new = jnp.maximum(m_old, m_new)
            alpha = jnp.exp2(m_old - m_new)
        m_ref[j] = m_new
        upd = jnp.dot(vt, jnp.exp2(s - m_new).astype(BF16), preferred_element_type=F32)
        acc_ref[j] = upd if first else acc_ref[j] * alpha + upd

    def vt_tile(tile, j):
        _, _, vt_ref, v_row = jobs[j]
        return vt_ref[0, tile, v_row:v_row + LANES, :]

    def vt_chunk(c, j):
        t0 = CTX_LEN // ROW_TILE + c * tiles_per_chunk
        return jnp.concatenate([vt_tile(t0 + i, j) for i in range(tiles_per_chunk)], axis=1)

    def chunk_rows(c):
        return CTX_LEN + c * KV_CHUNK, CTX_LEN + (c + 1) * KV_CHUNK

    @pl.when(jnp.logical_not(latent))
    def _():
        keys = key_loader()
        s = [qk(keys(j, 0, CTX_LEN), j) for j in range(n_jobs)]
        for j in range(n_jobs):
            softmax_pv(s[j], vt_tile(0, j), j, True)

    @pl.when(latent)
    def _():
        keys = key_loader()
        s = [qk(keys(j, 0, CTX_LEN), j) for j in range(n_jobs)]
        s_ref[0, 0] = qk(keys(0, *chunk_rows(0)), 0)
        for j in range(n_jobs):
            softmax_pv(s[j], vt_tile(0, j), j, True)
            if j + 1 < n_jobs:
                s_ref[0, j + 1] = qk(keys(j + 1, *chunk_rows(0)), j + 1)

        for c in range(n_chunks):
            slot = c % 2
            more = c + 1 < n_chunks
            if more:
                s_ref[1 - slot, 0] = qk(keys(0, *chunk_rows(c + 1)), 0)
            for j in range(n_jobs):
                softmax_pv(s_ref[slot, j], vt_chunk(c, j), j, False)
                if more and j + 1 < n_jobs:
                    s_ref[1 - slot, j + 1] = qk(keys(j + 1, *chunk_rows(c + 1)), j + 1)

    outs = []
    for j in range(n_jobs):
        acc = acc_ref[j]
        outs.append(acc[:LANES // 2, :] * (1.0 / acc[LANES // 2:LANES // 2 + 1, :]))
    return outs


def _rows_at(x, row0, total):
    parts = []
    if row0:
        parts.append(jnp.zeros((row0, x.shape[1]), x.dtype))
    parts.append(x)
    if total - row0 - x.shape[0]:
        parts.append(jnp.zeros((total - row0 - x.shape[0], x.shape[1]), x.dtype))
    return jnp.concatenate(parts, axis=0)


ATTN_JOBS = 2 * DA_HEADS + GQ_HEADS


def _attn_kernel(lam_init, aqt_ref, ak_ref, avt_ref, bqt_ref, bk_ref, bvt_ref, gate_ref, lam_ref,
                 sg_ref, ya_ref, yb_ref, acc_ref, m_ref, s_ref):
    lq1, lk1, lq2, lk2 = lam_ref[0:1, :], lam_ref[1:2, :], lam_ref[2:3, :], lam_ref[3:4, :]
    lam = (jnp.exp(jnp.sum(lq1 * lk1, axis=-1, keepdims=True))
           - jnp.exp(jnp.sum(lq2 * lk2, axis=-1, keepdims=True)) + lam_init)
    grp = GQ_HEADS // GQ_KV_HEADS
    jobs = [(_rows_at(aqt_ref[0, job * DA_QK:(job + 1) * DA_QK, :], job * DA_QK, BRANCH_W),
             ak_ref, avt_ref, (job // 2) * LANES) for job in range(2 * DA_HEADS)]
    jobs += [(_rows_at(bqt_ref[0, h * GQ_HD:(h + 1) * GQ_HD, :], (h // grp) * GQ_HD, GQ_KV_HEADS * GQ_HD),
              bk_ref, bvt_ref, (h // grp) * LANES) for h in range(GQ_HEADS)]
    o = _online_softmax_pv(pl.program_id(1) > 0, jobs, acc_ref, m_ref, s_ref)

    heads = []
    for h in range(DA_HEADS):
        d = o[2 * h] - lam * o[2 * h + 1]
        heads.append(d * lax.rsqrt(jnp.mean(d * d, axis=0, keepdims=True) + EPS))
    yt = jnp.concatenate(heads, axis=0) * (sg_ref[...] * (1.0 - lam_init))
    ya_ref[0] = (yt.T * gate_ref[0, :, 0:BRANCH_W].astype(F32)).astype(BF16)
    yb = jnp.concatenate(o[2 * DA_HEADS:], axis=0).T
    yb_ref[0] = (yb * gate_ref[0, :, BRANCH_W:2 * BRANCH_W].astype(F32)).astype(BF16)


def _attention(lam_init, aqt, ak, avt, bqt, bk, bvt, gate, lam_vecs, subln_col):
    b, s, _ = ak.shape
    t = ROW_TILE
    full = lambda a: pl.BlockSpec(a.shape, lambda i, j: (0,) * a.ndim)
    qt_spec = pl.BlockSpec((1, BRANCH_W, t), lambda i, j: (i, 0, j))
    keys = lambda a: pl.BlockSpec((1, s, a.shape[2]), lambda i, j: (i, 0, 0))
    vals = lambda a: pl.BlockSpec((1,) + a.shape[1:], lambda i, j: (i, 0, 0, 0))
    y_spec = pl.BlockSpec((1, t, BRANCH_W), lambda i, j: (i, j, 0))
    y_shape = jax.ShapeDtypeStruct((b, s, BRANCH_W), BF16)
    return pl.pallas_call(
        functools.partial(_attn_kernel, lam_init),
        grid=(b, s // t),
        in_specs=[qt_spec, keys(ak), vals(avt), qt_spec, keys(bk), vals(bvt),
                  pl.BlockSpec((1, t, gate.shape[2]), lambda i, j: (i, j, 0)),
                  full(lam_vecs), full(subln_col)],
        out_specs=[y_spec, y_spec],
        out_shape=[y_shape, y_shape],
        scratch_shapes=[pltpu.VMEM((ATTN_JOBS, LANES, t), F32), pltpu.VMEM((ATTN_JOBS, 1, t), F32),
                        pltpu.VMEM((2, ATTN_JOBS, KV_CHUNK, t), F32)],
        compiler_params=pltpu.CompilerParams(
            dimension_semantics=("parallel", "arbitrary"), vmem_limit_bytes=VMEM_LIMIT),
        name="attention",
    )(aqt, ak, avt, bqt, bk, bvt, gate, lam_vecs, subln_col)


def _chunk_cumsum(g, row_in_chunk, reverse):
    x = g
    sh = 1
    while sh < GLA_CHUNK:
        if reverse:
            nb = pltpu.roll(x, x.shape[0] - sh, axis=0)
            x = x + jnp.where(row_in_chunk < GLA_CHUNK - sh, nb, 0.0)
        else:
            nb = pltpu.roll(x, sh, axis=0)
            x = x + jnp.where(row_in_chunk >= sh, nb, 0.0)
        sh *= 2
    return x


def _gla_kernel(qk_ref, v_ref, g_ref, gate_ref, ng_ref, o_ref, acc_ref, st_ref):
    n_tiles = qk_ref.shape[1] // GLA_TILE
    ctx_tiles = CTX_LEN // GLA_TILE
    per = GLA_TILE // GLA_CHUNK
    ri = lax.broadcasted_iota(jnp.int32, (GLA_TILE, GLA_TILE), 0)
    ci = lax.broadcasted_iota(jnp.int32, (GLA_TILE, GLA_TILE), 1)
    same_chunk = (ri // GLA_CHUNK) == (ci // GLA_CHUNK)
    chunk_of_row = ri // GLA_CHUNK
    row_in_chunk = ri % GLA_CHUNK
    head_of_k = lax.broadcasted_iota(jnp.int32, (GLA_TILE, GLA_HEADS * GLA_DK), 1) // GLA_DK
    head_of_v = lax.broadcasted_iota(jnp.int32, (GLA_TILE, BRANCH_W), 1) // GLA_DV
    st_mask = ((lax.broadcasted_iota(jnp.int32, (BRANCH_W, GLA_HEADS * GLA_DK), 0) // GLA_DV)
               == (lax.broadcasted_iota(jnp.int32, (BRANCH_W, GLA_HEADS * GLA_DK), 1) // GLA_DK))
    m64 = _group_mean_matrix(BRANCH_W, GLA_DV)
    tris = [jnp.where(same_chunk & (ci <= ri), 1.0, 0.0), jnp.where(same_chunk & (ci >= ri), 1.0, 0.0)]
    tris4 = [jnp.concatenate([t] * GLA_HEADS, axis=1) for t in tris]

    nt_dims = (((1,), (1,)), ((), ()))
    tn_dims = (((0,), (0,)), ((), ()))

    def scan_step(i, carry):
        def tile_index(d, sub):
            t = i * GLA_STEP_TILES + sub
            if not d:
                return t
            return jnp.where(t < ctx_tiles, ctx_tiles - 1 - t, n_tiles + ctx_tiles - 1 - t)

        chains = [(bi, d, sub) for bi in range(GLA_BATCH) for d in (0, 1) for sub in range(GLA_STEP_TILES)]
        nc = len(chains)
        rows = [pl.ds(pl.multiple_of(tile_index(d, sub) * GLA_TILE, GLA_TILE), GLA_TILE)
                for _, d, sub in chains]
        ends = [[c * GLA_CHUNK if d else (c + 1) * GLA_CHUNK - 1 for c in range(per)] for _, d, _ in chains]

        cum = []
        for n, (bi, d, _) in enumerate(chains):
            cum.append(_chunk_cumsum(g_ref[bi, rows[n], 128 * d:128 * d + 128], row_in_chunk, bool(d)))

        qe, kd, kl, vb = [], [], [], []
        for n, (bi, d, _) in enumerate(chains):
            q = qk_ref[bi, rows[n], 0:128]
            k = qk_ref[bi, rows[n], 128:256]
            cum_last = jnp.concatenate(
                [jnp.broadcast_to(cum[n][e:e + 1, :], (GLA_CHUNK, 128)) for e in ends[n]], axis=0)
            qe.append(q * jnp.exp(cum[n]))
            kd.append(k * jnp.exp(-cum[n]))
            kl.append(k * jnp.exp(cum_last - cum[n]))
            vb.append(v_ref[bi, rows[n], :])

        qeb = [x.astype(BF16) for x in qe]
        att = []
        for n, (bi, d, _) in enumerate(chains):
            kd_heads = jnp.concatenate(
                [jnp.where(head_of_k == h, kd[n], 0.0).astype(BF16) for h in range(GLA_HEADS)], axis=0)
            a = lax.dot_general(qeb[n], kd_heads, nt_dims, preferred_element_type=F32)
            att.append((a * tris4[d]).astype(BF16))
        o = []
        for n in range(nc):
            v_heads = jnp.concatenate(
                [jnp.where(head_of_v == h, vb[n], jnp.zeros_like(vb[n])) for h in range(GLA_HEADS)], axis=0)
            o.append(jnp.dot(att[n], v_heads, preferred_element_type=F32))

        upd = []
        for n in range(nc):
            kl_chunks = jnp.concatenate(
                [jnp.where(chunk_of_row == c, kl[n], 0.0).astype(BF16) for c in range(per)], axis=1)
            upd.append(lax.dot_general(vb[n], kl_chunks, tn_dims, preferred_element_type=F32))
        seen = [[None] * per for _ in range(nc)]
        for slot in range(2 * GLA_BATCH):
            st = st_ref[slot]
            for n in range(slot * GLA_STEP_TILES, (slot + 1) * GLA_STEP_TILES):
                for c in (range(per - 1, -1, -1) if slot % 2 else range(per)):
                    seen[n][c] = st.astype(BF16)
                    decay = jnp.exp(cum[n][ends[n][c]:ends[n][c] + 1, :])
                    st = st * decay + jnp.where(st_mask, upd[n][:, c * 128:(c + 1) * 128], 0.0)
            st_ref[slot] = st
        for n in range(nc):
            qe_chunks = jnp.concatenate(
                [jnp.where(chunk_of_row == c, qe[n], 0.0).astype(BF16) for c in range(per)], axis=1)
            o_inter = lax.dot_general(qe_chunks, jnp.concatenate(seen[n], axis=1), nt_dims,
                                      preferred_element_type=F32)
            acc_ref[n // GLA_STEP_TILES, rows[n], :] = o[n] + o_inter
        return carry

    st_ref[...] = jnp.zeros_like(st_ref)
    lax.fori_loop(0, n_tiles // GLA_STEP_TILES, scan_step, 0)

    def finish_step(i, carry):
        rows = pl.ds(pl.multiple_of(i * ROW_TILE, ROW_TILE), ROW_TILE)
        tot = [acc_ref[2 * bi, rows, :] + acc_ref[2 * bi + 1, rows, :] for bi in range(GLA_BATCH)]
        sq = [_split2(t * t) for t in tot]
        ms = [sum(jnp.dot(p, m64, preferred_element_type=F32) for p in s) for s in sq]
        for bi in range(GLA_BATCH):
            y = tot[bi] * lax.rsqrt(ms[bi] + EPS) * ng_ref[...]
            o_ref[bi, rows, :] = (y * gate_ref[bi, rows, :].astype(F32)).astype(BF16)
        return carry

    lax.fori_loop(0, qk_ref.shape[1] // ROW_TILE, finish_step, 0)


def _gla(dqk, dv, dg, gate, ng):
    b, s, _ = dqk.shape
    nb = GLA_BATCH
    blk = lambda width, cb: pl.BlockSpec((nb, s, width), lambda i: (i, 0, cb))
    return pl.pallas_call(
        _gla_kernel,
        grid=(b // nb,),
        in_specs=[blk(256, 0), blk(256, 0), blk(256, 0), blk(256, 2),
                  pl.BlockSpec(ng.shape, lambda i: (0, 0))],
        out_specs=blk(256, 0),
        out_shape=jax.ShapeDtypeStruct((b, s, 256), BF16),
        scratch_shapes=[pltpu.VMEM((2 * nb, s, BRANCH_W), F32),
                        pltpu.VMEM((2 * nb, BRANCH_W, GLA_HEADS * GLA_DK), F32)],
        compiler_params=pltpu.CompilerParams(
            dimension_semantics=("parallel",), vmem_limit_bytes=VMEM_LIMIT),
        name="gla",
    )(dqk, dv, dg, gate, ng)


def _out_kernel(x_ref, mod_ref, ya_ref, yb_ref, yc_ref, yd_ref, w_ref, fg_ref, o_ref):
    y = jnp.concatenate([ya_ref[0], yb_ref[0], yc_ref[0], yd_ref[0]], axis=1)
    upd = jnp.dot(y, w_ref[...], preferred_element_type=F32)
    xn = x_ref[0] + mod_ref[0, 0, 2:3, :] * upd
    o_ref[0] = xn * lax.rsqrt(jnp.mean(xn * xn, axis=-1, keepdims=True) + EPS) * fg_ref[...]


def _out_proj(xs, modsel, ya, yb, yc, yd, wo, fg):
    b, s, _ = xs.shape
    t = ROW_TILE
    skip = CTX_LEN // t
    row = lambda width: pl.BlockSpec((1, t, width), lambda i, j: (i, j + skip, 0))
    return pl.pallas_call(
        _out_kernel,
        grid=(b, s // t - skip),
        in_specs=[
            row(D_MODEL),
            pl.BlockSpec((1, 1, 3, D_MODEL), lambda i, j: (i, 1, 0, 0)),
            row(256), row(256), row(256), row(256),
            pl.BlockSpec((None,) + wo.shape[1:], lambda i, j: (wo.shape[0] - 1, 0, 0)),
            pl.BlockSpec(fg.shape, lambda i, j: (0, 0)),
        ],
        out_specs=pl.BlockSpec((1, t, D_MODEL), lambda i, j: (i, j, 0)),
        out_shape=jax.ShapeDtypeStruct((b, s - skip * t, D_MODEL), F32),
        compiler_params=pltpu.CompilerParams(
            dimension_semantics=("parallel", "parallel"), vmem_limit_bytes=VMEM_LIMIT),
        name="out_proj_final",
    )(xs, modsel, ya, yb, yc, yd, wo, fg)


def _rope_tables(seq, dim, width):
    half = dim // 2
    quarter = half // 2
    lane = np.arange(width) % dim
    freq = ROPE_THETA ** (-(2.0 * (lane % quarter)) / half)
    pos_t = np.arange(seq)
    pos = np.where(lane[None, :] < half, (pos_t // GRID_W)[:, None], (pos_t % GRID_W)[:, None])
    ang = pos * freq[None, :]
    cos, sin = np.cos(ang), np.sin(ang)
    first = (lane % half) < quarter
    s_lo = np.where(first[None, :], -sin, 0.0)
    s_hi = np.where(first[None, :], 0.0, sin)
    lat = np.stack([cos, s_lo, s_hi])
    ctx = np.stack([np.ones((CTX_LEN, width)), np.zeros((CTX_LEN, width)), np.zeros((CTX_LEN, width))])
    return jnp.asarray(np.concatenate([ctx, lat], axis=1), dtype=F32)


def _pack_w_in(w):
    offs = [0]
    for n in (256, 256, 256, 256, 256, 128, 128, 256, 256, 256, 256, 128, 128, 256, 256, 16, 16):
        offs.append(offs[-1] + n)
    seg = [w[..., offs[i]:offs[i + 1]] for i in range(17)]
    aq, ak, av, az, bq, bk, bv, bz, cu, cv, cz, dq, dk, dv, dz, drf, drb = seg
    gap = jnp.zeros(w.shape[:-1] + (LANES // 2,), w.dtype)
    r_gap = jnp.concatenate([drf, drb, gap[..., :LANES // 2 - 2 * GLA_RANK]], axis=-1)
    av_sp = jnp.concatenate([av[..., 0:64], r_gap, av[..., 64:128], gap, av[..., 128:192], gap,
                             av[..., 192:256], gap], axis=-1)
    bvk = jnp.concatenate([bv[..., :64], bk[..., :64], bv[..., 64:], bk[..., 64:]], axis=-1)
    packed = jnp.concatenate([aq, ak, av_sp, az, bq, bvk, bz, cu, cv, cz, dq, dk, dv, dz], axis=-1)
    assert packed.shape[-1] == P_PACK
    return packed.astype(BF16)


def kernel(x, c, ctx, c_ctx, ada_w, ada_b, norm_g, w_in, da_lq1, da_lk1, da_lq2, da_lk2,
           da_subln_g, gq_qnorm_g, gq_knorm_g, sg_ln_g, sg_ln_b, sg_w, sg_b,
           gla_w2_f, gla_b_f, gla_w2_b, gla_b_b, gla_norm_g, w_out, final_norm_g):
    b, seq, d = x.shape
    assert (seq, d, ctx.shape[1]) == (seq // ROW_TILE * ROW_TILE, D_MODEL, CTX_LEN)

    n_mod = 32
    cpad = jnp.zeros((n_mod, d), F32).at[:b].set(c).at[b].set(c_ctx)
    mod = _modulation(cpad, ada_w, ada_b)

    taba = _rope_tables(seq, DA_QK, 256)
    tabb = _rope_tables(seq, GQ_HD, 256)
    fg = final_norm_g.reshape(1, d)
    wp_all, wo_all, sgw_all = _pack_w_in(w_in), w_out.astype(BF16), sg_w.astype(BF16)

    nl = DEPTH
    modsel_all = jnp.concatenate([jnp.broadcast_to(mod[:, b].reshape(nl, 1, 1, 3, d), (nl, b, 1, 3, d)),
                                  mod[:, :b].reshape(nl, b, 1, 3, d)], axis=2)
    w2_all = (jnp.zeros((nl, LANES, 256), F32)
              .at[:, R_LANE:R_LANE + GLA_RANK, 0:128].set(gla_w2_f)
              .at[:, R_LANE + GLA_RANK:R_LANE + 2 * GLA_RANK, 128:256].set(gla_w2_b)).astype(BF16)
    b2_all = jnp.concatenate([gla_b_f, gla_b_b], axis=1).reshape(nl, 1, 256)
    sgb_all = jnp.repeat(jnp.swapaxes(sg_b, 1, 2), BRANCH_W // SG_GROUPS, axis=2)
    tiled = lambda g, n: jnp.tile(g, (1, n)).reshape(nl, 1, g.shape[1] * n)
    gqq_all, gqk_all = tiled(gq_qnorm_g, GQ_HEADS), tiled(gq_knorm_g, GQ_KV_HEADS)
    subln_all, glan_all = tiled(da_subln_g, DA_HEADS), tiled(gla_norm_g, GLA_HEADS)
    lam_all = jnp.stack([da_lq1, da_lk1, da_lq2, da_lk2], axis=1)

    stream_in = (ctx, x)
    for i in range(DEPTH):
        lam_init = 0.8 - 0.6 * math.exp(-0.3 * i)
        (xs, aqt, ak, avt, bqt, bk, bvt, gate, yc, dqk, dv, dg) = _layer(
            i, stream_in, modsel_all[i], norm_g[i].reshape(1, d), wp_all, taba, tabb,
            gqq_all[i], gqk_all[i], sg_ln_g[i].reshape(1, 256), sg_ln_b[i].reshape(1, 256),
            sgw_all, sgb_all[i], w2_all[i], b2_all[i])
        ya, yb = _attention(lam_init, aqt, ak, avt, bqt, bk, bvt, gate, lam_all[i],
                            subln_all[i].reshape(BRANCH_W, 1))
        yd = _gla(dqk, dv, dg, gate, glan_all[i])
        stream_in = (xs, modsel_all[i], ya, yb, yc, yd, wo_all)
    return _out_proj(*stream_in, fg)
```

```python
import functools
import math

import jax
import jax.numpy as jnp
import numpy as np
from jax import lax
from jax.experimental import pallas as pl
from jax.experimental.pallas import tpu as pltpu

F32 = jnp.float32
BF16 = jnp.bfloat16

D_MODEL = 1024
DEPTH = 4
CTX_LEN = 256
GRID_W = 64
BRANCH_W = 256
ROPE_THETA = 10000.0
EPS = 1e-6
DA_HEADS = 4
DA_QK = 32
DA_V = 64
GQ_HEADS = 4
GQ_KV_HEADS = 2
GQ_HD = 64
SG_GROUPS = 4
SG_CHUNK = 128
GLA_HEADS = 4
GLA_DV = 64
GLA_DK = 32
GLA_RANK = 16
GLA_NORMALIZER = 16.0
GLA_CHUNK = 32

LANES = 128
ROW_TILE = 256
KV_CHUNK = 512
LAYER_BATCH = 2
OUT_BATCH = 4
GLA_TILE = 128
GLA_BATCH = 2
GLA_STEP_TILES = 3
VMEM_LIMIT = 56 * 1024 * 1024

C_AQ, C_AK, C_AV, C_AZ = 0, 256, 512, 1024
C_BQ, C_BVK, C_BZ = 1280, 1536, 1792
C_CU, C_CV, C_CZ = 2048, 2304, 2560
C_DQK, C_DV, C_DZ = 2816, 3072, 3328
P_PACK = 3584
R_LANE = 64
LOG2E = math.log2(math.e)


def _silu(x):
    return x * (1.0 / (1.0 + jnp.exp(-x)))


def _group_mean_matrix(width, group):
    r = lax.broadcasted_iota(jnp.int32, (width, width), 0) // group
    c = lax.broadcasted_iota(jnp.int32, (width, width), 1) // group
    return jnp.where(r == c, 1.0 / group, 0.0).astype(BF16)


def _split2(x):
    hi = x.astype(BF16)
    return hi, (x - hi.astype(F32)).astype(BF16)


def _group_mean(x, mat):
    return sum(jnp.dot(p, mat, preferred_element_type=F32) for p in _split2(x))


def _rope(x, cos, s_lo, s_hi, shift):
    outs = []
    for c in range(x.shape[1] // LANES):
        sl = slice(c * LANES, (c + 1) * LANES)
        xc = x[:, sl]
        up = pltpu.roll(xc, LANES - shift, axis=1)
        dn = pltpu.roll(xc, shift, axis=1)
        outs.append(xc * cos[:, sl] + up * s_lo[:, sl] + dn * s_hi[:, sl])
    return outs[0] if len(outs) == 1 else jnp.concatenate(outs, axis=1)


def _with_ones(v):
    lane = lax.broadcasted_iota(jnp.int32, v.shape, 1) % LANES
    return jnp.where(lane < LANES // 2, v, 1.0)


def _mod_kernel(c_ref, w_ref, b_ref, o_ref):
    s = _silu(c_ref[...]).astype(BF16)
    o_ref[0] = jnp.dot(s, w_ref[0].astype(BF16), preferred_element_type=F32) + b_ref[0]


def _modulation(cpad, ada_w, ada_b):
    n = cpad.shape[0]
    return pl.pallas_call(
        _mod_kernel,
        grid=(DEPTH, 3),
        in_specs=[
            pl.BlockSpec((n, D_MODEL), lambda i, j: (0, 0)),
            pl.BlockSpec((1, D_MODEL, D_MODEL), lambda i, j: (i, 0, j)),
            pl.BlockSpec((1, 1, D_MODEL), lambda i, j: (i, 0, j)),
        ],
        out_specs=pl.BlockSpec((1, n, D_MODEL), lambda i, j: (i, 0, j)),
        out_shape=jax.ShapeDtypeStruct((DEPTH, n, 3 * D_MODEL), F32),
        compiler_params=pltpu.CompilerParams(vmem_limit_bytes=VMEM_LIMIT),
        name="modulation",
    )(cpad, ada_w, ada_b.reshape(DEPTH, 1, 3 * D_MODEL))


def _layer_kernel(first, *refs):
    if first:
        ctx_ref, x_ref = refs[:2]
        refs = refs[2:]
    else:
        xs_ref, modp_ref, ya_ref, yb_ref, yc_in_ref, yd_ref, wo_ref = refs[:7]
        refs = refs[7:]
    (mod_ref, ng_ref, w_ref, taba_ref, tabb_ref, gqq_ref, gqk_ref, lng_ref, lnb_ref, sgw_ref,
     sgb_ref, w2_ref, b2_ref,
     xs_out_ref, aq_ref, ak_ref, av_ref, bq_ref, bk_ref, bv_ref, gate_ref, yc_ref,
     dqk_ref, dv_ref, dg_ref) = refs

    def residual_and_norm(bi):
        if first:
            x = jnp.where(pl.program_id(1) == 0, ctx_ref[bi], x_ref[bi])
        else:
            y_prev = jnp.concatenate([ya_ref[bi], yb_ref[bi], yc_in_ref[bi], yd_ref[bi]], axis=1)
            x = xs_ref[bi] + modp_ref[bi, 0, 2:3, :] * jnp.dot(y_prev, wo_ref[...],
                                                               preferred_element_type=F32)
        xs_out_ref[bi] = x
        shift = mod_ref[bi, 0, 0:1, :]
        scale = mod_ref[bi, 0, 1:2, :]
        y = x * lax.rsqrt(jnp.mean(x * x, axis=-1, keepdims=True) + EPS) * ng_ref[...]
        return (y * (1.0 + scale) + shift).astype(BF16)

    def project(bi, hb):
        def proj(lo, width):
            return jnp.dot(hb, w_ref[:, lo:lo + width], preferred_element_type=F32)

        m64 = _group_mean_matrix(BRANCH_W, GQ_HD)
        half_lane = lax.broadcasted_iota(jnp.int32, (ROW_TILE, LANES), 1) < LANES // 2
        p_bq = proj(C_BQ, 256)
        p_bvk = proj(C_BVK, 256)
        p_cv = proj(C_CV, 256)
        p_av = proj(C_AV, 512)

        bq_sq = _split2(p_bq * p_bq)
        bk = jnp.where(half_lane, pltpu.roll(p_bvk[:, :LANES], LANES // 2, axis=1), p_bvk[:, LANES:])
        bk_sq = _split2(bk * bk)
        mu = jnp.mean(p_cv, axis=-1, keepdims=True)
        cen = p_cv - mu
        var = jnp.mean(cen * cen, axis=-1, keepdims=True)
        vn = (cen * lax.rsqrt(var + EPS) * lng_ref[...] + lnb_ref[...]).astype(BF16)
        r = p_av[:, :LANES].astype(BF16)
        av_ref[bi, 0] = _with_ones(p_av).T.astype(BF16)
        bv_ref[bi, 0] = _with_ones(p_bvk).T.astype(BF16)

        ca, sa_lo, sa_hi = taba_ref[0], taba_ref[1], taba_ref[2]
        aq = _rope(proj(C_AQ, 256), ca, sa_lo, sa_hi, DA_QK // 4) * (DA_QK ** -0.5 * LOG2E)
        aq_ref[bi] = aq.T.astype(BF16)
        ak = _rope(proj(C_AK, 256), ca, sa_lo, sa_hi, DA_QK // 4)
        ak_ref[bi] = ak.astype(BF16)

        bq_ms = sum(jnp.dot(p, m64, preferred_element_type=F32) for p in bq_sq)
        bk_ms = sum(jnp.dot(p, m64[:LANES, :LANES], preferred_element_type=F32) for p in bk_sq)
        gl = jnp.dot(r, w2_ref[...], preferred_element_type=F32) + b2_ref[...]

        gate_ref[bi, :, 0:256] = _silu(proj(C_AZ, 256)).astype(BF16)
        gate_ref[bi, :, 256:512] = _silu(proj(C_BZ, 256)).astype(BF16)
        gate_ref[bi, :, 512:768] = _silu(proj(C_DZ, 256)).astype(BF16)

        lane_group = lax.broadcasted_iota(jnp.int32, (SG_CHUNK, BRANCH_W), 1) // (BRANCH_W // SG_GROUPS)
        mixed = []
        for n in range(ROW_TILE // SG_CHUNK):
            vchunk = vn[n * SG_CHUNK:(n + 1) * SG_CHUNK, :]
            acc = jnp.zeros((SG_CHUNK, BRANCH_W), F32)
            for g in range(SG_GROUPS):
                acc = jnp.where(lane_group == g,
                                jnp.dot(sgw_ref[g], vchunk, preferred_element_type=F32), acc)
            mixed.append(acc + sgb_ref[...])

        cb, sb_lo, sb_hi = tabb_ref[0], tabb_ref[1], tabb_ref[2]
        bq = p_bq * lax.rsqrt(bq_ms + EPS) * gqq_ref[...]
        bq_ref[bi] = (_rope(bq, cb, sb_lo, sb_hi, GQ_HD // 4) * (GQ_HD ** -0.5 * LOG2E)).T.astype(BF16)
        bk = bk * lax.rsqrt(bk_ms + EPS) * gqk_ref[...]
        bk = _rope(bk, cb[:, :LANES], sb_lo[:, :LANES], sb_hi[:, :LANES], GQ_HD // 4)
        bk_ref[bi] = bk.astype(BF16)

        yc = proj(C_CU, 256) * jnp.concatenate(mixed, axis=0) * _silu(proj(C_CZ, 256))
        yc_ref[bi] = yc.astype(BF16)

        q_scale = jnp.where(lax.broadcasted_iota(jnp.int32, (1, 256), 1) < 128, GLA_DK ** -0.5, 1.0)
        dqk_ref[bi] = proj(C_DQK, 256) * q_scale
        dv_ref[bi] = proj(C_DV, 256).astype(BF16)
        log_sig = jnp.minimum(gl, 0.0) - jnp.log(1.0 + jnp.exp(-jnp.abs(gl)))
        dg_ref[bi] = log_sig * (1.0 / GLA_NORMALIZER)

    nb = xs_out_ref.shape[0]
    hbs = [residual_and_norm(bi) for bi in range(nb)]
    for bi in range(nb):
        project(bi, hbs[bi])


def _layer(layer, stream_in, modsel, ng, wp, taba, tabb, gqq, gqk, lng, lnb, sgw, sgb, w2, b2):
    first = layer == 0
    t = ROW_TILE
    stacked = lambda a, idx: pl.BlockSpec((None,) + a.shape[1:], lambda i, j: (idx,) + (0,) * (a.ndim - 1))
    b = stream_in[0].shape[0]
    s = CTX_LEN + stream_in[1].shape[1] if first else stream_in[0].shape[1]
    nb = LAYER_BATCH
    row = lambda width: pl.BlockSpec((nb, t, width), lambda i, j: (i, j, 0))
    colT = lambda height: pl.BlockSpec((nb, height, t), lambda i, j: (i, 0, j))
    full = lambda a: pl.BlockSpec(a.shape, lambda i, j: (0,) * a.ndim)
    tab = pl.BlockSpec((3, t, 256), lambda i, j: (0, j, 0))
    mods = pl.BlockSpec((nb, 1, 3, D_MODEL), lambda i, j: (i, jnp.minimum(j, 1), 0, 0))
    shp = lambda width, dt: jax.ShapeDtypeStruct((b, s, width), dt)
    shpT = lambda height: jax.ShapeDtypeStruct((b, height, s), BF16)
    tileT = lambda height: pl.BlockSpec((nb, 1, height, t), lambda i, j: (i, j, 0, 0))
    if first:
        stream_specs = [pl.BlockSpec((nb, t, D_MODEL), lambda i, j: (i, 0, 0)),
                        pl.BlockSpec((nb, t, D_MODEL), lambda i, j: (i, jnp.maximum(j - 1, 0), 0))]
    else:
        stream_specs = [row(D_MODEL), mods, row(256), row(256), row(256), row(256),
                        stacked(stream_in[6], layer - 1)]
    return pl.pallas_call(
        functools.partial(_layer_kernel, first),
        grid=(b // nb, s // t),
        in_specs=stream_specs + [
            mods, full(ng), stacked(wp, layer), tab, tab, full(gqq), full(gqk), full(lng), full(lnb),
            stacked(sgw, layer), full(sgb), full(w2), full(b2),
        ],
        out_specs=[row(D_MODEL), colT(256), row(256), tileT(512), colT(256), row(128), tileT(256),
                   row(768), row(256), row(256), row(256), row(256)],
        out_shape=[shp(D_MODEL, F32),
                   shpT(256), shp(256, BF16), jax.ShapeDtypeStruct((b, s // t, 512, t), BF16),
                   shpT(256), shp(128, BF16), jax.ShapeDtypeStruct((b, s // t, 256, t), BF16),
                   shp(768, BF16), shp(256, BF16), shp(256, F32), shp(256, BF16), shp(256, F32)],
        compiler_params=pltpu.CompilerParams(
            dimension_semantics=("parallel", "arbitrary"), vmem_limit_bytes=VMEM_LIMIT),
        name="layer_first" if first else "layer",
    )(*stream_in, modsel, ng, wp, taba, tabb, gqq, gqk, lng, lnb, sgw, sgb, w2, b2)


def _max_over_rows(s):
    m = s
    for part in (256, 32):
        if m.shape[0] > part and m.shape[0] % part == 0:
            m = jnp.max(m.reshape(m.shape[0] // part, part, m.shape[1]), axis=0)
    return jnp.max(m, axis=0, keepdims=True)


def _online_softmax_pv(latent, jobs, acc_ref, m_ref, s_ref):
    n_jobs = len(jobs)
    tiles_per_chunk = KV_CHUNK // ROW_TILE
    n_chunks = (jobs[0][1].shape[1] - CTX_LEN) // KV_CHUNK

    def key_loader():
        loaded = {}

        def rows(j, lo, hi):
            k_ref = jobs[j][1]
            if (id(k_ref), lo) not in loaded:
                loaded[(id(k_ref), lo)] = k_ref[0, lo:hi, :]
            return loaded[(id(k_ref), lo)]

        return rows

    def qk(k_rows, j):
        return jnp.dot(k_rows, jobs[j][0], preferred_element_type=F32)

    def softmax_pv(s, vt, j, first):
        m_new = _max_over_rows(s)
        if not first:
            m_old = m_ref[j]
            m_new = jnp.maximum(m_old, m_new)
            alpha = jnp.exp2(m_old - m_new)
        m_ref[j] = m_new
        upd = jnp.dot(vt, jnp.exp2(s - m_new).astype(BF16), preferred_element_type=F32)
        acc_ref[j] = upd if first else acc_ref[j] * alpha + upd

    def vt_tile(tile, j):
        _, _, vt_ref, v_row = jobs[j]
        return vt_ref[0, tile, v_row:v_row + LANES, :]

    def vt_chunk(c, j):
        t0 = CTX_LEN // ROW_TILE + c * tiles_per_chunk
        return jnp.concatenate([vt_tile(t0 + i, j) for i in range(tiles_per_chunk)], axis=1)

    def chunk_rows(c):
        return CTX_LEN + c * KV_CHUNK, CTX_LEN + (c + 1) * KV_CHUNK

    @pl.when(jnp.logical_not(latent))
    def _():
        keys = key_loader()
        s = [qk(keys(j, 0, CTX_LEN), j) for j in range(n_jobs)]
        for j in range(n_jobs):
            softmax_pv(s[j], vt_tile(0, j), j, True)

    @pl.when(latent)
    def _():
        keys = key_loader()
        s = [qk(keys(j, 0, CTX_LEN), j) for j in range(n_jobs)]
        s_ref[0, 0] = qk(keys(0, *chunk_rows(0)), 0)
        for j in range(n_jobs):
            softmax_pv(s[j], vt_tile(0, j), j, True)
            if j + 1 < n_jobs:
                s_ref[0, j + 1] = qk(keys(j + 1, *chunk_rows(0)), j + 1)

        for c in range(n_chunks):
            slot = c % 2
            more = c + 1 < n_chunks
            if more:
                s_ref[1 - slot, 0] = qk(keys(0, *chunk_rows(c + 1)), 0)
            for j in range(n_jobs):
                softmax_pv(s_ref[slot, j], vt_chunk(c, j), j, False)
                if more and j + 1 < n_jobs:
                    s_ref[1 - slot, j + 1] = qk(keys(j + 1, *chunk_rows(c + 1)), j + 1)

    outs = []
    for j in range(n_jobs):
        acc = acc_ref[j]
        outs.append(acc[:LANES // 2, :] * (1.0 / acc[LANES // 2:LANES // 2 + 1, :]))
    return outs


def _rows_at(x, row0, total):
    parts = []
    if row0:
        parts.append(jnp.zeros((row0, x.shape[1]), x.dtype))
    parts.append(x)
    if total - row0 - x.shape[0]:
        parts.append(jnp.zeros((total - row0 - x.shape[0], x.shape[1]), x.dtype))
    return jnp.concatenate(parts, axis=0)


ATTN_JOBS = 2 * DA_HEADS + GQ_HEADS


def _attn_kernel(lam_init, aqt_ref, ak_ref, avt_ref, bqt_ref, bk_ref, bvt_ref, gate_ref, lam_ref,
                 sg_ref, ya_ref, yb_ref, acc_ref, m_ref, s_ref):
    lq1, lk1, lq2, lk2 = lam_ref[0:1, :], lam_ref[1:2, :], lam_ref[2:3, :], lam_ref[3:4, :]
    lam = (jnp.exp(jnp.sum(lq1 * lk1, axis=-1, keepdims=True))
           - jnp.exp(jnp.sum(lq2 * lk2, axis=-1, keepdims=True)) + lam_init)
    grp = GQ_HEADS // GQ_KV_HEADS
    jobs = [(_rows_at(aqt_ref[0, job * DA_QK:(job + 1) * DA_QK, :], job * DA_QK, BRANCH_W),
             ak_ref, avt_ref, (job // 2) * LANES) for job in range(2 * DA_HEADS)]
    jobs += [(_rows_at(bqt_ref[0, h * GQ_HD:(h + 1) * GQ_HD, :], (h // grp) * GQ_HD, GQ_KV_HEADS * GQ_HD),
              bk_ref, bvt_ref, (h // grp) * LANES) for h in range(GQ_HEADS)]
    o = _online_softmax_pv(pl.program_id(1) > 0, jobs, acc_ref, m_ref, s_ref)

    heads = []
    for h in range(DA_HEADS):
        d = o[2 * h] - lam * o[2 * h + 1]
        heads.append(d * lax.rsqrt(jnp.mean(d * d, axis=0, keepdims=True) + EPS))
    yt = jnp.concatenate(heads, axis=0) * (sg_ref[...] * (1.0 - lam_init))
    ya_ref[0] = (yt.T * gate_ref[0, :, 0:BRANCH_W].astype(F32)).astype(BF16)
    yb = jnp.concatenate(o[2 * DA_HEADS:], axis=0).T
    yb_ref[0] = (yb * gate_ref[0, :, BRANCH_W:2 * BRANCH_W].astype(F32)).astype(BF16)


def _attention(lam_init, aqt, ak, avt, bqt, bk, bvt, gate, lam_vecs, subln_col):
    b, s, _ = ak.shape
    t = ROW_TILE
    full = lambda a: pl.BlockSpec(a.shape, lambda i, j: (0,) * a.ndim)
    qt_spec = pl.BlockSpec((1, BRANCH_W, t), lambda i, j: (i, 0, j))
    keys = lambda a: pl.BlockSpec((1, s, a.shape[2]), lambda i, j: (i, 0, 0))
    vals = lambda a: pl.BlockSpec((1,) + a.shape[1:], lambda i, j: (i, 0, 0, 0))
    y_spec = pl.BlockSpec((1, t, BRANCH_W), lambda i, j: (i, j, 0))
    y_shape = jax.ShapeDtypeStruct((b, s, BRANCH_W), BF16)
    return pl.pallas_call(
        functools.partial(_attn_kernel, lam_init),
        grid=(b, s // t),
        in_specs=[qt_spec, keys(ak), vals(avt), qt_spec, keys(bk), vals(bvt),
                  pl.BlockSpec((1, t, gate.shape[2]), lambda i, j: (i, j, 0)),
                  full(lam_vecs), full(subln_col)],
        out_specs=[y_spec, y_spec],
        out_shape=[y_shape, y_shape],
        scratch_shapes=[pltpu.VMEM((ATTN_JOBS, LANES, t), F32), pltpu.VMEM((ATTN_JOBS, 1, t), F32),
                        pltpu.VMEM((2, ATTN_JOBS, KV_CHUNK, t), F32)],
        compiler_params=pltpu.CompilerParams(
            dimension_semantics=("parallel", "arbitrary"), vmem_limit_bytes=VMEM_LIMIT),
        name="attention",
    )(aqt, ak, avt, bqt, bk, bvt, gate, lam_vecs, subln_col)


def _chunk_cumsum(g, row_in_chunk, reverse):
    x = g
    sh = 1
    while sh < GLA_CHUNK:
        if reverse:
            nb = pltpu.roll(x, x.shape[0] - sh, axis=0)
            x = x + jnp.where(row_in_chunk < GLA_CHUNK - sh, nb, 0.0)
        else:
            nb = pltpu.roll(x, sh, axis=0)
            x = x + jnp.where(row_in_chunk >= sh, nb, 0.0)
        sh *= 2
    return x


def _gla_kernel(qk_ref, v_ref, g_ref, gate_ref, ng_ref, o_ref, acc_ref, st_ref):
    n_tiles = qk_ref.shape[1] // GLA_TILE
    ctx_tiles = CTX_LEN // GLA_TILE
    per = GLA_TILE // GLA_CHUNK
    ri = lax.broadcasted_iota(jnp.int32, (GLA_TILE, GLA_TILE), 0)
    ci = lax.broadcasted_iota(jnp.int32, (GLA_TILE, GLA_TILE), 1)
    same_chunk = (ri // GLA_CHUNK) == (ci // GLA_CHUNK)
    chunk_of_row = ri // GLA_CHUNK
    row_in_chunk = ri % GLA_CHUNK
    head_of_k = lax.broadcasted_iota(jnp.int32, (GLA_TILE, GLA_HEADS * GLA_DK), 1) // GLA_DK
    head_of_v = lax.broadcasted_iota(jnp.int32, (GLA_TILE, BRANCH_W), 1) // GLA_DV
    st_mask = ((lax.broadcasted_iota(jnp.int32, (BRANCH_W, GLA_HEADS * GLA_DK), 0) // GLA_DV)
               == (lax.broadcasted_iota(jnp.int32, (BRANCH_W, GLA_HEADS * GLA_DK), 1) // GLA_DK))
    m64 = _group_mean_matrix(BRANCH_W, GLA_DV)
    tris = [jnp.where(same_chunk & (ci <= ri), 1.0, 0.0), jnp.where(same_chunk & (ci >= ri), 1.0, 0.0)]
    tris4 = [jnp.concatenate([t] * GLA_HEADS, axis=1) for t in tris]

    nt_dims = (((1,), (1,)), ((), ()))
    tn_dims = (((0,), (0,)), ((), ()))

    def scan_step(i, carry):
        def tile_index(d, sub):
            t = i * GLA_STEP_TILES + sub
            if not d:
                return t
            return jnp.where(t < ctx_tiles, ctx_tiles - 1 - t, n_tiles + ctx_tiles - 1 - t)

        chains = [(bi, d, sub) for bi in range(GLA_BATCH) for d in (0, 1) for sub in range(GLA_STEP_TILES)]
        nc = len(chains)
        rows = [pl.ds(pl.multiple_of(tile_index(d, sub) * GLA_TILE, GLA_TILE), GLA_TILE)
                for _, d, sub in chains]
        ends = [[c * GLA_CHUNK if d else (c + 1) * GLA_CHUNK - 1 for c in range(per)] for _, d, _ in chains]

        cum = []
        for n, (bi, d, _) in enumerate(chains):
            cum.append(_chunk_cumsum(g_ref[bi, rows[n], 128 * d:128 * d + 128], row_in_chunk, bool(d)))

        qe, kd, kl, vb = [], [], [], []
        for n, (bi, d, _) in enumerate(chains):
            q = qk_ref[bi, rows[n], 0:128]
            k = qk_ref[bi, rows[n], 128:256]
            cum_last = jnp.concatenate(
                [jnp.broadcast_to(cum[n][e:e + 1, :], (GLA_CHUNK, 128)) for e in ends[n]], axis=0)
            qe.append(q * jnp.exp(cum[n]))
            kd.append(k * jnp.exp(-cum[n]))
            kl.append(k * jnp.exp(cum_last - cum[n]))
            vb.append(v_ref[bi, rows[n], :])

        qeb = [x.astype(BF16) for x in qe]
        att = []
        for n, (bi, d, _) in enumerate(chains):
            kd_heads = jnp.concatenate(
                [jnp.where(head_of_k == h, kd[n], 0.0).astype(BF16) for h in range(GLA_HEADS)], axis=0)
            a = lax.dot_general(qeb[n], kd_heads, nt_dims, preferred_element_type=F32)
            att.append((a * tris4[d]).astype(BF16))
        o = []
        for n in range(nc):
            v_heads = jnp.concatenate(
                [jnp.where(head_of_v == h, vb[n], jnp.zeros_like(vb[n])) for h in range(GLA_HEADS)], axis=0)
            o.append(jnp.dot(att[n], v_heads, preferred_element_type=F32))

        upd = []
        for n in range(nc):
            kl_chunks = jnp.concatenate(
                [jnp.where(chunk_of_row == c, kl[n], 0.0).astype(BF16) for c in range(per)], axis=1)
            upd.append(lax.dot_general(vb[n], kl_chunks, tn_dims, preferred_element_type=F32))
        seen = [[None] * per for _ in range(nc)]
        for slot in range(2 * GLA_BATCH):
            st = st_ref[slot]
            for n in range(slot * GLA_STEP_TILES, (slot + 1) * GLA_STEP_TILES):
                for c in (range(per - 1, -1, -1) if slot % 2 else range(per)):
                    seen[n][c] = st.astype(BF16)
                    decay = jnp.exp(cum[n][ends[n][c]:ends[n][c] + 1, :])
                    st = st * decay + jnp.where(st_mask, upd[n][:, c * 128:(c + 1) * 128], 0.0)
            st_ref[slot] = st
        for n in range(nc):
            qe_chunks = jnp.concatenate(
                [jnp.where(chunk_of_row == c, qe[n], 0.0).astype(BF16) for c in range(per)], axis=1)
            o_inter = lax.dot_general(qe_chunks, jnp.concatenate(seen[n], axis=1), nt_dims,
                                      preferred_element_type=F32)
            acc_ref[n // GLA_STEP_TILES, rows[n], :] = o[n] + o_inter
        return carry

    st_ref[...] = jnp.zeros_like(st_ref)
    lax.fori_loop(0, n_tiles // GLA_STEP_TILES, scan_step, 0)

    def finish_step(i, carry):
        rows = pl.ds(pl.multiple_of(i * ROW_TILE, ROW_TILE), ROW_TILE)
        tot = [acc_ref[2 * bi, rows, :] + acc_ref[2 * bi + 1, rows, :] for bi in range(GLA_BATCH)]
        sq = [_split2(t * t) for t in tot]
        ms = [sum(jnp.dot(p, m64, preferred_element_type=F32) for p in s) for s in sq]
        for bi in range(GLA_BATCH):
            y = tot[bi] * lax.rsqrt(ms[bi] + EPS) * ng_ref[...]
            o_ref[bi, rows, :] = (y * gate_ref[bi, rows, :].astype(F32)).astype(BF16)
        return carry

    lax.fori_loop(0, qk_ref.shape[1] // ROW_TILE, finish_step, 0)


def _gla(dqk, dv, dg, gate, ng):
    b, s, _ = dqk.shape
    nb = GLA_BATCH
    blk = lambda width, cb: pl.BlockSpec((nb, s, width), lambda i: (i, 0, cb))
    return pl.pallas_call(
        _gla_kernel,
        grid=(b // nb,),
        in_specs=[blk(256, 0), blk(256, 0), blk(256, 0), blk(256, 2),
                  pl.BlockSpec(ng.shape, lambda i: (0, 0))],
        out_specs=blk(256, 0),
        out_shape=jax.ShapeDtypeStruct((b, s, 256), BF16),
        scratch_shapes=[pltpu.VMEM((2 * nb, s, BRANCH_W), F32),
                        pltpu.VMEM((2 * nb, BRANCH_W, GLA_HEADS * GLA_DK), F32)],
        compiler_params=pltpu.CompilerParams(
            dimension_semantics=("parallel",), vmem_limit_bytes=VMEM_LIMIT),
        name="gla",
    )(dqk, dv, dg, gate, ng)


def _out_kernel(x_ref, mod_ref, ya_ref, yb_ref, yc_ref, yd_ref, w_ref, fg_ref, o_ref):
    for bi in range(x_ref.shape[0]):
        y = jnp.concatenate([ya_ref[bi], yb_ref[bi], yc_ref[bi], yd_ref[bi]], axis=1)
        upd = jnp.dot(y, w_ref[...], preferred_element_type=F32)
        xn = x_ref[bi] + mod_ref[bi, 0, 2:3, :] * upd
        o_ref[bi] = xn * lax.rsqrt(jnp.mean(xn * xn, axis=-1, keepdims=True) + EPS) * fg_ref[...]


def _out_proj(xs, modsel, ya, yb, yc, yd, wo, fg):
    b, s, _ = xs.shape
    t = ROW_TILE
    nb = OUT_BATCH
    skip = CTX_LEN // t
    row = lambda width: pl.BlockSpec((nb, t, width), lambda i, j: (i, j + skip, 0))
    return pl.pallas_call(
        _out_kernel,
        grid=(b // nb, s // t - skip),
        in_specs=[
            row(D_MODEL),
            pl.BlockSpec((nb, 1, 3, D_MODEL), lambda i, j: (i, 1, 0, 0)),
            row(256), row(256), row(256), row(256),
            pl.BlockSpec((None,) + wo.shape[1:], lambda i, j: (wo.shape[0] - 1, 0, 0)),
            pl.BlockSpec(fg.shape, lambda i, j: (0, 0)),
        ],
        out_specs=pl.BlockSpec((nb, t, D_MODEL), lambda i, j: (i, j, 0)),
        out_shape=jax.ShapeDtypeStruct((b, s - skip * t, D_MODEL), F32),
        compiler_params=pltpu.CompilerParams(
            dimension_semantics=("parallel", "parallel"), vmem_limit_bytes=VMEM_LIMIT),
        name="out_proj_final",
    )(xs, modsel, ya, yb, yc, yd, wo, fg)


def _rope_tables(seq, dim, width):
    half = dim // 2
    quarter = half // 2
    lane = np.arange(width) % dim
    freq = ROPE_THETA ** (-(2.0 * (lane % quarter)) / half)
    pos_t = np.arange(seq)
    pos = np.where(lane[None, :] < half, (pos_t // GRID_W)[:, None], (pos_t % GRID_W)[:, None])
    ang = pos * freq[None, :]
    cos, sin = np.cos(ang), np.sin(ang)
    first = (lane % half) < quarter
    s_lo = np.where(first[None, :], -sin, 0.0)
    s_hi = np.where(first[None, :], 0.0, sin)
    lat = np.stack([cos, s_lo, s_hi])
    ctx = np.stack([np.ones((CTX_LEN, width)), np.zeros((CTX_LEN, width)), np.zeros((CTX_LEN, width))])
    return jnp.asarray(np.concatenate([ctx, lat], axis=1), dtype=F32)


def _pack_w_in(w):
    offs = [0]
    for n in (256, 256, 256, 256, 256, 128, 128, 256, 256, 256, 256, 128, 128, 256, 256, 16, 16):
        offs.append(offs[-1] + n)
    seg = [w[..., offs[i]:offs[i + 1]] for i in range(17)]
    aq, ak, av, az, bq, bk, bv, bz, cu, cv, cz, dq, dk, dv, dz, drf, drb = seg
    gap = jnp.zeros(w.shape[:-1] + (LANES // 2,), w.dtype)
    r_gap = jnp.concatenate([drf, drb, gap[..., :LANES // 2 - 2 * GLA_RANK]], axis=-1)
    av_sp = jnp.concatenate([av[..., 0:64], r_gap, av[..., 64:128], gap, av[..., 128:192], gap,
                             av[..., 192:256], gap], axis=-1)
    bvk = jnp.concatenate([bv[..., :64], bk[..., :64], bv[..., 64:], bk[..., 64:]], axis=-1)
    packed = jnp.concatenate([aq, ak, av_sp, az, bq, bvk, bz, cu, cv, cz, dq, dk, dv, dz], axis=-1)
    assert packed.shape[-1] == P_PACK
    return packed.astype(BF16)


def kernel(x, c, ctx, c_ctx, ada_w, ada_b, norm_g, w_in, da_lq1, da_lk1, da_lq2, da_lk2,
           da_subln_g, gq_qnorm_g, gq_knorm_g, sg_ln_g, sg_ln_b, sg_w, sg_b,
           gla_w2_f, gla_b_f, gla_w2_b, gla_b_b, gla_norm_g, w_out, final_norm_g):
    b, seq, d = x.shape
    assert (seq, d, ctx.shape[1]) == (seq // ROW_TILE * ROW_TILE, D_MODEL, CTX_LEN)

    n_mod = 32
    cpad = jnp.zeros((n_mod, d), F32).at[:b].set(c).at[b].set(c_ctx)
    mod = _modulation(cpad, ada_w, ada_b)

    taba = _rope_tables(seq, DA_QK, 256)
    tabb = _rope_tables(seq, GQ_HD, 256)
    fg = final_norm_g.reshape(1, d)
    wp_all, wo_all, sgw_all = _pack_w_in(w_in), w_out.astype(BF16), sg_w.astype(BF16)

    nl = DEPTH
    modsel_all = jnp.concatenate([jnp.broadcast_to(mod[:, b].reshape(nl, 1, 1, 3, d), (nl, b, 1, 3, d)),
                                  mod[:, :b].reshape(nl, b, 1, 3, d)], axis=2)
    w2_all = (jnp.zeros((nl, LANES, 256), F32)
              .at[:, R_LANE:R_LANE + GLA_RANK, 0:128].set(gla_w2_f)
              .at[:, R_LANE + GLA_RANK:R_LANE + 2 * GLA_RANK, 128:256].set(gla_w2_b)).astype(BF16)
    b2_all = jnp.concatenate([gla_b_f, gla_b_b], axis=1).reshape(nl, 1, 256)
    sgb_all = jnp.repeat(jnp.swapaxes(sg_b, 1, 2), BRANCH_W // SG_GROUPS, axis=2)
    tiled = lambda g, n: jnp.tile(g, (1, n)).reshape(nl, 1, g.shape[1] * n)
    gqq_all, gqk_all = tiled(gq_qnorm_g, GQ_HEADS), tiled(gq_knorm_g, GQ_KV_HEADS)
    subln_all, glan_all = tiled(da_subln_g, DA_HEADS), tiled(gla_norm_g, GLA_HEADS)
    lam_all = jnp.stack([da_lq1, da_lk1, da_lq2, da_lk2], axis=1)

    stream_in = (ctx, x)
    for i in range(DEPTH):
        lam_init = 0.8 - 0.6 * math.exp(-0.3 * i)
        (xs, aqt, ak, avt, bqt, bk, bvt, gate, yc, dqk, dv, dg) = _layer(
            i, stream_in, modsel_all[i], norm_g[i].reshape(1, d), wp_all, taba, tabb,
            gqq_all[i], gqk_all[i], sg_ln_g[i].reshape(1, 256), sg_ln_b[i].reshape(1, 256),
            sgw_all, sgb_all[i], w2_all[i], b2_all[i])
        ya, yb = _attention(lam_init, aqt, ak, avt, bqt, bk, bvt, gate, lam_all[i],
                            subln_all[i].reshape(BRANCH_W, 1))
        yd = _gla(dqk, dv, dg, gate, glan_all[i])
        stream_in = (xs, modsel_all[i], ya, yb, yc, yd, wo_all)
    return _out_proj(*stream_in, fg)
```

```python
import functools
import math

import jax
import jax.numpy as jnp
import numpy as np
from jax import lax
from jax.experimental import pallas as pl
from jax.experimental.pallas import tpu as pltpu

F32 = jnp.float32
BF16 = jnp.bfloat16

D_MODEL = 1024
DEPTH = 4
CTX_LEN = 256
GRID_W = 64
BRANCH_W = 256
ROPE_THETA = 10000.0
EPS = 1e-6
DA_HEADS = 4
DA_QK = 32
DA_V = 64
GQ_HEADS = 4
GQ_KV_HEADS = 2
GQ_HD = 64
SG_GROUPS = 4
SG_CHUNK = 128
GLA_HEADS = 4
GLA_DV = 64
GLA_DK = 32
GLA_RANK = 16
GLA_NORMALIZER = 16.0
GLA_CHUNK = 32

LANES = 128
ROW_TILE = 256
KV_CHUNK = 512
LAYER_BATCH = 2
OUT_BATCH = 4
ATTN_BATCH = 2
GLA_TILE = 128
GLA_BATCH = 2
GLA_STEP_TILES = 3
VMEM_LIMIT = 56 * 1024 * 1024

C_AQ, C_AK, C_AV, C_AZ = 0, 256, 512, 1024
C_BQ, C_BVK, C_BZ = 1280, 1536, 1792
C_CU, C_CV, C_CZ = 2048, 2304, 2560
C_DQK, C_DV, C_DZ = 2816, 3072, 3328
P_PACK = 3584
R_LANE = 64
LOG2E = math.log2(math.e)


def _silu(x):
    return x * (1.0 / (1.0 + jnp.exp(-x)))


def _group_mean_matrix(width, group):
    r = lax.broadcasted_iota(jnp.int32, (width, width), 0) // group
    c = lax.broadcasted_iota(jnp.int32, (width, width), 1) // group
    return jnp.where(r == c, 1.0 / group, 0.0).astype(BF16)


def _split2(x):
    hi = x.astype(BF16)
    return hi, (x - hi.astype(F32)).astype(BF16)


def _group_mean(x, mat):
    return sum(jnp.dot(p, mat, preferred_element_type=F32) for p in _split2(x))


def _rope(x, cos, s_lo, s_hi, shift):
    outs = []
    for c in range(x.shape[1] // LANES):
        sl = slice(c * LANES, (c + 1) * LANES)
        xc = x[:, sl]
        up = pltpu.roll(xc, LANES - shift, axis=1)
        dn = pltpu.roll(xc, shift, axis=1)
        outs.append(xc * cos[:, sl] + up * s_lo[:, sl] + dn * s_hi[:, sl])
    return outs[0] if len(outs) == 1 else jnp.concatenate(outs, axis=1)


def _with_ones(v):
    lane = lax.broadcasted_iota(jnp.int32, v.shape, 1) % LANES
    return jnp.where(lane < LANES // 2, v, 1.0)


def _mod_kernel(c_ref, w_ref, b_ref, o_ref):
    s = _silu(c_ref[...]).astype(BF16)
    o_ref[0] = jnp.dot(s, w_ref[0].astype(BF16), preferred_element_type=F32) + b_ref[0]


def _modulation(cpad, ada_w, ada_b):
    n = cpad.shape[0]
    return pl.pallas_call(
        _mod_kernel,
        grid=(DEPTH, 3),
        in_specs=[
            pl.BlockSpec((n, D_MODEL), lambda i, j: (0, 0)),
            pl.BlockSpec((1, D_MODEL, D_MODEL), lambda i, j: (i, 0, j)),
            pl.BlockSpec((1, 1, D_MODEL), lambda i, j: (i, 0, j)),
        ],
        out_specs=pl.BlockSpec((1, n, D_MODEL), lambda i, j: (i, 0, j)),
        out_shape=jax.ShapeDtypeStruct((DEPTH, n, 3 * D_MODEL), F32),
        compiler_params=pltpu.CompilerParams(vmem_limit_bytes=VMEM_LIMIT),
        name="modulation",
    )(cpad, ada_w, ada_b.reshape(DEPTH, 1, 3 * D_MODEL))


def _layer_kernel(first, *refs):
    if first:
        ctx_ref, x_ref = refs[:2]
        refs = refs[2:]
    else:
        xs_ref, modp_ref, ya_ref, yb_ref, yc_in_ref, yd_ref, wo_ref = refs[:7]
        refs = refs[7:]
    (mod_ref, ng_ref, w_ref, taba_ref, tabb_ref, gqq_ref, gqk_ref, lng_ref, lnb_ref, sgw_ref,
     sgb_ref, w2_ref, b2_ref,
     xs_out_ref, aq_ref, ak_ref, av_ref, bq_ref, bk_ref, bv_ref, gate_ref, yc_ref,
     dqk_ref, dv_ref, dg_ref) = refs

    def residual_and_norm(bi):
        if first:
            x = jnp.where(pl.program_id(1) == 0, ctx_ref[bi], x_ref[bi])
        else:
            y_prev = jnp.concatenate([ya_ref[bi], yb_ref[bi], yc_in_ref[bi], yd_ref[bi]], axis=1)
            x = xs_ref[bi] + modp_ref[bi, 0, 2:3, :] * jnp.dot(y_prev, wo_ref[...],
                                                               preferred_element_type=F32)
        xs_out_ref[bi] = x
        shift = mod_ref[bi, 0, 0:1, :]
        scale = mod_ref[bi, 0, 1:2, :]
        y = x * lax.rsqrt(jnp.mean(x * x, axis=-1, keepdims=True) + EPS) * ng_ref[...]
        return (y * (1.0 + scale) + shift).astype(BF16)

    def project(bi, hb):
        def proj(lo, width):
            return jnp.dot(hb, w_ref[:, lo:lo + width], preferred_element_type=F32)

        m64 = _group_mean_matrix(BRANCH_W, GQ_HD)
        half_lane = lax.broadcasted_iota(jnp.int32, (ROW_TILE, LANES), 1) < LANES // 2
        p_bq = proj(C_BQ, 256)
        p_bvk = proj(C_BVK, 256)
        p_cv = proj(C_CV, 256)
        p_av = proj(C_AV, 512)

        bq_sq = _split2(p_bq * p_bq)
        bk = jnp.where(half_lane, pltpu.roll(p_bvk[:, :LANES], LANES // 2, axis=1), p_bvk[:, LANES:])
        bk_sq = _split2(bk * bk)
        mu = jnp.mean(p_cv, axis=-1, keepdims=True)
        cen = p_cv - mu
        var = jnp.mean(cen * cen, axis=-1, keepdims=True)
        vn = (cen * lax.rsqrt(var + EPS) * lng_ref[...] + lnb_ref[...]).astype(BF16)
        r = p_av[:, :LANES].astype(BF16)
        av_ref[bi, 0] = _with_ones(p_av).T.astype(BF16)
        bv_ref[bi, 0] = _with_ones(p_bvk).T.astype(BF16)

        ca, sa_lo, sa_hi = taba_ref[0], taba_ref[1], taba_ref[2]
        aq = _rope(proj(C_AQ, 256), ca, sa_lo, sa_hi, DA_QK // 4) * (DA_QK ** -0.5 * LOG2E)
        aq_ref[bi] = aq.T.astype(BF16)
        ak = _rope(proj(C_AK, 256), ca, sa_lo, sa_hi, DA_QK // 4)
        ak_ref[bi] = ak.astype(BF16)

        bq_ms = sum(jnp.dot(p, m64, preferred_element_type=F32) for p in bq_sq)
        bk_ms = sum(jnp.dot(p, m64[:LANES, :LANES], preferred_element_type=F32) for p in bk_sq)
        gl = jnp.dot(r, w2_ref[...], preferred_element_type=F32) + b2_ref[...]

        gate_ref[bi, :, 0:256] = _silu(proj(C_AZ, 256)).astype(BF16)
        gate_ref[bi, :, 256:512] = _silu(proj(C_BZ, 256)).astype(BF16)
        gate_ref[bi, :, 512:768] = _silu(proj(C_DZ, 256)).astype(BF16)

        lane_group = lax.broadcasted_iota(jnp.int32, (SG_CHUNK, BRANCH_W), 1) // (BRANCH_W // SG_GROUPS)
        mixed = []
        for n in range(ROW_TILE // SG_CHUNK):
            vchunk = vn[n * SG_CHUNK:(n + 1) * SG_CHUNK, :]
            acc = jnp.zeros((SG_CHUNK, BRANCH_W), F32)
            for g in range(SG_GROUPS):
                acc = jnp.where(lane_group == g,
                                jnp.dot(sgw_ref[g], vchunk, preferred_element_type=F32), acc)
            mixed.append(acc + sgb_ref[...])

        cb, sb_lo, sb_hi = tabb_ref[0], tabb_ref[1], tabb_ref[2]
        bq = p_bq * lax.rsqrt(bq_ms + EPS) * gqq_ref[...]
        bq_ref[bi] = (_rope(bq, cb, sb_lo, sb_hi, GQ_HD // 4) * (GQ_HD ** -0.5 * LOG2E)).T.astype(BF16)
        bk = bk * lax.rsqrt(bk_ms + EPS) * gqk_ref[...]
        bk = _rope(bk, cb[:, :LANES], sb_lo[:, :LANES], sb_hi[:, :LANES], GQ_HD // 4)
        bk_ref[bi] = bk.astype(BF16)

        yc = proj(C_CU, 256) * jnp.concatenate(mixed, axis=0) * _silu(proj(C_CZ, 256))
        yc_ref[bi] = yc.astype(BF16)

        q_scale = jnp.where(lax.broadcasted_iota(jnp.int32, (1, 256), 1) < 128, GLA_DK ** -0.5, 1.0)
        dqk_ref[bi] = proj(C_DQK, 256) * q_scale
        dv_ref[bi] = proj(C_DV, 256).astype(BF16)
        log_sig = jnp.minimum(gl, 0.0) - jnp.log(1.0 + jnp.exp(-jnp.abs(gl)))
        dg_ref[bi] = log_sig * (1.0 / GLA_NORMALIZER)

    nb = xs_out_ref.shape[0]
    hbs = [residual_and_norm(bi) for bi in range(nb)]
    for bi in range(nb):
        project(bi, hbs[bi])


def _layer(layer, stream_in, modsel, ng, wp, taba, tabb, gqq, gqk, lng, lnb, sgw, sgb, w2, b2):
    first = layer == 0
    t = ROW_TILE
    stacked = lambda a, idx: pl.BlockSpec((None,) + a.shape[1:], lambda i, j: (idx,) + (0,) * (a.ndim - 1))
    b = stream_in[0].shape[0]
    s = CTX_LEN + stream_in[1].shape[1] if first else stream_in[0].shape[1]
    nb = LAYER_BATCH
    row = lambda width: pl.BlockSpec((nb, t, width), lambda i, j: (i, j, 0))
    colT = lambda height: pl.BlockSpec((nb, height, t), lambda i, j: (i, 0, j))
    full = lambda a: pl.BlockSpec(a.shape, lambda i, j: (0,) * a.ndim)
    tab = pl.BlockSpec((3, t, 256), lambda i, j: (0, j, 0))
    mods = pl.BlockSpec((nb, 1, 3, D_MODEL), lambda i, j: (i, jnp.minimum(j, 1), 0, 0))
    shp = lambda width, dt: jax.ShapeDtypeStruct((b, s, width), dt)
    shpT = lambda height: jax.ShapeDtypeStruct((b, height, s), BF16)
    tileT = lambda height: pl.BlockSpec((nb, 1, height, t), lambda i, j: (i, j, 0, 0))
    if first:
        stream_specs = [pl.BlockSpec((nb, t, D_MODEL), lambda i, j: (i, 0, 0)),
                        pl.BlockSpec((nb, t, D_MODEL), lambda i, j: (i, jnp.maximum(j - 1, 0), 0))]
    else:
        stream_specs = [row(D_MODEL), mods, row(256), row(256), row(256), row(256),
                        stacked(stream_in[6], layer - 1)]
    return pl.pallas_call(
        functools.partial(_layer_kernel, first),
        grid=(b // nb, s // t),
        in_specs=stream_specs + [
            mods, full(ng), stacked(wp, layer), tab, tab, full(gqq), full(gqk), full(lng), full(lnb),
            stacked(sgw, layer), full(sgb), full(w2), full(b2),
        ],
        out_specs=[row(D_MODEL), colT(256), row(256), tileT(512), colT(256), row(128), tileT(256),
                   row(768), row(256), row(256), row(256), row(256)],
        out_shape=[shp(D_MODEL, F32),
                   shpT(256), shp(256, BF16), jax.ShapeDtypeStruct((b, s // t, 512, t), BF16),
                   shpT(256), shp(128, BF16), jax.ShapeDtypeStruct((b, s // t, 256, t), BF16),
                   shp(768, BF16), shp(256, BF16), shp(256, F32), shp(256, BF16), shp(256, F32)],
        compiler_params=pltpu.CompilerParams(
            dimension_semantics=("parallel", "arbitrary"), vmem_limit_bytes=VMEM_LIMIT),
        name="layer_first" if first else "layer",
    )(*stream_in, modsel, ng, wp, taba, tabb, gqq, gqk, lng, lnb, sgw, sgb, w2, b2)


def _max_over_rows(s):
    m = s
    for part in (256, 32):
        if m.shape[0] > part and m.shape[0] % part == 0:
            m = jnp.max(m.reshape(m.shape[0] // part, part, m.shape[1]), axis=0)
    return jnp.max(m, axis=0, keepdims=True)


def _online_softmax_pv(bi, latent, jobs, acc_ref, m_ref, s_ref):
    n_jobs = len(jobs)
    tiles_per_chunk = KV_CHUNK // ROW_TILE
    n_chunks = (jobs[0][1].shape[1] - CTX_LEN) // KV_CHUNK

    def key_loader():
        loaded = {}

        def rows(j, lo, hi):
            k_ref = jobs[j][1]
            if (id(k_ref), lo) not in loaded:
                loaded[(id(k_ref), lo)] = k_ref[bi, lo:hi, :]
            return loaded[(id(k_ref), lo)]

        return rows

    def qk(k_rows, j):
        return jnp.dot(k_rows, jobs[j][0], preferred_element_type=F32)

    def softmax_pv(s, vt, j, first):
        m_new = _max_over_rows(s)
        if not first:
            m_old = m_ref[j]
            m_new = jnp.maximum(m_old, m_new)
            alpha = jnp.exp2(m_old - m_new)
        m_ref[j] = m_new
        upd = jnp.dot(vt, jnp.exp2(s - m_new).astype(BF16), preferred_element_type=F32)
        acc_ref[j] = upd if first else acc_ref[j] * alpha + upd

    def vt_tile(tile, j):
        _, _, vt_ref, v_row = jobs[j]
        return vt_ref[bi, tile, v_row:v_row + LANES, :]

    def vt_chunk(c, j):
        t0 = CTX_LEN // ROW_TILE + c * tiles_per_chunk
        return jnp.concatenate([vt_tile(t0 + i, j) for i in range(tiles_per_chunk)], axis=1)

    def chunk_rows(c):
        return CTX_LEN + c * KV_CHUNK, CTX_LEN + (c + 1) * KV_CHUNK

    @pl.when(jnp.logical_not(latent))
    def _():
        keys = key_loader()
        s = [qk(keys(j, 0, CTX_LEN), j) for j in range(n_jobs)]
        for j in range(n_jobs):
            softmax_pv(s[j], vt_tile(0, j), j, True)

    @pl.when(latent)
    def _():
        keys = key_loader()
        s = [qk(keys(j, 0, CTX_LEN), j) for j in range(n_jobs)]
        s_ref[0, 0] = qk(keys(0, *chunk_rows(0)), 0)
        for j in range(n_jobs):
            softmax_pv(s[j], vt_tile(0, j), j, True)
            if j + 1 < n_jobs:
                s_ref[0, j + 1] = qk(keys(j + 1, *chunk_rows(0)), j + 1)

        for c in range(n_chunks):
            slot = c % 2
            more = c + 1 < n_chunks
            if more:
                s_ref[1 - slot, 0] = qk(keys(0, *chunk_rows(c + 1)), 0)
            for j in range(n_jobs):
                softmax_pv(s_ref[slot, j], vt_chunk(c, j), j, False)
                if more and j + 1 < n_jobs:
                    s_ref[1 - slot, j + 1] = qk(keys(j + 1, *chunk_rows(c + 1)), j + 1)

    outs = []
    for j in range(n_jobs):
        acc = acc_ref[j]
        outs.append(acc[:LANES // 2, :] * (1.0 / acc[LANES // 2:LANES // 2 + 1, :]))
    return outs


def _rows_at(x, row0, total):
    parts = []
    if row0:
        parts.append(jnp.zeros((row0, x.shape[1]), x.dtype))
    parts.append(x)
    if total - row0 - x.shape[0]:
        parts.append(jnp.zeros((total - row0 - x.shape[0], x.shape[1]), x.dtype))
    return jnp.concatenate(parts, axis=0)


ATTN_JOBS = 2 * DA_HEADS + GQ_HEADS


def _attn_kernel(lam_init, aqt_ref, ak_ref, avt_ref, bqt_ref, bk_ref, bvt_ref, gate_ref, lam_ref,
                 sg_ref, ya_ref, yb_ref, acc_ref, m_ref, s_ref):
    lq1, lk1, lq2, lk2 = lam_ref[0:1, :], lam_ref[1:2, :], lam_ref[2:3, :], lam_ref[3:4, :]
    lam = (jnp.exp(jnp.sum(lq1 * lk1, axis=-1, keepdims=True))
           - jnp.exp(jnp.sum(lq2 * lk2, axis=-1, keepdims=True)) + lam_init)
    grp = GQ_HEADS // GQ_KV_HEADS
    for bi in range(aqt_ref.shape[0]):
        jobs = [(_rows_at(aqt_ref[bi, job * DA_QK:(job + 1) * DA_QK, :], job * DA_QK, BRANCH_W),
                 ak_ref, avt_ref, (job // 2) * LANES) for job in range(2 * DA_HEADS)]
        jobs += [(_rows_at(bqt_ref[bi, h * GQ_HD:(h + 1) * GQ_HD, :], (h // grp) * GQ_HD,
                           GQ_KV_HEADS * GQ_HD),
                  bk_ref, bvt_ref, (h // grp) * LANES) for h in range(GQ_HEADS)]
        o = _online_softmax_pv(bi, pl.program_id(1) > 0, jobs, acc_ref, m_ref, s_ref)

        heads = []
        for h in range(DA_HEADS):
            d = o[2 * h] - lam * o[2 * h + 1]
            heads.append(d * lax.rsqrt(jnp.mean(d * d, axis=0, keepdims=True) + EPS))
        yt = jnp.concatenate(heads, axis=0) * (sg_ref[...] * (1.0 - lam_init))
        ya_ref[bi] = (yt.T * gate_ref[bi, :, 0:BRANCH_W].astype(F32)).astype(BF16)
        yb = jnp.concatenate(o[2 * DA_HEADS:], axis=0).T
        yb_ref[bi] = (yb * gate_ref[bi, :, BRANCH_W:2 * BRANCH_W].astype(F32)).astype(BF16)


def _attention(lam_init, aqt, ak, avt, bqt, bk, bvt, gate, lam_vecs, subln_col):
    b, s, _ = ak.shape
    t = ROW_TILE
    full = lambda a: pl.BlockSpec(a.shape, lambda i, j: (0,) * a.ndim)
    nb = ATTN_BATCH
    qt_spec = pl.BlockSpec((nb, BRANCH_W, t), lambda i, j: (i, 0, j))
    keys = lambda a: pl.BlockSpec((nb, s, a.shape[2]), lambda i, j: (i, 0, 0))
    vals = lambda a: pl.BlockSpec((nb,) + a.shape[1:], lambda i, j: (i, 0, 0, 0))
    y_spec = pl.BlockSpec((nb, t, BRANCH_W), lambda i, j: (i, j, 0))
    y_shape = jax.ShapeDtypeStruct((b, s, BRANCH_W), BF16)
    return pl.pallas_call(
        functools.partial(_attn_kernel, lam_init),
        grid=(b // nb, s // t),
        in_specs=[qt_spec, keys(ak), vals(avt), qt_spec, keys(bk), vals(bvt),
                  pl.BlockSpec((nb, t, gate.shape[2]), lambda i, j: (i, j, 0)),
                  full(lam_vecs), full(subln_col)],
        out_specs=[y_spec, y_spec],
        out_shape=[y_shape, y_shape],
        scratch_shapes=[pltpu.VMEM((ATTN_JOBS, LANES, t), F32), pltpu.VMEM((ATTN_JOBS, 1, t), F32),
                        pltpu.VMEM((2, ATTN_JOBS, KV_CHUNK, t), F32)],
        compiler_params=pltpu.CompilerParams(
            dimension_semantics=("parallel", "arbitrary"), vmem_limit_bytes=VMEM_LIMIT),
        name="attention",
    )(aqt, ak, avt, bqt, bk, bvt, gate, lam_vecs, subln_col)


def _chunk_cumsum(g, row_in_chunk, reverse):
    x = g
    sh = 1
    while sh < GLA_CHUNK:
        if reverse:
            nb = pltpu.roll(x, x.shape[0] - sh, axis=0)
            x = x + jnp.where(row_in_chunk < GLA_CHUNK - sh, nb, 0.0)
        else:
            nb = pltpu.roll(x, sh, axis=0)
            x = x + jnp.where(row_in_chunk >= sh, nb, 0.0)
        sh *= 2
    return x


def _gla_kernel(qk_ref, v_ref, g_ref, gate_ref, ng_ref, o_ref, acc_ref, st_ref):
    n_tiles = qk_ref.shape[1] // GLA_TILE
    ctx_tiles = CTX_LEN // GLA_TILE
    per = GLA_TILE // GLA_CHUNK
    ri = lax.broadcasted_iota(jnp.int32, (GLA_TILE, GLA_TILE), 0)
    ci = lax.broadcasted_iota(jnp.int32, (GLA_TILE, GLA_TILE), 1)
    same_chunk = (ri // GLA_CHUNK) == (ci // GLA_CHUNK)
    chunk_of_row = ri // GLA_CHUNK
    row_in_chunk = ri % GLA_CHUNK
    head_of_k = lax.broadcasted_iota(jnp.int32, (GLA_TILE, GLA_HEADS * GLA_DK), 1) // GLA_DK
    head_of_v = lax.broadcasted_iota(jnp.int32, (GLA_TILE, BRANCH_W), 1) // GLA_DV
    st_mask = ((lax.broadcasted_iota(jnp.int32, (BRANCH_W, GLA_HEADS * GLA_DK), 0) // GLA_DV)
               == (lax.broadcasted_iota(jnp.int32, (BRANCH_W, GLA_HEADS * GLA_DK), 1) // GLA_DK))
    m64 = _group_mean_matrix(BRANCH_W, GLA_DV)
    tris = [jnp.where(same_chunk & (ci <= ri), 1.0, 0.0), jnp.where(same_chunk & (ci >= ri), 1.0, 0.0)]
    tris4 = [jnp.concatenate([t] * GLA_HEADS, axis=1) for t in tris]

    nt_dims = (((1,), (1,)), ((), ()))
    tn_dims = (((0,), (0,)), ((), ()))

    def scan_step(i, carry):
        def tile_index(d, sub):
            t = i * GLA_STEP_TILES + sub
            if not d:
                return t
            return jnp.where(t < ctx_tiles, ctx_tiles - 1 - t, n_tiles + ctx_tiles - 1 - t)

        chains = [(bi, d, sub) for bi in range(GLA_BATCH) for d in (0, 1) for sub in range(GLA_STEP_TILES)]
        nc = len(chains)
        rows = [pl.ds(pl.multiple_of(tile_index(d, sub) * GLA_TILE, GLA_TILE), GLA_TILE)
                for _, d, sub in chains]
        ends = [[c * GLA_CHUNK if d else (c + 1) * GLA_CHUNK - 1 for c in range(per)] for _, d, _ in chains]

        cum = []
        for n, (bi, d, _) in enumerate(chains):
            cum.append(_chunk_cumsum(g_ref[bi, rows[n], 128 * d:128 * d + 128], row_in_chunk, bool(d)))

        qe, kd, kl, vb = [], [], [], []
        for n, (bi, d, _) in enumerate(chains):
            q = qk_ref[bi, rows[n], 0:128]
            k = qk_ref[bi, rows[n], 128:256]
            cum_last = jnp.concatenate(
                [jnp.broadcast_to(cum[n][e:e + 1, :], (GLA_CHUNK, 128)) for e in ends[n]], axis=0)
            qe.append(q * jnp.exp(cum[n]))
            kd.append(k * jnp.exp(-cum[n]))
            kl.append(k * jnp.exp(cum_last - cum[n]))
            vb.append(v_ref[bi, rows[n], :])

        qeb = [x.astype(BF16) for x in qe]
        att = []
        for n, (bi, d, _) in enumerate(chains):
            kd_heads = jnp.concatenate(
                [jnp.where(head_of_k == h, kd[n], 0.0).astype(BF16) for h in range(GLA_HEADS)], axis=0)
            a = lax.dot_general(qeb[n], kd_heads, nt_dims, preferred_element_type=F32)
            att.append((a * tris4[d]).astype(BF16))
        o = []
        for n in range(nc):
            v_heads = jnp.concatenate(
                [jnp.where(head_of_v == h, vb[n], jnp.zeros_like(vb[n])) for h in range(GLA_HEADS)], axis=0)
            o.append(jnp.dot(att[n], v_heads, preferred_element_type=F32))

        upd = []
        for n in range(nc):
            kl_chunks = jnp.concatenate(
                [jnp.where(chunk_of_row == c, kl[n], 0.0).astype(BF16) for c in range(per)], axis=1)
            upd.append(lax.dot_general(vb[n], kl_chunks, tn_dims, preferred_element_type=F32))
        seen = [[None] * per for _ in range(nc)]
        for slot in range(2 * GLA_BATCH):
            st = st_ref[slot]
            for n in range(slot * GLA_STEP_TILES, (slot + 1) * GLA_STEP_TILES):
                for c in (range(per - 1, -1, -1) if slot % 2 else range(per)):
                    seen[n][c] = st.astype(BF16)
                    decay = jnp.exp(cum[n][ends[n][c]:ends[n][c] + 1, :])
                    st = st * decay + jnp.where(st_mask, upd[n][:, c * 128:(c + 1) * 128], 0.0)
            st_ref[slot] = st
        for n in range(nc):
            qe_chunks = jnp.concatenate(
                [jnp.where(chunk_of_row == c, qe[n], 0.0).astype(BF16) for c in range(per)], axis=1)
            o_inter = lax.dot_general(qe_chunks, jnp.concatenate(seen[n], axis=1), nt_dims,
                                      preferred_element_type=F32)
            acc_ref[n // GLA_STEP_TILES, rows[n], :] = o[n] + o_inter
        return carry

    st_ref[...] = jnp.zeros_like(st_ref)
    lax.fori_loop(0, n_tiles // GLA_STEP_TILES, scan_step, 0)

    def finish_step(i, carry):
        rows = pl.ds(pl.multiple_of(i * ROW_TILE, ROW_TILE), ROW_TILE)
        tot = [acc_ref[2 * bi, rows, :] + acc_ref[2 * bi + 1, rows, :] for bi in range(GLA_BATCH)]
        sq = [_split2(t * t) for t in tot]
        ms = [sum(jnp.dot(p, m64, preferred_element_type=F32) for p in s) for s in sq]
        for bi in range(GLA_BATCH):
            y = tot[bi] * lax.rsqrt(ms[bi] + EPS) * ng_ref[...]
            o_ref[bi, rows, :] = (y * gate_ref[bi, rows, :].astype(F32)).astype(BF16)
        return carry

    lax.fori_loop(0, qk_ref.shape[1] // ROW_TILE, finish_step, 0)


def _gla(dqk, dv, dg, gate, ng):
    b, s, _ = dqk.shape
    nb = GLA_BATCH
    blk = lambda width, cb: pl.BlockSpec((nb, s, width), lambda i: (i, 0, cb))
    return pl.pallas_call(
        _gla_kernel,
        grid=(b // nb,),
        in_specs=[blk(256, 0), blk(256, 0), blk(256, 0), blk(256, 2),
                  pl.BlockSpec(ng.shape, lambda i: (0, 0))],
        out_specs=blk(256, 0),
        out_shape=jax.ShapeDtypeStruct((b, s, 256), BF16),
        scratch_shapes=[pltpu.VMEM((2 * nb, s, BRANCH_W), F32),
                        pltpu.VMEM((2 * nb, BRANCH_W, GLA_HEADS * GLA_DK), F32)],
        compiler_params=pltpu.CompilerParams(
            dimension_semantics=("parallel",), vmem_limit_bytes=VMEM_LIMIT),
        name="gla",
    )(dqk, dv, dg, gate, ng)


def _out_kernel(x_ref, mod_ref, ya_ref, yb_ref, yc_ref, yd_ref, w_ref, fg_ref, o_ref):
    for bi in range(x_ref.shape[0]):
        y = jnp.concatenate([ya_ref[bi], yb_ref[bi], yc_ref[bi], yd_ref[bi]], axis=1)
        upd = jnp.dot(y, w_ref[...], preferred_element_type=F32)
        xn = x_ref[bi] + mod_ref[bi, 0, 2:3, :] * upd
        o_ref[bi] = xn * lax.rsqrt(jnp.mean(xn * xn, axis=-1, keepdims=True) + EPS) * fg_ref[...]


def _out_proj(xs, modsel, ya, yb, yc, yd, wo, fg):
    b, s, _ = xs.shape
    t = ROW_TILE
    nb = OUT_BATCH
    skip = CTX_LEN // t
    row = lambda width: pl.BlockSpec((nb, t, width), lambda i, j: (i, j + skip, 0))
    return pl.pallas_call(
        _out_kernel,
        grid=(b // nb, s // t - skip),
        in_specs=[
            row(D_MODEL),
            pl.BlockSpec((nb, 1, 3, D_MODEL), lambda i, j: (i, 1, 0, 0)),
            row(256), row(256), row(256), row(256),
            pl.BlockSpec((None,) + wo.shape[1:], lambda i, j: (wo.shape[0] - 1, 0, 0)),
            pl.BlockSpec(fg.shape, lambda i, j: (0, 0)),
        ],
        out_specs=pl.BlockSpec((nb, t, D_MODEL), lambda i, j: (i, j, 0)),
        out_shape=jax.ShapeDtypeStruct((b, s - skip * t, D_MODEL), F32),
        compiler_params=pltpu.CompilerParams(
            dimension_semantics=("parallel", "parallel"), vmem_limit_bytes=VMEM_LIMIT),
        name="out_proj_final",
    )(xs, modsel, ya, yb, yc, yd, wo, fg)


def _rope_tables(seq, dim, width):
    half = dim // 2
    quarter = half // 2
    lane = np.arange(width) % dim
    freq = ROPE_THETA ** (-(2.0 * (lane % quarter)) / half)
    pos_t = np.arange(seq)
    pos = np.where(lane[None, :] < half, (pos_t // GRID_W)[:, None], (pos_t % GRID_W)[:, None])
    ang = pos * freq[None, :]
    cos, sin = np.cos(ang), np.sin(ang)
    first = (lane % half) < quarter
    s_lo = np.where(first[None, :], -sin, 0.0)
    s_hi = np.where(first[None, :], 0.0, sin)
    lat = np.stack([cos, s_lo, s_hi])
    ctx = np.stack([np.ones((CTX_LEN, width)), np.zeros((CTX_LEN, width)), np.zeros((CTX_LEN, width))])
    return jnp.asarray(np.concatenate([ctx, lat], axis=1), dtype=F32)


def _pack_w_in(w):
    offs = [0]
    for n in (256, 256, 256, 256, 256, 128, 128, 256, 256, 256, 256, 128, 128, 256, 256, 16, 16):
        offs.append(offs[-1] + n)
    seg = [w[..., offs[i]:offs[i + 1]] for i in range(17)]
    aq, ak, av, az, bq, bk, bv, bz, cu, cv, cz, dq, dk, dv, dz, drf, drb = seg
    gap = jnp.zeros(w.shape[:-1] + (LANES // 2,), w.dtype)
    r_gap = jnp.concatenate([drf, drb, gap[..., :LANES // 2 - 2 * GLA_RANK]], axis=-1)
    av_sp = jnp.concatenate([av[..., 0:64], r_gap, av[..., 64:128], gap, av[..., 128:192], gap,
                             av[..., 192:256], gap], axis=-1)
    bvk = jnp.concatenate([bv[..., :64], bk[..., :64], bv[..., 64:], bk[..., 64:]], axis=-1)
    packed = jnp.concatenate([aq, ak, av_sp, az, bq, bvk, bz, cu, cv, cz, dq, dk, dv, dz], axis=-1)
    assert packed.shape[-1] == P_PACK
    return packed.astype(BF16)


def kernel(x, c, ctx, c_ctx, ada_w, ada_b, norm_g, w_in, da_lq1, da_lk1, da_lq2, da_lk2,
           da_subln_g, gq_qnorm_g, gq_knorm_g, sg_ln_g, sg_ln_b, sg_w, sg_b,
           gla_w2_f, gla_b_f, gla_w2_b, gla_b_b, gla_norm_g, w_out, final_norm_g):
    b, seq, d = x.shape
    assert (seq, d, ctx.shape[1]) == (seq // ROW_TILE * ROW_TILE, D_MODEL, CTX_LEN)

    n_mod = 32
    cpad = jnp.zeros((n_mod, d), F32).at[:b].set(c).at[b].set(c_ctx)
    mod = _modulation(cpad, ada_w, ada_b)

    taba = _rope_tables(seq, DA_QK, 256)
    tabb = _rope_tables(seq, GQ_HD, 256)
    fg = final_norm_g.reshape(1, d)
    wp_all, wo_all, sgw_all = _pack_w_in(w_in), w_out.astype(BF16), sg_w.astype(BF16)

    nl = DEPTH
    modsel_all = jnp.concatenate([jnp.broadcast_to(mod[:, b].reshape(nl, 1, 1, 3, d), (nl, b, 1, 3, d)),
                                  mod[:, :b].reshape(nl, b, 1, 3, d)], axis=2)
    w2_all = (jnp.zeros((nl, LANES, 256), F32)
              .at[:, R_LANE:R_LANE + GLA_RANK, 0:128].set(gla_w2_f)
              .at[:, R_LANE + GLA_RANK:R_LANE + 2 * GLA_RANK, 128:256].set(gla_w2_b)).astype(BF16)
    b2_all = jnp.concatenate([gla_b_f, gla_b_b], axis=1).reshape(nl, 1, 256)
    sgb_all = jnp.repeat(jnp.swapaxes(sg_b, 1, 2), BRANCH_W // SG_GROUPS, axis=2)
    tiled = lambda g, n: jnp.tile(g, (1, n)).reshape(nl, 1, g.shape[1] * n)
    gqq_all, gqk_all = tiled(gq_qnorm_g, GQ_HEADS), tiled(gq_knorm_g, GQ_KV_HEADS)
    subln_all, glan_all = tiled(da_subln_g, DA_HEADS), tiled(gla_norm_g, GLA_HEADS)
    lam_all = jnp.stack([da_lq1, da_lk1, da_lq2, da_lk2], axis=1)

    stream_in = (ctx, x)
    for i in range(DEPTH):
        lam_init = 0.8 - 0.6 * math.exp(-0.3 * i)
        (xs, aqt, ak, avt, bqt, bk, bvt, gate, yc, dqk, dv, dg) = _layer(
            i, stream_in, modsel_all[i], norm_g[i].reshape(1, d), wp_all, taba, tabb,
            gqq_all[i], gqk_all[i], sg_ln_g[i].reshape(1, 256), sg_ln_b[i].reshape(1, 256),
            sgw_all, sgb_all[i], w2_all[i], b2_all[i])
        ya, yb = _attention(lam_init, aqt, ak, avt, bqt, bk, bvt, gate, lam_all[i],
                            subln_all[i].reshape(BRANCH_W, 1))
        yd = _gla(dqk, dv, dg, gate, glan_all[i])
        stream_in = (xs, modsel_all[i], ya, yb, yc, yd, wo_all)
    return _out_proj(*stream_in, fg)
```

```python
import functools
import math

import jax
import jax.numpy as jnp
import numpy as np
from jax import lax
from jax.experimental import pallas as pl
from jax.experimental.pallas import tpu as pltpu

F32 = jnp.float32
BF16 = jnp.bfloat16

D_MODEL = 1024
DEPTH = 4
CTX_LEN = 256
GRID_W = 64
BRANCH_W = 256
ROPE_THETA = 10000.0
EPS = 1e-6
DA_HEADS = 4
DA_QK = 32
DA_V = 64
GQ_HEADS = 4
GQ_KV_HEADS = 2
GQ_HD = 64
SG_GROUPS = 4
SG_CHUNK = 128
GLA_HEADS = 4
GLA_DV = 64
GLA_DK = 32
GLA_RANK = 16
GLA_NORMALIZER = 16.0
GLA_CHUNK = 32

LANES = 128
ROW_TILE = 256
KV_CHUNK = 512
LAYER_BATCH = 2
OUT_BATCH = 4
ATTN_BATCH = 2
GLA_TILE = 128
GLA_BATCH = 2
GLA_STEP_TILES = 3
VMEM_LIMIT = 56 * 1024 * 1024

C_AQ, C_AK, C_AV, C_AZ = 0, 256, 512, 1024
C_BQ, C_BVK, C_BZ = 1280, 1536, 1792
C_CU, C_CV, C_CZ = 2048, 2304, 2560
C_DQK, C_DV, C_DZ = 2816, 3072, 3328
P_PACK = 3584
R_LANE = 64
LOG2E = math.log2(math.e)


def _silu(x):
    return x * (1.0 / (1.0 + jnp.exp(-x)))


def _group_mean_matrix(width, group):
    r = lax.broadcasted_iota(jnp.int32, (width, width), 0) // group
    c = lax.broadcasted_iota(jnp.int32, (width, width), 1) // group
    return jnp.where(r == c, 1.0 / group, 0.0).astype(BF16)


def _split2(x):
    hi = x.astype(BF16)
    return hi, (x - hi.astype(F32)).astype(BF16)


def _group_mean(x, mat):
    return sum(jnp.dot(p, mat, preferred_element_type=F32) for p in _split2(x))


def _rope(x, cos, s_lo, s_hi, shift):
    outs = []
    for c in range(x.shape[1] // LANES):
        sl = slice(c * LANES, (c + 1) * LANES)
        xc = x[:, sl]
        up = pltpu.roll(xc, LANES - shift, axis=1)
        dn = pltpu.roll(xc, shift, axis=1)
        outs.append(xc * cos[:, sl] + up * s_lo[:, sl] + dn * s_hi[:, sl])
    return outs[0] if len(outs) == 1 else jnp.concatenate(outs, axis=1)


def _with_ones(v):
    lane = lax.broadcasted_iota(jnp.int32, v.shape, 1) % LANES
    return jnp.where(lane < LANES // 2, v, 1.0)


def _mod_kernel(c_ref, w_ref, b_ref, o_ref):
    s = _silu(c_ref[...]).astype(BF16)
    o_ref[0] = jnp.dot(s, w_ref[0].astype(BF16), preferred_element_type=F32) + b_ref[0]


def _modulation(cpad, ada_w, ada_b):
    n = cpad.shape[0]
    return pl.pallas_call(
        _mod_kernel,
        grid=(DEPTH, 3),
        in_specs=[
            pl.BlockSpec((n, D_MODEL), lambda i, j: (0, 0)),
            pl.BlockSpec((1, D_MODEL, D_MODEL), lambda i, j: (i, 0, j)),
            pl.BlockSpec((1, 1, D_MODEL), lambda i, j: (i, 0, j)),
        ],
        out_specs=pl.BlockSpec((1, n, D_MODEL), lambda i, j: (i, 0, j)),
        out_shape=jax.ShapeDtypeStruct((DEPTH, n, 3 * D_MODEL), F32),
        compiler_params=pltpu.CompilerParams(vmem_limit_bytes=VMEM_LIMIT),
        name="modulation",
    )(cpad, ada_w, ada_b.reshape(DEPTH, 1, 3 * D_MODEL))


def _layer_kernel(first, *refs):
    if first:
        ctx_ref, x_ref = refs[:2]
        refs = refs[2:]
    else:
        xs_ref, modp_ref, ya_ref, yb_ref, yc_in_ref, yd_ref, wo_ref = refs[:7]
        refs = refs[7:]
    (mod_ref, ng_ref, w_ref, taba_ref, tabb_ref, gqq_ref, gqk_ref, lng_ref, lnb_ref, sgw_ref,
     sgb_ref, w2_ref, b2_ref,
     xs_out_ref, aq_ref, ak_ref, av_ref, bq_ref, bk_ref, bv_ref, gate_ref, gated_ref, yc_ref,
     dqk_ref, dv_ref, dg_ref) = refs

    def residual_and_norm(bi):
        if first:
            x = jnp.where(pl.program_id(1) == 0, ctx_ref[bi], x_ref[bi])
        else:
            y_prev = jnp.concatenate([ya_ref[bi], yb_ref[bi], yc_in_ref[bi], yd_ref[bi]], axis=1)
            x = xs_ref[bi] + modp_ref[bi, 0, 2:3, :] * jnp.dot(y_prev, wo_ref[...],
                                                               preferred_element_type=F32)
        xs_out_ref[bi] = x
        shift = mod_ref[bi, 0, 0:1, :]
        scale = mod_ref[bi, 0, 1:2, :]
        y = x * lax.rsqrt(jnp.mean(x * x, axis=-1, keepdims=True) + EPS) * ng_ref[...]
        return (y * (1.0 + scale) + shift).astype(BF16)

    def project(bi, hb):
        def proj(lo, width):
            return jnp.dot(hb, w_ref[:, lo:lo + width], preferred_element_type=F32)

        m64 = _group_mean_matrix(BRANCH_W, GQ_HD)
        half_lane = lax.broadcasted_iota(jnp.int32, (ROW_TILE, LANES), 1) < LANES // 2
        p_bq = proj(C_BQ, 256)
        p_bvk = proj(C_BVK, 256)
        p_cv = proj(C_CV, 256)
        p_av = proj(C_AV, 512)

        bq_sq = _split2(p_bq * p_bq)
        bk = jnp.where(half_lane, pltpu.roll(p_bvk[:, :LANES], LANES // 2, axis=1), p_bvk[:, LANES:])
        bk_sq = _split2(bk * bk)
        mu = jnp.mean(p_cv, axis=-1, keepdims=True)
        cen = p_cv - mu
        var = jnp.mean(cen * cen, axis=-1, keepdims=True)
        vn = (cen * lax.rsqrt(var + EPS) * lng_ref[...] + lnb_ref[...]).astype(BF16)
        r = p_av[:, :LANES].astype(BF16)
        av_ref[bi, 0] = _with_ones(p_av).T.astype(BF16)
        bv_ref[bi, 0] = _with_ones(p_bvk).T.astype(BF16)

        ca, sa_lo, sa_hi = taba_ref[0], taba_ref[1], taba_ref[2]
        aq = _rope(proj(C_AQ, 256), ca, sa_lo, sa_hi, DA_QK // 4) * (DA_QK ** -0.5 * LOG2E)
        aq_ref[bi] = aq.T.astype(BF16)
        ak = _rope(proj(C_AK, 256), ca, sa_lo, sa_hi, DA_QK // 4)
        ak_ref[bi] = ak.astype(BF16)

        bq_ms = sum(jnp.dot(p, m64, preferred_element_type=F32) for p in bq_sq)
        bk_ms = sum(jnp.dot(p, m64[:LANES, :LANES], preferred_element_type=F32) for p in bk_sq)
        gl = jnp.dot(r, w2_ref[...], preferred_element_type=F32) + b2_ref[...]

        gate_ref[bi, :, 0:256] = _silu(proj(C_AZ, 256)).astype(BF16)
        gate_ref[bi, :, 256:512] = _silu(proj(C_BZ, 256)).astype(BF16)
        gated_ref[bi] = _silu(proj(C_DZ, 256)).astype(BF16)

        lane_group = lax.broadcasted_iota(jnp.int32, (SG_CHUNK, BRANCH_W), 1) // (BRANCH_W // SG_GROUPS)
        mixed = []
        for n in range(ROW_TILE // SG_CHUNK):
            vchunk = vn[n * SG_CHUNK:(n + 1) * SG_CHUNK, :]
            acc = jnp.zeros((SG_CHUNK, BRANCH_W), F32)
            for g in range(SG_GROUPS):
                acc = jnp.where(lane_group == g,
                                jnp.dot(sgw_ref[g], vchunk, preferred_element_type=F32), acc)
            mixed.append(acc + sgb_ref[...])

        cb, sb_lo, sb_hi = tabb_ref[0], tabb_ref[1], tabb_ref[2]
        bq = p_bq * lax.rsqrt(bq_ms + EPS) * gqq_ref[...]
        bq_ref[bi] = (_rope(bq, cb, sb_lo, sb_hi, GQ_HD // 4) * (GQ_HD ** -0.5 * LOG2E)).T.astype(BF16)
        bk = bk * lax.rsqrt(bk_ms + EPS) * gqk_ref[...]
        bk = _rope(bk, cb[:, :LANES], sb_lo[:, :LANES], sb_hi[:, :LANES], GQ_HD // 4)
        bk_ref[bi] = bk.astype(BF16)

        yc = proj(C_CU, 256) * jnp.concatenate(mixed, axis=0) * _silu(proj(C_CZ, 256))
        yc_ref[bi] = yc.astype(BF16)

        q_scale = jnp.where(lax.broadcasted_iota(jnp.int32, (1, 256), 1) < 128, GLA_DK ** -0.5, 1.0)
        dqk_ref[bi] = proj(C_DQK, 256) * q_scale
        dv_ref[bi] = proj(C_DV, 256).astype(BF16)
        log_sig = jnp.minimum(gl, 0.0) - jnp.log(1.0 + jnp.exp(-jnp.abs(gl)))
        dg_ref[bi] = log_sig * (1.0 / GLA_NORMALIZER)

    nb = xs_out_ref.shape[0]
    hbs = [residual_and_norm(bi) for bi in range(nb)]
    for bi in range(nb):
        project(bi, hbs[bi])


def _layer(layer, stream_in, modsel, ng, wp, taba, tabb, gqq, gqk, lng, lnb, sgw, sgb, w2, b2):
    first = layer == 0
    t = ROW_TILE
    stacked = lambda a, idx: pl.BlockSpec((None,) + a.shape[1:], lambda i, j: (idx,) + (0,) * (a.ndim - 1))
    b = stream_in[0].shape[0]
    s = CTX_LEN + stream_in[1].shape[1] if first else stream_in[0].shape[1]
    nb = LAYER_BATCH
    row = lambda width: pl.BlockSpec((nb, t, width), lambda i, j: (i, j, 0))
    colT = lambda height: pl.BlockSpec((nb, height, t), lambda i, j: (i, 0, j))
    full = lambda a: pl.BlockSpec(a.shape, lambda i, j: (0,) * a.ndim)
    tab = pl.BlockSpec((3, t, 256), lambda i, j: (0, j, 0))
    mods = pl.BlockSpec((nb, 1, 3, D_MODEL), lambda i, j: (i, jnp.minimum(j, 1), 0, 0))
    shp = lambda width, dt: jax.ShapeDtypeStruct((b, s, width), dt)
    shpT = lambda height: jax.ShapeDtypeStruct((b, height, s), BF16)
    tileT = lambda height: pl.BlockSpec((nb, 1, height, t), lambda i, j: (i, j, 0, 0))
    if first:
        stream_specs = [pl.BlockSpec((nb, t, D_MODEL), lambda i, j: (i, 0, 0)),
                        pl.BlockSpec((nb, t, D_MODEL), lambda i, j: (i, jnp.maximum(j - 1, 0), 0))]
    else:
        stream_specs = [row(D_MODEL), mods, row(256), row(256), row(256), row(256),
                        stacked(stream_in[6], layer - 1)]
    return pl.pallas_call(
        functools.partial(_layer_kernel, first),
        grid=(b // nb, s // t),
        in_specs=stream_specs + [
            mods, full(ng), stacked(wp, layer), tab, tab, full(gqq), full(gqk), full(lng), full(lnb),
            stacked(sgw, layer), full(sgb), full(w2), full(b2),
        ],
        out_specs=[row(D_MODEL), colT(256), row(256), tileT(512), colT(256), row(128), tileT(256),
                   row(512), row(256), row(256), row(256), row(256), row(256)],
        out_shape=[shp(D_MODEL, F32),
                   shpT(256), shp(256, BF16), jax.ShapeDtypeStruct((b, s // t, 512, t), BF16),
                   shpT(256), shp(128, BF16), jax.ShapeDtypeStruct((b, s // t, 256, t), BF16),
                   shp(512, BF16), shp(256, BF16), shp(256, BF16), shp(256, F32), shp(256, BF16),
                   shp(256, F32)],
        compiler_params=pltpu.CompilerParams(
            dimension_semantics=("parallel", "arbitrary"), vmem_limit_bytes=VMEM_LIMIT),
        name="layer_first" if first else "layer",
    )(*stream_in, modsel, ng, wp, taba, tabb, gqq, gqk, lng, lnb, sgw, sgb, w2, b2)


def _max_over_rows(s):
    m = s
    for part in (256, 32):
        if m.shape[0] > part and m.shape[0] % part == 0:
            m = jnp.max(m.reshape(m.shape[0] // part, part, m.shape[1]), axis=0)
    return jnp.max(m, axis=0, keepdims=True)


def _online_softmax_pv(bi, latent, jobs, acc_ref, m_ref, s_ref):
    n_jobs = len(jobs)
    tiles_per_chunk = KV_CHUNK // ROW_TILE
    n_chunks = (jobs[0][1].shape[1] - CTX_LEN) // KV_CHUNK

    def key_loader():
        loaded = {}

        def rows(j, lo, hi):
            k_ref = jobs[j][1]
            if (id(k_ref), lo) not in loaded:
                loaded[(id(k_ref), lo)] = k_ref[bi, lo:hi, :]
            return loaded[(id(k_ref), lo)]

        return rows

    def qk(k_rows, j):
        return jnp.dot(k_rows, jobs[j][0], preferred_element_type=F32)

    def softmax_pv(s, vt, j, first):
        m_new = _max_over_rows(s)
        if not first:
            m_old = m_ref[j]
            m_new = jnp.maximum(m_old, m_new)
            alpha = jnp.exp2(m_old - m_new)
        m_ref[j] = m_new
        upd = jnp.dot(vt, jnp.exp2(s - m_new).astype(BF16), preferred_element_type=F32)
        acc_ref[j] = upd if first else acc_ref[j] * alpha + upd

    def vt_tile(tile, j):
        _, _, vt_ref, v_row = jobs[j]
        return vt_ref[bi, tile, v_row:v_row + LANES, :]

    def vt_chunk(c, j):
        t0 = CTX_LEN // ROW_TILE + c * tiles_per_chunk
        return jnp.concatenate([vt_tile(t0 + i, j) for i in range(tiles_per_chunk)], axis=1)

    def chunk_rows(c):
        return CTX_LEN + c * KV_CHUNK, CTX_LEN + (c + 1) * KV_CHUNK

    @pl.when(jnp.logical_not(latent))
    def _():
        keys = key_loader()
        s = [qk(keys(j, 0, CTX_LEN), j) for j in range(n_jobs)]
        for j in range(n_jobs):
            softmax_pv(s[j], vt_tile(0, j), j, True)

    @pl.when(latent)
    def _():
        keys = key_loader()
        s = [qk(keys(j, 0, CTX_LEN), j) for j in range(n_jobs)]
        s_ref[0, 0] = qk(keys(0, *chunk_rows(0)), 0)
        for j in range(n_jobs):
            softmax_pv(s[j], vt_tile(0, j), j, True)
            if j + 1 < n_jobs:
                s_ref[0, j + 1] = qk(keys(j + 1, *chunk_rows(0)), j + 1)

        for c in range(n_chunks):
            slot = c % 2
            more = c + 1 < n_chunks
            if more:
                s_ref[1 - slot, 0] = qk(keys(0, *chunk_rows(c + 1)), 0)
            for j in range(n_jobs):
                softmax_pv(s_ref[slot, j], vt_chunk(c, j), j, False)
                if more and j + 1 < n_jobs:
                    s_ref[1 - slot, j + 1] = qk(keys(j + 1, *chunk_rows(c + 1)), j + 1)

    outs = []
    for j in range(n_jobs):
        acc = acc_ref[j]
        outs.append(acc[:LANES // 2, :] * (1.0 / acc[LANES // 2:LANES // 2 + 1, :]))
    return outs


def _rows_at(x, row0, total):
    parts = []
    if row0:
        parts.append(jnp.zeros((row0, x.shape[1]), x.dtype))
    parts.append(x)
    if total - row0 - x.shape[0]:
        parts.append(jnp.zeros((total - row0 - x.shape[0], x.shape[1]), x.dtype))
    return jnp.concatenate(parts, axis=0)


ATTN_JOBS = 2 * DA_HEADS + GQ_HEADS


def _attn_kernel(lam_init, aqt_ref, ak_ref, avt_ref, bqt_ref, bk_ref, bvt_ref, gate_ref, lam_ref,
                 sg_ref, ya_ref, yb_ref, acc_ref, m_ref, s_ref):
    lq1, lk1, lq2, lk2 = lam_ref[0:1, :], lam_ref[1:2, :], lam_ref[2:3, :], lam_ref[3:4, :]
    lam = (jnp.exp(jnp.sum(lq1 * lk1, axis=-1, keepdims=True))
           - jnp.exp(jnp.sum(lq2 * lk2, axis=-1, keepdims=True)) + lam_init)
    grp = GQ_HEADS // GQ_KV_HEADS
    for bi in range(aqt_ref.shape[0]):
        jobs = [(_rows_at(aqt_ref[bi, job * DA_QK:(job + 1) * DA_QK, :], job * DA_QK, BRANCH_W),
                 ak_ref, avt_ref, (job // 2) * LANES) for job in range(2 * DA_HEADS)]
        jobs += [(_rows_at(bqt_ref[bi, h * GQ_HD:(h + 1) * GQ_HD, :], (h // grp) * GQ_HD,
                           GQ_KV_HEADS * GQ_HD),
                  bk_ref, bvt_ref, (h // grp) * LANES) for h in range(GQ_HEADS)]
        o = _online_softmax_pv(bi, pl.program_id(1) > 0, jobs, acc_ref, m_ref, s_ref)

        heads = []
        for h in range(DA_HEADS):
            d = o[2 * h] - lam * o[2 * h + 1]
            heads.append(d * lax.rsqrt(jnp.mean(d * d, axis=0, keepdims=True) + EPS))
        yt = jnp.concatenate(heads, axis=0) * (sg_ref[...] * (1.0 - lam_init))
        ya_ref[bi] = (yt.T * gate_ref[bi, :, 0:BRANCH_W].astype(F32)).astype(BF16)
        yb = jnp.concatenate(o[2 * DA_HEADS:], axis=0).T
        yb_ref[bi] = (yb * gate_ref[bi, :, BRANCH_W:2 * BRANCH_W].astype(F32)).astype(BF16)


def _attention(lam_init, aqt, ak, avt, bqt, bk, bvt, gate, lam_vecs, subln_col):
    b, s, _ = ak.shape
    t = ROW_TILE
    full = lambda a: pl.BlockSpec(a.shape, lambda i, j: (0,) * a.ndim)
    nb = ATTN_BATCH
    qt_spec = pl.BlockSpec((nb, BRANCH_W, t), lambda i, j: (i, 0, j))
    keys = lambda a: pl.BlockSpec((nb, s, a.shape[2]), lambda i, j: (i, 0, 0))
    vals = lambda a: pl.BlockSpec((nb,) + a.shape[1:], lambda i, j: (i, 0, 0, 0))
    y_spec = pl.BlockSpec((nb, t, BRANCH_W), lambda i, j: (i, j, 0))
    y_shape = jax.ShapeDtypeStruct((b, s, BRANCH_W), BF16)
    return pl.pallas_call(
        functools.partial(_attn_kernel, lam_init),
        grid=(b // nb, s // t),
        in_specs=[qt_spec, keys(ak), vals(avt), qt_spec, keys(bk), vals(bvt),
                  pl.BlockSpec((nb, t, gate.shape[2]), lambda i, j: (i, j, 0)),
                  full(lam_vecs), full(subln_col)],
        out_specs=[y_spec, y_spec],
        out_shape=[y_shape, y_shape],
        scratch_shapes=[pltpu.VMEM((ATTN_JOBS, LANES, t), F32), pltpu.VMEM((ATTN_JOBS, 1, t), F32),
                        pltpu.VMEM((2, ATTN_JOBS, KV_CHUNK, t), F32)],
        compiler_params=pltpu.CompilerParams(
            dimension_semantics=("parallel", "arbitrary"), vmem_limit_bytes=VMEM_LIMIT),
        name="attention",
    )(aqt, ak, avt, bqt, bk, bvt, gate, lam_vecs, subln_col)


def _chunk_cumsum(g, row_in_chunk, reverse):
    x = g
    sh = 1
    while sh < GLA_CHUNK:
        if reverse:
            nb = pltpu.roll(x, x.shape[0] - sh, axis=0)
            x = x + jnp.where(row_in_chunk < GLA_CHUNK - sh, nb, 0.0)
        else:
            nb = pltpu.roll(x, sh, axis=0)
            x = x + jnp.where(row_in_chunk >= sh, nb, 0.0)
        sh *= 2
    return x


def _gla_kernel(qk_ref, v_ref, g_ref, gate_ref, ng_ref, o_ref, acc_ref, st_ref):
    n_tiles = qk_ref.shape[1] // GLA_TILE
    ctx_tiles = CTX_LEN // GLA_TILE
    per = GLA_TILE // GLA_CHUNK
    ri = lax.broadcasted_iota(jnp.int32, (GLA_TILE, GLA_TILE), 0)
    ci = lax.broadcasted_iota(jnp.int32, (GLA_TILE, GLA_TILE), 1)
    same_chunk = (ri // GLA_CHUNK) == (ci // GLA_CHUNK)
    chunk_of_row = ri // GLA_CHUNK
    row_in_chunk = ri % GLA_CHUNK
    head_of_k = lax.broadcasted_iota(jnp.int32, (GLA_TILE, GLA_HEADS * GLA_DK), 1) // GLA_DK
    head_of_v = lax.broadcasted_iota(jnp.int32, (GLA_TILE, BRANCH_W), 1) // GLA_DV
    st_mask = ((lax.broadcasted_iota(jnp.int32, (BRANCH_W, GLA_HEADS * GLA_DK), 0) // GLA_DV)
               == (lax.broadcasted_iota(jnp.int32, (BRANCH_W, GLA_HEADS * GLA_DK), 1) // GLA_DK))
    m64 = _group_mean_matrix(BRANCH_W, GLA_DV)
    tris = [jnp.where(same_chunk & (ci <= ri), 1.0, 0.0), jnp.where(same_chunk & (ci >= ri), 1.0, 0.0)]
    tris4 = [jnp.concatenate([t] * GLA_HEADS, axis=1) for t in tris]

    nt_dims = (((1,), (1,)), ((), ()))
    tn_dims = (((0,), (0,)), ((), ()))

    def scan_step(i, carry):
        def tile_index(d, sub):
            t = i * GLA_STEP_TILES + sub
            if not d:
                return t
            return jnp.where(t < ctx_tiles, ctx_tiles - 1 - t, n_tiles + ctx_tiles - 1 - t)

        chains = [(bi, d, sub) for bi in range(GLA_BATCH) for d in (0, 1) for sub in range(GLA_STEP_TILES)]
        nc = len(chains)
        rows = [pl.ds(pl.multiple_of(tile_index(d, sub) * GLA_TILE, GLA_TILE), GLA_TILE)
                for _, d, sub in chains]
        ends = [[c * GLA_CHUNK if d else (c + 1) * GLA_CHUNK - 1 for c in range(per)] for _, d, _ in chains]

        cum = []
        for n, (bi, d, _) in enumerate(chains):
            cum.append(_chunk_cumsum(g_ref[bi, rows[n], 128 * d:128 * d + 128], row_in_chunk, bool(d)))

        qe, kd, kl, vb = [], [], [], []
        for n, (bi, d, _) in enumerate(chains):
            q = qk_ref[bi, rows[n], 0:128]
            k = qk_ref[bi, rows[n], 128:256]
            cum_last = jnp.concatenate(
                [jnp.broadcast_to(cum[n][e:e + 1, :], (GLA_CHUNK, 128)) for e in ends[n]], axis=0)
            qe.append(q * jnp.exp(cum[n]))
            kd.append(k * jnp.exp(-cum[n]))
            kl.append(k * jnp.exp(cum_last - cum[n]))
            vb.append(v_ref[bi, rows[n], :])

        qeb = [x.astype(BF16) for x in qe]
        att = []
        for n, (bi, d, _) in enumerate(chains):
            kd_heads = jnp.concatenate(
                [jnp.where(head_of_k == h, kd[n], 0.0).astype(BF16) for h in range(GLA_HEADS)], axis=0)
            a = lax.dot_general(qeb[n], kd_heads, nt_dims, preferred_element_type=F32)
            att.append((a * tris4[d]).astype(BF16))
        o = []
        for n in range(nc):
            v_heads = jnp.concatenate(
                [jnp.where(head_of_v == h, vb[n], jnp.zeros_like(vb[n])) for h in range(GLA_HEADS)], axis=0)
            o.append(jnp.dot(att[n], v_heads, preferred_element_type=F32))

        upd = []
        for n in range(nc):
            kl_chunks = jnp.concatenate(
                [jnp.where(chunk_of_row == c, kl[n], 0.0).astype(BF16) for c in range(per)], axis=1)
            upd.append(lax.dot_general(vb[n], kl_chunks, tn_dims, preferred_element_type=F32))
        seen = [[None] * per for _ in range(nc)]
        for slot in range(2 * GLA_BATCH):
            st = st_ref[slot]
            for n in range(slot * GLA_STEP_TILES, (slot + 1) * GLA_STEP_TILES):
                for c in (range(per - 1, -1, -1) if slot % 2 else range(per)):
                    seen[n][c] = st.astype(BF16)
                    decay = jnp.exp(cum[n][ends[n][c]:ends[n][c] + 1, :])
                    st = st * decay + jnp.where(st_mask, upd[n][:, c * 128:(c + 1) * 128], 0.0)
            st_ref[slot] = st
        for n in range(nc):
            qe_chunks = jnp.concatenate(
                [jnp.where(chunk_of_row == c, qe[n], 0.0).astype(BF16) for c in range(per)], axis=1)
            o_inter = lax.dot_general(qe_chunks, jnp.concatenate(seen[n], axis=1), nt_dims,
                                      preferred_element_type=F32)
            acc_ref[n // GLA_STEP_TILES, rows[n], :] = o[n] + o_inter
        return carry

    st_ref[...] = jnp.zeros_like(st_ref)
    lax.fori_loop(0, n_tiles // GLA_STEP_TILES, scan_step, 0)

    def finish_step(i, carry):
        rows = pl.ds(pl.multiple_of(i * ROW_TILE, ROW_TILE), ROW_TILE)
        tot = [acc_ref[2 * bi, rows, :] + acc_ref[2 * bi + 1, rows, :] for bi in range(GLA_BATCH)]
        sq = [_split2(t * t) for t in tot]
        ms = [sum(jnp.dot(p, m64, preferred_element_type=F32) for p in s) for s in sq]
        for bi in range(GLA_BATCH):
            y = tot[bi] * lax.rsqrt(ms[bi] + EPS) * ng_ref[...]
            o_ref[bi, rows, :] = (y * gate_ref[bi, rows, :].astype(F32)).astype(BF16)
        return carry

    lax.fori_loop(0, qk_ref.shape[1] // ROW_TILE, finish_step, 0)


def _gla(dqk, dv, dg, gate, ng):
    b, s, _ = dqk.shape
    nb = GLA_BATCH
    blk = lambda width, cb: pl.BlockSpec((nb, s, width), lambda i: (i, 0, cb))
    return pl.pallas_call(
        _gla_kernel,
        grid=(b // nb,),
        in_specs=[blk(256, 0), blk(256, 0), blk(256, 0), blk(256, 0),
                  pl.BlockSpec(ng.shape, lambda i: (0, 0))],
        out_specs=blk(256, 0),
        out_shape=jax.ShapeDtypeStruct((b, s, 256), BF16),
        scratch_shapes=[pltpu.VMEM((2 * nb, s, BRANCH_W), F32),
                        pltpu.VMEM((2 * nb, BRANCH_W, GLA_HEADS * GLA_DK), F32)],
        compiler_params=pltpu.CompilerParams(
            dimension_semantics=("parallel",), vmem_limit_bytes=VMEM_LIMIT),
        name="gla",
    )(dqk, dv, dg, gate, ng)


def _out_kernel(x_ref, mod_ref, ya_ref, yb_ref, yc_ref, yd_ref, w_ref, fg_ref, o_ref):
    for bi in range(x_ref.shape[0]):
        y = jnp.concatenate([ya_ref[bi], yb_ref[bi], yc_ref[bi], yd_ref[bi]], axis=1)
        upd = jnp.dot(y, w_ref[...], preferred_element_type=F32)
        xn = x_ref[bi] + mod_ref[bi, 0, 2:3, :] * upd
        o_ref[bi] = xn * lax.rsqrt(jnp.mean(xn * xn, axis=-1, keepdims=True) + EPS) * fg_ref[...]


def _out_proj(xs, modsel, ya, yb, yc, yd, wo, fg):
    b, s, _ = xs.shape
    t = ROW_TILE
    nb = OUT_BATCH
    skip = CTX_LEN // t
    row = lambda width: pl.BlockSpec((nb, t, width), lambda i, j: (i, j + skip, 0))
    return pl.pallas_call(
        _out_kernel,
        grid=(b // nb, s // t - skip),
        in_specs=[
            row(D_MODEL),
            pl.BlockSpec((nb, 1, 3, D_MODEL), lambda i, j: (i, 1, 0, 0)),
            row(256), row(256), row(256), row(256),
            pl.BlockSpec((None,) + wo.shape[1:], lambda i, j: (wo.shape[0] - 1, 0, 0)),
            pl.BlockSpec(fg.shape, lambda i, j: (0, 0)),
        ],
        out_specs=pl.BlockSpec((nb, t, D_MODEL), lambda i, j: (i, j, 0)),
        out_shape=jax.ShapeDtypeStruct((b, s - skip * t, D_MODEL), F32),
        compiler_params=pltpu.CompilerParams(
            dimension_semantics=("parallel", "parallel"), vmem_limit_bytes=VMEM_LIMIT),
        name="out_proj_final",
    )(xs, modsel, ya, yb, yc, yd, wo, fg)


def _rope_tables(seq, dim, width):
    half = dim // 2
    quarter = half // 2
    lane = np.arange(width) % dim
    freq = ROPE_THETA ** (-(2.0 * (lane % quarter)) / half)
    pos_t = np.arange(seq)
    pos = np.where(lane[None, :] < half, (pos_t // GRID_W)[:, None], (pos_t % GRID_W)[:, None])
    ang = pos * freq[None, :]
    cos, sin = np.cos(ang), np.sin(ang)
    first = (lane % half) < quarter
    s_lo = np.where(first[None, :], -sin, 0.0)
    s_hi = np.where(first[None, :], 0.0, sin)
    lat = np.stack([cos, s_lo, s_hi])
    ctx = np.stack([np.ones((CTX_LEN, width)), np.zeros((CTX_LEN, width)), np.zeros((CTX_LEN, width))])
    return jnp.asarray(np.concatenate([ctx, lat], axis=1), dtype=F32)


def _pack_w_in(w):
    offs = [0]
    for n in (256, 256, 256, 256, 256, 128, 128, 256, 256, 256, 256, 128, 128, 256, 256, 16, 16):
        offs.append(offs[-1] + n)
    seg = [w[..., offs[i]:offs[i + 1]] for i in range(17)]
    aq, ak, av, az, bq, bk, bv, bz, cu, cv, cz, dq, dk, dv, dz, drf, drb = seg
    gap = jnp.zeros(w.shape[:-1] + (LANES // 2,), w.dtype)
    r_gap = jnp.concatenate([drf, drb, gap[..., :LANES // 2 - 2 * GLA_RANK]], axis=-1)
    av_sp = jnp.concatenate([av[..., 0:64], r_gap, av[..., 64:128], gap, av[..., 128:192], gap,
                             av[..., 192:256], gap], axis=-1)
    bvk = jnp.concatenate([bv[..., :64], bk[..., :64], bv[..., 64:], bk[..., 64:]], axis=-1)
    packed = jnp.concatenate([aq, ak, av_sp, az, bq, bvk, bz, cu, cv, cz, dq, dk, dv, dz], axis=-1)
    assert packed.shape[-1] == P_PACK
    return packed.astype(BF16)


def kernel(x, c, ctx, c_ctx, ada_w, ada_b, norm_g, w_in, da_lq1, da_lk1, da_lq2, da_lk2,
           da_subln_g, gq_qnorm_g, gq_knorm_g, sg_ln_g, sg_ln_b, sg_w, sg_b,
           gla_w2_f, gla_b_f, gla_w2_b, gla_b_b, gla_norm_g, w_out, final_norm_g):
    b, seq, d = x.shape
    assert (seq, d, ctx.shape[1]) == (seq // ROW_TILE * ROW_TILE, D_MODEL, CTX_LEN)

    n_mod = 32
    cpad = jnp.zeros((n_mod, d), F32).at[:b].set(c).at[b].set(c_ctx)
    mod = _modulation(cpad, ada_w, ada_b)

    taba = _rope_tables(seq, DA_QK, 256)
    tabb = _rope_tables(seq, GQ_HD, 256)
    fg = final_norm_g.reshape(1, d)
    wp_all, wo_all, sgw_all = _pack_w_in(w_in), w_out.astype(BF16), sg_w.astype(BF16)

    nl = DEPTH
    modsel_all = jnp.concatenate([jnp.broadcast_to(mod[:, b].reshape(nl, 1, 1, 3, d), (nl, b, 1, 3, d)),
                                  mod[:, :b].reshape(nl, b, 1, 3, d)], axis=2)
    w2_all = (jnp.zeros((nl, LANES, 256), F32)
              .at[:, R_LANE:R_LANE + GLA_RANK, 0:128].set(gla_w2_f)
              .at[:, R_LANE + GLA_RANK:R_LANE + 2 * GLA_RANK, 128:256].set(gla_w2_b)).astype(BF16)
    b2_all = jnp.concatenate([gla_b_f, gla_b_b], axis=1).reshape(nl, 1, 256)
    sgb_all = jnp.repeat(jnp.swapaxes(sg_b, 1, 2), BRANCH_W // SG_GROUPS, axis=2)
    tiled = lambda g, n: jnp.tile(g, (1, n)).reshape(nl, 1, g.shape[1] * n)
    gqq_all, gqk_all = tiled(gq_qnorm_g, GQ_HEADS), tiled(gq_knorm_g, GQ_KV_HEADS)
    subln_all, glan_all = tiled(da_subln_g, DA_HEADS), tiled(gla_norm_g, GLA_HEADS)
    lam_all = jnp.stack([da_lq1, da_lk1, da_lq2, da_lk2], axis=1)

    stream_in = (ctx, x)
    for i in range(DEPTH):
        lam_init = 0.8 - 0.6 * math.exp(-0.3 * i)
        (xs, aqt, ak, avt, bqt, bk, bvt, gate, gate_d, yc, dqk, dv, dg) = _layer(
            i, stream_in, modsel_all[i], norm_g[i].reshape(1, d), wp_all, taba, tabb,
            gqq_all[i], gqk_all[i], sg_ln_g[i].reshape(1, 256), sg_ln_b[i].reshape(1, 256),
            sgw_all, sgb_all[i], w2_all[i], b2_all[i])
        ya, yb = _attention(lam_init, aqt, ak, avt, bqt, bk, bvt, gate, lam_all[i],
                            subln_all[i].reshape(BRANCH_W, 1))
        yd = _gla(dqk, dv, dg, gate_d, glan_all[i])
        stream_in = (xs, modsel_all[i], ya, yb, yc, yd, wo_all)
    return _out_proj(*stream_in, fg)
```

```python
import functools
import math

import jax
import jax.numpy as jnp
import numpy as np
from jax import lax
from jax.experimental import pallas as pl
from jax.experimental.pallas import tpu as pltpu

F32 = jnp.float32
BF16 = jnp.bfloat16

D_MODEL = 1024
DEPTH = 4
CTX_LEN = 256
GRID_W = 64
BRANCH_W = 256
ROPE_THETA = 10000.0
EPS = 1e-6
DA_HEADS = 4
DA_QK = 32
DA_V = 64
GQ_HEADS = 4
GQ_KV_HEADS = 2
GQ_HD = 64
SG_GROUPS = 4
SG_CHUNK = 128
GLA_HEADS = 4
GLA_DV = 64
GLA_DK = 32
GLA_RANK = 16
GLA_NORMALIZER = 16.0
GLA_CHUNK = 32

LANES = 128
V_ROWS = 80
ROW_TILE = 256
KV_CHUNK = 512
LAYER_BATCH = 2
OUT_BATCH = 4
ATTN_BATCH = 2
GLA_TILE = 128
GLA_BATCH = 2
GLA_STEP_TILES = 3
VMEM_LIMIT = 56 * 1024 * 1024

C_AQ, C_AK, C_AV, C_AZ = 0, 256, 512, 1024
C_BQ, C_BVK, C_BZ = 1280, 1536, 1792
C_CU, C_CV, C_CZ = 2048, 2304, 2560
C_DQK, C_DV, C_DZ = 2816, 3072, 3328
P_PACK = 3584
R_LANE = 64
LOG2E = math.log2(math.e)


def _silu(x):
    return x * (1.0 / (1.0 + jnp.exp(-x)))


def _group_mean_matrix(width, group):
    r = lax.broadcasted_iota(jnp.int32, (width, width), 0) // group
    c = lax.broadcasted_iota(jnp.int32, (width, width), 1) // group
    return jnp.where(r == c, 1.0 / group, 0.0).astype(BF16)


def _split2(x):
    hi = x.astype(BF16)
    return hi, (x - hi.astype(F32)).astype(BF16)


def _group_mean(x, mat):
    return sum(jnp.dot(p, mat, preferred_element_type=F32) for p in _split2(x))


def _rope(x, cos, s_lo, s_hi, shift):
    outs = []
    for c in range(x.shape[1] // LANES):
        sl = slice(c * LANES, (c + 1) * LANES)
        xc = x[:, sl]
        up = pltpu.roll(xc, LANES - shift, axis=1)
        dn = pltpu.roll(xc, shift, axis=1)
        outs.append(xc * cos[:, sl] + up * s_lo[:, sl] + dn * s_hi[:, sl])
    return outs[0] if len(outs) == 1 else jnp.concatenate(outs, axis=1)


def _with_ones(v):
    lane = lax.broadcasted_iota(jnp.int32, v.shape, 1) % LANES
    return jnp.where(lane < LANES // 2, v, 1.0)


def _value_rows(vt):
    return jnp.concatenate([vt[h * LANES:h * LANES + V_ROWS, :] for h in range(vt.shape[0] // LANES)], axis=0)


def _mod_kernel(c_ref, w_ref, b_ref, o_ref):
    s = _silu(c_ref[...]).astype(BF16)
    o_ref[0] = jnp.dot(s, w_ref[0].astype(BF16), preferred_element_type=F32) + b_ref[0]


def _modulation(cpad, ada_w, ada_b):
    n = cpad.shape[0]
    return pl.pallas_call(
        _mod_kernel,
        grid=(DEPTH, 3),
        in_specs=[
            pl.BlockSpec((n, D_MODEL), lambda i, j: (0, 0)),
            pl.BlockSpec((1, D_MODEL, D_MODEL), lambda i, j: (i, 0, j)),
            pl.BlockSpec((1, 1, D_MODEL), lambda i, j: (i, 0, j)),
        ],
        out_specs=pl.BlockSpec((1, n, D_MODEL), lambda i, j: (i, 0, j)),
        out_shape=jax.ShapeDtypeStruct((DEPTH, n, 3 * D_MODEL), F32),
        compiler_params=pltpu.CompilerParams(vmem_limit_bytes=VMEM_LIMIT),
        name="modulation",
    )(cpad, ada_w, ada_b.reshape(DEPTH, 1, 3 * D_MODEL))


def _layer_kernel(first, *refs):
    if first:
        ctx_ref, x_ref = refs[:2]
        refs = refs[2:]
    else:
        xs_ref, modp_ref, ya_ref, yb_ref, yc_in_ref, yd_ref, wo_ref = refs[:7]
        refs = refs[7:]
    (mod_ref, ng_ref, w_ref, taba_ref, tabb_ref, gqq_ref, gqk_ref, lng_ref, lnb_ref, sgw_ref,
     sgb_ref, w2_ref, b2_ref,
     xs_out_ref, aq_ref, ak_ref, av_ref, bq_ref, bk_ref, bv_ref, gate_ref, yc_ref,
     dqk_ref, dv_ref, dg_ref) = refs

    def residual_and_norm(bi):
        if first:
            x = jnp.where(pl.program_id(1) == 0, ctx_ref[bi], x_ref[bi])
        else:
            y_prev = jnp.concatenate([ya_ref[bi], yb_ref[bi], yc_in_ref[bi], yd_ref[bi]], axis=1)
            x = xs_ref[bi] + modp_ref[bi, 0, 2:3, :] * jnp.dot(y_prev, wo_ref[...],
                                                               preferred_element_type=F32)
        xs_out_ref[bi] = x
        shift = mod_ref[bi, 0, 0:1, :]
        scale = mod_ref[bi, 0, 1:2, :]
        y = x * lax.rsqrt(jnp.mean(x * x, axis=-1, keepdims=True) + EPS) * ng_ref[...]
        return (y * (1.0 + scale) + shift).astype(BF16)

    def project(bi, hb):
        def proj(lo, width):
            return jnp.dot(hb, w_ref[:, lo:lo + width], preferred_element_type=F32)

        m64 = _group_mean_matrix(BRANCH_W, GQ_HD)
        half_lane = lax.broadcasted_iota(jnp.int32, (ROW_TILE, LANES), 1) < LANES // 2
        p_bq = proj(C_BQ, 256)
        p_bvk = proj(C_BVK, 256)
        p_cv = proj(C_CV, 256)
        p_av = proj(C_AV, 512)

        bq_sq = _split2(p_bq * p_bq)
        bk = jnp.where(half_lane, pltpu.roll(p_bvk[:, :LANES], LANES // 2, axis=1), p_bvk[:, LANES:])
        bk_sq = _split2(bk * bk)
        mu = jnp.mean(p_cv, axis=-1, keepdims=True)
        cen = p_cv - mu
        var = jnp.mean(cen * cen, axis=-1, keepdims=True)
        vn = (cen * lax.rsqrt(var + EPS) * lng_ref[...] + lnb_ref[...]).astype(BF16)
        r = p_av[:, :LANES].astype(BF16)
        av_ref[bi, 0] = _value_rows(_with_ones(p_av).T).astype(BF16)
        bv_ref[bi, 0] = _value_rows(_with_ones(p_bvk).T).astype(BF16)

        ca, sa_lo, sa_hi = taba_ref[0], taba_ref[1], taba_ref[2]
        aq = _rope(proj(C_AQ, 256), ca, sa_lo, sa_hi, DA_QK // 4) * (DA_QK ** -0.5 * LOG2E)
        aq_ref[bi] = aq.T.astype(BF16)
        ak = _rope(proj(C_AK, 256), ca, sa_lo, sa_hi, DA_QK // 4)
        ak_ref[bi] = ak.astype(BF16)

        bq_ms = sum(jnp.dot(p, m64, preferred_element_type=F32) for p in bq_sq)
        bk_ms = sum(jnp.dot(p, m64[:LANES, :LANES], preferred_element_type=F32) for p in bk_sq)
        gl = jnp.dot(r, w2_ref[...], preferred_element_type=F32) + b2_ref[...]

        gate_ref[bi, :, 0:256] = _silu(proj(C_AZ, 256)).astype(BF16)
        gate_ref[bi, :, 256:512] = _silu(proj(C_BZ, 256)).astype(BF16)
        gate_ref[bi, :, 512:768] = _silu(proj(C_DZ, 256)).astype(BF16)

        lane_group = lax.broadcasted_iota(jnp.int32, (SG_CHUNK, BRANCH_W), 1) // (BRANCH_W // SG_GROUPS)
        mixed = []
        for n in range(ROW_TILE // SG_CHUNK):
            vchunk = vn[n * SG_CHUNK:(n + 1) * SG_CHUNK, :]
            acc = jnp.zeros((SG_CHUNK, BRANCH_W), F32)
            for g in range(SG_GROUPS):
                acc = jnp.where(lane_group == g,
                                jnp.dot(sgw_ref[g], vchunk, preferred_element_type=F32), acc)
            mixed.append(acc + sgb_ref[...])

        cb, sb_lo, sb_hi = tabb_ref[0], tabb_ref[1], tabb_ref[2]
        bq = p_bq * lax.rsqrt(bq_ms + EPS) * gqq_ref[...]
        bq_ref[bi] = (_rope(bq, cb, sb_lo, sb_hi, GQ_HD // 4) * (GQ_HD ** -0.5 * LOG2E)).T.astype(BF16)
        bk = bk * lax.rsqrt(bk_ms + EPS) * gqk_ref[...]
        bk = _rope(bk, cb[:, :LANES], sb_lo[:, :LANES], sb_hi[:, :LANES], GQ_HD // 4)
        bk_ref[bi] = bk.astype(BF16)

        yc = proj(C_CU, 256) * jnp.concatenate(mixed, axis=0) * _silu(proj(C_CZ, 256))
        yc_ref[bi] = yc.astype(BF16)

        q_scale = jnp.where(lax.broadcasted_iota(jnp.int32, (1, 256), 1) < 128, GLA_DK ** -0.5, 1.0)
        dqk_ref[bi] = proj(C_DQK, 256) * q_scale
        dv_ref[bi] = proj(C_DV, 256).astype(BF16)
        log_sig = jnp.minimum(gl, 0.0) - jnp.log(1.0 + jnp.exp(-jnp.abs(gl)))
        dg_ref[bi] = log_sig * (1.0 / GLA_NORMALIZER)

    nb = xs_out_ref.shape[0]
    hbs = [residual_and_norm(bi) for bi in range(nb)]
    for bi in range(nb):
        project(bi, hbs[bi])


def _layer(layer, stream_in, modsel, ng, wp, taba, tabb, gqq, gqk, lng, lnb, sgw, sgb, w2, b2):
    first = layer == 0
    t = ROW_TILE
    stacked = lambda a, idx: pl.BlockSpec((None,) + a.shape[1:], lambda i, j: (idx,) + (0,) * (a.ndim - 1))
    b = stream_in[0].shape[0]
    s = CTX_LEN + stream_in[1].shape[1] if first else stream_in[0].shape[1]
    nb = LAYER_BATCH
    row = lambda width: pl.BlockSpec((nb, t, width), lambda i, j: (i, j, 0))
    colT = lambda height: pl.BlockSpec((nb, height, t), lambda i, j: (i, 0, j))
    full = lambda a: pl.BlockSpec(a.shape, lambda i, j: (0,) * a.ndim)
    tab = pl.BlockSpec((3, t, 256), lambda i, j: (0, j, 0))
    mods = pl.BlockSpec((nb, 1, 3, D_MODEL), lambda i, j: (i, jnp.minimum(j, 1), 0, 0))
    shp = lambda width, dt: jax.ShapeDtypeStruct((b, s, width), dt)
    shpT = lambda height: jax.ShapeDtypeStruct((b, height, s), BF16)
    tileT = lambda height: pl.BlockSpec((nb, 1, height, t), lambda i, j: (i, j, 0, 0))
    if first:
        stream_specs = [pl.BlockSpec((nb, t, D_MODEL), lambda i, j: (i, 0, 0)),
                        pl.BlockSpec((nb, t, D_MODEL), lambda i, j: (i, jnp.maximum(j - 1, 0), 0))]
    else:
        stream_specs = [row(D_MODEL), mods, row(256), row(256), row(256), row(256),
                        stacked(stream_in[6], layer - 1)]
    return pl.pallas_call(
        functools.partial(_layer_kernel, first),
        grid=(b // nb, s // t),
        in_specs=stream_specs + [
            mods, full(ng), stacked(wp, layer), tab, tab, full(gqq), full(gqk), full(lng), full(lnb),
            stacked(sgw, layer), full(sgb), full(w2), full(b2),
        ],
        out_specs=[row(D_MODEL), colT(256), row(256), tileT(DA_HEADS * V_ROWS), colT(256), row(128),
                   tileT(GQ_KV_HEADS * V_ROWS),
                   row(768), row(256), row(256), row(256), row(256)],
        out_shape=[shp(D_MODEL, F32),
                   shpT(256), shp(256, BF16), jax.ShapeDtypeStruct((b, s // t, DA_HEADS * V_ROWS, t), BF16),
                   shpT(256), shp(128, BF16),
                   jax.ShapeDtypeStruct((b, s // t, GQ_KV_HEADS * V_ROWS, t), BF16),
                   shp(768, BF16), shp(256, BF16), shp(256, F32), shp(256, BF16), shp(256, F32)],
        compiler_params=pltpu.CompilerParams(
            dimension_semantics=("parallel", "arbitrary"), vmem_limit_bytes=VMEM_LIMIT),
        name="layer_first" if first else "layer",
    )(*stream_in, modsel, ng, wp, taba, tabb, gqq, gqk, lng, lnb, sgw, sgb, w2, b2)


def _max_over_rows(s):
    m = s
    for part in (256, 32):
        if m.shape[0] > part and m.shape[0] % part == 0:
            m = jnp.max(m.reshape(m.shape[0] // part, part, m.shape[1]), axis=0)
    return jnp.max(m, axis=0, keepdims=True)


def _online_softmax_pv(bi, latent, jobs, acc_ref, m_ref, s_ref):
    n_jobs = len(jobs)
    tiles_per_chunk = KV_CHUNK // ROW_TILE
    n_chunks = (jobs[0][1].shape[1] - CTX_LEN) // KV_CHUNK

    def key_loader():
        loaded = {}

        def rows(j, lo, hi):
            k_ref = jobs[j][1]
            if (id(k_ref), lo) not in loaded:
                loaded[(id(k_ref), lo)] = k_ref[bi, lo:hi, :]
            return loaded[(id(k_ref), lo)]

        return rows

    def qk(k_rows, j):
        return jnp.dot(k_rows, jobs[j][0], preferred_element_type=F32)

    def softmax_pv(s, vt, j, first):
        m_new = _max_over_rows(s)
        if not first:
            m_old = m_ref[j]
            m_new = jnp.maximum(m_old, m_new)
            alpha = jnp.exp2(m_old - m_new)
        m_ref[j] = m_new
        upd = jnp.dot(vt, jnp.exp2(s - m_new).astype(BF16), preferred_element_type=F32)
        acc_ref[j] = upd if first else acc_ref[j] * alpha + upd

    def vt_tile(tile, j):
        _, _, vt_ref, v_row = jobs[j]
        return vt_ref[bi, tile, v_row:v_row + V_ROWS, :]

    def vt_chunk(c, j):
        t0 = CTX_LEN // ROW_TILE + c * tiles_per_chunk
        return jnp.concatenate([vt_tile(t0 + i, j) for i in range(tiles_per_chunk)], axis=1)

    def chunk_rows(c):
        return CTX_LEN + c * KV_CHUNK, CTX_LEN + (c + 1) * KV_CHUNK

    @pl.when(jnp.logical_not(latent))
    def _():
        keys = key_loader()
        s = [qk(keys(j, 0, CTX_LEN), j) for j in range(n_jobs)]
        for j in range(n_jobs):
            softmax_pv(s[j], vt_tile(0, j), j, True)

    @pl.when(latent)
    def _():
        keys = key_loader()
        s = [qk(keys(j, 0, CTX_LEN), j) for j in range(n_jobs)]
        s_ref[0, 0] = qk(keys(0, *chunk_rows(0)), 0)
        for j in range(n_jobs):
            softmax_pv(s[j], vt_tile(0, j), j, True)
            if j + 1 < n_jobs:
                s_ref[0, j + 1] = qk(keys(j + 1, *chunk_rows(0)), j + 1)

        for c in range(n_chunks):
            slot = c % 2
            more = c + 1 < n_chunks
            if more:
                s_ref[1 - slot, 0] = qk(keys(0, *chunk_rows(c + 1)), 0)
            for j in range(n_jobs):
                softmax_pv(s_ref[slot, j], vt_chunk(c, j), j, False)
                if more and j + 1 < n_jobs:
                    s_ref[1 - slot, j + 1] = qk(keys(j + 1, *chunk_rows(c + 1)), j + 1)

    outs = []
    for j in range(n_jobs):
        acc = acc_ref[j]
        outs.append(acc[:LANES // 2, :] * (1.0 / acc[LANES // 2:LANES // 2 + 1, :]))
    return outs


def _rows_at(x, row0, total):
    parts = []
    if row0:
        parts.append(jnp.zeros((row0, x.shape[1]), x.dtype))
    parts.append(x)
    if total - row0 - x.shape[0]:
        parts.append(jnp.zeros((total - row0 - x.shape[0], x.shape[1]), x.dtype))
    return jnp.concatenate(parts, axis=0)


ATTN_JOBS = 2 * DA_HEADS + GQ_HEADS


def _attn_kernel(lam_init, aqt_ref, ak_ref, avt_ref, bqt_ref, bk_ref, bvt_ref, gate_ref, lam_ref,
                 sg_ref, ya_ref, yb_ref, acc_ref, m_ref, s_ref):
    lq1, lk1, lq2, lk2 = lam_ref[0:1, :], lam_ref[1:2, :], lam_ref[2:3, :], lam_ref[3:4, :]
    lam = (jnp.exp(jnp.sum(lq1 * lk1, axis=-1, keepdims=True))
           - jnp.exp(jnp.sum(lq2 * lk2, axis=-1, keepdims=True)) + lam_init)
    grp = GQ_HEADS // GQ_KV_HEADS
    for bi in range(aqt_ref.shape[0]):
        jobs = [(_rows_at(aqt_ref[bi, job * DA_QK:(job + 1) * DA_QK, :], job * DA_QK, BRANCH_W),
                 ak_ref, avt_ref, (job // 2) * V_ROWS) for job in range(2 * DA_HEADS)]
        jobs += [(_rows_at(bqt_ref[bi, h * GQ_HD:(h + 1) * GQ_HD, :], (h // grp) * GQ_HD,
                           GQ_KV_HEADS * GQ_HD),
                  bk_ref, bvt_ref, (h // grp) * V_ROWS) for h in range(GQ_HEADS)]
        o = _online_softmax_pv(bi, pl.program_id(1) > 0, jobs, acc_ref, m_ref, s_ref)

        heads = []
        for h in range(DA_HEADS):
            d = o[2 * h] - lam * o[2 * h + 1]
            heads.append(d * lax.rsqrt(jnp.mean(d * d, axis=0, keepdims=True) + EPS))
        yt = jnp.concatenate(heads, axis=0) * (sg_ref[...] * (1.0 - lam_init))
        ya_ref[bi] = (yt.T * gate_ref[bi, :, 0:BRANCH_W].astype(F32)).astype(BF16)
        yb = jnp.concatenate(o[2 * DA_HEADS:], axis=0).T
        yb_ref[bi] = (yb * gate_ref[bi, :, BRANCH_W:2 * BRANCH_W].astype(F32)).astype(BF16)


def _attention(lam_init, aqt, ak, avt, bqt, bk, bvt, gate, lam_vecs, subln_col):
    b, s, _ = ak.shape
    t = ROW_TILE
    full = lambda a: pl.BlockSpec(a.shape, lambda i, j: (0,) * a.ndim)
    nb = ATTN_BATCH
    qt_spec = pl.BlockSpec((nb, BRANCH_W, t), lambda i, j: (i, 0, j))
    keys = lambda a: pl.BlockSpec((nb, s, a.shape[2]), lambda i, j: (i, 0, 0))
    vals = lambda a: pl.BlockSpec((nb,) + a.shape[1:], lambda i, j: (i, 0, 0, 0))
    y_spec = pl.BlockSpec((nb, t, BRANCH_W), lambda i, j: (i, j, 0))
    y_shape = jax.ShapeDtypeStruct((b, s, BRANCH_W), BF16)
    return pl.pallas_call(
        functools.partial(_attn_kernel, lam_init),
        grid=(b // nb, s // t),
        in_specs=[qt_spec, keys(ak), vals(avt), qt_spec, keys(bk), vals(bvt),
                  pl.BlockSpec((nb, t, gate.shape[2]), lambda i, j: (i, j, 0)),
                  full(lam_vecs), full(subln_col)],
        out_specs=[y_spec, y_spec],
        out_shape=[y_shape, y_shape],
        scratch_shapes=[pltpu.VMEM((ATTN_JOBS, V_ROWS, t), F32), pltpu.VMEM((ATTN_JOBS, 1, t), F32),
                        pltpu.VMEM((2, ATTN_JOBS, KV_CHUNK, t), F32)],
        compiler_params=pltpu.CompilerParams(
            dimension_semantics=("parallel", "arbitrary"), vmem_limit_bytes=VMEM_LIMIT),
        name="attention",
    )(aqt, ak, avt, bqt, bk, bvt, gate, lam_vecs, subln_col)


def _chunk_cumsum(g, row_in_chunk, reverse):
    x = g
    sh = 1
    while sh < GLA_CHUNK:
        if reverse:
            nb = pltpu.roll(x, x.shape[0] - sh, axis=0)
            x = x + jnp.where(row_in_chunk < GLA_CHUNK - sh, nb, 0.0)
        else:
            nb = pltpu.roll(x, sh, axis=0)
            x = x + jnp.where(row_in_chunk >= sh, nb, 0.0)
        sh *= 2
    return x


def _gla_kernel(qk_ref, v_ref, g_ref, gate_ref, ng_ref, o_ref, acc_ref, st_ref):
    n_tiles = qk_ref.shape[1] // GLA_TILE
    ctx_tiles = CTX_LEN // GLA_TILE
    per = GLA_TILE // GLA_CHUNK
    ri = lax.broadcasted_iota(jnp.int32, (GLA_TILE, GLA_TILE), 0)
    ci = lax.broadcasted_iota(jnp.int32, (GLA_TILE, GLA_TILE), 1)
    same_chunk = (ri // GLA_CHUNK) == (ci // GLA_CHUNK)
    chunk_of_row = ri // GLA_CHUNK
    row_in_chunk = ri % GLA_CHUNK
    head_of_k = lax.broadcasted_iota(jnp.int32, (GLA_TILE, GLA_HEADS * GLA_DK), 1) // GLA_DK
    head_of_v = lax.broadcasted_iota(jnp.int32, (GLA_TILE, BRANCH_W), 1) // GLA_DV
    st_mask = ((lax.broadcasted_iota(jnp.int32, (BRANCH_W, GLA_HEADS * GLA_DK), 0) // GLA_DV)
               == (lax.broadcasted_iota(jnp.int32, (BRANCH_W, GLA_HEADS * GLA_DK), 1) // GLA_DK))
    m64 = _group_mean_matrix(BRANCH_W, GLA_DV)
    tris = [jnp.where(same_chunk & (ci <= ri), 1.0, 0.0), jnp.where(same_chunk & (ci >= ri), 1.0, 0.0)]
    tris4 = [jnp.concatenate([t] * GLA_HEADS, axis=1) for t in tris]

    nt_dims = (((1,), (1,)), ((), ()))
    tn_dims = (((0,), (0,)), ((), ()))

    def scan_step(i, carry):
        def tile_index(d, sub):
            t = i * GLA_STEP_TILES + sub
            if not d:
                return t
            return jnp.where(t < ctx_tiles, ctx_tiles - 1 - t, n_tiles + ctx_tiles - 1 - t)

        chains = [(bi, d, sub) for bi in range(GLA_BATCH) for d in (0, 1) for sub in range(GLA_STEP_TILES)]
        nc = len(chains)
        rows = [pl.ds(pl.multiple_of(tile_index(d, sub) * GLA_TILE, GLA_TILE), GLA_TILE)
                for _, d, sub in chains]
        ends = [[c * GLA_CHUNK if d else (c + 1) * GLA_CHUNK - 1 for c in range(per)] for _, d, _ in chains]

        cum = []
        for n, (bi, d, _) in enumerate(chains):
            cum.append(_chunk_cumsum(g_ref[bi, rows[n], 128 * d:128 * d + 128], row_in_chunk, bool(d)))

        qe, kd, kl, vb = [], [], [], []
        for n, (bi, d, _) in enumerate(chains):
            q = qk_ref[bi, rows[n], 0:128]
            k = qk_ref[bi, rows[n], 128:256]
            cum_last = jnp.concatenate(
                [jnp.broadcast_to(cum[n][e:e + 1, :], (GLA_CHUNK, 128)) for e in ends[n]], axis=0)
            qe.append(q * jnp.exp(cum[n]))
            kd.append(k * jnp.exp(-cum[n]))
            kl.append(k * jnp.exp(cum_last - cum[n]))
            vb.append(v_ref[bi, rows[n], :])

        qeb = [x.astype(BF16) for x in qe]
        att = []
        for n, (bi, d, _) in enumerate(chains):
            kd_heads = jnp.concatenate(
                [jnp.where(head_of_k == h, kd[n], 0.0).astype(BF16) for h in range(GLA_HEADS)], axis=0)
            a = lax.dot_general(qeb[n], kd_heads, nt_dims, preferred_element_type=F32)
            att.append((a * tris4[d]).astype(BF16))
        o = []
        for n in range(nc):
            v_heads = jnp.concatenate(
                [jnp.where(head_of_v == h, vb[n], jnp.zeros_like(vb[n])) for h in range(GLA_HEADS)], axis=0)
            o.append(jnp.dot(att[n], v_heads, preferred_element_type=F32))

        upd = []
        for n in range(nc):
            kl_chunks = jnp.concatenate(
                [jnp.where(chunk_of_row == c, kl[n], 0.0).astype(BF16) for c in range(per)], axis=1)
            upd.append(lax.dot_general(vb[n], kl_chunks, tn_dims, preferred_element_type=F32))
        seen = [[None] * per for _ in range(nc)]
        for slot in range(2 * GLA_BATCH):
            st = st_ref[slot]
            for n in range(slot * GLA_STEP_TILES, (slot + 1) * GLA_STEP_TILES):
                for c in (range(per - 1, -1, -1) if slot % 2 else range(per)):
                    seen[n][c] = st.astype(BF16)
                    decay = jnp.exp(cum[n][ends[n][c]:ends[n][c] + 1, :])
                    st = st * decay + jnp.where(st_mask, upd[n][:, c * 128:(c + 1) * 128], 0.0)
            st_ref[slot] = st
        for n in range(nc):
            qe_chunks = jnp.concatenate(
                [jnp.where(chunk_of_row == c, qe[n], 0.0).astype(BF16) for c in range(per)], axis=1)
            o_inter = lax.dot_general(qe_chunks, jnp.concatenate(seen[n], axis=1), nt_dims,
                                      preferred_element_type=F32)
            acc_ref[n // GLA_STEP_TILES, rows[n], :] = o[n] + o_inter
        return carry

    st_ref[...] = jnp.zeros_like(st_ref)
    lax.fori_loop(0, n_tiles // GLA_STEP_TILES, scan_step, 0)

    def finish_step(i, carry):
        rows = pl.ds(pl.multiple_of(i * ROW_TILE, ROW_TILE), ROW_TILE)
        tot = [acc_ref[2 * bi, rows, :] + acc_ref[2 * bi + 1, rows, :] for bi in range(GLA_BATCH)]
        sq = [_split2(t * t) for t in tot]
        ms = [sum(jnp.dot(p, m64, preferred_element_type=F32) for p in s) for s in sq]
        for bi in range(GLA_BATCH):
            y = tot[bi] * lax.rsqrt(ms[bi] + EPS) * ng_ref[...]
            o_ref[bi, rows, :] = (y * gate_ref[bi, rows, :].astype(F32)).astype(BF16)
        return carry

    lax.fori_loop(0, qk_ref.shape[1] // ROW_TILE, finish_step, 0)


def _gla(dqk, dv, dg, gate, ng):
    b, s, _ = dqk.shape
    nb = GLA_BATCH
    blk = lambda width, cb: pl.BlockSpec((nb, s, width), lambda i: (i, 0, cb))
    return pl.pallas_call(
        _gla_kernel,
        grid=(b // nb,),
        in_specs=[blk(256, 0), blk(256, 0), blk(256, 0), blk(256, 2),
                  pl.BlockSpec(ng.shape, lambda i: (0, 0))],
        out_specs=blk(256, 0),
        out_shape=jax.ShapeDtypeStruct((b, s, 256), BF16),
        scratch_shapes=[pltpu.VMEM((2 * nb, s, BRANCH_W), F32),
                        pltpu.VMEM((2 * nb, BRANCH_W, GLA_HEADS * GLA_DK), F32)],
        compiler_params=pltpu.CompilerParams(
            dimension_semantics=("parallel",), vmem_limit_bytes=VMEM_LIMIT),
        name="gla",
    )(dqk, dv, dg, gate, ng)


def _out_kernel(x_ref, mod_ref, ya_ref, yb_ref, yc_ref, yd_ref, w_ref, fg_ref, o_ref):
    for bi in range(x_ref.shape[0]):
        y = jnp.concatenate([ya_ref[bi], yb_ref[bi], yc_ref[bi], yd_ref[bi]], axis=1)
        upd = jnp.dot(y, w_ref[...], preferred_element_type=F32)
        xn = x_ref[bi] + mod_ref[bi, 0, 2:3, :] * upd
        o_ref[bi] = xn * lax.rsqrt(jnp.mean(xn * xn, axis=-1, keepdims=True) + EPS) * fg_ref[...]


def _out_proj(xs, modsel, ya, yb, yc, yd, wo, fg):
    b, s, _ = xs.shape
    t = ROW_TILE
    nb = OUT_BATCH
    skip = CTX_LEN // t
    row = lambda width: pl.BlockSpec((nb, t, width), lambda i, j: (i, j + skip, 0))
    return pl.pallas_call(
        _out_kernel,
        grid=(b // nb, s // t - skip),
        in_specs=[
            row(D_MODEL),
            pl.BlockSpec((nb, 1, 3, D_MODEL), lambda i, j: (i, 1, 0, 0)),
            row(256), row(256), row(256), row(256),
            pl.BlockSpec((None,) + wo.shape[1:], lambda i, j: (wo.shape[0] - 1, 0, 0)),
            pl.BlockSpec(fg.shape, lambda i, j: (0, 0)),
        ],
        out_specs=pl.BlockSpec((nb, t, D_MODEL), lambda i, j: (i, j, 0)),
        out_shape=jax.ShapeDtypeStruct((b, s - skip * t, D_MODEL), F32),
        compiler_params=pltpu.CompilerParams(
            dimension_semantics=("parallel", "parallel"), vmem_limit_bytes=VMEM_LIMIT),
        name="out_proj_final",
    )(xs, modsel, ya, yb, yc, yd, wo, fg)


def _rope_tables(seq, dim, width):
    half = dim // 2
    quarter = half // 2
    lane = np.arange(width) % dim
    freq = ROPE_THETA ** (-(2.0 * (lane % quarter)) / half)
    pos_t = np.arange(seq)
    pos = np.where(lane[None, :] < half, (pos_t // GRID_W)[:, None], (pos_t % GRID_W)[:, None])
    ang = pos * freq[None, :]
    cos, sin = np.cos(ang), np.sin(ang)
    first = (lane % half) < quarter
    s_lo = np.where(first[None, :], -sin, 0.0)
    s_hi = np.where(first[None, :], 0.0, sin)
    lat = np.stack([cos, s_lo, s_hi])
    ctx = np.stack([np.ones((CTX_LEN, width)), np.zeros((CTX_LEN, width)), np.zeros((CTX_LEN, width))])
    return jnp.asarray(np.concatenate([ctx, lat], axis=1), dtype=F32)


def _pack_w_in(w):
    offs = [0]
    for n in (256, 256, 256, 256, 256, 128, 128, 256, 256, 256, 256, 128, 128, 256, 256, 16, 16):
        offs.append(offs[-1] + n)
    seg = [w[..., offs[i]:offs[i + 1]] for i in range(17)]
    aq, ak, av, az, bq, bk, bv, bz, cu, cv, cz, dq, dk, dv, dz, drf, drb = seg
    gap = jnp.zeros(w.shape[:-1] + (LANES // 2,), w.dtype)
    r_gap = jnp.concatenate([drf, drb, gap[..., :LANES // 2 - 2 * GLA_RANK]], axis=-1)
    av_sp = jnp.concatenate([av[..., 0:64], r_gap, av[..., 64:128], gap, av[..., 128:192], gap,
                             av[..., 192:256], gap], axis=-1)
    bvk = jnp.concatenate([bv[..., :64], bk[..., :64], bv[..., 64:], bk[..., 64:]], axis=-1)
    packed = jnp.concatenate([aq, ak, av_sp, az, bq, bvk, bz, cu, cv, cz, dq, dk, dv, dz], axis=-1)
    assert packed.shape[-1] == P_PACK
    return packed.astype(BF16)


def kernel(x, c, ctx, c_ctx, ada_w, ada_b, norm_g, w_in, da_lq1, da_lk1, da_lq2, da_lk2,
           da_subln_g, gq_qnorm_g, gq_knorm_g, sg_ln_g, sg_ln_b, sg_w, sg_b,
           gla_w2_f, gla_b_f, gla_w2_b, gla_b_b, gla_norm_g, w_out, final_norm_g):
    b, seq, d = x.shape
    assert (seq, d, ctx.shape[1]) == (seq // ROW_TILE * ROW_TILE, D_MODEL, CTX_LEN)

    n_mod = 32
    cpad = jnp.zeros((n_mod, d), F32).at[:b].set(c).at[b].set(c_ctx)
    mod = _modulation(cpad, ada_w, ada_b)

    taba = _rope_tables(seq, DA_QK, 256)
    tabb = _rope_tables(seq, GQ_HD, 256)
    fg = final_norm_g.reshape(1, d)
    wp_all, wo_all, sgw_all = _pack_w_in(w_in), w_out.astype(BF16), sg_w.astype(BF16)

    nl = DEPTH
    modsel_all = jnp.concatenate([jnp.broadcast_to(mod[:, b].reshape(nl, 1, 1, 3, d), (nl, b, 1, 3, d)),
                                  mod[:, :b].reshape(nl, b, 1, 3, d)], axis=2)
    w2_all = (jnp.zeros((nl, LANES, 256), F32)
              .at[:, R_LANE:R_LANE + GLA_RANK, 0:128].set(gla_w2_f)
              .at[:, R_LANE + GLA_RANK:R_LANE + 2 * GLA_RANK, 128:256].set(gla_w2_b)).astype(BF16)
    b2_all = jnp.concatenate([gla_b_f, gla_b_b], axis=1).reshape(nl, 1, 256)
    sgb_all = jnp.repeat(jnp.swapaxes(sg_b, 1, 2), BRANCH_W // SG_GROUPS, axis=2)
    tiled = lambda g, n: jnp.tile(g, (1, n)).reshape(nl, 1, g.shape[1] * n)
    gqq_all, gqk_all = tiled(gq_qnorm_g, GQ_HEADS), tiled(gq_knorm_g, GQ_KV_HEADS)
    subln_all, glan_all = tiled(da_subln_g, DA_HEADS), tiled(gla_norm_g, GLA_HEADS)
    lam_all = jnp.stack([da_lq1, da_lk1, da_lq2, da_lk2], axis=1)

    stream_in = (ctx, x)
    for i in range(DEPTH):
        lam_init = 0.8 - 0.6 * math.exp(-0.3 * i)
        (xs, aqt, ak, avt, bqt, bk, bvt, gate, yc, dqk, dv, dg) = _layer(
            i, stream_in, modsel_all[i], norm_g[i].reshape(1, d), wp_all, taba, tabb,
            gqq_all[i], gqk_all[i], sg_ln_g[i].reshape(1, 256), sg_ln_b[i].reshape(1, 256),
            sgw_all, sgb_all[i], w2_all[i], b2_all[i])
        ya, yb = _attention(lam_init, aqt, ak, avt, bqt, bk, bvt, gate, lam_all[i],
                            subln_all[i].reshape(BRANCH_W, 1))
        yd = _gla(dqk, dv, dg, gate, glan_all[i])
        stream_in = (xs, modsel_all[i], ya, yb, yc, yd, wo_all)
    return _out_proj(*stream_in, fg)
```

```python
import functools
import math

import jax
import jax.numpy as jnp
import numpy as np
from jax import lax
from jax.experimental import pallas as pl
from jax.experimental.pallas import tpu as pltpu

F32 = jnp.float32
BF16 = jnp.bfloat16

D_MODEL = 1024
DEPTH = 4
CTX_LEN = 256
GRID_W = 64
BRANCH_W = 256
ROPE_THETA = 10000.0
EPS = 1e-6
DA_HEADS = 4
DA_QK = 32
DA_V = 64
GQ_HEADS = 4
GQ_KV_HEADS = 2
GQ_HD = 64
SG_GROUPS = 4
SG_CHUNK = 128
GLA_HEADS = 4
GLA_DV = 64
GLA_DK = 32
GLA_RANK = 16
GLA_NORMALIZER = 16.0
GLA_CHUNK = 32

LANES = 128
ROW_TILE = 256
KV_CHUNK = 512
LAYER_BATCH = 2
OUT_BATCH = 4
ATTN_BATCH = 2
GLA_TILE = 128
GLA_BATCH = 1
GLA_STEP_TILES = 3
VMEM_LIMIT = 56 * 1024 * 1024

C_AQ, C_AK, C_AV, C_AZ = 0, 256, 512, 1024
C_BQ, C_BVK, C_BZ = 1280, 1536, 1792
C_CU, C_CV, C_CZ = 2048, 2304, 2560
C_DQK, C_DV, C_DZ = 2816, 3072, 3328
P_PACK = 3584
R_LANE = 64
LOG2E = math.log2(math.e)


def _silu(x):
    return x * (1.0 / (1.0 + jnp.exp(-x)))


def _group_mean_matrix(width, group):
    r = lax.broadcasted_iota(jnp.int32, (width, width), 0) // group
    c = lax.broadcasted_iota(jnp.int32, (width, width), 1) // group
    return jnp.where(r == c, 1.0 / group, 0.0).astype(BF16)


def _split2(x):
    hi = x.astype(BF16)
    return hi, (x - hi.astype(F32)).astype(BF16)


def _group_mean(x, mat):
    return sum(jnp.dot(p, mat, preferred_element_type=F32) for p in _split2(x))


def _rope(x, cos, s_lo, s_hi, shift):
    outs = []
    for c in range(x.shape[1] // LANES):
        sl = slice(c * LANES, (c + 1) * LANES)
        xc = x[:, sl]
        up = pltpu.roll(xc, LANES - shift, axis=1)
        dn = pltpu.roll(xc, shift, axis=1)
        outs.append(xc * cos[:, sl] + up * s_lo[:, sl] + dn * s_hi[:, sl])
    return outs[0] if len(outs) == 1 else jnp.concatenate(outs, axis=1)


def _with_ones(v):
    lane = lax.broadcasted_iota(jnp.int32, v.shape, 1) % LANES
    return jnp.where(lane < LANES // 2, v, 1.0)


def _mod_kernel(c_ref, w_ref, b_ref, o_ref):
    s = _silu(c_ref[...]).astype(BF16)
    o_ref[0] = jnp.dot(s, w_ref[0].astype(BF16), preferred_element_type=F32) + b_ref[0]


def _modulation(cpad, ada_w, ada_b):
    n = cpad.shape[0]
    return pl.pallas_call(
        _mod_kernel,
        grid=(DEPTH, 3),
        in_specs=[
            pl.BlockSpec((n, D_MODEL), lambda i, j: (0, 0)),
            pl.BlockSpec((1, D_MODEL, D_MODEL), lambda i, j: (i, 0, j)),
            pl.BlockSpec((1, 1, D_MODEL), lambda i, j: (i, 0, j)),
        ],
        out_specs=pl.BlockSpec((1, n, D_MODEL), lambda i, j: (i, 0, j)),
        out_shape=jax.ShapeDtypeStruct((DEPTH, n, 3 * D_MODEL), F32),
        compiler_params=pltpu.CompilerParams(vmem_limit_bytes=VMEM_LIMIT),
        name="modulation",
    )(cpad, ada_w, ada_b.reshape(DEPTH, 1, 3 * D_MODEL))


def _layer_kernel(first, *refs):
    if first:
        ctx_ref, x_ref = refs[:2]
        refs = refs[2:]
    else:
        xs_ref, modp_ref, ya_ref, yb_ref, yc_in_ref, yd_ref, wo_ref = refs[:7]
        refs = refs[7:]
    (mod_ref, ng_ref, w_ref, taba_ref, tabb_ref, gqq_ref, gqk_ref, lng_ref, lnb_ref, sgw_ref,
     sgb_ref, w2_ref, b2_ref,
     xs_out_ref, aq_ref, ak_ref, av_ref, bq_ref, bk_ref, bv_ref, gate_ref, yc_ref,
     dqk_ref, dv_ref, dg_ref) = refs

    def residual_and_norm(bi):
        if first:
            x = jnp.where(pl.program_id(1) == 0, ctx_ref[bi], x_ref[bi])
        else:
            y_prev = jnp.concatenate([ya_ref[bi], yb_ref[bi], yc_in_ref[bi], yd_ref[bi]], axis=1)
            x = xs_ref[bi] + modp_ref[bi, 0, 2:3, :] * jnp.dot(y_prev, wo_ref[...],
                                                               preferred_element_type=F32)
        xs_out_ref[bi] = x
        shift = mod_ref[bi, 0, 0:1, :]
        scale = mod_ref[bi, 0, 1:2, :]
        y = x * lax.rsqrt(jnp.mean(x * x, axis=-1, keepdims=True) + EPS) * ng_ref[...]
        return (y * (1.0 + scale) + shift).astype(BF16)

    def project(bi, hb):
        def proj(lo, width):
            return jnp.dot(hb, w_ref[:, lo:lo + width], preferred_element_type=F32)

        m64 = _group_mean_matrix(BRANCH_W, GQ_HD)
        half_lane = lax.broadcasted_iota(jnp.int32, (ROW_TILE, LANES), 1) < LANES // 2
        p_bq = proj(C_BQ, 256)
        p_bvk = proj(C_BVK, 256)
        p_cv = proj(C_CV, 256)
        p_av = proj(C_AV, 512)

        bq_sq = _split2(p_bq * p_bq)
        bk = jnp.where(half_lane, pltpu.roll(p_bvk[:, :LANES], LANES // 2, axis=1), p_bvk[:, LANES:])
        bk_sq = _split2(bk * bk)
        mu = jnp.mean(p_cv, axis=-1, keepdims=True)
        cen = p_cv - mu
        var = jnp.mean(cen * cen, axis=-1, keepdims=True)
        vn = (cen * lax.rsqrt(var + EPS) * lng_ref[...] + lnb_ref[...]).astype(BF16)
        r = p_av[:, :LANES].astype(BF16)
        av_ref[bi, 0] = _with_ones(p_av).T.astype(BF16)
        bv_ref[bi, 0] = _with_ones(p_bvk).T.astype(BF16)

        ca, sa_lo, sa_hi = taba_ref[0], taba_ref[1], taba_ref[2]
        aq = _rope(proj(C_AQ, 256), ca, sa_lo, sa_hi, DA_QK // 4) * (DA_QK ** -0.5 * LOG2E)
        aq_ref[bi] = aq.T.astype(BF16)
        ak = _rope(proj(C_AK, 256), ca, sa_lo, sa_hi, DA_QK // 4)
        ak_ref[bi] = ak.astype(BF16)

        bq_ms = sum(jnp.dot(p, m64, preferred_element_type=F32) for p in bq_sq)
        bk_ms = sum(jnp.dot(p, m64[:LANES, :LANES], preferred_element_type=F32) for p in bk_sq)
        gl = jnp.dot(r, w2_ref[...], preferred_element_type=F32) + b2_ref[...]

        gate_ref[bi, :, 0:256] = _silu(proj(C_AZ, 256)).astype(BF16)
        gate_ref[bi, :, 256:512] = _silu(proj(C_BZ, 256)).astype(BF16)
        gate_ref[bi, :, 512:768] = _silu(proj(C_DZ, 256)).astype(BF16)

        lane_group = lax.broadcasted_iota(jnp.int32, (SG_CHUNK, BRANCH_W), 1) // (BRANCH_W // SG_GROUPS)
        mixed = []
        for n in range(ROW_TILE // SG_CHUNK):
            vchunk = vn[n * SG_CHUNK:(n + 1) * SG_CHUNK, :]
            acc = jnp.zeros((SG_CHUNK, BRANCH_W), F32)
            for g in range(SG_GROUPS):
                acc = jnp.where(lane_group == g,
                                jnp.dot(sgw_ref[g], vchunk, preferred_element_type=F32), acc)
            mixed.append(acc + sgb_ref[...])

        cb, sb_lo, sb_hi = tabb_ref[0], tabb_ref[1], tabb_ref[2]
        bq = p_bq * lax.rsqrt(bq_ms + EPS) * gqq_ref[...]
        bq_ref[bi] = (_rope(bq, cb, sb_lo, sb_hi, GQ_HD // 4) * (GQ_HD ** -0.5 * LOG2E)).T.astype(BF16)
        bk = bk * lax.rsqrt(bk_ms + EPS) * gqk_ref[...]
        bk = _rope(bk, cb[:, :LANES], sb_lo[:, :LANES], sb_hi[:, :LANES], GQ_HD // 4)
        bk_ref[bi] = bk.astype(BF16)

        yc = proj(C_CU, 256) * jnp.concatenate(mixed, axis=0) * _silu(proj(C_CZ, 256))
        yc_ref[bi] = yc.astype(BF16)

        q_scale = jnp.where(lax.broadcasted_iota(jnp.int32, (1, 256), 1) < 128, GLA_DK ** -0.5, 1.0)
        dqk_ref[bi] = proj(C_DQK, 256) * q_scale
        dv_ref[bi] = proj(C_DV, 256).astype(BF16)
        log_sig = jnp.minimum(gl, 0.0) - jnp.log(1.0 + jnp.exp(-jnp.abs(gl)))
        dg_ref[bi] = log_sig * (1.0 / GLA_NORMALIZER)

    nb = xs_out_ref.shape[0]
    hbs = [residual_and_norm(bi) for bi in range(nb)]
    for bi in range(nb):
        project(bi, hbs[bi])


def _layer(layer, stream_in, modsel, ng, wp, taba, tabb, gqq, gqk, lng, lnb, sgw, sgb, w2, b2):
    first = layer == 0
    t = ROW_TILE
    stacked = lambda a, idx: pl.BlockSpec((None,) + a.shape[1:], lambda i, j: (idx,) + (0,) * (a.ndim - 1))
    b = stream_in[0].shape[0]
    s = CTX_LEN + stream_in[1].shape[1] if first else stream_in[0].shape[1]
    nb = LAYER_BATCH
    row = lambda width: pl.BlockSpec((nb, t, width), lambda i, j: (i, j, 0))
    colT = lambda height: pl.BlockSpec((nb, height, t), lambda i, j: (i, 0, j))
    full = lambda a: pl.BlockSpec(a.shape, lambda i, j: (0,) * a.ndim)
    tab = pl.BlockSpec((3, t, 256), lambda i, j: (0, j, 0))
    mods = pl.BlockSpec((nb, 1, 3, D_MODEL), lambda i, j: (i, jnp.minimum(j, 1), 0, 0))
    shp = lambda width, dt: jax.ShapeDtypeStruct((b, s, width), dt)
    shpT = lambda height: jax.ShapeDtypeStruct((b, height, s), BF16)
    tileT = lambda height: pl.BlockSpec((nb, 1, height, t), lambda i, j: (i, j, 0, 0))
    if first:
        stream_specs = [pl.BlockSpec((nb, t, D_MODEL), lambda i, j: (i, 0, 0)),
                        pl.BlockSpec((nb, t, D_MODEL), lambda i, j: (i, jnp.maximum(j - 1, 0), 0))]
    else:
        stream_specs = [row(D_MODEL), mods, row(256), row(256), row(256), row(256),
                        stacked(stream_in[6], layer - 1)]
    return pl.pallas_call(
        functools.partial(_layer_kernel, first),
        grid=(b // nb, s // t),
        in_specs=stream_specs + [
            mods, full(ng), stacked(wp, layer), tab, tab, full(gqq), full(gqk), full(lng), full(lnb),
            stacked(sgw, layer), full(sgb), full(w2), full(b2),
        ],
        out_specs=[row(D_MODEL), colT(256), row(256), tileT(512), colT(256), row(128), tileT(256),
                   row(768), row(256), row(256), row(256), row(256)],
        out_shape=[shp(D_MODEL, F32),
                   shpT(256), shp(256, BF16), jax.ShapeDtypeStruct((b, s // t, 512, t), BF16),
                   shpT(256), shp(128, BF16), jax.ShapeDtypeStruct((b, s // t, 256, t), BF16),
                   shp(768, BF16), shp(256, BF16), shp(256, F32), shp(256, BF16), shp(256, F32)],
        compiler_params=pltpu.CompilerParams(
            dimension_semantics=("parallel", "arbitrary"), vmem_limit_bytes=VMEM_LIMIT),
        name="layer_first" if first else "layer",
    )(*stream_in, modsel, ng, wp, taba, tabb, gqq, gqk, lng, lnb, sgw, sgb, w2, b2)


def _max_over_rows(s):
    m = s
    for part in (256, 32):
        if m.shape[0] > part and m.shape[0] % part == 0:
            m = jnp.max(m.reshape(m.shape[0] // part, part, m.shape[1]), axis=0)
    return jnp.max(m, axis=0, keepdims=True)


def _online_softmax_pv(bi, latent, jobs, acc_ref, m_ref, s_ref):
    n_jobs = len(jobs)
    tiles_per_chunk = KV_CHUNK // ROW_TILE
    n_chunks = (jobs[0][1].shape[1] - CTX_LEN) // KV_CHUNK

    def key_loader():
        loaded = {}

        def rows(j, lo, hi):
            k_ref = jobs[j][1]
            if (id(k_ref), lo) not in loaded:
                loaded[(id(k_ref), lo)] = k_ref[bi, lo:hi, :]
            return loaded[(id(k_ref), lo)]

        return rows

    def qk(k_rows, j):
        return jnp.dot(k_rows, jobs[j][0], preferred_element_type=F32)

    def softmax_pv(s, vt, j, first):
        m_new = _max_over_rows(s)
        if not first:
            m_old = m_ref[j]
            m_new = jnp.maximum(m_old, m_new)
            alpha = jnp.exp2(m_old - m_new)
        m_ref[j] = m_new
        upd = jnp.dot(vt, jnp.exp2(s - m_new).astype(BF16), preferred_element_type=F32)
        acc_ref[j] = upd if first else acc_ref[j] * alpha + upd

    def vt_tile(tile, j):
        _, _, vt_ref, v_row = jobs[j]
        return vt_ref[bi, tile, v_row:v_row + LANES, :]

    def vt_chunk(c, j):
        t0 = CTX_LEN // ROW_TILE + c * tiles_per_chunk
        return jnp.concatenate([vt_tile(t0 + i, j) for i in range(tiles_per_chunk)], axis=1)

    def chunk_rows(c):
        return CTX_LEN + c * KV_CHUNK, CTX_LEN + (c + 1) * KV_CHUNK

    @pl.when(jnp.logical_not(latent))
    def _():
        keys = key_loader()
        s = [qk(keys(j, 0, CTX_LEN), j) for j in range(n_jobs)]
        for j in range(n_jobs):
            softmax_pv(s[j], vt_tile(0, j), j, True)

    @pl.when(latent)
    def _():
        keys = key_loader()
        s = [qk(keys(j, 0, CTX_LEN), j) for j in range(n_jobs)]
        s_ref[0, 0] = qk(keys(0, *chunk_rows(0)), 0)
        for j in range(n_jobs):
            softmax_pv(s[j], vt_tile(0, j), j, True)
            if j + 1 < n_jobs:
                s_ref[0, j + 1] = qk(keys(j + 1, *chunk_rows(0)), j + 1)

        for c in range(n_chunks):
            slot = c % 2
            more = c + 1 < n_chunks
            if more:
                s_ref[1 - slot, 0] = qk(keys(0, *chunk_rows(c + 1)), 0)
            for j in range(n_jobs):
                softmax_pv(s_ref[slot, j], vt_chunk(c, j), j, False)
                if more and j + 1 < n_jobs:
                    s_ref[1 - slot, j + 1] = qk(keys(j + 1, *chunk_rows(c + 1)), j + 1)

    outs = []
    for j in range(n_jobs):
        acc = acc_ref[j]
        outs.append(acc[:LANES // 2, :] * (1.0 / acc[LANES // 2:LANES // 2 + 1, :]))
    return outs


def _rows_at(x, row0, total):
    parts = []
    if row0:
        parts.append(jnp.zeros((row0, x.shape[1]), x.dtype))
    parts.append(x)
    if total - row0 - x.shape[0]:
        parts.append(jnp.zeros((total - row0 - x.shape[0], x.shape[1]), x.dtype))
    return jnp.concatenate(parts, axis=0)


ATTN_JOBS = 2 * DA_HEADS + GQ_HEADS


def _attn_kernel(lam_init, aqt_ref, ak_ref, avt_ref, bqt_ref, bk_ref, bvt_ref, gate_ref, lam_ref,
                 sg_ref, ya_ref, yb_ref, acc_ref, m_ref, s_ref):
    lq1, lk1, lq2, lk2 = lam_ref[0:1, :], lam_ref[1:2, :], lam_ref[2:3, :], lam_ref[3:4, :]
    lam = (jnp.exp(jnp.sum(lq1 * lk1, axis=-1, keepdims=True))
           - jnp.exp(jnp.sum(lq2 * lk2, axis=-1, keepdims=True)) + lam_init)
    grp = GQ_HEADS // GQ_KV_HEADS
    for bi in range(aqt_ref.shape[0]):
        jobs = [(_rows_at(aqt_ref[bi, job * DA_QK:(job + 1) * DA_QK, :], job * DA_QK, BRANCH_W),
                 ak_ref, avt_ref, (job // 2) * LANES) for job in range(2 * DA_HEADS)]
        jobs += [(_rows_at(bqt_ref[bi, h * GQ_HD:(h + 1) * GQ_HD, :], (h // grp) * GQ_HD,
                           GQ_KV_HEADS * GQ_HD),
                  bk_ref, bvt_ref, (h // grp) * LANES) for h in range(GQ_HEADS)]
        o = _online_softmax_pv(bi, pl.program_id(1) > 0, jobs, acc_ref, m_ref, s_ref)

        heads = []
        for h in range(DA_HEADS):
            d = o[2 * h] - lam * o[2 * h + 1]
            heads.append(d * lax.rsqrt(jnp.mean(d * d, axis=0, keepdims=True) + EPS))
        yt = jnp.concatenate(heads, axis=0) * (sg_ref[...] * (1.0 - lam_init))
        ya_ref[bi] = (yt.T * gate_ref[bi, :, 0:BRANCH_W].astype(F32)).astype(BF16)
        yb = jnp.concatenate(o[2 * DA_HEADS:], axis=0).T
        yb_ref[bi] = (yb * gate_ref[bi, :, BRANCH_W:2 * BRANCH_W].astype(F32)).astype(BF16)


def _attention(lam_init, aqt, ak, avt, bqt, bk, bvt, gate, lam_vecs, subln_col):
    b, s, _ = ak.shape
    t = ROW_TILE
    full = lambda a: pl.BlockSpec(a.shape, lambda i, j: (0,) * a.ndim)
    nb = ATTN_BATCH
    qt_spec = pl.BlockSpec((nb, BRANCH_W, t), lambda i, j: (i, 0, j))
    keys = lambda a: pl.BlockSpec((nb, s, a.shape[2]), lambda i, j: (i, 0, 0))
    vals = lambda a: pl.BlockSpec((nb,) + a.shape[1:], lambda i, j: (i, 0, 0, 0))
    y_spec = pl.BlockSpec((nb, t, BRANCH_W), lambda i, j: (i, j, 0))
    y_shape = jax.ShapeDtypeStruct((b, s, BRANCH_W), BF16)
    return pl.pallas_call(
        functools.partial(_attn_kernel, lam_init),
        grid=(b // nb, s // t),
        in_specs=[qt_spec, keys(ak), vals(avt), qt_spec, keys(bk), vals(bvt),
                  pl.BlockSpec((nb, t, gate.shape[2]), lambda i, j: (i, j, 0)),
                  full(lam_vecs), full(subln_col)],
        out_specs=[y_spec, y_spec],
        out_shape=[y_shape, y_shape],
        scratch_shapes=[pltpu.VMEM((ATTN_JOBS, LANES, t), F32), pltpu.VMEM((ATTN_JOBS, 1, t), F32),
                        pltpu.VMEM((2, ATTN_JOBS, KV_CHUNK, t), F32)],
        compiler_params=pltpu.CompilerParams(
            dimension_semantics=("parallel", "arbitrary"), vmem_limit_bytes=VMEM_LIMIT),
        name="attention",
    )(aqt, ak, avt, bqt, bk, bvt, gate, lam_vecs, subln_col)


def _chunk_cumsum(g, row_in_chunk, reverse):
    x = g
    sh = 1
    while sh < GLA_CHUNK:
        if reverse:
            nb = pltpu.roll(x, x.shape[0] - sh, axis=0)
            x = x + jnp.where(row_in_chunk < GLA_CHUNK - sh, nb, 0.0)
        else:
            nb = pltpu.roll(x, sh, axis=0)
            x = x + jnp.where(row_in_chunk >= sh, nb, 0.0)
        sh *= 2
    return x


def _gla_kernel(qk_ref, v_ref, g_ref, gate_ref, ng_ref, o_ref, acc_ref, st_ref):
    n_tiles = qk_ref.shape[1] // GLA_TILE
    ctx_tiles = CTX_LEN // GLA_TILE
    per = GLA_TILE // GLA_CHUNK
    ri = lax.broadcasted_iota(jnp.int32, (GLA_TILE, GLA_TILE), 0)
    ci = lax.broadcasted_iota(jnp.int32, (GLA_TILE, GLA_TILE), 1)
    same_chunk = (ri // GLA_CHUNK) == (ci // GLA_CHUNK)
    chunk_of_row = ri // GLA_CHUNK
    row_in_chunk = ri % GLA_CHUNK
    head_of_k = lax.broadcasted_iota(jnp.int32, (GLA_TILE, GLA_HEADS * GLA_DK), 1) // GLA_DK
    head_of_v = lax.broadcasted_iota(jnp.int32, (GLA_TILE, BRANCH_W), 1) // GLA_DV
    st_mask = ((lax.broadcasted_iota(jnp.int32, (BRANCH_W, GLA_HEADS * GLA_DK), 0) // GLA_DV)
               == (lax.broadcasted_iota(jnp.int32, (BRANCH_W, GLA_HEADS * GLA_DK), 1) // GLA_DK))
    m64 = _group_mean_matrix(BRANCH_W, GLA_DV)
    tris = [jnp.where(same_chunk & (ci <= ri), 1.0, 0.0), jnp.where(same_chunk & (ci >= ri), 1.0, 0.0)]
    tris4 = [jnp.concatenate([t] * GLA_HEADS, axis=1) for t in tris]

    nt_dims = (((1,), (1,)), ((), ()))
    tn_dims = (((0,), (0,)), ((), ()))

    def scan_step(i, carry):
        def tile_index(d, sub):
            t = i * GLA_STEP_TILES + sub
            if not d:
                return t
            return jnp.where(t < ctx_tiles, ctx_tiles - 1 - t, n_tiles + ctx_tiles - 1 - t)

        chains = [(bi, d, sub) for bi in range(GLA_BATCH) for d in (0, 1) for sub in range(GLA_STEP_TILES)]
        nc = len(chains)
        rows = [pl.ds(pl.multiple_of(tile_index(d, sub) * GLA_TILE, GLA_TILE), GLA_TILE)
                for _, d, sub in chains]
        ends = [[c * GLA_CHUNK if d else (c + 1) * GLA_CHUNK - 1 for c in range(per)] for _, d, _ in chains]

        cum = []
        for n, (bi, d, _) in enumerate(chains):
            cum.append(_chunk_cumsum(g_ref[bi, rows[n], 128 * d:128 * d + 128], row_in_chunk, bool(d)))

        qe, kd, kl, vb = [], [], [], []
        for n, (bi, d, _) in enumerate(chains):
            q = qk_ref[bi, rows[n], 0:128]
            k = qk_ref[bi, rows[n], 128:256]
            cum_last = jnp.concatenate(
                [jnp.broadcast_to(cum[n][e:e + 1, :], (GLA_CHUNK, 128)) for e in ends[n]], axis=0)
            qe.append(q * jnp.exp(cum[n]))
            kd.append(k * jnp.exp(-cum[n]))
            kl.append(k * jnp.exp(cum_last - cum[n]))
            vb.append(v_ref[bi, rows[n], :])

        qeb = [x.astype(BF16) for x in qe]
        att = []
        for n, (bi, d, _) in enumerate(chains):
            kd_heads = jnp.concatenate(
                [jnp.where(head_of_k == h, kd[n], 0.0).astype(BF16) for h in range(GLA_HEADS)], axis=0)
            a = lax.dot_general(qeb[n], kd_heads, nt_dims, preferred_element_type=F32)
            att.append((a * tris4[d]).astype(BF16))
        o = []
        for n in range(nc):
            v_heads = jnp.concatenate(
                [jnp.where(head_of_v == h, vb[n], jnp.zeros_like(vb[n])) for h in range(GLA_HEADS)], axis=0)
            o.append(jnp.dot(att[n], v_heads, preferred_element_type=F32))

        upd = []
        for n in range(nc):
            kl_chunks = jnp.concatenate(
                [jnp.where(chunk_of_row == c, kl[n], 0.0).astype(BF16) for c in range(per)], axis=1)
            upd.append(lax.dot_general(vb[n], kl_chunks, tn_dims, preferred_element_type=F32))
        seen = [[None] * per for _ in range(nc)]
        for slot in range(2 * GLA_BATCH):
            st = st_ref[slot]
            for n in range(slot * GLA_STEP_TILES, (slot + 1) * GLA_STEP_TILES):
                for c in (range(per - 1, -1, -1) if slot % 2 else range(per)):
                    seen[n][c] = st.astype(BF16)
                    decay = jnp.exp(cum[n][ends[n][c]:ends[n][c] + 1, :])
                    st = st * decay + jnp.where(st_mask, upd[n][:, c * 128:(c + 1) * 128], 0.0)
            st_ref[slot] = st
        for n in range(nc):
            qe_chunks = jnp.concatenate(
                [jnp.where(chunk_of_row == c, qe[n], 0.0).astype(BF16) for c in range(per)], axis=1)
            o_inter = lax.dot_general(qe_chunks, jnp.concatenate(seen[n], axis=1), nt_dims,
                                      preferred_element_type=F32)
            acc_ref[n // GLA_STEP_TILES, rows[n], :] = o[n] + o_inter
        return carry

    st_ref[...] = jnp.zeros_like(st_ref)
    lax.fori_loop(0, n_tiles // GLA_STEP_TILES, scan_step, 0)

    def finish_step(i, carry):
        rows = pl.ds(pl.multiple_of(i * ROW_TILE, ROW_TILE), ROW_TILE)
        tot = [acc_ref[2 * bi, rows, :] + acc_ref[2 * bi + 1, rows, :] for bi in range(GLA_BATCH)]
        sq = [_split2(t * t) for t in tot]
        ms = [sum(jnp.dot(p, m64, preferred_element_type=F32) for p in s) for s in sq]
        for bi in range(GLA_BATCH):
            y = tot[bi] * lax.rsqrt(ms[bi] + EPS) * ng_ref[...]
            o_ref[bi, rows, :] = (y * gate_ref[bi, rows, :].astype(F32)).astype(BF16)
        return carry

    lax.fori_loop(0, qk_ref.shape[1] // ROW_TILE, finish_step, 0)


def _gla(dqk, dv, dg, gate, ng):
    b, s, _ = dqk.shape
    nb = GLA_BATCH
    blk = lambda width, cb: pl.BlockSpec((nb, s, width), lambda i: (i, 0, cb))
    return pl.pallas_call(
        _gla_kernel,
        grid=(b // nb,),
        in_specs=[blk(256, 0), blk(256, 0), blk(256, 0), blk(256, 2),
                  pl.BlockSpec(ng.shape, lambda i: (0, 0))],
        out_specs=blk(256, 0),
        out_shape=jax.ShapeDtypeStruct((b, s, 256), BF16),
        scratch_shapes=[pltpu.VMEM((2 * nb, s, BRANCH_W), F32),
                        pltpu.VMEM((2 * nb, BRANCH_W, GLA_HEADS * GLA_DK), F32)],
        compiler_params=pltpu.CompilerParams(
            dimension_semantics=("parallel",), vmem_limit_bytes=VMEM_LIMIT),
        name="gla",
    )(dqk, dv, dg, gate, ng)


def _out_kernel(x_ref, mod_ref, ya_ref, yb_ref, yc_ref, yd_ref, w_ref, fg_ref, o_ref):
    for bi in range(x_ref.shape[0]):
        y = jnp.concatenate([ya_ref[bi], yb_ref[bi], yc_ref[bi], yd_ref[bi]], axis=1)
        upd = jnp.dot(y, w_ref[...], preferred_element_type=F32)
        xn = x_ref[bi] + mod_ref[bi, 0, 2:3, :] * upd
        o_ref[bi] = xn * lax.rsqrt(jnp.mean(xn * xn, axis=-1, keepdims=True) + EPS) * fg_ref[...]


def _out_proj(xs, modsel, ya, yb, yc, yd, wo, fg):
    b, s, _ = xs.shape
    t = ROW_TILE
    nb = OUT_BATCH
    skip = CTX_LEN // t
    row = lambda width: pl.BlockSpec((nb, t, width), lambda i, j: (i, j + skip, 0))
    return pl.pallas_call(
        _out_kernel,
        grid=(b // nb, s // t - skip),
        in_specs=[
            row(D_MODEL),
            pl.BlockSpec((nb, 1, 3, D_MODEL), lambda i, j: (i, 1, 0, 0)),
            row(256), row(256), row(256), row(256),
            pl.BlockSpec((None,) + wo.shape[1:], lambda i, j: (wo.shape[0] - 1, 0, 0)),
            pl.BlockSpec(fg.shape, lambda i, j: (0, 0)),
        ],
        out_specs=pl.BlockSpec((nb, t, D_MODEL), lambda i, j: (i, j, 0)),
        out_shape=jax.ShapeDtypeStruct((b, s - skip * t, D_MODEL), F32),
        compiler_params=pltpu.CompilerParams(
            dimension_semantics=("parallel", "parallel"), vmem_limit_bytes=VMEM_LIMIT),
        name="out_proj_final",
    )(xs, modsel, ya, yb, yc, yd, wo, fg)


def _rope_tables(seq, dim, width):
    half = dim // 2
    quarter = half // 2
    lane = np.arange(width) % dim
    freq = ROPE_THETA ** (-(2.0 * (lane % quarter)) / half)
    pos_t = np.arange(seq)
    pos = np.where(lane[None, :] < half, (pos_t // GRID_W)[:, None], (pos_t % GRID_W)[:, None])
    ang = pos * freq[None, :]
    cos, sin = np.cos(ang), np.sin(ang)
    first = (lane % half) < quarter
    s_lo = np.where(first[None, :], -sin, 0.0)
    s_hi = np.where(first[None, :], 0.0, sin)
    lat = np.stack([cos, s_lo, s_hi])
    ctx = np.stack([np.ones((CTX_LEN, width)), np.zeros((CTX_LEN, width)), np.zeros((CTX_LEN, width))])
    return jnp.asarray(np.concatenate([ctx, lat], axis=1), dtype=F32)


def _pack_w_in(w):
    offs = [0]
    for n in (256, 256, 256, 256, 256, 128, 128, 256, 256, 256, 256, 128, 128, 256, 256, 16, 16):
        offs.append(offs[-1] + n)
    seg = [w[..., offs[i]:offs[i + 1]] for i in range(17)]
    aq, ak, av, az, bq, bk, bv, bz, cu, cv, cz, dq, dk, dv, dz, drf, drb = seg
    gap = jnp.zeros(w.shape[:-1] + (LANES // 2,), w.dtype)
    r_gap = jnp.concatenate([drf, drb, gap[..., :LANES // 2 - 2 * GLA_RANK]], axis=-1)
    av_sp = jnp.concatenate([av[..., 0:64], r_gap, av[..., 64:128], gap, av[..., 128:192], gap,
                             av[..., 192:256], gap], axis=-1)
    bvk = jnp.concatenate([bv[..., :64], bk[..., :64], bv[..., 64:], bk[..., 64:]], axis=-1)
    packed = jnp.concatenate([aq, ak, av_sp, az, bq, bvk, bz, cu, cv, cz, dq, dk, dv, dz], axis=-1)
    assert packed.shape[-1] == P_PACK
    return packed.astype(BF16)


def kernel(x, c, ctx, c_ctx, ada_w, ada_b, norm_g, w_in, da_lq1, da_lk1, da_lq2, da_lk2,
           da_subln_g, gq_qnorm_g, gq_knorm_g, sg_ln_g, sg_ln_b, sg_w, sg_b,
           gla_w2_f, gla_b_f, gla_w2_b, gla_b_b, gla_norm_g, w_out, final_norm_g):
    b, seq, d = x.shape
    assert (seq, d, ctx.shape[1]) == (seq // ROW_TILE * ROW_TILE, D_MODEL, CTX_LEN)

    n_mod = 32
    cpad = jnp.zeros((n_mod, d), F32).at[:b].set(c).at[b].set(c_ctx)
    mod = _modulation(cpad, ada_w, ada_b)

    taba = _rope_tables(seq, DA_QK, 256)
    tabb = _rope_tables(seq, GQ_HD, 256)
    fg = final_norm_g.reshape(1, d)
    wp_all, wo_all, sgw_all = _pack_w_in(w_in), w_out.astype(BF16), sg_w.astype(BF16)

    nl = DEPTH
    modsel_all = jnp.concatenate([jnp.broadcast_to(mod[:, b].reshape(nl, 1, 1, 3, d), (nl, b, 1, 3, d)),
                                  mod[:, :b].reshape(nl, b, 1, 3, d)], axis=2)
    w2_all = (jnp.zeros((nl, LANES, 256), F32)
              .at[:, R_LANE:R_LANE + GLA_RANK, 0:128].set(gla_w2_f)
              .at[:, R_LANE + GLA_RANK:R_LANE + 2 * GLA_RANK, 128:256].set(gla_w2_b)).astype(BF16)
    b2_all = jnp.concatenate([gla_b_f, gla_b_b], axis=1).reshape(nl, 1, 256)
    sgb_all = jnp.repeat(jnp.swapaxes(sg_b, 1, 2), BRANCH_W // SG_GROUPS, axis=2)
    tiled = lambda g, n: jnp.tile(g, (1, n)).reshape(nl, 1, g.shape[1] * n)
    gqq_all, gqk_all = tiled(gq_qnorm_g, GQ_HEADS), tiled(gq_knorm_g, GQ_KV_HEADS)
    subln_all, glan_all = tiled(da_subln_g, DA_HEADS), tiled(gla_norm_g, GLA_HEADS)
    lam_all = jnp.stack([da_lq1, da_lk1, da_lq2, da_lk2], axis=1)

    stream_in = (ctx, x)
    for i in range(DEPTH):
        lam_init = 0.8 - 0.6 * math.exp(-0.3 * i)
        (xs, aqt, ak, avt, bqt, bk, bvt, gate, yc, dqk, dv, dg) = _layer(
            i, stream_in, modsel_all[i], norm_g[i].reshape(1, d), wp_all, taba, tabb,
            gqq_all[i], gqk_all[i], sg_ln_g[i].reshape(1, 256), sg_ln_b[i].reshape(1, 256),
            sgw_all, sgb_all[i], w2_all[i], b2_all[i])
        ya, yb = _attention(lam_init, aqt, ak, avt, bqt, bk, bvt, gate, lam_all[i],
                            subln_all[i].reshape(BRANCH_W, 1))
        yd = _gla(dqk, dv, dg, gate, glan_all[i])
        stream_in = (xs, modsel_all[i], ya, yb, yc, yd, wo_all)
    return _out_proj(*stream_in, fg)
```

```python
import functools
import math

import jax
import jax.numpy as jnp
import numpy as np
from jax import lax
from jax.experimental import pallas as pl
from jax.experimental.pallas import tpu as pltpu

F32 = jnp.float32
BF16 = jnp.bfloat16

D_MODEL = 1024
DEPTH = 4
CTX_LEN = 256
GRID_W = 64
BRANCH_W = 256
ROPE_THETA = 10000.0
EPS = 1e-6
DA_HEADS = 4
DA_QK = 32
DA_V = 64
GQ_HEADS = 4
GQ_KV_HEADS = 2
GQ_HD = 64
SG_GROUPS = 4
SG_CHUNK = 128
GLA_HEADS = 4
GLA_DV = 64
GLA_DK = 32
GLA_RANK = 16
GLA_NORMALIZER = 16.0
GLA_CHUNK = 32

LANES = 128
ROW_TILE = 256
KV_CHUNK = 512
LAYER_BATCH = 2
OUT_BATCH = 4
ATTN_BATCH = 2
GLA_TILE = 128
GLA_BATCH = 2
VMEM_LIMIT = 56 * 1024 * 1024

C_AQ, C_AK, C_AV, C_AZ = 0, 256, 512, 1024
C_BQ, C_BVK, C_BZ = 1280, 1536, 1792
C_CU, C_CV, C_CZ = 2048, 2304, 2560
C_DQK, C_DV, C_DZ = 2816, 3072, 3328
P_PACK = 3584
R_LANE = 64
LOG2E = math.log2(math.e)


def _silu(x):
    return x * (1.0 / (1.0 + jnp.exp(-x)))


def _group_mean_matrix(width, group):
    r = lax.broadcasted_iota(jnp.int32, (width, width), 0) // group
    c = lax.broadcasted_iota(jnp.int32, (width, width), 1) // group
    return jnp.where(r == c, 1.0 / group, 0.0).astype(BF16)


def _split2(x):
    hi = x.astype(BF16)
    return hi, (x - hi.astype(F32)).astype(BF16)


def _group_mean(x, mat):
    return sum(jnp.dot(p, mat, preferred_element_type=F32) for p in _split2(x))


def _rope(x, cos, s_lo, s_hi, shift):
    outs = []
    for c in range(x.shape[1] // LANES):
        sl = slice(c * LANES, (c + 1) * LANES)
        xc = x[:, sl]
        up = pltpu.roll(xc, LANES - shift, axis=1)
        dn = pltpu.roll(xc, shift, axis=1)
        outs.append(xc * cos[:, sl] + up * s_lo[:, sl] + dn * s_hi[:, sl])
    return outs[0] if len(outs) == 1 else jnp.concatenate(outs, axis=1)


def _with_ones(v):
    lane = lax.broadcasted_iota(jnp.int32, v.shape, 1) % LANES
    return jnp.where(lane < LANES // 2, v, 1.0)


def _mod_kernel(c_ref, w_ref, b_ref, o_ref):
    s = _silu(c_ref[...]).astype(BF16)
    o_ref[0] = jnp.dot(s, w_ref[0].astype(BF16), preferred_element_type=F32) + b_ref[0]


def _modulation(cpad, ada_w, ada_b):
    n = cpad.shape[0]
    return pl.pallas_call(
        _mod_kernel,
        grid=(DEPTH, 3),
        in_specs=[
            pl.BlockSpec((n, D_MODEL), lambda i, j: (0, 0)),
            pl.BlockSpec((1, D_MODEL, D_MODEL), lambda i, j: (i, 0, j)),
            pl.BlockSpec((1, 1, D_MODEL), lambda i, j: (i, 0, j)),
        ],
        out_specs=pl.BlockSpec((1, n, D_MODEL), lambda i, j: (i, 0, j)),
        out_shape=jax.ShapeDtypeStruct((DEPTH, n, 3 * D_MODEL), F32),
        compiler_params=pltpu.CompilerParams(vmem_limit_bytes=VMEM_LIMIT),
        name="modulation",
    )(cpad, ada_w, ada_b.reshape(DEPTH, 1, 3 * D_MODEL))


def _layer_kernel(first, *refs):
    if first:
        ctx_ref, x_ref = refs[:2]
        refs = refs[2:]
    else:
        xs_ref, modp_ref, ya_ref, yb_ref, yc_in_ref, yd_ref, wo_ref = refs[:7]
        refs = refs[7:]
    (mod_ref, ng_ref, w_ref, taba_ref, tabb_ref, gqq_ref, gqk_ref, lng_ref, lnb_ref, sgw_ref,
     sgb_ref, w2_ref, b2_ref,
     xs_out_ref, aq_ref, ak_ref, av_ref, bq_ref, bk_ref, bv_ref, gate_ref, yc_ref,
     dqk_ref, dv_ref, dg_ref) = refs

    def residual_and_norm(bi):
        if first:
            x = jnp.where(pl.program_id(1) == 0, ctx_ref[bi], x_ref[bi])
        else:
            y_prev = jnp.concatenate([ya_ref[bi], yb_ref[bi], yc_in_ref[bi], yd_ref[bi]], axis=1)
            x = xs_ref[bi] + modp_ref[bi, 0, 2:3, :] * jnp.dot(y_prev, wo_ref[...],
                                                               preferred_element_type=F32)
        xs_out_ref[bi] = x
        shift = mod_ref[bi, 0, 0:1, :]
        scale = mod_ref[bi, 0, 1:2, :]
        y = x * lax.rsqrt(jnp.mean(x * x, axis=-1, keepdims=True) + EPS) * ng_ref[...]
        return (y * (1.0 + scale) + shift).astype(BF16)

    def project(bi, hb):
        def proj(lo, width):
            return jnp.dot(hb, w_ref[:, lo:lo + width], preferred_element_type=F32)

        m64 = _group_mean_matrix(BRANCH_W, GQ_HD)
        half_lane = lax.broadcasted_iota(jnp.int32, (ROW_TILE, LANES), 1) < LANES // 2
        p_bq = proj(C_BQ, 256)
        p_bvk = proj(C_BVK, 256)
        p_cv = proj(C_CV, 256)
        p_av = proj(C_AV, 512)

        bq_sq = _split2(p_bq * p_bq)
        bk = jnp.where(half_lane, pltpu.roll(p_bvk[:, :LANES], LANES // 2, axis=1), p_bvk[:, LANES:])
        bk_sq = _split2(bk * bk)
        mu = jnp.mean(p_cv, axis=-1, keepdims=True)
        cen = p_cv - mu
        var = jnp.mean(cen * cen, axis=-1, keepdims=True)
        vn = (cen * lax.rsqrt(var + EPS) * lng_ref[...] + lnb_ref[...]).astype(BF16)
        r = p_av[:, :LANES].astype(BF16)
        av_ref[bi, 0] = _with_ones(p_av).T.astype(BF16)
        bv_ref[bi, 0] = _with_ones(p_bvk).T.astype(BF16)

        ca, sa_lo, sa_hi = taba_ref[0], taba_ref[1], taba_ref[2]
        aq = _rope(proj(C_AQ, 256), ca, sa_lo, sa_hi, DA_QK // 4) * (DA_QK ** -0.5 * LOG2E)
        aq_ref[bi] = aq.T.astype(BF16)
        ak = _rope(proj(C_AK, 256), ca, sa_lo, sa_hi, DA_QK // 4)
        ak_ref[bi] = ak.astype(BF16)

        bq_ms = sum(jnp.dot(p, m64, preferred_element_type=F32) for p in bq_sq)
        bk_ms = sum(jnp.dot(p, m64[:LANES, :LANES], preferred_element_type=F32) for p in bk_sq)
        gl = jnp.dot(r, w2_ref[...], preferred_element_type=F32) + b2_ref[...]

        gate_ref[bi, :, 0:256] = _silu(proj(C_AZ, 256)).astype(BF16)
        gate_ref[bi, :, 256:512] = _silu(proj(C_BZ, 256)).astype(BF16)
        gate_ref[bi, :, 512:768] = _silu(proj(C_DZ, 256)).astype(BF16)

        lane_group = lax.broadcasted_iota(jnp.int32, (SG_CHUNK, BRANCH_W), 1) // (BRANCH_W // SG_GROUPS)
        mixed = []
        for n in range(ROW_TILE // SG_CHUNK):
            vchunk = vn[n * SG_CHUNK:(n + 1) * SG_CHUNK, :]
            acc = jnp.zeros((SG_CHUNK, BRANCH_W), F32)
            for g in range(SG_GROUPS):
                acc = jnp.where(lane_group == g,
                                jnp.dot(sgw_ref[g], vchunk, preferred_element_type=F32), acc)
            mixed.append(acc + sgb_ref[...])

        cb, sb_lo, sb_hi = tabb_ref[0], tabb_ref[1], tabb_ref[2]
        bq = p_bq * lax.rsqrt(bq_ms + EPS) * gqq_ref[...]
        bq_ref[bi] = (_rope(bq, cb, sb_lo, sb_hi, GQ_HD // 4) * (GQ_HD ** -0.5 * LOG2E)).T.astype(BF16)
        bk = bk * lax.rsqrt(bk_ms + EPS) * gqk_ref[...]
        bk = _rope(bk, cb[:, :LANES], sb_lo[:, :LANES], sb_hi[:, :LANES], GQ_HD // 4)
        bk_ref[bi] = bk.astype(BF16)

        yc = proj(C_CU, 256) * jnp.concatenate(mixed, axis=0) * _silu(proj(C_CZ, 256))
        yc_ref[bi] = yc.astype(BF16)

        q_scale = jnp.where(lax.broadcasted_iota(jnp.int32, (1, 256), 1) < 128, GLA_DK ** -0.5, 1.0)
        dqk_ref[bi] = proj(C_DQK, 256) * q_scale
        dv_ref[bi] = proj(C_DV, 256).astype(BF16)
        log_sig = jnp.minimum(gl, 0.0) - jnp.log(1.0 + jnp.exp(-jnp.abs(gl)))
        dg_ref[bi] = log_sig * (1.0 / GLA_NORMALIZER)

    nb = xs_out_ref.shape[0]
    hbs = [residual_and_norm(bi) for bi in range(nb)]
    for bi in range(nb):
        project(bi, hbs[bi])


def _layer(layer, stream_in, modsel, ng, wp, taba, tabb, gqq, gqk, lng, lnb, sgw, sgb, w2, b2):
    first = layer == 0
    t = ROW_TILE
    stacked = lambda a, idx: pl.BlockSpec((None,) + a.shape[1:], lambda i, j: (idx,) + (0,) * (a.ndim - 1))
    b = stream_in[0].shape[0]
    s = CTX_LEN + stream_in[1].shape[1] if first else stream_in[0].shape[1]
    nb = LAYER_BATCH
    row = lambda width: pl.BlockSpec((nb, t, width), lambda i, j: (i, j, 0))
    colT = lambda height: pl.BlockSpec((nb, height, t), lambda i, j: (i, 0, j))
    full = lambda a: pl.BlockSpec(a.shape, lambda i, j: (0,) * a.ndim)
    tab = pl.BlockSpec((3, t, 256), lambda i, j: (0, j, 0))
    mods = pl.BlockSpec((nb, 1, 3, D_MODEL), lambda i, j: (i, jnp.minimum(j, 1), 0, 0))
    shp = lambda width, dt: jax.ShapeDtypeStruct((b, s, width), dt)
    shpT = lambda height: jax.ShapeDtypeStruct((b, height, s), BF16)
    tileT = lambda height: pl.BlockSpec((nb, 1, height, t), lambda i, j: (i, j, 0, 0))
    if first:
        stream_specs = [pl.BlockSpec((nb, t, D_MODEL), lambda i, j: (i, 0, 0)),
                        pl.BlockSpec((nb, t, D_MODEL), lambda i, j: (i, jnp.maximum(j - 1, 0), 0))]
    else:
        stream_specs = [row(D_MODEL), mods, row(256), row(256), row(256), row(256),
                        stacked(stream_in[6], layer - 1)]
    return pl.pallas_call(
        functools.partial(_layer_kernel, first),
        grid=(b // nb, s // t),
        in_specs=stream_specs + [
            mods, full(ng), stacked(wp, layer), tab, tab, full(gqq), full(gqk), full(lng), full(lnb),
            stacked(sgw, layer), full(sgb), full(w2), full(b2),
        ],
        out_specs=[row(D_MODEL), colT(256), row(256), tileT(512), colT(256), row(128), tileT(256),
                   row(768), row(256), row(256), row(256), row(256)],
        out_shape=[shp(D_MODEL, F32),
                   shpT(256), shp(256, BF16), jax.ShapeDtypeStruct((b, s // t, 512, t), BF16),
                   shpT(256), shp(128, BF16), jax.ShapeDtypeStruct((b, s // t, 256, t), BF16),
                   shp(768, BF16), shp(256, BF16), shp(256, F32), shp(256, BF16), shp(256, F32)],
        compiler_params=pltpu.CompilerParams(
            dimension_semantics=("parallel", "arbitrary"), vmem_limit_bytes=VMEM_LIMIT),
        name="layer_first" if first else "layer",
    )(*stream_in, modsel, ng, wp, taba, tabb, gqq, gqk, lng, lnb, sgw, sgb, w2, b2)


def _max_over_rows(s):
    m = s
    for part in (256, 32):
        if m.shape[0] > part and m.shape[0] % part == 0:
            m = jnp.max(m.reshape(m.shape[0] // part, part, m.shape[1]), axis=0)
    return jnp.max(m, axis=0, keepdims=True)


def _online_softmax_pv(bi, latent, jobs, acc_ref, m_ref, s_ref):
    n_jobs = len(jobs)
    tiles_per_chunk = KV_CHUNK // ROW_TILE
    n_chunks = (jobs[0][1].shape[1] - CTX_LEN) // KV_CHUNK

    def key_loader():
        loaded = {}

        def rows(j, lo, hi):
            k_ref = jobs[j][1]
            if (id(k_ref), lo) not in loaded:
                loaded[(id(k_ref), lo)] = k_ref[bi, lo:hi, :]
            return loaded[(id(k_ref), lo)]

        return rows

    def qk(k_rows, j):
        return jnp.dot(k_rows, jobs[j][0], preferred_element_type=F32)

    def softmax_pv(s, vt, j, first):
        m_new = _max_over_rows(s)
        if not first:
            m_old = m_ref[j]
            m_new = jnp.maximum(m_old, m_new)
            alpha = jnp.exp2(m_old - m_new)
        m_ref[j] = m_new
        upd = jnp.dot(vt, jnp.exp2(s - m_new).astype(BF16), preferred_element_type=F32)
        acc_ref[j] = upd if first else acc_ref[j] * alpha + upd

    def vt_tile(tile, j):
        _, _, vt_ref, v_row = jobs[j]
        return vt_ref[bi, tile, v_row:v_row + LANES, :]

    def vt_chunk(c, j):
        t0 = CTX_LEN // ROW_TILE + c * tiles_per_chunk
        return jnp.concatenate([vt_tile(t0 + i, j) for i in range(tiles_per_chunk)], axis=1)

    def chunk_rows(c):
        return CTX_LEN + c * KV_CHUNK, CTX_LEN + (c + 1) * KV_CHUNK

    @pl.when(jnp.logical_not(latent))
    def _():
        keys = key_loader()
        s = [qk(keys(j, 0, CTX_LEN), j) for j in range(n_jobs)]
        for j in range(n_jobs):
            softmax_pv(s[j], vt_tile(0, j), j, True)

    @pl.when(latent)
    def _():
        keys = key_loader()
        s = [qk(keys(j, 0, CTX_LEN), j) for j in range(n_jobs)]
        s_ref[0, 0] = qk(keys(0, *chunk_rows(0)), 0)
        for j in range(n_jobs):
            softmax_pv(s[j], vt_tile(0, j), j, True)
            if j + 1 < n_jobs:
                s_ref[0, j + 1] = qk(keys(j + 1, *chunk_rows(0)), j + 1)

        for c in range(n_chunks):
            slot = c % 2
            more = c + 1 < n_chunks
            if more:
                s_ref[1 - slot, 0] = qk(keys(0, *chunk_rows(c + 1)), 0)
            for j in range(n_jobs):
                softmax_pv(s_ref[slot, j], vt_chunk(c, j), j, False)
                if more and j + 1 < n_jobs:
                    s_ref[1 - slot, j + 1] = qk(keys(j + 1, *chunk_rows(c + 1)), j + 1)

    outs = []
    for j in range(n_jobs):
        acc = acc_ref[j]
        outs.append(acc[:LANES // 2, :] * (1.0 / acc[LANES // 2:LANES // 2 + 1, :]))
    return outs


def _rows_at(x, row0, total):
    parts = []
    if row0:
        parts.append(jnp.zeros((row0, x.shape[1]), x.dtype))
    parts.append(x)
    if total - row0 - x.shape[0]:
        parts.append(jnp.zeros((total - row0 - x.shape[0], x.shape[1]), x.dtype))
    return jnp.concatenate(parts, axis=0)


ATTN_JOBS = 2 * DA_HEADS + GQ_HEADS


def _attn_kernel(lam_init, aqt_ref, ak_ref, avt_ref, bqt_ref, bk_ref, bvt_ref, gate_ref, lam_ref,
                 sg_ref, ya_ref, yb_ref, acc_ref, m_ref, s_ref):
    lq1, lk1, lq2, lk2 = lam_ref[0:1, :], lam_ref[1:2, :], lam_ref[2:3, :], lam_ref[3:4, :]
    lam = (jnp.exp(jnp.sum(lq1 * lk1, axis=-1, keepdims=True))
           - jnp.exp(jnp.sum(lq2 * lk2, axis=-1, keepdims=True)) + lam_init)
    grp = GQ_HEADS // GQ_KV_HEADS
    for bi in range(aqt_ref.shape[0]):
        jobs = [(_rows_at(aqt_ref[bi, job * DA_QK:(job + 1) * DA_QK, :], job * DA_QK, BRANCH_W),
                 ak_ref, avt_ref, (job // 2) * LANES) for job in range(2 * DA_HEADS)]
        jobs += [(_rows_at(bqt_ref[bi, h * GQ_HD:(h + 1) * GQ_HD, :], (h // grp) * GQ_HD,
                           GQ_KV_HEADS * GQ_HD),
                  bk_ref, bvt_ref, (h // grp) * LANES) for h in range(GQ_HEADS)]
        o = _online_softmax_pv(bi, pl.program_id(1) > 0, jobs, acc_ref, m_ref, s_ref)

        heads = []
        for h in range(DA_HEADS):
            d = o[2 * h] - lam * o[2 * h + 1]
            heads.append(d * lax.rsqrt(jnp.mean(d * d, axis=0, keepdims=True) + EPS))
        yt = jnp.concatenate(heads, axis=0) * (sg_ref[...] * (1.0 - lam_init))
        ya_ref[bi] = (yt.T * gate_ref[bi, :, 0:BRANCH_W].astype(F32)).astype(BF16)
        yb = jnp.concatenate(o[2 * DA_HEADS:], axis=0).T
        yb_ref[bi] = (yb * gate_ref[bi, :, BRANCH_W:2 * BRANCH_W].astype(F32)).astype(BF16)


def _attention(lam_init, aqt, ak, avt, bqt, bk, bvt, gate, lam_vecs, subln_col):
    b, s, _ = ak.shape
    t = ROW_TILE
    full = lambda a: pl.BlockSpec(a.shape, lambda i, j: (0,) * a.ndim)
    nb = ATTN_BATCH
    qt_spec = pl.BlockSpec((nb, BRANCH_W, t), lambda i, j: (i, 0, j))
    keys = lambda a: pl.BlockSpec((nb, s, a.shape[2]), lambda i, j: (i, 0, 0))
    vals = lambda a: pl.BlockSpec((nb,) + a.shape[1:], lambda i, j: (i, 0, 0, 0))
    y_spec = pl.BlockSpec((nb, t, BRANCH_W), lambda i, j: (i, j, 0))
    y_shape = jax.ShapeDtypeStruct((b, s, BRANCH_W), BF16)
    return pl.pallas_call(
        functools.partial(_attn_kernel, lam_init),
        grid=(b // nb, s // t),
        in_specs=[qt_spec, keys(ak), vals(avt), qt_spec, keys(bk), vals(bvt),
                  pl.BlockSpec((nb, t, gate.shape[2]), lambda i, j: (i, j, 0)),
                  full(lam_vecs), full(subln_col)],
        out_specs=[y_spec, y_spec],
        out_shape=[y_shape, y_shape],
        scratch_shapes=[pltpu.VMEM((ATTN_JOBS, LANES, t), F32), pltpu.VMEM((ATTN_JOBS, 1, t), F32),
                        pltpu.VMEM((2, ATTN_JOBS, KV_CHUNK, t), F32)],
        compiler_params=pltpu.CompilerParams(
            dimension_semantics=("parallel", "arbitrary"), vmem_limit_bytes=VMEM_LIMIT),
        name="attention",
    )(aqt, ak, avt, bqt, bk, bvt, gate, lam_vecs, subln_col)


def _chunk_cumsum(g, row_in_chunk, reverse):
    x = g
    sh = 1
    while sh < GLA_CHUNK:
        if reverse:
            nb = pltpu.roll(x, x.shape[0] - sh, axis=0)
            x = x + jnp.where(row_in_chunk < GLA_CHUNK - sh, nb, 0.0)
        else:
            nb = pltpu.roll(x, sh, axis=0)
            x = x + jnp.where(row_in_chunk >= sh, nb, 0.0)
        sh *= 2
    return x


def _gla_kernel(qkf_ref, qkr_ref, vf_ref, vr_ref, gf_ref, gr_ref, gate_ref, ng_ref, o_ref, acc_ref, st_ref):
    n_tiles = gate_ref.shape[1] // GLA_TILE
    n_blocks = gate_ref.shape[1] // ROW_TILE
    step = pl.program_id(1)
    ctx_tiles = CTX_LEN // GLA_TILE
    per = GLA_TILE // GLA_CHUNK
    ri = lax.broadcasted_iota(jnp.int32, (GLA_TILE, GLA_TILE), 0)
    ci = lax.broadcasted_iota(jnp.int32, (GLA_TILE, GLA_TILE), 1)
    same_chunk = (ri // GLA_CHUNK) == (ci // GLA_CHUNK)
    chunk_of_row = ri // GLA_CHUNK
    row_in_chunk = ri % GLA_CHUNK
    head_of_k = lax.broadcasted_iota(jnp.int32, (GLA_TILE, GLA_HEADS * GLA_DK), 1) // GLA_DK
    head_of_v = lax.broadcasted_iota(jnp.int32, (GLA_TILE, BRANCH_W), 1) // GLA_DV
    st_mask = ((lax.broadcasted_iota(jnp.int32, (BRANCH_W, GLA_HEADS * GLA_DK), 0) // GLA_DV)
               == (lax.broadcasted_iota(jnp.int32, (BRANCH_W, GLA_HEADS * GLA_DK), 1) // GLA_DK))
    m64 = _group_mean_matrix(BRANCH_W, GLA_DV)
    tris = [jnp.where(same_chunk & (ci <= ri), 1.0, 0.0), jnp.where(same_chunk & (ci >= ri), 1.0, 0.0)]
    tris4 = [jnp.concatenate([t] * GLA_HEADS, axis=1) for t in tris]

    nt_dims = (((1,), (1,)), ((), ()))
    tn_dims = (((0,), (0,)), ((), ()))

    def scan_step(i, carry):
        tiles = ROW_TILE // GLA_TILE
        ctx_blocks = CTX_LEN // ROW_TILE
        rev_block = jnp.where(step < ctx_blocks, ctx_blocks - 1 - step, n_blocks + ctx_blocks - 1 - step)
        chains = [(bi, d, sub) for bi in range(GLA_BATCH) for d in (0, 1) for sub in range(tiles)]
        nc = len(chains)
        local = [(tiles - 1 - sub if d else sub) * GLA_TILE for _, d, sub in chains]
        src = [slice(lo, lo + GLA_TILE) for lo in local]
        rows = [pl.ds(pl.multiple_of((rev_block if d else step) * ROW_TILE + lo, GLA_TILE), GLA_TILE)
                for (_, d, _), lo in zip(chains, local)]
        ends = [[c * GLA_CHUNK if d else (c + 1) * GLA_CHUNK - 1 for c in range(per)] for _, d, _ in chains]

        cum = []
        for n, (bi, d, _) in enumerate(chains):
            cum.append(_chunk_cumsum((gr_ref if d else gf_ref)[bi, src[n], :], row_in_chunk, bool(d)))

        qe, kd, kl, vb = [], [], [], []
        for n, (bi, d, _) in enumerate(chains):
            qk_ref = qkr_ref if d else qkf_ref
            q = qk_ref[bi, src[n], 0:128]
            k = qk_ref[bi, src[n], 128:256]
            cum_last = jnp.concatenate(
                [jnp.broadcast_to(cum[n][e:e + 1, :], (GLA_CHUNK, 128)) for e in ends[n]], axis=0)
            qe.append(q * jnp.exp(cum[n]))
            kd.append(k * jnp.exp(-cum[n]))
            kl.append(k * jnp.exp(cum_last - cum[n]))
            vb.append((vr_ref if d else vf_ref)[bi, src[n], :])

        qeb = [x.astype(BF16) for x in qe]
        att = []
        for n, (bi, d, _) in enumerate(chains):
            kd_heads = jnp.concatenate(
                [jnp.where(head_of_k == h, kd[n], 0.0).astype(BF16) for h in range(GLA_HEADS)], axis=0)
            a = lax.dot_general(qeb[n], kd_heads, nt_dims, preferred_element_type=F32)
            att.append((a * tris4[d]).astype(BF16))
        o = []
        for n in range(nc):
            v_heads = jnp.concatenate(
                [jnp.where(head_of_v == h, vb[n], jnp.zeros_like(vb[n])) for h in range(GLA_HEADS)], axis=0)
            o.append(jnp.dot(att[n], v_heads, preferred_element_type=F32))

        upd = []
        for n in range(nc):
            kl_chunks = jnp.concatenate(
                [jnp.where(chunk_of_row == c, kl[n], 0.0).astype(BF16) for c in range(per)], axis=1)
            upd.append(lax.dot_general(vb[n], kl_chunks, tn_dims, preferred_element_type=F32))
        seen = [[None] * per for _ in range(nc)]
        for slot in range(2 * GLA_BATCH):
            st = st_ref[slot]
            for n in range(slot * tiles, (slot + 1) * tiles):
                for c in (range(per - 1, -1, -1) if slot % 2 else range(per)):
                    seen[n][c] = st.astype(BF16)
                    decay = jnp.exp(cum[n][ends[n][c]:ends[n][c] + 1, :])
                    st = st * decay + jnp.where(st_mask, upd[n][:, c * 128:(c + 1) * 128], 0.0)
            st_ref[slot] = st
        for n in range(nc):
            qe_chunks = jnp.concatenate(
                [jnp.where(chunk_of_row == c, qe[n], 0.0).astype(BF16) for c in range(per)], axis=1)
            o_inter = lax.dot_general(qe_chunks, jnp.concatenate(seen[n], axis=1), nt_dims,
                                      preferred_element_type=F32)
            acc_ref[n // tiles, rows[n], :] = o[n] + o_inter
        return carry

    @pl.when(step == 0)
    def _():
        st_ref[...] = jnp.zeros_like(st_ref)

    scan_step(step, 0)

    def finish_step(i, carry):
        rows = pl.ds(pl.multiple_of(i * ROW_TILE, ROW_TILE), ROW_TILE)
        tot = [acc_ref[2 * bi, rows, :] + acc_ref[2 * bi + 1, rows, :] for bi in range(GLA_BATCH)]
        sq = [_split2(t * t) for t in tot]
        ms = [sum(jnp.dot(p, m64, preferred_element_type=F32) for p in s) for s in sq]
        for bi in range(GLA_BATCH):
            y = tot[bi] * lax.rsqrt(ms[bi] + EPS) * ng_ref[...]
            o_ref[bi, rows, :] = (y * gate_ref[bi, rows, :].astype(F32)).astype(BF16)
        return carry

    @pl.when(step == n_blocks - 1)
    def _():
        lax.fori_loop(0, n_blocks, finish_step, 0)


def _gla(dqk, dv, dg, gate, ng):
    b, s, _ = dqk.shape
    nb = GLA_BATCH
    t = ROW_TILE
    n_blocks, ctx_blocks = s // t, CTX_LEN // t
    fwd = lambda p: p
    rev = lambda p: jnp.where(p < ctx_blocks, ctx_blocks - 1 - p, n_blocks + ctx_blocks - 1 - p)
    stream = lambda width, order, cb: pl.BlockSpec((nb, t, width), lambda i, p: (i, order(p), cb))
    whole = lambda cb: pl.BlockSpec((nb, s, 256), lambda i, p: (i, 0, cb))
    return pl.pallas_call(
        _gla_kernel,
        grid=(b // nb, n_blocks),
        in_specs=[stream(256, fwd, 0), stream(256, rev, 0), stream(256, fwd, 0), stream(256, rev, 0),
                  stream(128, fwd, 0), stream(128, rev, 1), whole(2),
                  pl.BlockSpec(ng.shape, lambda i, p: (0, 0))],
        out_specs=whole(0),
        out_shape=jax.ShapeDtypeStruct((b, s, 256), BF16),
        scratch_shapes=[pltpu.VMEM((2 * nb, s, BRANCH_W), F32),
                        pltpu.VMEM((2 * nb, BRANCH_W, GLA_HEADS * GLA_DK), F32)],
        compiler_params=pltpu.CompilerParams(
            dimension_semantics=("parallel", "arbitrary"), vmem_limit_bytes=VMEM_LIMIT),
        name="gla",
    )(dqk, dqk, dv, dv, dg, dg, gate, ng)


def _out_kernel(x_ref, mod_ref, ya_ref, yb_ref, yc_ref, yd_ref, w_ref, fg_ref, o_ref):
    for bi in range(x_ref.shape[0]):
        y = jnp.concatenate([ya_ref[bi], yb_ref[bi], yc_ref[bi], yd_ref[bi]], axis=1)
        upd = jnp.dot(y, w_ref[...], preferred_element_type=F32)
        xn = x_ref[bi] + mod_ref[bi, 0, 2:3, :] * upd
        o_ref[bi] = xn * lax.rsqrt(jnp.mean(xn * xn, axis=-1, keepdims=True) + EPS) * fg_ref[...]


def _out_proj(xs, modsel, ya, yb, yc, yd, wo, fg):
    b, s, _ = xs.shape
    t = ROW_TILE
    nb = OUT_BATCH
    skip = CTX_LEN // t
    row = lambda width: pl.BlockSpec((nb, t, width), lambda i, j: (i, j + skip, 0))
    return pl.pallas_call(
        _out_kernel,
        grid=(b // nb, s // t - skip),
        in_specs=[
            row(D_MODEL),
            pl.BlockSpec((nb, 1, 3, D_MODEL), lambda i, j: (i, 1, 0, 0)),
            row(256), row(256), row(256), row(256),
            pl.BlockSpec((None,) + wo.shape[1:], lambda i, j: (wo.shape[0] - 1, 0, 0)),
            pl.BlockSpec(fg.shape, lambda i, j: (0, 0)),
        ],
        out_specs=pl.BlockSpec((nb, t, D_MODEL), lambda i, j: (i, j, 0)),
        out_shape=jax.ShapeDtypeStruct((b, s - skip * t, D_MODEL), F32),
        compiler_params=pltpu.CompilerParams(
            dimension_semantics=("parallel", "parallel"), vmem_limit_bytes=VMEM_LIMIT),
        name="out_proj_final",
    )(xs, modsel, ya, yb, yc, yd, wo, fg)


def _rope_tables(seq, dim, width):
    half = dim // 2
    quarter = half // 2
    lane = np.arange(width) % dim
    freq = ROPE_THETA ** (-(2.0 * (lane % quarter)) / half)
    pos_t = np.arange(seq)
    pos = np.where(lane[None, :] < half, (pos_t // GRID_W)[:, None], (pos_t % GRID_W)[:, None])
    ang = pos * freq[None, :]
    cos, sin = np.cos(ang), np.sin(ang)
    first = (lane % half) < quarter
    s_lo = np.where(first[None, :], -sin, 0.0)
    s_hi = np.where(first[None, :], 0.0, sin)
    lat = np.stack([cos, s_lo, s_hi])
    ctx = np.stack([np.ones((CTX_LEN, width)), np.zeros((CTX_LEN, width)), np.zeros((CTX_LEN, width))])
    return jnp.asarray(np.concatenate([ctx, lat], axis=1), dtype=F32)


def _pack_w_in(w):
    offs = [0]
    for n in (256, 256, 256, 256, 256, 128, 128, 256, 256, 256, 256, 128, 128, 256, 256, 16, 16):
        offs.append(offs[-1] + n)
    seg = [w[..., offs[i]:offs[i + 1]] for i in range(17)]
    aq, ak, av, az, bq, bk, bv, bz, cu, cv, cz, dq, dk, dv, dz, drf, drb = seg
    gap = jnp.zeros(w.shape[:-1] + (LANES // 2,), w.dtype)
    r_gap = jnp.concatenate([drf, drb, gap[..., :LANES // 2 - 2 * GLA_RANK]], axis=-1)
    av_sp = jnp.concatenate([av[..., 0:64], r_gap, av[..., 64:128], gap, av[..., 128:192], gap,
                             av[..., 192:256], gap], axis=-1)
    bvk = jnp.concatenate([bv[..., :64], bk[..., :64], bv[..., 64:], bk[..., 64:]], axis=-1)
    packed = jnp.concatenate([aq, ak, av_sp, az, bq, bvk, bz, cu, cv, cz, dq, dk, dv, dz], axis=-1)
    assert packed.shape[-1] == P_PACK
    return packed.astype(BF16)


def kernel(x, c, ctx, c_ctx, ada_w, ada_b, norm_g, w_in, da_lq1, da_lk1, da_lq2, da_lk2,
           da_subln_g, gq_qnorm_g, gq_knorm_g, sg_ln_g, sg_ln_b, sg_w, sg_b,
           gla_w2_f, gla_b_f, gla_w2_b, gla_b_b, gla_norm_g, w_out, final_norm_g):
    b, seq, d = x.shape
    assert (seq, d, ctx.shape[1]) == (seq // ROW_TILE * ROW_TILE, D_MODEL, CTX_LEN)

    n_mod = 32
    cpad = jnp.zeros((n_mod, d), F32).at[:b].set(c).at[b].set(c_ctx)
    mod = _modulation(cpad, ada_w, ada_b)

    taba = _rope_tables(seq, DA_QK, 256)
    tabb = _rope_tables(seq, GQ_HD, 256)
    fg = final_norm_g.reshape(1, d)
    wp_all, wo_all, sgw_all = _pack_w_in(w_in), w_out.astype(BF16), sg_w.astype(BF16)

    nl = DEPTH
    modsel_all = jnp.concatenate([jnp.broadcast_to(mod[:, b].reshape(nl, 1, 1, 3, d), (nl, b, 1, 3, d)),
                                  mod[:, :b].reshape(nl, b, 1, 3, d)], axis=2)
    w2_all = (jnp.zeros((nl, LANES, 256), F32)
              .at[:, R_LANE:R_LANE + GLA_RANK, 0:128].set(gla_w2_f)
              .at[:, R_LANE + GLA_RANK:R_LANE + 2 * GLA_RANK, 128:256].set(gla_w2_b)).astype(BF16)
    b2_all = jnp.concatenate([gla_b_f, gla_b_b], axis=1).reshape(nl, 1, 256)
    sgb_all = jnp.repeat(jnp.swapaxes(sg_b, 1, 2), BRANCH_W // SG_GROUPS, axis=2)
    tiled = lambda g, n: jnp.tile(g, (1, n)).reshape(nl, 1, g.shape[1] * n)
    gqq_all, gqk_all = tiled(gq_qnorm_g, GQ_HEADS), tiled(gq_knorm_g, GQ_KV_HEADS)
    subln_all, glan_all = tiled(da_subln_g, DA_HEADS), tiled(gla_norm_g, GLA_HEADS)
    lam_all = jnp.stack([da_lq1, da_lk1, da_lq2, da_lk2], axis=1)

    stream_in = (ctx, x)
    for i in range(DEPTH):
        lam_init = 0.8 - 0.6 * math.exp(-0.3 * i)
        (xs, aqt, ak, avt, bqt, bk, bvt, gate, yc, dqk, dv, dg) = _layer(
            i, stream_in, modsel_all[i], norm_g[i].reshape(1, d), wp_all, taba, tabb,
            gqq_all[i], gqk_all[i], sg_ln_g[i].reshape(1, 256), sg_ln_b[i].reshape(1, 256),
            sgw_all, sgb_all[i], w2_all[i], b2_all[i])
        ya, yb = _attention(lam_init, aqt, ak, avt, bqt, bk, bvt, gate, lam_all[i],
                            subln_all[i].reshape(BRANCH_W, 1))
        yd = _gla(dqk, dv, dg, gate, glan_all[i])
        stream_in = (xs, modsel_all[i], ya, yb, yc, yd, wo_all)
    return _out_proj(*stream_in, fg)
```

```python
import functools
import math

import jax
import jax.numpy as jnp
import numpy as np
from jax import lax
from jax.experimental import pallas as pl
from jax.experimental.pallas import tpu as pltpu

F32 = jnp.float32
BF16 = jnp.bfloat16

D_MODEL = 1024
DEPTH = 4
CTX_LEN = 256
GRID_W = 64
BRANCH_W = 256
ROPE_THETA = 10000.0
EPS = 1e-6
DA_HEADS = 4
DA_QK = 32
DA_V = 64
GQ_HEADS = 4
GQ_KV_HEADS = 2
GQ_HD = 64
SG_GROUPS = 4
SG_CHUNK = 128
GLA_HEADS = 4
GLA_DV = 64
GLA_DK = 32
GLA_RANK = 16
GLA_NORMALIZER = 16.0
GLA_CHUNK = 32

LANES = 128
ROW_TILE = 256
KV_CHUNK = 512
LAYER_BATCH = 2
OUT_BATCH = 4
ATTN_BATCH = 2
GLA_TILE = 128
GLA_BATCH = 2
GLA_STEP_TILES = 3
VMEM_LIMIT = 56 * 1024 * 1024

C_AQ, C_AK, C_AV, C_AZ = 0, 256, 512, 1024
C_BQ, C_BVK, C_BZ = 1280, 1536, 1792
C_CU, C_CV, C_CZ = 2048, 2304, 2560
C_DQK, C_DV, C_DZ = 2816, 3072, 3328
P_PACK = 3584
R_LANE = 64
LOG2E = math.log2(math.e)


def _silu(x):
    return x * (1.0 / (1.0 + jnp.exp(-x)))


def _group_mean_matrix(width, group):
    r = lax.broadcasted_iota(jnp.int32, (width, width), 0) // group
    c = lax.broadcasted_iota(jnp.int32, (width, width), 1) // group
    return jnp.where(r == c, 1.0 / group, 0.0).astype(BF16)


def _split2(x):
    hi = x.astype(BF16)
    return hi, (x - hi.astype(F32)).astype(BF16)


def _group_mean(x, mat):
    return sum(jnp.dot(p, mat, preferred_element_type=F32) for p in _split2(x))


def _rope(x, cos, s_lo, s_hi, shift):
    outs = []
    for c in range(x.shape[1] // LANES):
        sl = slice(c * LANES, (c + 1) * LANES)
        xc = x[:, sl]
        up = pltpu.roll(xc, LANES - shift, axis=1)
        dn = pltpu.roll(xc, shift, axis=1)
        outs.append(xc * cos[:, sl] + up * s_lo[:, sl] + dn * s_hi[:, sl])
    return outs[0] if len(outs) == 1 else jnp.concatenate(outs, axis=1)


def _with_ones(v):
    lane = lax.broadcasted_iota(jnp.int32, v.shape, 1) % LANES
    return jnp.where(lane < LANES // 2, v, 1.0)


def _mod_kernel(c_ref, w_ref, b_ref, o_ref):
    s = _silu(c_ref[...]).astype(BF16)
    o_ref[0] = jnp.dot(s, w_ref[0].astype(BF16), preferred_element_type=F32) + b_ref[0]


def _modulation(cpad, ada_w, ada_b):
    n = cpad.shape[0]
    return pl.pallas_call(
        _mod_kernel,
        grid=(DEPTH, 3),
        in_specs=[
            pl.BlockSpec((n, D_MODEL), lambda i, j: (0, 0)),
            pl.BlockSpec((1, D_MODEL, D_MODEL), lambda i, j: (i, 0, j)),
            pl.BlockSpec((1, 1, D_MODEL), lambda i, j: (i, 0, j)),
        ],
        out_specs=pl.BlockSpec((1, n, D_MODEL), lambda i, j: (i, 0, j)),
        out_shape=jax.ShapeDtypeStruct((DEPTH, n, 3 * D_MODEL), F32),
        compiler_params=pltpu.CompilerParams(vmem_limit_bytes=VMEM_LIMIT),
        name="modulation",
    )(cpad, ada_w, ada_b.reshape(DEPTH, 1, 3 * D_MODEL))


def _layer_kernel(first, *refs):
    if first:
        ctx_ref, x_ref = refs[:2]
        refs = refs[2:]
    else:
        xs_ref, modp_ref, ya_ref, yb_ref, yc_in_ref, yd_ref, wo_ref = refs[:7]
        refs = refs[7:]
    (mod_ref, ng_ref, w_ref, taba_ref, tabb_ref, gqq_ref, gqk_ref, lng_ref, lnb_ref, sgw_ref,
     sgb_ref, w2_ref, b2_ref,
     xs_out_ref, aq_ref, ak_ref, av_ref, bq_ref, bk_ref, bv_ref, gate_ref, yc_ref,
     dqk_ref, dv_ref, dg_ref) = refs

    def residual_and_norm(bi):
        if first:
            x = jnp.where(pl.program_id(1) == 0, ctx_ref[bi], x_ref[bi])
        else:
            y_prev = jnp.concatenate([ya_ref[bi], yb_ref[bi], yc_in_ref[bi], yd_ref[bi]], axis=1)
            x = xs_ref[bi] + modp_ref[bi, 0, 2:3, :] * jnp.dot(y_prev, wo_ref[...],
                                                               preferred_element_type=F32)
        xs_out_ref[bi] = x
        shift = mod_ref[bi, 0, 0:1, :]
        scale = mod_ref[bi, 0, 1:2, :]
        y = x * lax.rsqrt(jnp.mean(x * x, axis=-1, keepdims=True) + EPS) * ng_ref[...]
        return (y * (1.0 + scale) + shift).astype(BF16)

    def project(bi, hb):
        def proj(lo, width):
            return jnp.dot(hb, w_ref[:, lo:lo + width], preferred_element_type=F32)

        m64 = _group_mean_matrix(BRANCH_W, GQ_HD)
        half_lane = lax.broadcasted_iota(jnp.int32, (ROW_TILE, LANES), 1) < LANES // 2
        p_bq = proj(C_BQ, 256)
        p_bvk = proj(C_BVK, 256)
        p_cv = proj(C_CV, 256)
        p_av = proj(C_AV, 512)

        bq_sq = _split2(p_bq * p_bq)
        bk = jnp.where(half_lane, pltpu.roll(p_bvk[:, :LANES], LANES // 2, axis=1), p_bvk[:, LANES:])
        bk_sq = _split2(bk * bk)
        mu = jnp.mean(p_cv, axis=-1, keepdims=True)
        cen = p_cv - mu
        var = jnp.mean(cen * cen, axis=-1, keepdims=True)
        vn = (cen * lax.rsqrt(var + EPS) * lng_ref[...] + lnb_ref[...]).astype(BF16)
        r = p_av[:, :LANES].astype(BF16)
        av_ref[bi, 0] = _with_ones(p_av).T.astype(BF16)
        bv_ref[bi, 0] = _with_ones(p_bvk).T.astype(BF16)

        ca, sa_lo, sa_hi = taba_ref[0], taba_ref[1], taba_ref[2]
        aq = _rope(proj(C_AQ, 256), ca, sa_lo, sa_hi, DA_QK // 4) * (DA_QK ** -0.5 * LOG2E)
        aq_ref[bi] = aq.T.astype(BF16)
        ak = _rope(proj(C_AK, 256), ca, sa_lo, sa_hi, DA_QK // 4)
        ak_ref[bi] = ak.astype(BF16)

        bq_ms = sum(jnp.dot(p, m64, preferred_element_type=F32) for p in bq_sq)
        bk_ms = sum(jnp.dot(p, m64[:LANES, :LANES], preferred_element_type=F32) for p in bk_sq)
        gl = jnp.dot(r, w2_ref[...], preferred_element_type=F32) + b2_ref[...]

        gate_ref[bi, :, 0:256] = _silu(proj(C_AZ, 256)).astype(BF16)
        gate_ref[bi, :, 256:512] = _silu(proj(C_BZ, 256)).astype(BF16)
        gate_ref[bi, :, 512:768] = _silu(proj(C_DZ, 256)).astype(BF16)

        lane_group = lax.broadcasted_iota(jnp.int32, (SG_CHUNK, BRANCH_W), 1) // (BRANCH_W // SG_GROUPS)
        mixed = []
        for n in range(ROW_TILE // SG_CHUNK):
            vchunk = vn[n * SG_CHUNK:(n + 1) * SG_CHUNK, :]
            acc = jnp.zeros((SG_CHUNK, BRANCH_W), F32)
            for g in range(SG_GROUPS):
                acc = jnp.where(lane_group == g,
                                jnp.dot(sgw_ref[g], vchunk, preferred_element_type=F32), acc)
            mixed.append(acc + sgb_ref[...])

        cb, sb_lo, sb_hi = tabb_ref[0], tabb_ref[1], tabb_ref[2]
        bq = p_bq * lax.rsqrt(bq_ms + EPS) * gqq_ref[...]
        bq_ref[bi] = (_rope(bq, cb, sb_lo, sb_hi, GQ_HD // 4) * (GQ_HD ** -0.5 * LOG2E)).T.astype(BF16)
        bk = bk * lax.rsqrt(bk_ms + EPS) * gqk_ref[...]
        bk = _rope(bk, cb[:, :LANES], sb_lo[:, :LANES], sb_hi[:, :LANES], GQ_HD // 4)
        bk_ref[bi] = bk.astype(BF16)

        yc = proj(C_CU, 256) * jnp.concatenate(mixed, axis=0) * _silu(proj(C_CZ, 256))
        yc_ref[bi] = yc.astype(BF16)

        q_scale = jnp.where(lax.broadcasted_iota(jnp.int32, (1, 256), 1) < 128, GLA_DK ** -0.5, 1.0)
        dqk_ref[bi] = proj(C_DQK, 256) * q_scale
        dv_ref[bi] = proj(C_DV, 256).astype(BF16)
        log_sig = jnp.minimum(gl, 0.0) - jnp.log(1.0 + jnp.exp(-jnp.abs(gl)))
        dg_ref[bi] = log_sig * (1.0 / GLA_NORMALIZER)

    nb = xs_out_ref.shape[0]
    hbs = [residual_and_norm(bi) for bi in range(nb)]
    for bi in range(nb):
        project(bi, hbs[bi])


def _layer(layer, stream_in, modsel, ng, wp, taba, tabb, gqq, gqk, lng, lnb, sgw, sgb, w2, b2):
    first = layer == 0
    t = ROW_TILE
    stacked = lambda a, idx: pl.BlockSpec((None,) + a.shape[1:], lambda i, j: (idx,) + (0,) * (a.ndim - 1),
                                          pipeline_mode=pl.Buffered(1))
    b = stream_in[0].shape[0]
    s = CTX_LEN + stream_in[1].shape[1] if first else stream_in[0].shape[1]
    nb = LAYER_BATCH
    row = lambda width: pl.BlockSpec((nb, t, width), lambda i, j: (i, j, 0))
    colT = lambda height: pl.BlockSpec((nb, height, t), lambda i, j: (i, 0, j))
    full = lambda a: pl.BlockSpec(a.shape, lambda i, j: (0,) * a.ndim)
    tab = pl.BlockSpec((3, t, 256), lambda i, j: (0, j, 0))
    mods = pl.BlockSpec((nb, 1, 3, D_MODEL), lambda i, j: (i, jnp.minimum(j, 1), 0, 0))
    shp = lambda width, dt: jax.ShapeDtypeStruct((b, s, width), dt)
    shpT = lambda height: jax.ShapeDtypeStruct((b, height, s), BF16)
    tileT = lambda height: pl.BlockSpec((nb, 1, height, t), lambda i, j: (i, j, 0, 0))
    if first:
        stream_specs = [pl.BlockSpec((nb, t, D_MODEL), lambda i, j: (i, 0, 0)),
                        pl.BlockSpec((nb, t, D_MODEL), lambda i, j: (i, jnp.maximum(j - 1, 0), 0))]
    else:
        stream_specs = [row(D_MODEL), mods, row(256), row(256), row(256), row(256),
                        stacked(stream_in[6], layer - 1)]
    return pl.pallas_call(
        functools.partial(_layer_kernel, first),
        grid=(b // nb, s // t),
        in_specs=stream_specs + [
            mods, full(ng), stacked(wp, layer), tab, tab, full(gqq), full(gqk), full(lng), full(lnb),
            stacked(sgw, layer), full(sgb), full(w2), full(b2),
        ],
        out_specs=[row(D_MODEL), colT(256), row(256), tileT(512), colT(256), row(128), tileT(256),
                   row(768), row(256), row(256), row(256), row(256)],
        out_shape=[shp(D_MODEL, F32),
                   shpT(256), shp(256, BF16), jax.ShapeDtypeStruct((b, s // t, 512, t), BF16),
                   shpT(256), shp(128, BF16), jax.ShapeDtypeStruct((b, s // t, 256, t), BF16),
                   shp(768, BF16), shp(256, BF16), shp(256, F32), shp(256, BF16), shp(256, F32)],
        compiler_params=pltpu.CompilerParams(
            dimension_semantics=("parallel", "arbitrary"), vmem_limit_bytes=VMEM_LIMIT),
        name="layer_first" if first else "layer",
    )(*stream_in, modsel, ng, wp, taba, tabb, gqq, gqk, lng, lnb, sgw, sgb, w2, b2)


def _max_over_rows(s):
    m = s
    for part in (256, 32):
        if m.shape[0] > part and m.shape[0] % part == 0:
            m = jnp.max(m.reshape(m.shape[0] // part, part, m.shape[1]), axis=0)
    return jnp.max(m, axis=0, keepdims=True)


def _online_softmax_pv(bi, latent, jobs, acc_ref, m_ref, s_ref):
    n_jobs = len(jobs)
    tiles_per_chunk = KV_CHUNK // ROW_TILE
    n_chunks = (jobs[0][1].shape[1] - CTX_LEN) // KV_CHUNK

    def key_loader():
        loaded = {}

        def rows(j, lo, hi):
            k_ref = jobs[j][1]
            if (id(k_ref), lo) not in loaded:
                loaded[(id(k_ref), lo)] = k_ref[bi, lo:hi, :]
            return loaded[(id(k_ref), lo)]

        return rows

    def qk(k_rows, j):
        return jnp.dot(k_rows, jobs[j][0], preferred_element_type=F32)

    def softmax_pv(s, vt, j, first):
        m_new = _max_over_rows(s)
        if not first:
            m_old = m_ref[j]
            m_new = jnp.maximum(m_old, m_new)
            alpha = jnp.exp2(m_old - m_new)
        m_ref[j] = m_new
        upd = jnp.dot(vt, jnp.exp2(s - m_new).astype(BF16), preferred_element_type=F32)
        acc_ref[j] = upd if first else acc_ref[j] * alpha + upd

    def vt_tile(tile, j):
        _, _, vt_ref, v_row = jobs[j]
        return vt_ref[bi, tile, v_row:v_row + LANES, :]

    def vt_chunk(c, j):
        t0 = CTX_LEN // ROW_TILE + c * tiles_per_chunk
        return jnp.concatenate([vt_tile(t0 + i, j) for i in range(tiles_per_chunk)], axis=1)

    def chunk_rows(c):
        return CTX_LEN + c * KV_CHUNK, CTX_LEN + (c + 1) * KV_CHUNK

    @pl.when(jnp.logical_not(latent))
    def _():
        keys = key_loader()
        s = [qk(keys(j, 0, CTX_LEN), j) for j in range(n_jobs)]
        for j in range(n_jobs):
            softmax_pv(s[j], vt_tile(0, j), j, True)

    @pl.when(latent)
    def _():
        keys = key_loader()
        s = [qk(keys(j, 0, CTX_LEN), j) for j in range(n_jobs)]
        s_ref[0, 0] = qk(keys(0, *chunk_rows(0)), 0)
        for j in range(n_jobs):
            softmax_pv(s[j], vt_tile(0, j), j, True)
            if j + 1 < n_jobs:
                s_ref[0, j + 1] = qk(keys(j + 1, *chunk_rows(0)), j + 1)

        for c in range(n_chunks):
            slot = c % 2
            more = c + 1 < n_chunks
            if more:
                s_ref[1 - slot, 0] = qk(keys(0, *chunk_rows(c + 1)), 0)
            for j in range(n_jobs):
                softmax_pv(s_ref[slot, j], vt_chunk(c, j), j, False)
                if more and j + 1 < n_jobs:
                    s_ref[1 - slot, j + 1] = qk(keys(j + 1, *chunk_rows(c + 1)), j + 1)

    outs = []
    for j in range(n_jobs):
        acc = acc_ref[j]
        outs.append(acc[:LANES // 2, :] * (1.0 / acc[LANES // 2:LANES // 2 + 1, :]))
    return outs


def _rows_at(x, row0, total):
    parts = []
    if row0:
        parts.append(jnp.zeros((row0, x.shape[1]), x.dtype))
    parts.append(x)
    if total - row0 - x.shape[0]:
        parts.append(jnp.zeros((total - row0 - x.shape[0], x.shape[1]), x.dtype))
    return jnp.concatenate(parts, axis=0)


ATTN_JOBS = 2 * DA_HEADS + GQ_HEADS


def _attn_kernel(lam_init, aqt_ref, ak_ref, avt_ref, bqt_ref, bk_ref, bvt_ref, gate_ref, lam_ref,
                 sg_ref, ya_ref, yb_ref, acc_ref, m_ref, s_ref):
    lq1, lk1, lq2, lk2 = lam_ref[0:1, :], lam_ref[1:2, :], lam_ref[2:3, :], lam_ref[3:4, :]
    lam = (jnp.exp(jnp.sum(lq1 * lk1, axis=-1, keepdims=True))
           - jnp.exp(jnp.sum(lq2 * lk2, axis=-1, keepdims=True)) + lam_init)
    grp = GQ_HEADS // GQ_KV_HEADS
    for bi in range(aqt_ref.shape[0]):
        jobs = [(_rows_at(aqt_ref[bi, job * DA_QK:(job + 1) * DA_QK, :], job * DA_QK, BRANCH_W),
                 ak_ref, avt_ref, (job // 2) * LANES) for job in range(2 * DA_HEADS)]
        jobs += [(_rows_at(bqt_ref[bi, h * GQ_HD:(h + 1) * GQ_HD, :], (h // grp) * GQ_HD,
                           GQ_KV_HEADS * GQ_HD),
                  bk_ref, bvt_ref, (h // grp) * LANES) for h in range(GQ_HEADS)]
        o = _online_softmax_pv(bi, pl.program_id(1) > 0, jobs, acc_ref, m_ref, s_ref)

        heads = []
        for h in range(DA_HEADS):
            d = o[2 * h] - lam * o[2 * h + 1]
            heads.append(d * lax.rsqrt(jnp.mean(d * d, axis=0, keepdims=True) + EPS))
        yt = jnp.concatenate(heads, axis=0) * (sg_ref[...] * (1.0 - lam_init))
        ya_ref[bi] = (yt.T * gate_ref[bi, :, 0:BRANCH_W].astype(F32)).astype(BF16)
        yb = jnp.concatenate(o[2 * DA_HEADS:], axis=0).T
        yb_ref[bi] = (yb * gate_ref[bi, :, BRANCH_W:2 * BRANCH_W].astype(F32)).astype(BF16)


def _attention(lam_init, aqt, ak, avt, bqt, bk, bvt, gate, lam_vecs, subln_col):
    b, s, _ = ak.shape
    t = ROW_TILE
    full = lambda a: pl.BlockSpec(a.shape, lambda i, j: (0,) * a.ndim)
    nb = ATTN_BATCH
    qt_spec = pl.BlockSpec((nb, BRANCH_W, t), lambda i, j: (i, 0, j))
    keys = lambda a: pl.BlockSpec((nb, s, a.shape[2]), lambda i, j: (i, 0, 0))
    vals = lambda a: pl.BlockSpec((nb,) + a.shape[1:], lambda i, j: (i, 0, 0, 0))
    y_spec = pl.BlockSpec((nb, t, BRANCH_W), lambda i, j: (i, j, 0))
    y_shape = jax.ShapeDtypeStruct((b, s, BRANCH_W), BF16)
    return pl.pallas_call(
        functools.partial(_attn_kernel, lam_init),
        grid=(b // nb, s // t),
        in_specs=[qt_spec, keys(ak), vals(avt), qt_spec, keys(bk), vals(bvt),
                  pl.BlockSpec((nb, t, gate.shape[2]), lambda i, j: (i, j, 0)),
                  full(lam_vecs), full(subln_col)],
        out_specs=[y_spec, y_spec],
        out_shape=[y_shape, y_shape],
        scratch_shapes=[pltpu.VMEM((ATTN_JOBS, LANES, t), F32), pltpu.VMEM((ATTN_JOBS, 1, t), F32),
                        pltpu.VMEM((2, ATTN_JOBS, KV_CHUNK, t), F32)],
        compiler_params=pltpu.CompilerParams(
            dimension_semantics=("parallel", "arbitrary"), vmem_limit_bytes=VMEM_LIMIT),
        name="attention",
    )(aqt, ak, avt, bqt, bk, bvt, gate, lam_vecs, subln_col)


def _chunk_cumsum(g, row_in_chunk, reverse):
    x = g
    sh = 1
    while sh < GLA_CHUNK:
        if reverse:
            nb = pltpu.roll(x, x.shape[0] - sh, axis=0)
            x = x + jnp.where(row_in_chunk < GLA_CHUNK - sh, nb, 0.0)
        else:
            nb = pltpu.roll(x, sh, axis=0)
            x = x + jnp.where(row_in_chunk >= sh, nb, 0.0)
        sh *= 2
    return x


def _gla_kernel(qk_ref, v_ref, g_ref, gate_ref, ng_ref, o_ref, acc_ref, st_ref):
    n_tiles = qk_ref.shape[1] // GLA_TILE
    ctx_tiles = CTX_LEN // GLA_TILE
    per = GLA_TILE // GLA_CHUNK
    ri = lax.broadcasted_iota(jnp.int32, (GLA_TILE, GLA_TILE), 0)
    ci = lax.broadcasted_iota(jnp.int32, (GLA_TILE, GLA_TILE), 1)
    same_chunk = (ri // GLA_CHUNK) == (ci // GLA_CHUNK)
    chunk_of_row = ri // GLA_CHUNK
    row_in_chunk = ri % GLA_CHUNK
    head_of_k = lax.broadcasted_iota(jnp.int32, (GLA_TILE, GLA_HEADS * GLA_DK), 1) // GLA_DK
    head_of_v = lax.broadcasted_iota(jnp.int32, (GLA_TILE, BRANCH_W), 1) // GLA_DV
    st_mask = ((lax.broadcasted_iota(jnp.int32, (BRANCH_W, GLA_HEADS * GLA_DK), 0) // GLA_DV)
               == (lax.broadcasted_iota(jnp.int32, (BRANCH_W, GLA_HEADS * GLA_DK), 1) // GLA_DK))
    m64 = _group_mean_matrix(BRANCH_W, GLA_DV)
    tris = [jnp.where(same_chunk & (ci <= ri), 1.0, 0.0), jnp.where(same_chunk & (ci >= ri), 1.0, 0.0)]
    tris4 = [jnp.concatenate([t] * GLA_HEADS, axis=1) for t in tris]

    nt_dims = (((1,), (1,)), ((), ()))
    tn_dims = (((0,), (0,)), ((), ()))

    def scan_step(i, carry):
        def tile_index(d, sub):
            t = i * GLA_STEP_TILES + sub
            if not d:
                return t
            return jnp.where(t < ctx_tiles, ctx_tiles - 1 - t, n_tiles + ctx_tiles - 1 - t)

        chains = [(bi, d, sub) for bi in range(GLA_BATCH) for d in (0, 1) for sub in range(GLA_STEP_TILES)]
        nc = len(chains)
        rows = [pl.ds(pl.multiple_of(tile_index(d, sub) * GLA_TILE, GLA_TILE), GLA_TILE)
                for _, d, sub in chains]
        ends = [[c * GLA_CHUNK if d else (c + 1) * GLA_CHUNK - 1 for c in range(per)] for _, d, _ in chains]

        cum = []
        for n, (bi, d, _) in enumerate(chains):
            cum.append(_chunk_cumsum(g_ref[bi, rows[n], 128 * d:128 * d + 128], row_in_chunk, bool(d)))

        qe, kd, kl, vb = [], [], [], []
        for n, (bi, d, _) in enumerate(chains):
            q = qk_ref[bi, rows[n], 0:128]
            k = qk_ref[bi, rows[n], 128:256]
            cum_last = jnp.concatenate(
                [jnp.broadcast_to(cum[n][e:e + 1, :], (GLA_CHUNK, 128)) for e in ends[n]], axis=0)
            qe.append(q * jnp.exp(cum[n]))
            kd.append(k * jnp.exp(-cum[n]))
            kl.append(k * jnp.exp(cum_last - cum[n]))
            vb.append(v_ref[bi, rows[n], :])

        qeb = [x.astype(BF16) for x in qe]
        att = []
        for n, (bi, d, _) in enumerate(chains):
            kd_heads = jnp.concatenate(
                [jnp.where(head_of_k == h, kd[n], 0.0).astype(BF16) for h in range(GLA_HEADS)], axis=0)
            a = lax.dot_general(qeb[n], kd_heads, nt_dims, preferred_element_type=F32)
            att.append((a * tris4[d]).astype(BF16))
        o = []
        for n in range(nc):
            v_heads = jnp.concatenate(
                [jnp.where(head_of_v == h, vb[n], jnp.zeros_like(vb[n])) for h in range(GLA_HEADS)], axis=0)
            o.append(jnp.dot(att[n], v_heads, preferred_element_type=F32))

        upd = []
        for n in range(nc):
            kl_chunks = jnp.concatenate(
                [jnp.where(chunk_of_row == c, kl[n], 0.0).astype(BF16) for c in range(per)], axis=1)
            upd.append(lax.dot_general(vb[n], kl_chunks, tn_dims, preferred_element_type=F32))
        seen = [[None] * per for _ in range(nc)]
        for slot in range(2 * GLA_BATCH):
            st = st_ref[slot]
            for n in range(slot * GLA_STEP_TILES, (slot + 1) * GLA_STEP_TILES):
                for c in (range(per - 1, -1, -1) if slot % 2 else range(per)):
                    seen[n][c] = st.astype(BF16)
                    decay = jnp.exp(cum[n][ends[n][c]:ends[n][c] + 1, :])
                    st = st * decay + jnp.where(st_mask, upd[n][:, c * 128:(c + 1) * 128], 0.0)
            st_ref[slot] = st
        for n in range(nc):
            qe_chunks = jnp.concatenate(
                [jnp.where(chunk_of_row == c, qe[n], 0.0).astype(BF16) for c in range(per)], axis=1)
            o_inter = lax.dot_general(qe_chunks, jnp.concatenate(seen[n], axis=1), nt_dims,
                                      preferred_element_type=F32)
            acc_ref[n // GLA_STEP_TILES, rows[n], :] = o[n] + o_inter
        return carry

    st_ref[...] = jnp.zeros_like(st_ref)
    lax.fori_loop(0, n_tiles // GLA_STEP_TILES, scan_step, 0)

    def finish_step(i, carry):
        rows = pl.ds(pl.multiple_of(i * ROW_TILE, ROW_TILE), ROW_TILE)
        tot = [acc_ref[2 * bi, rows, :] + acc_ref[2 * bi + 1, rows, :] for bi in range(GLA_BATCH)]
        sq = [_split2(t * t) for t in tot]
        ms = [sum(jnp.dot(p, m64, preferred_element_type=F32) for p in s) for s in sq]
        for bi in range(GLA_BATCH):
            y = tot[bi] * lax.rsqrt(ms[bi] + EPS) * ng_ref[...]
            o_ref[bi, rows, :] = (y * gate_ref[bi, rows, :].astype(F32)).astype(BF16)
        return carry

    lax.fori_loop(0, qk_ref.shape[1] // ROW_TILE, finish_step, 0)


def _gla(dqk, dv, dg, gate, ng):
    b, s, _ = dqk.shape
    nb = GLA_BATCH
    blk = lambda width, cb: pl.BlockSpec((nb, s, width), lambda i: (i, 0, cb))
    return pl.pallas_call(
        _gla_kernel,
        grid=(b // nb,),
        in_specs=[blk(256, 0), blk(256, 0), blk(256, 0), blk(256, 2),
                  pl.BlockSpec(ng.shape, lambda i: (0, 0))],
        out_specs=blk(256, 0),
        out_shape=jax.ShapeDtypeStruct((b, s, 256), BF16),
        scratch_shapes=[pltpu.VMEM((2 * nb, s, BRANCH_W), F32),
                        pltpu.VMEM((2 * nb, BRANCH_W, GLA_HEADS * GLA_DK), F32)],
        compiler_params=pltpu.CompilerParams(
            dimension_semantics=("parallel",), vmem_limit_bytes=VMEM_LIMIT),
        name="gla",
    )(dqk, dv, dg, gate, ng)


def _out_kernel(x_ref, mod_ref, ya_ref, yb_ref, yc_ref, yd_ref, w_ref, fg_ref, o_ref):
    for bi in range(x_ref.shape[0]):
        y = jnp.concatenate([ya_ref[bi], yb_ref[bi], yc_ref[bi], yd_ref[bi]], axis=1)
        upd = jnp.dot(y, w_ref[...], preferred_element_type=F32)
        xn = x_ref[bi] + mod_ref[bi, 0, 2:3, :] * upd
        o_ref[bi] = xn * lax.rsqrt(jnp.mean(xn * xn, axis=-1, keepdims=True) + EPS) * fg_ref[...]


def _out_proj(xs, modsel, ya, yb, yc, yd, wo, fg):
    b, s, _ = xs.shape
    t = ROW_TILE
    nb = OUT_BATCH
    skip = CTX_LEN // t
    row = lambda width: pl.BlockSpec((nb, t, width), lambda i, j: (i, j + skip, 0))
    return pl.pallas_call(
        _out_kernel,
        grid=(b // nb, s // t - skip),
        in_specs=[
            row(D_MODEL),
            pl.BlockSpec((nb, 1, 3, D_MODEL), lambda i, j: (i, 1, 0, 0)),
            row(256), row(256), row(256), row(256),
            pl.BlockSpec((None,) + wo.shape[1:], lambda i, j: (wo.shape[0] - 1, 0, 0)),
            pl.BlockSpec(fg.shape, lambda i, j: (0, 0)),
        ],
        out_specs=pl.BlockSpec((nb, t, D_MODEL), lambda i, j: (i, j, 0)),
        out_shape=jax.ShapeDtypeStruct((b, s - skip * t, D_MODEL), F32),
        compiler_params=pltpu.CompilerParams(
            dimension_semantics=("parallel", "parallel"), vmem_limit_bytes=VMEM_LIMIT),
        name="out_proj_final",
    )(xs, modsel, ya, yb, yc, yd, wo, fg)


def _rope_tables(seq, dim, width):
    half = dim // 2
    quarter = half // 2
    lane = np.arange(width) % dim
    freq = ROPE_THETA ** (-(2.0 * (lane % quarter)) / half)
    pos_t = np.arange(seq)
    pos = np.where(lane[None, :] < half, (pos_t // GRID_W)[:, None], (pos_t % GRID_W)[:, None])
    ang = pos * freq[None, :]
    cos, sin = np.cos(ang), np.sin(ang)
    first = (lane % half) < quarter
    s_lo = np.where(first[None, :], -sin, 0.0)
    s_hi = np.where(first[None, :], 0.0, sin)
    lat = np.stack([cos, s_lo, s_hi])
    ctx = np.stack([np.ones((CTX_LEN, width)), np.zeros((CTX_LEN, width)), np.zeros((CTX_LEN, width))])
    return jnp.asarray(np.concatenate([ctx, lat], axis=1), dtype=F32)


def _pack_w_in(w):
    offs = [0]
    for n in (256, 256, 256, 256, 256, 128, 128, 256, 256, 256, 256, 128, 128, 256, 256, 16, 16):
        offs.append(offs[-1] + n)
    seg = [w[..., offs[i]:offs[i + 1]] for i in range(17)]
    aq, ak, av, az, bq, bk, bv, bz, cu, cv, cz, dq, dk, dv, dz, drf, drb = seg
    gap = jnp.zeros(w.shape[:-1] + (LANES // 2,), w.dtype)
    r_gap = jnp.concatenate([drf, drb, gap[..., :LANES // 2 - 2 * GLA_RANK]], axis=-1)
    av_sp = jnp.concatenate([av[..., 0:64], r_gap, av[..., 64:128], gap, av[..., 128:192], gap,
                             av[..., 192:256], gap], axis=-1)
    bvk = jnp.concatenate([bv[..., :64], bk[..., :64], bv[..., 64:], bk[..., 64:]], axis=-1)
    packed = jnp.concatenate([aq, ak, av_sp, az, bq, bvk, bz, cu, cv, cz, dq, dk, dv, dz], axis=-1)
    assert packed.shape[-1] == P_PACK
    return packed.astype(BF16)


def kernel(x, c, ctx, c_ctx, ada_w, ada_b, norm_g, w_in, da_lq1, da_lk1, da_lq2, da_lk2,
           da_subln_g, gq_qnorm_g, gq_knorm_g, sg_ln_g, sg_ln_b, sg_w, sg_b,
           gla_w2_f, gla_b_f, gla_w2_b, gla_b_b, gla_norm_g, w_out, final_norm_g):
    b, seq, d = x.shape
    assert (seq, d, ctx.shape[1]) == (seq // ROW_TILE * ROW_TILE, D_MODEL, CTX_LEN)

    n_mod = 32
    cpad = jnp.zeros((n_mod, d), F32).at[:b].set(c).at[b].set(c_ctx)
    mod = _modulation(cpad, ada_w, ada_b)

    taba = _rope_tables(seq, DA_QK, 256)
    tabb = _rope_tables(seq, GQ_HD, 256)
    fg = final_norm_g.reshape(1, d)
    wp_all, wo_all, sgw_all = _pack_w_in(w_in), w_out.astype(BF16), sg_w.astype(BF16)

    nl = DEPTH
    modsel_all = jnp.concatenate([jnp.broadcast_to(mod[:, b].reshape(nl, 1, 1, 3, d), (nl, b, 1, 3, d)),
                                  mod[:, :b].reshape(nl, b, 1, 3, d)], axis=2)
    w2_all = (jnp.zeros((nl, LANES, 256), F32)
              .at[:, R_LANE:R_LANE + GLA_RANK, 0:128].set(gla_w2_f)
              .at[:, R_LANE + GLA_RANK:R_LANE + 2 * GLA_RANK, 128:256].set(gla_w2_b)).astype(BF16)
    b2_all = jnp.concatenate([gla_b_f, gla_b_b], axis=1).reshape(nl, 1, 256)
    sgb_all = jnp.repeat(jnp.swapaxes(sg_b, 1, 2), BRANCH_W // SG_GROUPS, axis=2)
    tiled = lambda g, n: jnp.tile(g, (1, n)).reshape(nl, 1, g.shape[1] * n)
    gqq_all, gqk_all = tiled(gq_qnorm_g, GQ_HEADS), tiled(gq_knorm_g, GQ_KV_HEADS)
    subln_all, glan_all = tiled(da_subln_g, DA_HEADS), tiled(gla_norm_g, GLA_HEADS)
    lam_all = jnp.stack([da_lq1, da_lk1, da_lq2, da_lk2], axis=1)

    stream_in = (ctx, x)
    for i in range(DEPTH):
        lam_init = 0.8 - 0.6 * math.exp(-0.3 * i)
        (xs, aqt, ak, avt, bqt, bk, bvt, gate, yc, dqk, dv, dg) = _layer(
            i, stream_in, modsel_all[i], norm_g[i].reshape(1, d), wp_all, taba, tabb,
            gqq_all[i], gqk_all[i], sg_ln_g[i].reshape(1, 256), sg_ln_b[i].reshape(1, 256),
            sgw_all, sgb_all[i], w2_all[i], b2_all[i])
        ya, yb = _attention(lam_init, aqt, ak, avt, bqt, bk, bvt, gate, lam_all[i],
                            subln_all[i].reshape(BRANCH_W, 1))
        yd = _gla(dqk, dv, dg, gate, glan_all[i])
        stream_in = (xs, modsel_all[i], ya, yb, yc, yd, wo_all)
    return _out_proj(*stream_in, fg)
```
